```python
import jax, jax.numpy as jnp
from jax import lax
import numpy as np

D_MODEL = 1024
BATCH = 8
SEQ = 2048
DEPTH = 1
DEC_BATCH = 8
DEC_SEQ = 16
PAST_LEN = 2048

CHUNK = 64
N_META = 16
D_CONV = D_MODEL // 2
CONV_W = 3
N_HEADS = 8
HEAD_DIM = 64
D_ATTN = N_HEADS * HEAD_DIM
D_FF = 4 * D_MODEL
Q_BLOCK = 128
EPS = 1e-6
ATTN_SCALE = HEAD_DIM ** -0.5
IN_COLS = 3 * D_CONV + 3 * D_ATTN + N_HEADS + 2 * D_MODEL

kernel_name = "hybrid_shortconv_fox_gated_stream_step"


def rms_norm(x, g):
    xf = x.astype(jnp.float32)
    n = xf * lax.rsqrt(jnp.mean(xf * xf, axis=-1, keepdims=True) + EPS)
    return (n * g.astype(jnp.float32)).astype(x.dtype)


def mixer_inputs(h, norm_g, w_in, b_f, q_g, k_g):
    xn = rms_norm(h, norm_g)
    u = xn @ w_in
    cuts = [D_CONV, 2 * D_CONV, 3 * D_CONV,
            3 * D_CONV + D_ATTN, 3 * D_CONV + 2 * D_ATTN, 3 * D_CONV + 3 * D_ATTN,
            3 * D_CONV + 3 * D_ATTN + N_HEADS]
    cb, cc, ch, q, k, v, fl, gl = jnp.split(u, cuts, axis=-1)
    bt = h.shape[:2]
    q = rms_norm(q.reshape(*bt, N_HEADS, HEAD_DIM), q_g)
    k = rms_norm(k.reshape(*bt, N_HEADS, HEAD_DIM), k_g)
    v = v.reshape(*bt, N_HEADS, HEAD_DIM)
    logf = jax.nn.log_sigmoid(fl.astype(jnp.float32) + b_f.astype(jnp.float32))
    return cb, cc, ch, q, k, v, logf, gl


def fox_attend(q, cq, qpos, k, ck, kpos, v):
    s = jnp.einsum("bqhd,bkhd->bhqk", q, k).astype(jnp.float32) * ATTN_SCALE
    s = s + jnp.swapaxes(cq, 1, 2)[..., :, None] - jnp.swapaxes(ck, 1, 2)[..., None, :]
    s = jnp.where(kpos[None, :] <= qpos[:, None], s, -jnp.inf)
    p = jax.nn.softmax(s, axis=-1).astype(v.dtype)
    return jnp.einsum("bhqk,bkhd->bqhd", p, v)


def short_conv(cb, cc, ch, left, conv_w, conv_b):
    z = cc * ch
    zp = jnp.concatenate([left.astype(z.dtype), z], axis=1)
    t = z.shape[1]
    y = sum(zp[:, i:i + t] * conv_w[i] for i in range(CONV_W)) + conv_b
    return cb * y, zp[:, -(CONV_W - 1):]


def gated_merge(conv_out, attn_out, gl, w_br_conv, w_br_attn, w_out):
    g_conv, g_attn = jnp.split(jax.nn.sigmoid(gl), 2, axis=-1)
    attn_flat = attn_out.reshape(*attn_out.shape[:2], D_ATTN)
    merged = g_conv * (conv_out @ w_br_conv) + g_attn * (attn_flat @ w_br_attn)
    return merged @ w_out


def sq_relu_mlp(h, norm_g, w_up, w_down):
    a = jax.nn.relu(rms_norm(h, norm_g) @ w_up)
    return h + (a * a) @ w_down


def setup_inputs(seed: int = 0) -> dict:
    key = jax.random.key(seed)
    ks = jax.random.split(key, 24)

    def nrm(k, shape, scale=1.0):
        return jax.random.normal(k, shape, jnp.float32) * scale

    return {
        "x_prompt": nrm(ks[0], (BATCH, SEQ, D_MODEL)),
        "x_sample": nrm(ks[1], (DEC_BATCH, DEC_SEQ, D_MODEL)),
        "cache_k": nrm(ks[2], (DEPTH, DEC_BATCH, PAST_LEN, N_HEADS, HEAD_DIM)),
        "cache_v": nrm(ks[3], (DEPTH, DEC_BATCH, PAST_LEN, N_HEADS, HEAD_DIM)),
        "cache_logf": jax.nn.log_sigmoid(nrm(ks[4], (DEPTH, DEC_BATCH, PAST_LEN, N_HEADS)) + 2.0),
        "state_conv": nrm(ks[5], (DEPTH, DEC_BATCH, CONV_W - 1, D_CONV)),
        "meta": nrm(ks[6], (N_META, D_MODEL)),
        "norm1_g": 1.0 + nrm(ks[7], (DEPTH, D_MODEL), 0.02),
        "w_in": nrm(ks[8], (DEPTH, D_MODEL, IN_COLS), D_MODEL ** -0.5),
        "b_f": 1.0 + nrm(ks[9], (DEPTH, N_HEADS), 0.1),
        "conv_w": nrm(ks[10], (DEPTH, CONV_W, D_CONV), CONV_W ** -0.5),
        "conv_b": nrm(ks[11], (DEPTH, D_CONV), 0.01),
        "q_norm_g": 1.0 + nrm(ks[12], (DEPTH, HEAD_DIM), 0.02),
        "k_norm_g": 1.0 + nrm(ks[13], (DEPTH, HEAD_DIM), 0.02),
        "w_br_conv": nrm(ks[14], (DEPTH, D_CONV, D_MODEL), D_CONV ** -0.5),
        "w_br_attn": nrm(ks[15], (DEPTH, D_ATTN, D_MODEL), D_ATTN ** -0.5),
        "w_out": nrm(ks[16], (DEPTH, D_MODEL, D_MODEL), D_MODEL ** -0.5),
        "norm2_g": 1.0 + nrm(ks[17], (DEPTH, D_MODEL), 0.02),
        "w_up": nrm(ks[18], (DEPTH, D_MODEL, D_FF), D_MODEL ** -0.5),
        "w_down": nrm(ks[19], (DEPTH, D_FF, D_MODEL), D_FF ** -0.5),
    }


def reference(x_prompt, x_sample, cache_k, cache_v, cache_logf, state_conv, meta,
              norm1_g, w_in, b_f, conv_w, conv_b, q_norm_g, k_norm_g,
              w_br_conv, w_br_attn, w_out, norm2_g, w_up, w_down):
    b, seq, _ = x_prompt.shape
    n_blk = seq // Q_BLOCK
    length = N_META + seq
    pos = jnp.arange(length)
    blk_pos = (N_META + jnp.arange(seq)).reshape(n_blk, Q_BLOCK)

    past = cache_k.shape[2]
    dec_seq = x_sample.shape[1]
    kpos_s = jnp.arange(past + dec_seq)
    qpos_s = past + jnp.arange(dec_seq)

    hp = jnp.concatenate(
        [jnp.broadcast_to(meta[None].astype(x_prompt.dtype), (b, N_META, D_MODEL)), x_prompt], axis=1)
    hs = x_sample

    pk, pv, pf, pc, sk, sv, sf, sc = [], [], [], [], [], [], [], []
    for l in range(DEPTH):
        cb, cc, ch, q, k, v, logf, gl = mixer_inputs(hp, norm1_g[l], w_in[l], b_f[l], q_norm_g[l], k_norm_g[l])
        c = jnp.cumsum(logf, axis=1)
        o_meta = fox_attend(q[:, :N_META], c[:, :N_META], pos[:N_META],
                            k[:, :N_META], c[:, :N_META], pos[:N_META], v[:, :N_META])
        qb = jnp.swapaxes(q[:, N_META:].reshape(b, n_blk, Q_BLOCK, N_HEADS, HEAD_DIM), 0, 1)
        cqb = jnp.swapaxes(c[:, N_META:].reshape(b, n_blk, Q_BLOCK, N_HEADS), 0, 1)
        o_blk = lax.map(lambda a: fox_attend(a[0], a[1], a[2], k, c, pos, v), (qb, cqb, blk_pos))
        o_real = jnp.swapaxes(o_blk, 0, 1).reshape(b, seq, N_HEADS, HEAD_DIM)
        attn_p = jnp.concatenate([o_meta, o_real], axis=1)
        conv_p, conv_rows_p = short_conv(cb, cc, ch, jnp.zeros((b, CONV_W - 1, D_CONV), ch.dtype),
                                         conv_w[l], conv_b[l])
        hp = hp + gated_merge(conv_p, attn_p, gl, w_br_conv[l], w_br_attn[l], w_out[l])
        pk.append(k); pv.append(v); pf.append(logf); pc.append(conv_rows_p)
        if l == DEPTH - 1:
            hp = hp[:, N_META:]
        hp = sq_relu_mlp(hp, norm2_g[l], w_up[l], w_down[l])

        cb, cc, ch, q, k, v, logf, gl = mixer_inputs(hs, norm1_g[l], w_in[l], b_f[l], q_norm_g[l], k_norm_g[l])
        cum = jnp.cumsum(cache_logf[l].astype(jnp.float32), axis=1)
        ck_cache = cum - cum[:, -1:]
        cq = jnp.cumsum(logf, axis=1)
        k_all = jnp.concatenate([cache_k[l].astype(k.dtype), k], axis=1)
        v_all = jnp.concatenate([cache_v[l].astype(v.dtype), v], axis=1)
        ck_all = jnp.concatenate([ck_cache, cq], axis=1)
        attn_s = fox_attend(q, cq, qpos_s, k_all, ck_all, kpos_s, v_all)
        conv_s, conv_rows_s = short_conv(cb, cc, ch, state_conv[l], conv_w[l], conv_b[l])
        hs = hs + gated_merge(conv_s, attn_s, gl, w_br_conv[l], w_br_attn[l], w_out[l])
        hs = sq_relu_mlp(hs, norm2_g[l], w_up[l], w_down[l])
        sk.append(k); sv.append(v); sf.append(logf); sc.append(conv_rows_s)

    return (hp, hs,
            jnp.stack(pk), jnp.stack(pv), jnp.stack(pf), jnp.stack(pc),
            jnp.stack(sk), jnp.stack(sv), jnp.stack(sf), jnp.stack(sc))
```

```python
import functools

import jax
import jax.numpy as jnp
from jax import lax
from jax.experimental import pallas as pl
from jax.experimental.pallas import tpu as pltpu

D_MODEL = 1024
D_CONV = D_MODEL // 2
CONV_W = 3
N_HEADS = 8
HEAD_DIM = 64
D_ATTN = N_HEADS * HEAD_DIM
D_FF = 4 * D_MODEL
N_META = 16
EPS = 1e-6
ATTN_SCALE = HEAD_DIM ** -0.5

F32 = jnp.float32
BF16 = jnp.bfloat16

VMEM_LIMIT_BYTES = 56 * 1024 * 1024
LANES = 128
PROJ_ROWS = 512
MLP_ROWS = 512
ATTN_BLOCK = 256
MASKED_BIAS = 1e30
ZPAD = 8


def _dot(a, b):
    return jnp.dot(a, b, preferred_element_type=F32)


def _dot_nt(a, b):
    return lax.dot_general(a, b, (((1,), (1,)), ((), ())), preferred_element_type=F32)


def _dot_exact(a, b):
    return jnp.dot(a, b, preferred_element_type=F32, precision=lax.Precision.HIGHEST)


def _log_sigmoid(x):
    return jnp.minimum(x, 0.0) - jnp.log1p(jnp.exp(-jnp.abs(x)))


def _cumsum_few(x, axis):
    n = x.shape[axis]
    idx = lax.broadcasted_iota(jnp.int32, x.shape, axis)
    out = jnp.zeros(x.shape, F32)
    for i in range(n):
        term = x[i:i + 1, :] if axis == 0 else x[:, i:i + 1]
        out = out + jnp.where(idx >= i, term, 0.0)
    return out


def _triangles(n):
    r = lax.broadcasted_iota(jnp.int32, (n, n), 0)
    c = lax.broadcasted_iota(jnp.int32, (n, n), 1)
    upper = (r <= c).astype(F32)
    lower = (r >= c).astype(F32)
    return upper, lower


def _const_spec(shape):
    nd = len(shape)
    return pl.BlockSpec(shape, lambda *_: (0,) * nd, pipeline_mode=pl.Buffered(1))


def _params(n_axes):
    return pltpu.CompilerParams(
        dimension_semantics=("arbitrary",) * n_axes,
        vmem_limit_bytes=VMEM_LIMIT_BYTES)


def _proj_kernel(x_ref, left_ref, g1_ref, wm_ref, wgl_ref, wfl_ref, wflt_ref,
                 bfr_ref, bfc_ref, cw_ref, cb_ref, qg_ref, kg_ref, bd_ref,
                 conv_ref, q_ref, k_ref, v_ref, lf_ref, lft_ref, gate_ref, zlast_ref,
                 zbuf, *, n_seg, seg_len, tiles_per_seq):
    first = (pl.program_id(0) % tiles_per_seq) == 0

    x = x_ref[...]
    ms = jnp.mean(x * x, axis=-1, keepdims=True)
    xn = (x * lax.rsqrt(ms + EPS) * g1_ref[...]).astype(BF16)

    def seg(j):
        return _dot(xn, wm_ref[:, j * D_CONV:(j + 1) * D_CONV])

    cb = seg(0)
    z = seg(1) * seg(2)

    @pl.when(first)
    def _():
        zbuf[:, ZPAD - 2:ZPAD, :] = left_ref[...]

    w0 = cw_ref[0:1, :]
    w1 = cw_ref[1:2, :]
    w2 = cw_ref[2:3, :]
    for s in range(n_seg):
        r0 = s * seg_len
        zs = z[r0:r0 + seg_len]
        zbuf[s, ZPAD:ZPAD + seg_len, :] = zs
        zm1 = zbuf[s, ZPAD - 1:ZPAD - 1 + seg_len, :]
        zm2 = zbuf[s, ZPAD - 2:ZPAD - 2 + seg_len, :]
        y = zm2 * w0 + zm1 * w1 + zs * w2 + cb_ref[...]
        conv_ref[r0:r0 + seg_len, :] = (cb[r0:r0 + seg_len] * y).astype(BF16)
        tail = zbuf[s, ZPAD + seg_len - 2:ZPAD + seg_len, :]
        zlast_ref[s] = tail
        zbuf[s, ZPAD - 2:ZPAD, :] = tail

    def head_norm(u, g):
        ssq = _dot((u * u).astype(BF16), bd_ref[...])
        return u * lax.rsqrt(ssq * (1.0 / HEAD_DIM) + EPS) * g

    q_ref[...] = (head_norm(seg(3), qg_ref[...]) * ATTN_SCALE).astype(BF16)
    k_ref[...] = head_norm(seg(4), kg_ref[...])
    v_ref[...] = seg(5)

    fl = _dot(xn, wfl_ref[...])[:, :N_HEADS]
    lf_ref[...] = _log_sigmoid(fl + bfr_ref[...])
    lft_ref[...] = _log_sigmoid(_dot_nt(wflt_ref[...], xn) + bfc_ref[...])

    for c in range(2 * D_MODEL // D_CONV):
        gl = _dot(xn, wgl_ref[:, c * D_CONV:(c + 1) * D_CONV])
        gate_ref[:, c * D_CONV:(c + 1) * D_CONV] = jax.nn.sigmoid(gl).astype(BF16)


def _project(x2d, left, wts, *, rows, n_seg, seg_len, tiles_per_seq, lft_shape, lft_spec):
    n_rows = x2d.shape[0]
    grid = (n_rows // rows,)
    n_seq = left.shape[0]

    def row_spec(width):
        return pl.BlockSpec((rows, width), lambda i: (i, 0))

    seq_spec = pl.BlockSpec((n_seg, CONV_W - 1, D_CONV), lambda i: (i // tiles_per_seq, 0, 0))
    in_specs = [row_spec(D_MODEL), seq_spec] + [_const_spec(w.shape) for w in wts]
    out_shape = (
        jax.ShapeDtypeStruct((n_rows, D_CONV), BF16),
        jax.ShapeDtypeStruct((n_rows, D_ATTN), BF16),
        jax.ShapeDtypeStruct((n_rows, D_ATTN), F32),
        jax.ShapeDtypeStruct((n_rows, D_ATTN), F32),
        jax.ShapeDtypeStruct((n_rows, N_HEADS), F32),
        jax.ShapeDtypeStruct(lft_shape, F32),
        jax.ShapeDtypeStruct((n_rows, 2 * D_MODEL), BF16),
        jax.ShapeDtypeStruct((n_seq, CONV_W - 1, D_CONV), F32),
    )
    out_specs = (row_spec(D_CONV), row_spec(D_ATTN), row_spec(D_ATTN), row_spec(D_ATTN),
                 row_spec(N_HEADS), lft_spec, row_spec(2 * D_MODEL), seq_spec)
    kern = functools.partial(_proj_kernel, n_seg=n_seg, seg_len=seg_len,
                             tiles_per_seq=tiles_per_seq)
    return pl.pallas_call(
        kern, grid=grid, in_specs=in_specs, out_specs=out_specs, out_shape=out_shape,
        scratch_shapes=[pltpu.VMEM((n_seg, seg_len + ZPAD, D_CONV), F32)],
        compiler_params=_params(1), name="proj",
    )(x2d, left, *wts)


def _prompt_attn_kernel(q_ref, k_ref, v_ref, lf_ref, lftm_ref, lft_ref, o_ref,
                        kb, vb, ccol, crow_meta, crow, *, seq):
    t = pl.program_id(1)
    blk = ATTN_BLOCK
    n_blk = seq // blk

    @pl.when(t == 0)
    def _():
        kb[0:N_META, :] = k_ref[0:N_META, :].astype(BF16)
        vb[0:N_META, :] = v_ref[0:N_META, :].astype(BF16)
        kb[N_META:LANES, :] = jnp.zeros((LANES - N_META, D_ATTN), BF16)
        vb[N_META:LANES, :] = jnp.zeros((LANES - N_META, D_ATTN), BF16)
        for j in range(n_blk):
            src = slice(N_META + j * blk, N_META + (j + 1) * blk)
            dst = slice(LANES + j * blk, LANES + (j + 1) * blk)
            kb[dst, :] = k_ref[src, :].astype(BF16)
            vb[dst, :] = v_ref[src, :].astype(BF16)

        upper, lower = _triangles(blk)
        meta_r = _cumsum_few(lftm_ref[...], axis=1)
        meta_c = _cumsum_few(lf_ref[0:N_META, :], axis=0)
        crow_meta[...] = jnp.full((N_HEADS, LANES), MASKED_BIAS, F32)
        crow_meta[:, 0:N_META] = meta_r
        off_r = meta_r[:, N_META - 1:N_META]
        off_c = meta_c[N_META - 1:N_META, :]
        for j in range(n_blk):
            loc_r = _dot_exact(lft_ref[:, j * blk:(j + 1) * blk], upper) + off_r
            loc_c = _dot_exact(
                lower, lf_ref[N_META + j * blk:N_META + (j + 1) * blk, :]) + off_c
            crow[j] = loc_r
            ccol[j * blk:(j + 1) * blk, :] = loc_c
            off_r = loc_r[:, blk - 1:blk]
            off_c = loc_c[blk - 1:blk, :]

    q = q_ref[...]
    row0 = pl.multiple_of(t * blk, blk)
    cq_all = ccol[pl.ds(row0, blk), :]
    tri = (lax.broadcasted_iota(jnp.int32, (blk, blk), 1)
           <= lax.broadcasted_iota(jnp.int32, (blk, blk), 0))

    heads = [slice(h * HEAD_DIM, (h + 1) * HEAD_DIM) for h in range(N_HEADS)]

    def update(carry, s, v_rows, hs):
        m, l, acc = carry
        m_new = jnp.maximum(m, jnp.max(s, axis=-1, keepdims=True))
        alpha = jnp.exp(m - m_new)
        p = jnp.exp(s - m_new)
        l = alpha * l + jnp.sum(p, axis=-1, keepdims=True)
        acc = alpha * acc + _dot(p.astype(BF16), vb[v_rows, hs])
        return m_new, l, acc

    def block_scores(j, h):
        rows = pl.ds(pl.multiple_of(LANES + j * blk, LANES), blk)
        s = _dot_nt(q[:, heads[h]], kb[rows, heads[h]])
        return s + cq_all[:, h:h + 1] - crow[j, h:h + 1, :], rows

    carries = []
    for h, hs in enumerate(heads):
        s0 = _dot_nt(q[:, hs], kb[0:LANES, hs]) + cq_all[:, h:h + 1] - crow_meta[h:h + 1, :]
        m0 = jnp.max(s0, axis=-1, keepdims=True)
        p0 = jnp.exp(s0 - m0)
        carries.append((m0, jnp.sum(p0, axis=-1, keepdims=True),
                        _dot(p0.astype(BF16), vb[0:LANES, hs])))

    def body(j, carries):
        out = []
        for h, hs in enumerate(heads):
            s, rows = block_scores(j, h)
            out.append(update(carries[h], s, rows, hs))
        return tuple(out)

    carries = lax.fori_loop(0, t, body, tuple(carries))

    outs = []
    for h, hs in enumerate(heads):
        s_diag, rows = block_scores(t, h)
        _, l, acc = update(carries[h], jnp.where(tri, s_diag, -jnp.inf), rows, hs)
        outs.append(acc / l)

    o_ref[...] = jnp.concatenate(outs, axis=-1).astype(BF16)


def _prompt_attention(q, k_all, v_all, lf_all, lft_meta, lft_real):
    b, seq, _ = q.shape
    length = k_all.shape[1]
    blk = ATTN_BLOCK
    grid = (b, seq // blk)
    full = lambda width: pl.BlockSpec((None, length, width), lambda i, t: (i, 0, 0))
    kern = functools.partial(_prompt_attn_kernel, seq=seq)
    return pl.pallas_call(
        kern, grid=grid,
        in_specs=[pl.BlockSpec((None, blk, D_ATTN), lambda i, t: (i, t, 0)),
                  full(D_ATTN), full(D_ATTN), full(N_HEADS),
                  _const_spec(lft_meta.shape),
                  pl.BlockSpec((None, N_HEADS, seq), lambda i, t: (i, 0, 0))],
        out_specs=pl.BlockSpec((None, blk, D_ATTN), lambda i, t: (i, t, 0)),
        out_shape=jax.ShapeDtypeStruct((b, seq, D_ATTN), BF16),
        scratch_shapes=[pltpu.VMEM((LANES + seq, D_ATTN), BF16),
                        pltpu.VMEM((LANES + seq, D_ATTN), BF16),
                        pltpu.VMEM((seq, N_HEADS), F32),
                        pltpu.VMEM((N_HEADS, LANES), F32),
                        pltpu.VMEM((seq // blk, N_HEADS, blk), F32)],
        compiler_params=_params(2), name="prompt_attn",
    )(q, k_all, v_all, lf_all, lft_meta, lft_real)


def _sample_attn_kernel(q_ref, kn_ref, vn_ref, lf_ref, lft_ref, ck_ref, cv_ref, clft_ref,
                        o_ref, kall, vall, crow, *, past, dec):
    n_keys = kall.shape[0]
    pad = n_keys - past - dec
    blk = ATTN_BLOCK
    kall[0:past, :] = ck_ref[...].astype(BF16)
    vall[0:past, :] = cv_ref[...].astype(BF16)
    kall[past:past + dec, :] = kn_ref[...].astype(BF16)
    vall[past:past + dec, :] = vn_ref[...].astype(BF16)
    kall[past + dec:, :] = jnp.zeros((pad, D_ATTN), BF16)
    vall[past + dec:, :] = jnp.zeros((pad, D_ATTN), BF16)

    upper, _ = _triangles(blk)
    off = jnp.zeros((N_HEADS, 1), F32)
    for j in range(past // blk):
        loc = _dot_exact(clft_ref[:, j * blk:(j + 1) * blk], upper) + off
        crow[:, j * blk:(j + 1) * blk] = loc
        off = loc[:, blk - 1:blk]
    crow[:, 0:past] = crow[:, 0:past] - off

    cq_c = _cumsum_few(lf_ref[...], axis=0)
    cq_r = _cumsum_few(lft_ref[...], axis=1)
    crow[:, past:] = jnp.full((N_HEADS, dec + pad), MASKED_BIAS, F32)
    crow[:, past:past + dec] = cq_r

    q = q_ref[...]
    lane_head = lax.broadcasted_iota(jnp.int32, (dec, D_ATTN), 1) // HEAD_DIM
    q_exp = jnp.concatenate(
        [jnp.where(lane_head == h, q, jnp.zeros_like(q)) for h in range(N_HEADS)], axis=0)
    s_all = _dot_nt(q_exp, kall[...])

    kpos = lax.broadcasted_iota(jnp.int32, (dec, n_keys), 1)
    qpos = past + lax.broadcasted_iota(jnp.int32, (dec, n_keys), 0)
    visible = kpos <= qpos
    probs = []
    norms = []
    for h in range(N_HEADS):
        s = s_all[h * dec:(h + 1) * dec, :] + cq_c[:, h:h + 1] - crow[h:h + 1, :]
        s = jnp.where(visible, s, -jnp.inf)
        p = jnp.exp(s - jnp.max(s, axis=-1, keepdims=True))
        norms.append(jnp.sum(p, axis=-1, keepdims=True))
        probs.append(p.astype(BF16))
    o_all = _dot(jnp.concatenate(probs, axis=0), vall[...])
    out = jnp.zeros((dec, D_ATTN), F32)
    for h in range(N_HEADS):
        o = o_all[h * dec:(h + 1) * dec, :] / norms[h]
        out = out + jnp.where(lane_head == h, o, 0.0)
    o_ref[...] = out.astype(BF16)


def _sample_attention(q, k_new, v_new, lf, lft, cache_k, cache_v, cache_lft):
    b, dec, _ = q.shape
    past = cache_k.shape[1]
    n_keys = past + ATTN_BLOCK
    per_b = lambda *shape: pl.BlockSpec((None,) + shape, lambda i: (i,) + (0,) * len(shape))
    kern = functools.partial(_sample_attn_kernel, past=past, dec=dec)
    return pl.pallas_call(
        kern, grid=(b,),
        in_specs=[per_b(dec, D_ATTN), per_b(dec, D_ATTN), per_b(dec, D_ATTN),
                  per_b(dec, N_HEADS), per_b(N_HEADS, dec),
                  per_b(past, D_ATTN), per_b(past, D_ATTN), per_b(N_HEADS, past)],
        out_specs=per_b(dec, D_ATTN),
        out_shape=jax.ShapeDtypeStruct((b, dec, D_ATTN), BF16),
        scratch_shapes=[pltpu.VMEM((n_keys, D_ATTN), BF16),
                        pltpu.VMEM((n_keys, D_ATTN), BF16),
                        pltpu.VMEM((N_HEADS, n_keys), F32)],
        compiler_params=_params(1), name="sample_attn",
    )(q, k_new, v_new, lf, lft, cache_k, cache_v, cache_lft)


def _mlp_kernel(x_ref, conv_ref, attn_ref, gate_ref, wbc_ref, wba_ref, wo_ref, g2_ref,
                wup_ref, wdn_ref, y_ref):
    ya = _dot(conv_ref[...], wbc_ref[...])
    yb = _dot(attn_ref[...], wba_ref[...])
    merged = (gate_ref[:, 0:D_MODEL].astype(F32) * ya
              + gate_ref[:, D_MODEL:2 * D_MODEL].astype(F32) * yb)
    h = x_ref[...] + _dot(merged.astype(BF16), wo_ref[...])
    ms = jnp.mean(h * h, axis=-1, keepdims=True)
    hn = (h * lax.rsqrt(ms + EPS) * g2_ref[...]).astype(BF16)
    acc = h
    for c in range(D_FF // D_MODEL):
        cols = slice(c * D_MODEL, (c + 1) * D_MODEL)
        a = jnp.maximum(_dot(hn, wup_ref[:, cols]), 0.0)
        acc = acc + _dot((a * a).astype(BF16), wdn_ref[cols, :])
    y_ref[...] = acc


def _merge_mlp(x2d, conv, attn, gates, wts, *, rows):
    n_rows = x2d.shape[0]
    row_spec = lambda width: pl.BlockSpec((rows, width), lambda i: (i, 0))
    return pl.pallas_call(
        _mlp_kernel, grid=(n_rows // rows,),
        in_specs=[row_spec(D_MODEL), row_spec(D_CONV), row_spec(D_ATTN), row_spec(2 * D_MODEL)]
        + [_const_spec(w.shape) for w in wts],
        out_specs=row_spec(D_MODEL),
        out_shape=jax.ShapeDtypeStruct((n_rows, D_MODEL), F32),
        compiler_params=_params(1), name="merge_mlp",
    )(x2d, conv, attn, gates, *wts)


def kernel(x_prompt, x_sample, cache_k, cache_v, cache_logf, state_conv, meta,
           norm1_g, w_in, b_f, conv_w, conv_b, q_norm_g, k_norm_g,
           w_br_conv, w_br_attn, w_out, norm2_g, w_up, w_down):
    b, seq, _ = x_prompt.shape
    db, dec, _ = x_sample.shape
    past = cache_k.shape[2]
    n_main = 3 * D_CONV + 3 * D_ATTN

    w = w_in[0]
    w_fl = w[:, n_main:n_main + N_HEADS].astype(BF16)
    head_of = jnp.arange(D_ATTN) // HEAD_DIM
    proj_wts = (
        norm1_g[0][None, :],
        w[:, :n_main].astype(BF16),
        w[:, n_main + N_HEADS:].astype(BF16),
        jnp.pad(w_fl, ((0, 0), (0, LANES - N_HEADS))),
        w_fl.T,
        b_f[0][None, :], b_f[0][:, None],
        conv_w[0], conv_b[0][None, :],
        jnp.tile(q_norm_g[0], N_HEADS)[None, :], jnp.tile(k_norm_g[0], N_HEADS)[None, :],
        (head_of[:, None] == head_of[None, :]).astype(BF16),
    )
    mlp_wts = (w_br_conv[0].astype(BF16), w_br_attn[0].astype(BF16), w_out[0].astype(BF16),
               norm2_g[0][None, :], w_up[0].astype(BF16), w_down[0].astype(BF16))

    n_small = N_META + db * dec
    x_small = jnp.concatenate([meta, x_sample.reshape(db * dec, D_MODEL)], axis=0)
    left_small = jnp.concatenate(
        [jnp.zeros((1, CONV_W - 1, D_CONV), F32), state_conv[0]], axis=0)
    (conv_s, q_s, k_s, v_s, lf_s, lft_s, gate_s, zlast_s) = _project(
        x_small, left_small, proj_wts, rows=n_small, n_seg=1 + db, seg_len=dec,
        tiles_per_seq=1, lft_shape=(N_HEADS, n_small),
        lft_spec=pl.BlockSpec((N_HEADS, n_small), lambda i: (0, 0)))

    tiles_per_seq = seq // PROJ_ROWS
    left_p = jnp.broadcast_to(zlast_s[0:1], (b, CONV_W - 1, D_CONV))
    (conv_p, q_p, k_p, v_p, lf_p, lft_p, gate_p, zlast_p) = _project(
        x_prompt.reshape(b * seq, D_MODEL), left_p, proj_wts, rows=PROJ_ROWS, n_seg=1,
        seg_len=PROJ_ROWS, tiles_per_seq=tiles_per_seq, lft_shape=(b, N_HEADS, seq),
        lft_spec=pl.BlockSpec((None, N_HEADS, PROJ_ROWS),
                              lambda i: (i // tiles_per_seq, 0, i % tiles_per_seq)))

    def with_meta(meta_rows, real):
        lead = jnp.broadcast_to(meta_rows[None], (b,) + meta_rows.shape)
        return jnp.concatenate([lead, real], axis=1)

    k_all = with_meta(k_s[:N_META], k_p.reshape(b, seq, D_ATTN))
    v_all = with_meta(v_s[:N_META], v_p.reshape(b, seq, D_ATTN))
    lf_all = with_meta(lf_s[:N_META], lf_p.reshape(b, seq, N_HEADS))

    attn_p = _prompt_attention(q_p.reshape(b, seq, D_ATTN), k_all, v_all, lf_all,
                               lft_s[:, :N_META], lft_p)
    y_prompt = _merge_mlp(x_prompt.reshape(b * seq, D_MODEL), conv_p,
                          attn_p.reshape(b * seq, D_ATTN), gate_p, mlp_wts, rows=MLP_ROWS)

    k_new = k_s[N_META:].reshape(db, dec, D_ATTN)
    v_new = v_s[N_META:].reshape(db, dec, D_ATTN)
    lf_new = lf_s[N_META:].reshape(db, dec, N_HEADS)
    attn_s = _sample_attention(
        q_s[N_META:].reshape(db, dec, D_ATTN), k_new, v_new, lf_new,
        jnp.swapaxes(lf_new, 1, 2),
        cache_k[0].reshape(db, past, D_ATTN), cache_v[0].reshape(db, past, D_ATTN),
        jnp.swapaxes(cache_logf[0], 1, 2))
    y_sample = _merge_mlp(x_sample.reshape(db * dec, D_MODEL), conv_s[N_META:],
                          attn_s.reshape(db * dec, D_ATTN), gate_s[N_META:], mlp_wts,
                          rows=db * dec)

    return (y_prompt.reshape(b, seq, D_MODEL),
            y_sample.reshape(db, dec, D_MODEL),
            k_all.reshape(1, b, N_META + seq, N_HEADS, HEAD_DIM),
            v_all.reshape(1, b, N_META + seq, N_HEADS, HEAD_DIM),
            lf_all[None],
            zlast_p[None],
            k_new.reshape(1, db, dec, N_HEADS, HEAD_DIM),
            v_new.reshape(1, db, dec, N_HEADS, HEAD_DIM),
            lf_new[None],
            zlast_s[1:][None])
```

```python
import functools

import jax
import jax.numpy as jnp
from jax import lax
from jax.experimental import pallas as pl
from jax.experimental.pallas import tpu as pltpu

D_MODEL = 1024
D_CONV = D_MODEL // 2
CONV_W = 3
N_HEADS = 8
HEAD_DIM = 64
D_ATTN = N_HEADS * HEAD_DIM
D_FF = 4 * D_MODEL
N_META = 16
EPS = 1e-6
ATTN_SCALE = HEAD_DIM ** -0.5

F32 = jnp.float32
BF16 = jnp.bfloat16

VMEM_LIMIT_BYTES = 56 * 1024 * 1024
LANES = 128
PROJ_ROWS = 512
MLP_ROWS = 512
ATTN_BLOCK = 256
MASKED_BIAS = 1e30
ZPAD = 8
N_SPLIT = 3


def _dot(a, b):
    return jnp.dot(a, b, preferred_element_type=F32)


def _dot_nt(a, b):
    return lax.dot_general(a, b, (((1,), (1,)), ((), ())), preferred_element_type=F32)


def _dot_exact(a, b):
    return jnp.dot(a, b, preferred_element_type=F32, precision=lax.Precision.HIGHEST)


def _log_sigmoid(x):
    return jnp.minimum(x, 0.0) - jnp.log1p(jnp.exp(-jnp.abs(x)))


def _cumsum_few(x, axis):
    n = x.shape[axis]
    idx = lax.broadcasted_iota(jnp.int32, x.shape, axis)
    out = jnp.zeros(x.shape, F32)
    for i in range(n):
        term = x[i:i + 1, :] if axis == 0 else x[:, i:i + 1]
        out = out + jnp.where(idx >= i, term, 0.0)
    return out


def _triangles(n):
    r = lax.broadcasted_iota(jnp.int32, (n, n), 0)
    c = lax.broadcasted_iota(jnp.int32, (n, n), 1)
    upper = (r <= c).astype(F32)
    lower = (r >= c).astype(F32)
    return upper, lower


def _split3(c):
    hi = c.astype(BF16).astype(F32)
    r1 = c - hi
    mid = r1.astype(BF16).astype(F32)
    return hi, mid, r1 - mid


def _const_spec(shape):
    nd = len(shape)
    return pl.BlockSpec(shape, lambda *_: (0,) * nd, pipeline_mode=pl.Buffered(1))


def _params(n_axes):
    return pltpu.CompilerParams(
        dimension_semantics=("arbitrary",) * n_axes,
        vmem_limit_bytes=VMEM_LIMIT_BYTES)


def _head_norm_t(ut, g_col):
    out = []
    for h in range(N_HEADS):
        blk = ut[h * HEAD_DIM:(h + 1) * HEAD_DIM, :]
        ms = jnp.mean(blk * blk, axis=0, keepdims=True)
        out.append(blk * lax.rsqrt(ms + EPS) * g_col[h * HEAD_DIM:(h + 1) * HEAD_DIM, :])
    return jnp.concatenate(out, axis=0)


def _proj_kernel(*refs, n_seg, seg_len, tiles_per_seq, shifted):
    (x_ref, left_ref, g1_ref, wa_ref, wk_ref, wgl_ref, wt_ref, wfl_ref, wflt_ref,
     bfr_ref, bfc_ref, cw_ref, cb_ref, qgc_ref, kgc_ref, kgr_ref, bd_ref) = refs[:17]
    if shifted:
        ktm_ref, vtm_ref, lftm_ref = refs[17:20]
        (conv_ref, kb_ref, qt_ref, kt_ref, vt_ref, lftp_ref, lft_ref, lfp_ref, gate_ref,
         zlast_ref, zbuf, kcar, vcar, lcar) = refs[20:]
    else:
        wq_ref, wv_ref, qgr_ref = refs[17:20]
        (conv_ref, q_ref, k_ref, v_ref, kt_ref, vt_ref, lft_ref, lfp_ref, gate_ref,
         zlast_ref, zbuf) = refs[20:]

    steps_per_seq = tiles_per_seq + (1 if shifted else 0)
    step = pl.program_id(0) % steps_per_seq
    first = step == 0

    def shifted_store(out_ref, car_ref, tile):
        rolled = pltpu.roll(tile, N_META, axis=1)
        lane = lax.broadcasted_iota(jnp.int32, (tile.shape[0], LANES), 1)
        out_ref[:, 0:LANES] = jnp.where(lane < N_META, car_ref[...], rolled[:, 0:LANES])
        out_ref[:, LANES:] = rolled[:, LANES:]
        car_ref[...] = rolled[:, 0:LANES]

    if shifted:
        @pl.when(first)
        def _():
            kcar[...] = ktm_ref[:, 0:LANES]
            vcar[...] = vtm_ref[:, 0:LANES]
            lcar[...] = lftm_ref[:, 0:LANES]

        @pl.when(step == tiles_per_seq)
        def _():
            for out_ref, car_ref in ((kt_ref, kcar), (vt_ref, vcar), (lftp_ref, lcar)):
                out_ref[...] = jnp.zeros(out_ref.shape, F32)
                out_ref[:, 0:LANES] = car_ref[...]

    @pl.when(step < tiles_per_seq)
    def _():
        x = x_ref[...]
        ms = jnp.mean(x * x, axis=-1, keepdims=True)
        xn = (x * lax.rsqrt(ms + EPS) * g1_ref[...]).astype(BF16)

        cb = _dot(xn, wa_ref[:, 0:D_CONV])
        z = _dot(xn, wa_ref[:, D_CONV:2 * D_CONV]) * _dot(xn, wa_ref[:, 2 * D_CONV:3 * D_CONV])

        @pl.when(first)
        def _():
            zbuf[:, ZPAD - 2:ZPAD, :] = left_ref[...]

        w0 = cw_ref[0:1, :]
        w1 = cw_ref[1:2, :]
        w2 = cw_ref[2:3, :]
        for s in range(n_seg):
            r0 = s * seg_len
            zs = z[r0:r0 + seg_len]
            zbuf[s, ZPAD:ZPAD + seg_len, :] = zs
            zm1 = zbuf[s, ZPAD - 1:ZPAD - 1 + seg_len, :]
            zm2 = zbuf[s, ZPAD - 2:ZPAD - 2 + seg_len, :]
            y = zm2 * w0 + zm1 * w1 + zs * w2 + cb_ref[...]
            conv_ref[r0:r0 + seg_len, :] = (cb[r0:r0 + seg_len] * y).astype(BF16)
            tail = zbuf[s, ZPAD + seg_len - 2:ZPAD + seg_len, :]
            zlast_ref[s] = tail
            zbuf[s, ZPAD - 2:ZPAD, :] = tail

        def head_norm(u, g_row):
            ssq = _dot((u * u).astype(BF16), bd_ref[...])
            return u * lax.rsqrt(ssq * (1.0 / HEAD_DIM) + EPS) * g_row

        k_rows = head_norm(_dot(xn, wk_ref[...]), kgr_ref[...])

        kt = _head_norm_t(_dot_nt(wt_ref[D_ATTN:2 * D_ATTN, :], xn), kgc_ref[...])
        vt = _dot_nt(wt_ref[2 * D_ATTN:3 * D_ATTN, :], xn)
        lft = _log_sigmoid(_dot_nt(wflt_ref[...], xn) + bfc_ref[...])
        lfp_ref[...] = _log_sigmoid(_dot(xn, wfl_ref[...]) + bfr_ref[...])

        if shifted:
            kb_ref[...] = k_rows.astype(BF16)
            qt = _head_norm_t(_dot_nt(wt_ref[0:D_ATTN, :], xn), qgc_ref[...])
            qt_ref[...] = (qt * ATTN_SCALE).astype(BF16)
            lft_ref[...] = lft
            shifted_store(kt_ref, kcar, kt)
            shifted_store(vt_ref, vcar, vt)
            shifted_store(lftp_ref, lcar, lft)
        else:
            k_ref[...] = k_rows
            q_ref[...] = (head_norm(_dot(xn, wq_ref[...]), qgr_ref[...]) * ATTN_SCALE).astype(BF16)
            v_ref[...] = _dot(xn, wv_ref[...])
            kt_ref[...] = kt
            vt_ref[...] = vt
            lft_ref[...] = lft

        for c in range(2 * D_MODEL // D_CONV):
            gl = _dot(xn, wgl_ref[:, c * D_CONV:(c + 1) * D_CONV])
            gate_ref[:, c * D_CONV:(c + 1) * D_CONV] = jax.nn.sigmoid(gl).astype(BF16)


def _project_small(x2d, left, wts, extra):
    n_rows = x2d.shape[0]
    n_seq = left.shape[0]
    seg_len = n_rows // n_seq
    full = lambda *shape: pl.BlockSpec(shape, lambda i: (0,) * len(shape))
    out_shape = (
        jax.ShapeDtypeStruct((n_rows, D_CONV), BF16),
        jax.ShapeDtypeStruct((n_rows, D_ATTN), BF16),
        jax.ShapeDtypeStruct((n_rows, D_ATTN), F32),
        jax.ShapeDtypeStruct((n_rows, D_ATTN), F32),
        jax.ShapeDtypeStruct((D_ATTN, n_rows), F32),
        jax.ShapeDtypeStruct((D_ATTN, n_rows), F32),
        jax.ShapeDtypeStruct((N_HEADS, n_rows), F32),
        jax.ShapeDtypeStruct((n_rows, LANES), F32),
        jax.ShapeDtypeStruct((n_rows, 2 * D_MODEL), BF16),
        jax.ShapeDtypeStruct((n_seq, CONV_W - 1, D_CONV), F32),
    )
    kern = functools.partial(_proj_kernel, n_seg=n_seq, seg_len=seg_len, tiles_per_seq=1,
                             shifted=False)
    return pl.pallas_call(
        kern, grid=(1,),
        in_specs=[full(*x2d.shape), full(*left.shape)]
        + [_const_spec(w.shape) for w in wts + extra],
        out_specs=tuple(full(*s.shape) for s in out_shape), out_shape=out_shape,
        scratch_shapes=[pltpu.VMEM((n_seq, seg_len + ZPAD, D_CONV), F32)],
        compiler_params=_params(1), name="proj_small",
    )(x2d, left, *wts, *extra)


def _project_prompt(x2d, left, wts, extra, *, b, seq):
    rows = PROJ_ROWS
    tiles = seq // rows
    steps = tiles + 1
    length = N_META + seq

    def tile_of(i):
        return (i // steps) * tiles + jnp.minimum(i % steps, tiles - 1)

    row_spec = lambda width: pl.BlockSpec((rows, width), lambda i: (tile_of(i), 0))
    seq_spec = pl.BlockSpec((1, CONV_W - 1, D_CONV), lambda i: (i // steps, 0, 0))
    tok_spec = lambda feat: pl.BlockSpec(
        (None, feat, rows), lambda i: (i // steps, 0, jnp.minimum(i % steps, tiles - 1)))
    pos_spec = lambda feat: pl.BlockSpec(
        (None, feat, rows), lambda i: (i // steps, 0, i % steps))
    out_shape = (
        jax.ShapeDtypeStruct((b * seq, D_CONV), BF16),
        jax.ShapeDtypeStruct((b * seq, D_ATTN), BF16),
        jax.ShapeDtypeStruct((b, D_ATTN, seq), BF16),
        jax.ShapeDtypeStruct((b, D_ATTN, length), F32),
        jax.ShapeDtypeStruct((b, D_ATTN, length), F32),
        jax.ShapeDtypeStruct((b, N_HEADS, length), F32),
        jax.ShapeDtypeStruct((b, N_HEADS, seq), F32),
        jax.ShapeDtypeStruct((b * seq, LANES), F32),
        jax.ShapeDtypeStruct((b * seq, 2 * D_MODEL), BF16),
        jax.ShapeDtypeStruct((b, CONV_W - 1, D_CONV), F32),
    )
    out_specs = (row_spec(D_CONV), row_spec(D_ATTN), tok_spec(D_ATTN), pos_spec(D_ATTN),
                 pos_spec(D_ATTN), pos_spec(N_HEADS), tok_spec(N_HEADS), row_spec(LANES),
                 row_spec(2 * D_MODEL), seq_spec)
    kern = functools.partial(_proj_kernel, n_seg=1, seg_len=rows, tiles_per_seq=tiles,
                             shifted=True)
    return pl.pallas_call(
        kern, grid=(b * steps,),
        in_specs=[row_spec(D_MODEL), seq_spec] + [_const_spec(w.shape) for w in wts + extra],
        out_specs=out_specs, out_shape=out_shape,
        scratch_shapes=[pltpu.VMEM((1, rows + ZPAD, D_CONV), F32),
                        pltpu.VMEM((D_ATTN, LANES), F32),
                        pltpu.VMEM((D_ATTN, LANES), F32),
                        pltpu.VMEM((N_HEADS, LANES), F32)],
        compiler_params=_params(1), name="proj",
    )(x2d, left, *wts, *extra)


def _prompt_attn_kernel(qt_ref, kb_ref, kbm_ref, vt_ref, lft_ref, lftm_ref, lfp_ref, lfpm_ref,
                        o_ref, kpos, vb, kbias, crow, qcat, m_s, l_s, acc_s, *, seq):
    blk = ATTN_BLOCK
    n_blk = seq // blk
    length = N_META + seq
    n_pos = kpos.shape[0]
    n_bias = N_SPLIT * N_HEADS

    kpos[0:N_META, :] = kbm_ref[...]
    kpos[N_META:length, :] = kb_ref[...]
    kpos[length:, :] = jnp.zeros((n_pos - length, D_ATTN), BF16)
    for j in range(n_blk):
        vb[j] = vt_ref[:, j * blk:(j + 1) * blk].astype(BF16)
    vb[n_blk] = jnp.zeros((D_ATTN, blk), BF16)
    vb[n_blk, :, 0:N_META] = vt_ref[:, seq:length].astype(BF16)

    upper, lower = _triangles(blk)

    def store_kbias(rows, c_col):
        hi, mid, lo = _split3(c_col)
        lane = lax.broadcasted_iota(jnp.int32, c_col.shape, 1)
        grp = lane // N_HEADS
        part = jnp.where(grp == 0, hi, jnp.where(grp == 1, mid, lo))
        kbias[rows, :] = jnp.where(lane < n_bias, -part,
                                   jnp.where(lane < 2 * n_bias, 1.0, 0.0)).astype(BF16)

    off_c = jnp.zeros((1, LANES), F32)
    for j in range(n_blk):
        if j == 0:
            lf_blk = jnp.concatenate([lfpm_ref[...], lfp_ref[0:blk - N_META, :]], axis=0)
        else:
            lf_blk = lfp_ref[j * blk - N_META:(j + 1) * blk - N_META, :]
        c_col = _dot_exact(lower, lf_blk) + off_c
        store_kbias(slice(j * blk, (j + 1) * blk), c_col)
        off_c = c_col[blk - 1:blk, :]
    c_col = _cumsum_few(lfp_ref[seq - N_META:seq, :], axis=0) + off_c
    store_kbias(slice(seq, length), c_col)
    kbias[length:, :] = jnp.zeros((n_pos - length, LANES), BF16)

    meta_r = _cumsum_few(lftm_ref[...], axis=1)
    off_r = meta_r[:, N_META - 1:N_META]
    for j in range(n_blk):
        c_row = _dot_exact(lft_ref[:, j * blk:(j + 1) * blk], upper) + off_r
        crow[j] = c_row
        off_r = c_row[:, blk - 1:blk]

    row128 = lax.broadcasted_iota(jnp.int32, (LANES, blk), 0)
    col128 = lax.broadcasted_iota(jnp.int32, (LANES, blk), 1)
    krow = lax.broadcasted_iota(jnp.int32, (blk, blk), 0)
    qcol = lax.broadcasted_iota(jnp.int32, (blk, blk), 1)

    def q_block(t, _):
        tok0 = pl.multiple_of(t * blk, blk)
        hi, mid, lo = _split3(crow[t])
        bias_rows = jnp.concatenate(
            [jnp.ones((n_bias, blk), F32), hi, mid, lo,
             jnp.zeros((LANES - 2 * n_bias, blk), F32)], axis=0)
        for h in range(N_HEADS):
            pair = qt_ref[(h // 2) * LANES:(h // 2 + 1) * LANES, pl.ds(tok0, blk)]
            in_head = (row128 // HEAD_DIM) == (h % 2)
            qcat[h, 0:LANES, :] = jnp.where(in_head, pair, jnp.zeros_like(pair))
            qcat[h, LANES:, :] = jnp.where(row128 % N_HEADS == h, bias_rows, 0.0).astype(BF16)
        m_s[...] = jnp.full(m_s.shape, -jnp.inf, F32)
        l_s[...] = jnp.zeros(l_s.shape, F32)
        acc_s[...] = jnp.zeros(acc_s.shape, F32)

        def key_block(pos0, n_rows, v_blk, visible):
            for h in range(N_HEADS):
                g = h // 2
                hs = slice(h * HEAD_DIM, (h + 1) * HEAD_DIM)
                kc = jnp.concatenate([kpos[pl.ds(pos0, n_rows), g * LANES:(g + 1) * LANES],
                                      kbias[pl.ds(pos0, n_rows), :]], axis=1)
                s = _dot(kc, qcat[h])
                if visible is not None:
                    s = jnp.where(visible, s, -jnp.inf)
                m_old = m_s[h:h + 1, :]
                m_new = jnp.maximum(m_old, jnp.max(s, axis=0, keepdims=True))
                alpha = jnp.exp(m_old - m_new)
                p = jnp.exp(s - m_new)
                m_s[h:h + 1, :] = m_new
                l_s[h:h + 1, :] = alpha * l_s[h:h + 1, :] + jnp.sum(p, axis=0, keepdims=True)
                acc_s[hs, :] = alpha * acc_s[hs, :] + _dot(v_blk(hs), p.astype(BF16))

        def full_block(j, _):
            key_block(pl.multiple_of(j * blk, blk), blk, lambda hs: vb[j, hs, :], None)
            return 0

        lax.fori_loop(0, t, full_block, 0)
        key_block(tok0, blk, lambda hs: vb[t, hs, :], krow <= qcol + N_META)
        key_block(pl.multiple_of(tok0 + blk, blk), LANES, lambda hs: vb[t + 1, hs, 0:LANES],
                  row128 <= col128 + (N_META - blk))

        inv = 1.0 / l_s[...]
        o_t = jnp.concatenate(
            [acc_s[h * HEAD_DIM:(h + 1) * HEAD_DIM, :] * inv[h:h + 1, :]
             for h in range(N_HEADS)], axis=0)
        o_ref[pl.ds(tok0, blk), :] = o_t.T.astype(BF16)
        return 0

    lax.fori_loop(0, n_blk, q_block, 0)


def _prompt_attention(qt, kb, kb_meta, vt, lft, lft_meta, lfp, lfp_meta):
    b, _, seq = qt.shape
    length = vt.shape[2]
    blk = ATTN_BLOCK
    n_blk = seq // blk
    n_pos = (n_blk + 1) * blk + LANES
    per_b = lambda *shape: pl.BlockSpec((None,) + shape, lambda i: (i,) + (0,) * len(shape))
    kern = functools.partial(_prompt_attn_kernel, seq=seq)
    return pl.pallas_call(
        kern, grid=(b,),
        in_specs=[per_b(D_ATTN, seq), per_b(seq, D_ATTN), _const_spec(kb_meta.shape),
                  per_b(D_ATTN, length), per_b(N_HEADS, seq), _const_spec(lft_meta.shape),
                  per_b(seq, LANES), _const_spec(lfp_meta.shape)],
        out_specs=per_b(seq, D_ATTN),
        out_shape=jax.ShapeDtypeStruct((b, seq, D_ATTN), BF16),
        scratch_shapes=[pltpu.VMEM((n_pos, D_ATTN), BF16),
                        pltpu.VMEM((n_blk + 1, D_ATTN, blk), BF16),
                        pltpu.VMEM((n_pos, LANES), BF16),
                        pltpu.VMEM((n_blk, N_HEADS, blk), F32),
                        pltpu.VMEM((N_HEADS, 2 * LANES, blk), BF16),
                        pltpu.VMEM((N_HEADS, blk), F32),
                        pltpu.VMEM((N_HEADS, blk), F32),
                        pltpu.VMEM((D_ATTN, blk), F32)],
        compiler_params=_params(1), name="prompt_attn",
    )(qt, kb, kb_meta, vt, lft, lft_meta, lfp, lfp_meta)


def _sample_attn_kernel(q_ref, kn_ref, vn_ref, lf_ref, lft_ref, ckt_ref, cvt_ref, clft_ref,
                        o_ref, crow, *, past, dec):
    blk = ATTN_BLOCK
    n_keys = past + LANES

    upper, _ = _triangles(blk)
    off = jnp.zeros((N_HEADS, 1), F32)
    for j in range(past // blk):
        loc = _dot_exact(clft_ref[:, j * blk:(j + 1) * blk], upper) + off
        crow[:, j * blk:(j + 1) * blk] = loc
        off = loc[:, blk - 1:blk]
    crow[:, 0:past] = crow[:, 0:past] - off

    cq_c = _cumsum_few(lf_ref[:, 0:N_HEADS], axis=0)
    cq_r = _cumsum_few(lft_ref[...], axis=1)
    crow[:, past:] = jnp.full((N_HEADS, LANES), MASKED_BIAS, F32)
    crow[:, past:past + dec] = cq_r

    q = q_ref[...]
    lane_head = lax.broadcasted_iota(jnp.int32, (dec, D_ATTN), 1) // HEAD_DIM
    q_exp = jnp.concatenate(
        [jnp.where(lane_head == h, q, jnp.zeros_like(q)) for h in range(N_HEADS)], axis=0)
    pad_rows = jnp.zeros((LANES - dec, D_ATTN), BF16)
    k_new = jnp.concatenate([kn_ref[...].astype(BF16), pad_rows], axis=0)
    v_new = jnp.concatenate([vn_ref[...].astype(BF16), pad_rows], axis=0)
    s_all = jnp.concatenate(
        [_dot(q_exp, ckt_ref[...].astype(BF16)), _dot_nt(q_exp, k_new)], axis=1)

    kpos = lax.broadcasted_iota(jnp.int32, (dec, n_keys), 1)
    qpos = past + lax.broadcasted_iota(jnp.int32, (dec, n_keys), 0)
    visible = kpos <= qpos
    probs = []
    norms = []
    for h in range(N_HEADS):
        s = s_all[h * dec:(h + 1) * dec, :] + cq_c[:, h:h + 1] - crow[h:h + 1, :]
        s = jnp.where(visible, s, -jnp.inf)
        p = jnp.exp(s - jnp.max(s, axis=-1, keepdims=True))
        norms.append(jnp.sum(p, axis=-1, keepdims=True))
        probs.append(p.astype(BF16))
    p_all = jnp.concatenate(probs, axis=0)
    o_all = (_dot_nt(p_all[:, 0:past], cvt_ref[...].astype(BF16))
             + _dot(p_all[:, past:], v_new))
    out = jnp.zeros((dec, D_ATTN), F32)
    for h in range(N_HEADS):
        o = o_all[h * dec:(h + 1) * dec, :] / norms[h]
        out = out + jnp.where(lane_head == h, o, 0.0)
    o_ref[...] = out.astype(BF16)


def _sample_attention(q, k_new, v_new, lfp, lft, cache_kt, cache_vt, cache_lft):
    b, dec, _ = q.shape
    past = cache_kt.shape[2]
    per_b = lambda *shape: pl.BlockSpec((None,) + shape, lambda i: (i,) + (0,) * len(shape))
    kern = functools.partial(_sample_attn_kernel, past=past, dec=dec)
    return pl.pallas_call(
        kern, grid=(b,),
        in_specs=[per_b(dec, D_ATTN), per_b(dec, D_ATTN), per_b(dec, D_ATTN),
                  per_b(dec, LANES), per_b(N_HEADS, dec),
                  per_b(D_ATTN, past), per_b(D_ATTN, past), per_b(N_HEADS, past)],
        out_specs=per_b(dec, D_ATTN),
        out_shape=jax.ShapeDtypeStruct((b, dec, D_ATTN), BF16),
        scratch_shapes=[pltpu.VMEM((N_HEADS, past + LANES), F32)],
        compiler_params=_params(1), name="sample_attn",
    )(q, k_new, v_new, lfp, lft, cache_kt, cache_vt, cache_lft)


def _mlp_kernel(x_ref, conv_ref, attn_ref, gate_ref, wbc_ref, wba_ref, wo_ref, g2_ref,
                wup_ref, wdn_ref, y_ref):
    ya = _dot(conv_ref[...], wbc_ref[...])
    yb = _dot(attn_ref[...], wba_ref[...])
    merged = (gate_ref[:, 0:D_MODEL].astype(F32) * ya
              + gate_ref[:, D_MODEL:2 * D_MODEL].astype(F32) * yb)
    h = x_ref[...] + _dot(merged.astype(BF16), wo_ref[...])
    ms = jnp.mean(h * h, axis=-1, keepdims=True)
    hn = (h * lax.rsqrt(ms + EPS) * g2_ref[...]).astype(BF16)
    acc = h
    for c in range(D_FF // D_MODEL):
        cols = slice(c * D_MODEL, (c + 1) * D_MODEL)
        a = jnp.maximum(_dot(hn, wup_ref[:, cols]), 0.0)
        acc = acc + _dot((a * a).astype(BF16), wdn_ref[cols, :])
    y_ref[...] = acc


def _merge_mlp(x2d, conv, attn, gates, wts, *, rows):
    n_rows = x2d.shape[0]
    row_spec = lambda width: pl.BlockSpec((rows, width), lambda i: (i, 0))
    return pl.pallas_call(
        _mlp_kernel, grid=(n_rows // rows,),
        in_specs=[row_spec(D_MODEL), row_spec(D_CONV), row_spec(D_ATTN), row_spec(2 * D_MODEL)]
        + [_const_spec(w.shape) for w in wts],
        out_specs=row_spec(D_MODEL),
        out_shape=jax.ShapeDtypeStruct((n_rows, D_MODEL), F32),
        compiler_params=_params(1), name="merge_mlp",
    )(x2d, conv, attn, gates, *wts)


def kernel(x_prompt, x_sample, cache_k, cache_v, cache_logf, state_conv, meta,
           norm1_g, w_in, b_f, conv_w, conv_b, q_norm_g, k_norm_g,
           w_br_conv, w_br_attn, w_out, norm2_g, w_up, w_down):
    b, seq, _ = x_prompt.shape
    db, dec, _ = x_sample.shape
    past = cache_k.shape[2]
    length = N_META + seq
    n_main = 3 * D_CONV + 3 * D_ATTN
    q0, k0, v0 = 3 * D_CONV, 3 * D_CONV + D_ATTN, 3 * D_CONV + 2 * D_ATTN

    w = w_in[0].astype(BF16)
    w_fl = w[:, n_main:n_main + N_HEADS]
    head_of = jnp.arange(D_ATTN) // HEAD_DIM
    qg = jnp.tile(q_norm_g[0], N_HEADS)
    kg = jnp.tile(k_norm_g[0], N_HEADS)
    proj_wts = (
        norm1_g[0][None, :],
        w[:, 0:3 * D_CONV], w[:, k0:k0 + D_ATTN], w[:, n_main + N_HEADS:],
        w[:, q0:n_main].T,
        jnp.tile(w_fl, (1, LANES // N_HEADS)), w_fl.T,
        jnp.tile(b_f[0], LANES // N_HEADS)[None, :], b_f[0][:, None],
        conv_w[0], conv_b[0][None, :],
        qg[:, None], kg[:, None], kg[None, :],
        (head_of[:, None] == head_of[None, :]).astype(BF16),
    )
    mlp_wts = (w_br_conv[0].astype(BF16), w_br_attn[0].astype(BF16), w_out[0].astype(BF16),
               norm2_g[0][None, :], w_up[0].astype(BF16), w_down[0].astype(BF16))

    x_small = jnp.concatenate([meta, x_sample.reshape(db * dec, D_MODEL)], axis=0)
    left_small = jnp.concatenate(
        [jnp.zeros((1, CONV_W - 1, D_CONV), F32), state_conv[0]], axis=0)
    (conv_s, q_s, k_s, v_s, kt_s, vt_s, lft_s, lfp_s, gate_s, zlast_s) = _project_small(
        x_small, left_small, proj_wts,
        (w[:, q0:q0 + D_ATTN], w[:, v0:v0 + D_ATTN], qg[None, :]))

    left_p = jnp.broadcast_to(zlast_s[0:1], (b, CONV_W - 1, D_CONV))
    (conv_p, kb_p, qt_p, kt_p, vt_p, lftp_p, lft_p, lfp_p, gate_p, zlast_p) = _project_prompt(
        x_prompt.reshape(b * seq, D_MODEL), left_p, proj_wts, (kt_s, vt_s, lft_s),
        b=b, seq=seq)

    attn_p = _prompt_attention(
        qt_p, kb_p.reshape(b, seq, D_ATTN), k_s[:N_META].astype(BF16), vt_p,
        lft_p, lft_s[:, :N_META], lfp_p.reshape(b, seq, LANES), lfp_s[:N_META])
    y_prompt = _merge_mlp(x_prompt.reshape(b * seq, D_MODEL), conv_p,
                          attn_p.reshape(b * seq, D_ATTN), gate_p, mlp_wts, rows=MLP_ROWS)

    k_new = k_s[N_META:].reshape(db, dec, D_ATTN)
    v_new = v_s[N_META:].reshape(db, dec, D_ATTN)
    lf_new = lfp_s[N_META:, :N_HEADS].reshape(db, dec, N_HEADS)
    cache_kt = jnp.transpose(cache_k[0], (0, 2, 3, 1)).reshape(db, D_ATTN, past)
    cache_vt = jnp.transpose(cache_v[0], (0, 2, 3, 1)).reshape(db, D_ATTN, past)
    attn_s = _sample_attention(
        q_s[N_META:].reshape(db, dec, D_ATTN), k_new, v_new,
        lfp_s[N_META:].reshape(db, dec, LANES), jnp.swapaxes(lf_new, 1, 2),
        cache_kt, cache_vt, jnp.swapaxes(cache_logf[0], 1, 2))
    y_sample = _merge_mlp(x_sample.reshape(db * dec, D_MODEL), conv_s[N_META:],
                          attn_s.reshape(db * dec, D_ATTN), gate_s[N_META:], mlp_wts,
                          rows=db * dec)

    def heads_last(t):
        return jnp.transpose(t.reshape(b, N_HEADS, HEAD_DIM, length), (0, 3, 1, 2))[None]

    return (y_prompt.reshape(b, seq, D_MODEL),
            y_sample.reshape(db, dec, D_MODEL),
            heads_last(kt_p),
            heads_last(vt_p),
            jnp.swapaxes(lftp_p, 1, 2)[None],
            zlast_p[None],
            k_new.reshape(1, db, dec, N_HEADS, HEAD_DIM),
            v_new.reshape(1, db, dec, N_HEADS, HEAD_DIM),
            lf_new[None],
            zlast_s[1:][None])
```

```python
import functools

import jax
import jax.numpy as jnp
from jax import lax
from jax.experimental import pallas as pl
from jax.experimental.pallas import tpu as pltpu

D_MODEL = 1024
D_CONV = D_MODEL // 2
CONV_W = 3
N_HEADS = 8
HEAD_DIM = 64
D_ATTN = N_HEADS * HEAD_DIM
D_FF = 4 * D_MODEL
N_META = 16
EPS = 1e-6
ATTN_SCALE = HEAD_DIM ** -0.5

F32 = jnp.float32
BF16 = jnp.bfloat16

VMEM_LIMIT_BYTES = 56 * 1024 * 1024
LANES = 128
PROJ_ROWS = 512
MLP_ROWS = 512
ATTN_BLOCK = 256
MASKED_BIAS = 1e30
ZPAD = 8
N_SPLIT = 3
V_SLAB = HEAD_DIM + 16
LOG2E = 1.4426950408889634


def _dot(a, b):
    return jnp.dot(a, b, preferred_element_type=F32)


def _dot_nt(a, b):
    return lax.dot_general(a, b, (((1,), (1,)), ((), ())), preferred_element_type=F32)


def _dot_exact(a, b):
    return jnp.dot(a, b, preferred_element_type=F32, precision=lax.Precision.HIGHEST)


def _log_sigmoid(x):
    return jnp.minimum(x, 0.0) - jnp.log1p(jnp.exp(-jnp.abs(x)))


def _cumsum_few(x, axis):
    n = x.shape[axis]
    idx = lax.broadcasted_iota(jnp.int32, x.shape, axis)
    out = jnp.zeros(x.shape, F32)
    for i in range(n):
        term = x[i:i + 1, :] if axis == 0 else x[:, i:i + 1]
        out = out + jnp.where(idx >= i, term, 0.0)
    return out


def _triangles(n):
    r = lax.broadcasted_iota(jnp.int32, (n, n), 0)
    c = lax.broadcasted_iota(jnp.int32, (n, n), 1)
    upper = (r <= c).astype(F32)
    lower = (r >= c).astype(F32)
    return upper, lower


def _split3(c):
    hi = c.astype(BF16).astype(F32)
    r1 = c - hi
    mid = r1.astype(BF16).astype(F32)
    return hi, mid, r1 - mid


def _const_spec(shape):
    nd = len(shape)
    return pl.BlockSpec(shape, lambda *_: (0,) * nd, pipeline_mode=pl.Buffered(1))


def _params(n_axes):
    return pltpu.CompilerParams(
        dimension_semantics=("arbitrary",) * n_axes,
        vmem_limit_bytes=VMEM_LIMIT_BYTES)


def _head_norm_t(ut, g_col):
    out = []
    for h in range(N_HEADS):
        blk = ut[h * HEAD_DIM:(h + 1) * HEAD_DIM, :]
        ms = jnp.mean(blk * blk, axis=0, keepdims=True)
        out.append(blk * lax.rsqrt(ms + EPS) * g_col[h * HEAD_DIM:(h + 1) * HEAD_DIM, :])
    return jnp.concatenate(out, axis=0)


def _proj_kernel(*refs, n_seg, seg_len, tiles_per_seq, shifted):
    (x_ref, left_ref, g1_ref, wa_ref, wk_ref, wgl_ref, wt_ref, wfl_ref, wflt_ref,
     bfr_ref, bfc_ref, cw_ref, cb_ref, qgc_ref, kgc_ref, kgr_ref, bd_ref) = refs[:17]
    if shifted:
        ktm_ref, vtm_ref, lftm_ref = refs[17:20]
        (conv_ref, kb_ref, qt_ref, kt_ref, vt_ref, lftp_ref, lft_ref, lfp_ref, gate_ref,
         zlast_ref, zbuf, kcar, vcar, lcar) = refs[20:]
    else:
        wq_ref, wv_ref, qgr_ref = refs[17:20]
        (conv_ref, q_ref, k_ref, v_ref, kt_ref, vt_ref, lft_ref, lfp_ref, gate_ref,
         zlast_ref, zbuf) = refs[20:]

    steps_per_seq = tiles_per_seq + (1 if shifted else 0)
    step = pl.program_id(0) % steps_per_seq
    first = step == 0

    def shifted_store(out_ref, car_ref, tile):
        rolled = pltpu.roll(tile, N_META, axis=1)
        lane = lax.broadcasted_iota(jnp.int32, (tile.shape[0], LANES), 1)
        out_ref[:, 0:LANES] = jnp.where(lane < N_META, car_ref[...], rolled[:, 0:LANES])
        out_ref[:, LANES:] = rolled[:, LANES:]
        car_ref[...] = rolled[:, 0:LANES]

    if shifted:
        @pl.when(first)
        def _():
            kcar[...] = ktm_ref[:, 0:LANES]
            vcar[...] = vtm_ref[:, 0:LANES]
            lcar[...] = lftm_ref[:, 0:LANES]

        @pl.when(step == tiles_per_seq)
        def _():
            for out_ref, car_ref in ((kt_ref, kcar), (vt_ref, vcar), (lftp_ref, lcar)):
                out_ref[...] = jnp.zeros(out_ref.shape, F32)
                out_ref[:, 0:LANES] = car_ref[...]

    @pl.when(step < tiles_per_seq)
    def _():
        x = x_ref[...]
        ms = jnp.mean(x * x, axis=-1, keepdims=True)
        xn = (x * lax.rsqrt(ms + EPS) * g1_ref[...]).astype(BF16)

        cb = _dot(xn, wa_ref[:, 0:D_CONV])
        z = _dot(xn, wa_ref[:, D_CONV:2 * D_CONV]) * _dot(xn, wa_ref[:, 2 * D_CONV:3 * D_CONV])

        @pl.when(first)
        def _():
            zbuf[:, ZPAD - 2:ZPAD, :] = left_ref[...]

        w0 = cw_ref[0:1, :]
        w1 = cw_ref[1:2, :]
        w2 = cw_ref[2:3, :]
        for s in range(n_seg):
            r0 = s * seg_len
            zs = z[r0:r0 + seg_len]
            zbuf[s, ZPAD:ZPAD + seg_len, :] = zs
            zm1 = zbuf[s, ZPAD - 1:ZPAD - 1 + seg_len, :]
            zm2 = zbuf[s, ZPAD - 2:ZPAD - 2 + seg_len, :]
            y = zm2 * w0 + zm1 * w1 + zs * w2 + cb_ref[...]
            conv_ref[r0:r0 + seg_len, :] = (cb[r0:r0 + seg_len] * y).astype(BF16)
            tail = zbuf[s, ZPAD + seg_len - 2:ZPAD + seg_len, :]
            zlast_ref[s] = tail
            zbuf[s, ZPAD - 2:ZPAD, :] = tail

        def head_norm(u, g_row):
            ssq = _dot((u * u).astype(BF16), bd_ref[...])
            return u * lax.rsqrt(ssq * (1.0 / HEAD_DIM) + EPS) * g_row

        k_rows = head_norm(_dot(xn, wk_ref[...]), kgr_ref[...])

        kt = _head_norm_t(_dot_nt(wt_ref[D_ATTN:2 * D_ATTN, :], xn), kgc_ref[...])
        vt = _dot_nt(wt_ref[2 * D_ATTN:3 * D_ATTN, :], xn)
        lft = _log_sigmoid(_dot_nt(wflt_ref[...], xn) + bfc_ref[...])
        lfp_ref[...] = _log_sigmoid(_dot(xn, wfl_ref[...]) + bfr_ref[...])

        if shifted:
            kb_ref[...] = k_rows.astype(BF16)
            qt = _head_norm_t(_dot_nt(wt_ref[0:D_ATTN, :], xn), qgc_ref[...])
            qt_ref[...] = (qt * (ATTN_SCALE * LOG2E)).astype(BF16)
            lft_ref[...] = lft
            shifted_store(kt_ref, kcar, kt)
            shifted_store(vt_ref, vcar, vt)
            shifted_store(lftp_ref, lcar, lft)
        else:
            k_ref[...] = k_rows
            q_ref[...] = (head_norm(_dot(xn, wq_ref[...]), qgr_ref[...]) * ATTN_SCALE).astype(BF16)
            v_ref[...] = _dot(xn, wv_ref[...])
            kt_ref[...] = kt
            vt_ref[...] = vt
            lft_ref[...] = lft

        for c in range(2 * D_MODEL // D_CONV):
            gl = _dot(xn, wgl_ref[:, c * D_CONV:(c + 1) * D_CONV])
            gate_ref[:, c * D_CONV:(c + 1) * D_CONV] = jax.nn.sigmoid(gl).astype(BF16)


def _project_small(x2d, left, wts, extra):
    n_rows = x2d.shape[0]
    n_seq = left.shape[0]
    seg_len = n_rows // n_seq
    full = lambda *shape: pl.BlockSpec(shape, lambda i: (0,) * len(shape))
    out_shape = (
        jax.ShapeDtypeStruct((n_rows, D_CONV), BF16),
        jax.ShapeDtypeStruct((n_rows, D_ATTN), BF16),
        jax.ShapeDtypeStruct((n_rows, D_ATTN), F32),
        jax.ShapeDtypeStruct((n_rows, D_ATTN), F32),
        jax.ShapeDtypeStruct((D_ATTN, n_rows), F32),
        jax.ShapeDtypeStruct((D_ATTN, n_rows), F32),
        jax.ShapeDtypeStruct((N_HEADS, n_rows), F32),
        jax.ShapeDtypeStruct((n_rows, LANES), F32),
        jax.ShapeDtypeStruct((n_rows, 2 * D_MODEL), BF16),
        jax.ShapeDtypeStruct((n_seq, CONV_W - 1, D_CONV), F32),
    )
    kern = functools.partial(_proj_kernel, n_seg=n_seq, seg_len=seg_len, tiles_per_seq=1,
                             shifted=False)
    return pl.pallas_call(
        kern, grid=(1,),
        in_specs=[full(*x2d.shape), full(*left.shape)]
        + [_const_spec(w.shape) for w in wts + extra],
        out_specs=tuple(full(*s.shape) for s in out_shape), out_shape=out_shape,
        scratch_shapes=[pltpu.VMEM((n_seq, seg_len + ZPAD, D_CONV), F32)],
        compiler_params=_params(1), name="proj_small",
    )(x2d, left, *wts, *extra)


def _project_prompt(x2d, left, wts, extra, *, b, seq):
    rows = PROJ_ROWS
    tiles = seq // rows
    steps = tiles + 1
    length = N_META + seq

    def tile_of(i):
        return (i // steps) * tiles + jnp.minimum(i % steps, tiles - 1)

    row_spec = lambda width: pl.BlockSpec((rows, width), lambda i: (tile_of(i), 0))
    seq_spec = pl.BlockSpec((1, CONV_W - 1, D_CONV), lambda i: (i // steps, 0, 0))
    tok_spec = lambda feat: pl.BlockSpec(
        (None, feat, rows), lambda i: (i // steps, 0, jnp.minimum(i % steps, tiles - 1)))
    pos_spec = lambda feat: pl.BlockSpec(
        (None, feat, rows), lambda i: (i // steps, 0, i % steps))
    out_shape = (
        jax.ShapeDtypeStruct((b * seq, D_CONV), BF16),
        jax.ShapeDtypeStruct((b * seq, D_ATTN), BF16),
        jax.ShapeDtypeStruct((b, D_ATTN, seq), BF16),
        jax.ShapeDtypeStruct((b, D_ATTN, length), F32),
        jax.ShapeDtypeStruct((b, D_ATTN, length), F32),
        jax.ShapeDtypeStruct((b, N_HEADS, length), F32),
        jax.ShapeDtypeStruct((b, N_HEADS, seq), F32),
        jax.ShapeDtypeStruct((b * seq, LANES), F32),
        jax.ShapeDtypeStruct((b * seq, 2 * D_MODEL), BF16),
        jax.ShapeDtypeStruct((b, CONV_W - 1, D_CONV), F32),
    )
    out_specs = (row_spec(D_CONV), row_spec(D_ATTN), tok_spec(D_ATTN), pos_spec(D_ATTN),
                 pos_spec(D_ATTN), pos_spec(N_HEADS), tok_spec(N_HEADS), row_spec(LANES),
                 row_spec(2 * D_MODEL), seq_spec)
    kern = functools.partial(_proj_kernel, n_seg=1, seg_len=rows, tiles_per_seq=tiles,
                             shifted=True)
    return pl.pallas_call(
        kern, grid=(b * steps,),
        in_specs=[row_spec(D_MODEL), seq_spec] + [_const_spec(w.shape) for w in wts + extra],
        out_specs=out_specs, out_shape=out_shape,
        scratch_shapes=[pltpu.VMEM((1, rows + ZPAD, D_CONV), F32),
                        pltpu.VMEM((D_ATTN, LANES), F32),
                        pltpu.VMEM((D_ATTN, LANES), F32),
                        pltpu.VMEM((N_HEADS, LANES), F32)],
        compiler_params=_params(1), name="proj",
    )(x2d, left, *wts, *extra)


def _prompt_attn_kernel(qt_ref, kb_ref, kbm_ref, vt_ref, lft_ref, lftm_ref, lfp_ref, lfpm_ref,
                        o_ref, kpos, vb, kbias, crow, qcat, m_s, acc_s, sbuf, *, seq):
    blk = ATTN_BLOCK
    n_blk = seq // blk
    length = N_META + seq
    n_pos = kpos.shape[0]
    n_bias = N_SPLIT * N_HEADS

    kpos[0:N_META, :] = kbm_ref[...]
    kpos[N_META:length, :] = kb_ref[...]
    kpos[length:, :] = jnp.zeros((n_pos - length, D_ATTN), BF16)
    ones_row = (lax.broadcasted_iota(jnp.int32, (V_SLAB - HEAD_DIM, blk), 0) == 0).astype(BF16)
    vb[n_blk] = jnp.zeros((N_HEADS * V_SLAB, blk), BF16)
    for j in range(n_blk + 1):
        for h in range(N_HEADS):
            rows = slice(h * HEAD_DIM, (h + 1) * HEAD_DIM)
            slab = slice(h * V_SLAB, h * V_SLAB + HEAD_DIM)
            if j < n_blk:
                vb[j, slab, :] = vt_ref[rows, j * blk:(j + 1) * blk].astype(BF16)
            else:
                vb[j, slab, 0:N_META] = vt_ref[rows, seq:length].astype(BF16)
            vb[j, h * V_SLAB + HEAD_DIM:(h + 1) * V_SLAB, :] = ones_row

    upper, lower = _triangles(blk)

    def store_kbias(rows, c_col):
        hi, mid, lo = _split3(c_col * LOG2E)
        lane = lax.broadcasted_iota(jnp.int32, c_col.shape, 1)
        grp = lane // N_HEADS
        part = jnp.where(grp == 0, hi, jnp.where(grp == 1, mid, lo))
        kbias[rows, :] = jnp.where(lane < n_bias, -part,
                                   jnp.where(lane < 2 * n_bias, 1.0, 0.0)).astype(BF16)

    off_c = jnp.zeros((1, LANES), F32)
    for j in range(n_blk):
        if j == 0:
            lf_blk = jnp.concatenate([lfpm_ref[...], lfp_ref[0:blk - N_META, :]], axis=0)
        else:
            lf_blk = lfp_ref[j * blk - N_META:(j + 1) * blk - N_META, :]
        c_col = _dot_exact(lower, lf_blk) + off_c
        store_kbias(slice(j * blk, (j + 1) * blk), c_col)
        off_c = c_col[blk - 1:blk, :]
    c_col = _cumsum_few(lfp_ref[seq - N_META:seq, :], axis=0) + off_c
    store_kbias(slice(seq, length), c_col)
    kbias[length:, :] = jnp.zeros((n_pos - length, LANES), BF16)

    meta_r = _cumsum_few(lftm_ref[...], axis=1)
    off_r = meta_r[:, N_META - 1:N_META]
    for j in range(n_blk):
        c_row = _dot_exact(lft_ref[:, j * blk:(j + 1) * blk], upper) + off_r
        crow[j] = c_row
        off_r = c_row[:, blk - 1:blk]

    row128 = lax.broadcasted_iota(jnp.int32, (LANES, blk), 0)
    col128 = lax.broadcasted_iota(jnp.int32, (LANES, blk), 1)
    krow = lax.broadcasted_iota(jnp.int32, (blk, blk), 0)
    qcol = lax.broadcasted_iota(jnp.int32, (blk, blk), 1)

    def q_block(t, _):
        tok0 = pl.multiple_of(t * blk, blk)
        hi, mid, lo = _split3(crow[t] * LOG2E)
        bias_rows = jnp.concatenate(
            [jnp.ones((n_bias, blk), F32), hi, mid, lo,
             jnp.zeros((LANES - 2 * n_bias, blk), F32)], axis=0)
        for h in range(N_HEADS):
            pair = qt_ref[(h // 2) * LANES:(h // 2 + 1) * LANES, pl.ds(tok0, blk)]
            in_head = (row128 // HEAD_DIM) == (h % 2)
            qcat[h, 0:LANES, :] = jnp.where(in_head, pair, jnp.zeros_like(pair))
            qcat[h, LANES:, :] = jnp.where(row128 % N_HEADS == h, bias_rows, 0.0).astype(BF16)
        m_s[...] = jnp.full(m_s.shape, -jnp.inf, F32)
        acc_s[...] = jnp.zeros(acc_s.shape, F32)

        def key_block(pos0, n_rows, v_blk, visible):
            m_blk = []
            for h in range(N_HEADS):
                g = h // 2
                kc = jnp.concatenate([kpos[pl.ds(pos0, n_rows), g * LANES:(g + 1) * LANES],
                                      kbias[pl.ds(pos0, n_rows), :]], axis=1)
                s = _dot(kc, qcat[h])
                if visible is not None:
                    s = jnp.where(visible, s, -jnp.inf)
                sbuf[h, 0:n_rows, :] = s
                m_blk.append(jnp.max(s, axis=0, keepdims=True))
            for h in range(N_HEADS):
                slab = slice(h * V_SLAB, (h + 1) * V_SLAB)
                m_old = m_s[h:h + 1, :]
                m_new = jnp.maximum(m_old, m_blk[h])
                alpha = jnp.exp2(m_old - m_new)
                p = jnp.exp2(sbuf[h, 0:n_rows, :] - m_new)
                m_s[h:h + 1, :] = m_new
                acc_s[slab, :] = alpha * acc_s[slab, :] + _dot(v_blk(slab), p.astype(BF16))

        def full_block(j, _):
            key_block(pl.multiple_of(j * blk, blk), blk, lambda slab: vb[j, slab, :], None)
            return 0

        lax.fori_loop(0, t, full_block, 0)
        key_block(tok0, blk, lambda slab: vb[t, slab, :], krow <= qcol + N_META)
        key_block(pl.multiple_of(tok0 + blk, blk), LANES,
                  lambda slab: vb[t + 1, slab, 0:LANES], row128 <= col128 + (N_META - blk))

        o_t = []
        for h in range(N_HEADS):
            norm = acc_s[h * V_SLAB + HEAD_DIM:h * V_SLAB + HEAD_DIM + 1, :]
            o_t.append(acc_s[h * V_SLAB:h * V_SLAB + HEAD_DIM, :] * (1.0 / norm))
        o_ref[pl.ds(tok0, blk), :] = jnp.concatenate(o_t, axis=0).T.astype(BF16)
        return 0

    lax.fori_loop(0, n_blk, q_block, 0)


def _prompt_attention(qt, kb, kb_meta, vt, lft, lft_meta, lfp, lfp_meta):
    b, _, seq = qt.shape
    length = vt.shape[2]
    blk = ATTN_BLOCK
    n_blk = seq // blk
    n_pos = (n_blk + 1) * blk + LANES
    per_b = lambda *shape: pl.BlockSpec((None,) + shape, lambda i: (i,) + (0,) * len(shape))
    kern = functools.partial(_prompt_attn_kernel, seq=seq)
    return pl.pallas_call(
        kern, grid=(b,),
        in_specs=[per_b(D_ATTN, seq), per_b(seq, D_ATTN), _const_spec(kb_meta.shape),
                  per_b(D_ATTN, length), per_b(N_HEADS, seq), _const_spec(lft_meta.shape),
                  per_b(seq, LANES), _const_spec(lfp_meta.shape)],
        out_specs=per_b(seq, D_ATTN),
        out_shape=jax.ShapeDtypeStruct((b, seq, D_ATTN), BF16),
        scratch_shapes=[pltpu.VMEM((n_pos, D_ATTN), BF16),
                        pltpu.VMEM((n_blk + 1, N_HEADS * V_SLAB, blk), BF16),
                        pltpu.VMEM((n_pos, LANES), BF16),
                        pltpu.VMEM((n_blk, N_HEADS, blk), F32),
                        pltpu.VMEM((N_HEADS, 2 * LANES, blk), BF16),
                        pltpu.VMEM((N_HEADS, blk), F32),
                        pltpu.VMEM((N_HEADS * V_SLAB, blk), F32),
                        pltpu.VMEM((N_HEADS, blk, blk), F32)],
        compiler_params=_params(1), name="prompt_attn",
    )(qt, kb, kb_meta, vt, lft, lft_meta, lfp, lfp_meta)


def _sample_attn_kernel(q_ref, kn_ref, vn_ref, lf_ref, lft_ref, ckt_ref, cvt_ref, clft_ref,
                        o_ref, crow, *, past, dec):
    blk = ATTN_BLOCK
    n_keys = past + LANES

    upper, _ = _triangles(blk)
    off = jnp.zeros((N_HEADS, 1), F32)
    for j in range(past // blk):
        loc = _dot_exact(clft_ref[:, j * blk:(j + 1) * blk], upper) + off
        crow[:, j * blk:(j + 1) * blk] = loc
        off = loc[:, blk - 1:blk]
    crow[:, 0:past] = crow[:, 0:past] - off

    cq_c = _cumsum_few(lf_ref[:, 0:N_HEADS], axis=0)
    cq_r = _cumsum_few(lft_ref[...], axis=1)
    crow[:, past:] = jnp.full((N_HEADS, LANES), MASKED_BIAS, F32)
    crow[:, past:past + dec] = cq_r

    q = q_ref[...]
    lane_head = lax.broadcasted_iota(jnp.int32, (dec, D_ATTN), 1) // HEAD_DIM
    q_exp = jnp.concatenate(
        [jnp.where(lane_head == h, q, jnp.zeros_like(q)) for h in range(N_HEADS)], axis=0)
    pad_rows = jnp.zeros((LANES - dec, D_ATTN), BF16)
    k_new = jnp.concatenate([kn_ref[...].astype(BF16), pad_rows], axis=0)
    v_new = jnp.concatenate([vn_ref[...].astype(BF16), pad_rows], axis=0)
    s_all = jnp.concatenate(
        [_dot(q_exp, ckt_ref[...].astype(BF16)), _dot_nt(q_exp, k_new)], axis=1)

    kpos = lax.broadcasted_iota(jnp.int32, (dec, n_keys), 1)
    qpos = past + lax.broadcasted_iota(jnp.int32, (dec, n_keys), 0)
    visible = kpos <= qpos
    probs = []
    norms = []
    for h in range(N_HEADS):
        s = s_all[h * dec:(h + 1) * dec, :] + cq_c[:, h:h + 1] - crow[h:h + 1, :]
        s = jnp.where(visible, s, -jnp.inf)
        p = jnp.exp(s - jnp.max(s, axis=-1, keepdims=True))
        norms.append(jnp.sum(p, axis=-1, keepdims=True))
        probs.append(p.astype(BF16))
    p_all = jnp.concatenate(probs, axis=0)
    o_all = (_dot_nt(p_all[:, 0:past], cvt_ref[...].astype(BF16))
             + _dot(p_all[:, past:], v_new))
    out = jnp.zeros((dec, D_ATTN), F32)
    for h in range(N_HEADS):
        o = o_all[h * dec:(h + 1) * dec, :] / norms[h]
        out = out + jnp.where(lane_head == h, o, 0.0)
    o_ref[...] = out.astype(BF16)


def _sample_attention(q, k_new, v_new, lfp, lft, cache_kt, cache_vt, cache_lft):
    b, dec, _ = q.shape
    past = cache_kt.shape[2]
    per_b = lambda *shape: pl.BlockSpec((None,) + shape, lambda i: (i,) + (0,) * len(shape))
    kern = functools.partial(_sample_attn_kernel, past=past, dec=dec)
    return pl.pallas_call(
        kern, grid=(b,),
        in_specs=[per_b(dec, D_ATTN), per_b(dec, D_ATTN), per_b(dec, D_ATTN),
                  per_b(dec, LANES), per_b(N_HEADS, dec),
                  per_b(D_ATTN, past), per_b(D_ATTN, past), per_b(N_HEADS, past)],
        out_specs=per_b(dec, D_ATTN),
        out_shape=jax.ShapeDtypeStruct((b, dec, D_ATTN), BF16),
        scratch_shapes=[pltpu.VMEM((N_HEADS, past + LANES), F32)],
        compiler_params=_params(1), name="sample_attn",
    )(q, k_new, v_new, lfp, lft, cache_kt, cache_vt, cache_lft)


def _mlp_kernel(x_ref, conv_ref, attn_ref, gate_ref, wbc_ref, wba_ref, wo_ref, g2_ref,
                wup_ref, wdn_ref, y_ref):
    ya = _dot(conv_ref[...], wbc_ref[...])
    yb = _dot(attn_ref[...], wba_ref[...])
    merged = (gate_ref[:, 0:D_MODEL].astype(F32) * ya
              + gate_ref[:, D_MODEL:2 * D_MODEL].astype(F32) * yb)
    h = x_ref[...] + _dot(merged.astype(BF16), wo_ref[...])
    ms = jnp.mean(h * h, axis=-1, keepdims=True)
    hn = (h * lax.rsqrt(ms + EPS) * g2_ref[...]).astype(BF16)
    acc = h
    for c in range(D_FF // D_MODEL):
        cols = slice(c * D_MODEL, (c + 1) * D_MODEL)
        a = jnp.maximum(_dot(hn, wup_ref[:, cols]), 0.0)
        acc = acc + _dot((a * a).astype(BF16), wdn_ref[cols, :])
    y_ref[...] = acc


def _merge_mlp(x2d, conv, attn, gates, wts, *, rows):
    n_rows = x2d.shape[0]
    row_spec = lambda width: pl.BlockSpec((rows, width), lambda i: (i, 0))
    return pl.pallas_call(
        _mlp_kernel, grid=(n_rows // rows,),
        in_specs=[row_spec(D_MODEL), row_spec(D_CONV), row_spec(D_ATTN), row_spec(2 * D_MODEL)]
        + [_const_spec(w.shape) for w in wts],
        out_specs=row_spec(D_MODEL),
        out_shape=jax.ShapeDtypeStruct((n_rows, D_MODEL), F32),
        compiler_params=_params(1), name="merge_mlp",
    )(x2d, conv, attn, gates, *wts)


def kernel(x_prompt, x_sample, cache_k, cache_v, cache_logf, state_conv, meta,
           norm1_g, w_in, b_f, conv_w, conv_b, q_norm_g, k_norm_g,
           w_br_conv, w_br_attn, w_out, norm2_g, w_up, w_down):
    b, seq, _ = x_prompt.shape
    db, dec, _ = x_sample.shape
    past = cache_k.shape[2]
    length = N_META + seq
    n_main = 3 * D_CONV + 3 * D_ATTN
    q0, k0, v0 = 3 * D_CONV, 3 * D_CONV + D_ATTN, 3 * D_CONV + 2 * D_ATTN

    w = w_in[0].astype(BF16)
    w_fl = w[:, n_main:n_main + N_HEADS]
    head_of = jnp.arange(D_ATTN) // HEAD_DIM
    qg = jnp.tile(q_norm_g[0], N_HEADS)
    kg = jnp.tile(k_norm_g[0], N_HEADS)
    proj_wts = (
        norm1_g[0][None, :],
        w[:, 0:3 * D_CONV], w[:, k0:k0 + D_ATTN], w[:, n_main + N_HEADS:],
        w[:, q0:n_main].T,
        jnp.tile(w_fl, (1, LANES // N_HEADS)), w_fl.T,
        jnp.tile(b_f[0], LANES // N_HEADS)[None, :], b_f[0][:, None],
        conv_w[0], conv_b[0][None, :],
        qg[:, None], kg[:, None], kg[None, :],
        (head_of[:, None] == head_of[None, :]).astype(BF16),
    )
    mlp_wts = (w_br_conv[0].astype(BF16), w_br_attn[0].astype(BF16), w_out[0].astype(BF16),
               norm2_g[0][None, :], w_up[0].astype(BF16), w_down[0].astype(BF16))

    x_small = jnp.concatenate([meta, x_sample.reshape(db * dec, D_MODEL)], axis=0)
    left_small = jnp.concatenate(
        [jnp.zeros((1, CONV_W - 1, D_CONV), F32), state_conv[0]], axis=0)
    (conv_s, q_s, k_s, v_s, kt_s, vt_s, lft_s, lfp_s, gate_s, zlast_s) = _project_small(
        x_small, left_small, proj_wts,
        (w[:, q0:q0 + D_ATTN], w[:, v0:v0 + D_ATTN], qg[None, :]))

    left_p = jnp.broadcast_to(zlast_s[0:1], (b, CONV_W - 1, D_CONV))
    (conv_p, kb_p, qt_p, kt_p, vt_p, lftp_p, lft_p, lfp_p, gate_p, zlast_p) = _project_prompt(
        x_prompt.reshape(b * seq, D_MODEL), left_p, proj_wts, (kt_s, vt_s, lft_s),
        b=b, seq=seq)

    attn_p = _prompt_attention(
        qt_p, kb_p.reshape(b, seq, D_ATTN), k_s[:N_META].astype(BF16), vt_p,
        lft_p, lft_s[:, :N_META], lfp_p.reshape(b, seq, LANES), lfp_s[:N_META])
    y_prompt = _merge_mlp(x_prompt.reshape(b * seq, D_MODEL), conv_p,
                          attn_p.reshape(b * seq, D_ATTN), gate_p, mlp_wts, rows=MLP_ROWS)

    k_new = k_s[N_META:].reshape(db, dec, D_ATTN)
    v_new = v_s[N_META:].reshape(db, dec, D_ATTN)
    lf_new = lfp_s[N_META:, :N_HEADS].reshape(db, dec, N_HEADS)
    cache_kt = jnp.transpose(cache_k[0], (0, 2, 3, 1)).reshape(db, D_ATTN, past)
    cache_vt = jnp.transpose(cache_v[0], (0, 2, 3, 1)).reshape(db, D_ATTN, past)
    attn_s = _sample_attention(
        q_s[N_META:].reshape(db, dec, D_ATTN), k_new, v_new,
        lfp_s[N_META:].reshape(db, dec, LANES), jnp.swapaxes(lf_new, 1, 2),
        cache_kt, cache_vt, jnp.swapaxes(cache_logf[0], 1, 2))
    y_sample = _merge_mlp(x_sample.reshape(db * dec, D_MODEL), conv_s[N_META:],
                          attn_s.reshape(db * dec, D_ATTN), gate_s[N_META:], mlp_wts,
                          rows=db * dec)

    def heads_last(t):
        return jnp.transpose(t.reshape(b, N_HEADS, HEAD_DIM, length), (0, 3, 1, 2))[None]

    return (y_prompt.reshape(b, seq, D_MODEL),
            y_sample.reshape(db, dec, D_MODEL),
            heads_last(kt_p),
            heads_last(vt_p),
            jnp.swapaxes(lftp_p, 1, 2)[None],
            zlast_p[None],
            k_new.reshape(1, db, dec, N_HEADS, HEAD_DIM),
            v_new.reshape(1, db, dec, N_HEADS, HEAD_DIM),
            lf_new[None],
            zlast_s[1:][None])
```

```python
import functools

import jax
import jax.numpy as jnp
from jax import lax
from jax.experimental import pallas as pl
from jax.experimental.pallas import tpu as pltpu

D_MODEL = 1024
D_CONV = D_MODEL // 2
CONV_W = 3
N_HEADS = 8
HEAD_DIM = 64
D_ATTN = N_HEADS * HEAD_DIM
D_FF = 4 * D_MODEL
N_META = 16
EPS = 1e-6
ATTN_SCALE = HEAD_DIM ** -0.5

F32 = jnp.float32
BF16 = jnp.bfloat16

VMEM_LIMIT_BYTES = 56 * 1024 * 1024
LANES = 128
PROJ_ROWS = 512
MLP_ROWS = 512
ATTN_BLOCK = 256
MASKED_BIAS = 1e30
ZPAD = 8
N_SPLIT = 3
V_SLAB = HEAD_DIM + 16
LOG2E = 1.4426950408889634
SKIP_LOG2 = 40.0
NORM_SLACK = 1.02


def _dot(a, b):
    return jnp.dot(a, b, preferred_element_type=F32)


def _dot_nt(a, b):
    return lax.dot_general(a, b, (((1,), (1,)), ((), ())), preferred_element_type=F32)


def _dot_exact(a, b):
    return jnp.dot(a, b, preferred_element_type=F32, precision=lax.Precision.HIGHEST)


def _log_sigmoid(x):
    return jnp.minimum(x, 0.0) - jnp.log1p(jnp.exp(-jnp.abs(x)))


def _cumsum_few(x, axis):
    n = x.shape[axis]
    idx = lax.broadcasted_iota(jnp.int32, x.shape, axis)
    out = jnp.zeros(x.shape, F32)
    for i in range(n):
        term = x[i:i + 1, :] if axis == 0 else x[:, i:i + 1]
        out = out + jnp.where(idx >= i, term, 0.0)
    return out


def _triangles(n):
    r = lax.broadcasted_iota(jnp.int32, (n, n), 0)
    c = lax.broadcasted_iota(jnp.int32, (n, n), 1)
    upper = (r <= c).astype(F32)
    lower = (r >= c).astype(F32)
    return upper, lower


def _split3(c):
    hi = c.astype(BF16).astype(F32)
    r1 = c - hi
    mid = r1.astype(BF16).astype(F32)
    return hi, mid, r1 - mid


def _const_spec(shape):
    nd = len(shape)
    return pl.BlockSpec(shape, lambda *_: (0,) * nd, pipeline_mode=pl.Buffered(1))


def _params(n_axes):
    return pltpu.CompilerParams(
        dimension_semantics=("arbitrary",) * n_axes,
        vmem_limit_bytes=VMEM_LIMIT_BYTES)


def _head_norm_t(ut, g_col):
    out = []
    for h in range(N_HEADS):
        blk = ut[h * HEAD_DIM:(h + 1) * HEAD_DIM, :]
        ms = jnp.mean(blk * blk, axis=0, keepdims=True)
        out.append(blk * lax.rsqrt(ms + EPS) * g_col[h * HEAD_DIM:(h + 1) * HEAD_DIM, :])
    return jnp.concatenate(out, axis=0)


def _proj_kernel(*refs, n_seg, seg_len, tiles_per_seq, shifted):
    (x_ref, left_ref, g1_ref, wa_ref, wgl_ref, wt_ref, wfl_ref, wflt_ref,
     bfr_ref, bfc_ref, cw_ref, cb_ref, qgc_ref, kgc_ref) = refs[:14]
    if shifted:
        ktm_ref, vtm_ref, lftm_ref = refs[14:17]
        (conv_ref, qt_ref, kt_ref, vt_ref, lftp_ref, lft_ref, lfp_ref, gate_ref,
         zlast_ref, zbuf, kcar, vcar, lcar) = refs[17:]
    else:
        wq_ref, wk_ref, wv_ref, qgr_ref, kgr_ref, bd_ref = refs[14:20]
        (conv_ref, q_ref, k_ref, v_ref, kt_ref, vt_ref, lft_ref, lfp_ref, gate_ref,
         zlast_ref, zbuf) = refs[20:]

    steps_per_seq = tiles_per_seq + (1 if shifted else 0)
    step = pl.program_id(0) % steps_per_seq
    first = step == 0

    def shifted_store(out_ref, car_ref, tile):
        rolled = pltpu.roll(tile, N_META, axis=1)
        lane = lax.broadcasted_iota(jnp.int32, (tile.shape[0], LANES), 1)
        out_ref[:, 0:LANES] = jnp.where(lane < N_META, car_ref[...], rolled[:, 0:LANES])
        out_ref[:, LANES:] = rolled[:, LANES:]
        car_ref[...] = rolled[:, 0:LANES]

    if shifted:
        @pl.when(first)
        def _():
            kcar[...] = ktm_ref[:, 0:LANES]
            vcar[...] = vtm_ref[:, 0:LANES]
            lcar[...] = lftm_ref[:, 0:LANES]

        @pl.when(step == tiles_per_seq)
        def _():
            for out_ref, car_ref in ((kt_ref, kcar), (vt_ref, vcar), (lftp_ref, lcar)):
                out_ref[...] = jnp.zeros(out_ref.shape, F32)
                out_ref[:, 0:LANES] = car_ref[...]

    @pl.when(step < tiles_per_seq)
    def _():
        x = x_ref[...]
        ms = jnp.mean(x * x, axis=-1, keepdims=True)
        xn = (x * lax.rsqrt(ms + EPS) * g1_ref[...]).astype(BF16)

        cb = _dot(xn, wa_ref[:, 0:D_CONV])
        z = _dot(xn, wa_ref[:, D_CONV:2 * D_CONV]) * _dot(xn, wa_ref[:, 2 * D_CONV:3 * D_CONV])

        @pl.when(first)
        def _():
            zbuf[:, ZPAD - 2:ZPAD, :] = left_ref[...]

        w0 = cw_ref[0:1, :]
        w1 = cw_ref[1:2, :]
        w2 = cw_ref[2:3, :]
        for s in range(n_seg):
            r0 = s * seg_len
            zs = z[r0:r0 + seg_len]
            zbuf[s, ZPAD:ZPAD + seg_len, :] = zs
            zm1 = zbuf[s, ZPAD - 1:ZPAD - 1 + seg_len, :]
            zm2 = zbuf[s, ZPAD - 2:ZPAD - 2 + seg_len, :]
            y = zm2 * w0 + zm1 * w1 + zs * w2 + cb_ref[...]
            conv_ref[r0:r0 + seg_len, :] = (cb[r0:r0 + seg_len] * y).astype(BF16)
            tail = zbuf[s, ZPAD + seg_len - 2:ZPAD + seg_len, :]
            zlast_ref[s] = tail
            zbuf[s, ZPAD - 2:ZPAD, :] = tail

        kt = _head_norm_t(_dot_nt(wt_ref[D_ATTN:2 * D_ATTN, :], xn), kgc_ref[...])
        vt = _dot_nt(wt_ref[2 * D_ATTN:3 * D_ATTN, :], xn)
        lft = _log_sigmoid(_dot_nt(wflt_ref[...], xn) + bfc_ref[...])
        lfp_ref[...] = _log_sigmoid(_dot(xn, wfl_ref[...]) + bfr_ref[...])

        if shifted:
            qt = _head_norm_t(_dot_nt(wt_ref[0:D_ATTN, :], xn), qgc_ref[...])
            qt_ref[...] = (qt * (ATTN_SCALE * LOG2E)).astype(BF16)
            lft_ref[...] = lft
            shifted_store(kt_ref, kcar, kt)
            shifted_store(vt_ref, vcar, vt)
            shifted_store(lftp_ref, lcar, lft)
        else:
            def head_norm(u, g_row):
                ssq = _dot((u * u).astype(BF16), bd_ref[...])
                return u * lax.rsqrt(ssq * (1.0 / HEAD_DIM) + EPS) * g_row

            k_ref[...] = head_norm(_dot(xn, wk_ref[...]), kgr_ref[...])
            q_ref[...] = (head_norm(_dot(xn, wq_ref[...]), qgr_ref[...]) * ATTN_SCALE).astype(BF16)
            v_ref[...] = _dot(xn, wv_ref[...])
            kt_ref[...] = kt
            vt_ref[...] = vt
            lft_ref[...] = lft

        for c in range(2 * D_MODEL // D_CONV):
            gl = _dot(xn, wgl_ref[:, c * D_CONV:(c + 1) * D_CONV])
            gate_ref[:, c * D_CONV:(c + 1) * D_CONV] = jax.nn.sigmoid(gl).astype(BF16)


def _project_small(x2d, left, wts, extra):
    n_rows = x2d.shape[0]
    n_seq = left.shape[0]
    seg_len = n_rows // n_seq
    full = lambda *shape: pl.BlockSpec(shape, lambda i: (0,) * len(shape))
    out_shape = (
        jax.ShapeDtypeStruct((n_rows, D_CONV), BF16),
        jax.ShapeDtypeStruct((n_rows, D_ATTN), BF16),
        jax.ShapeDtypeStruct((n_rows, D_ATTN), F32),
        jax.ShapeDtypeStruct((n_rows, D_ATTN), F32),
        jax.ShapeDtypeStruct((D_ATTN, n_rows), F32),
        jax.ShapeDtypeStruct((D_ATTN, n_rows), F32),
        jax.ShapeDtypeStruct((N_HEADS, n_rows), F32),
        jax.ShapeDtypeStruct((n_rows, LANES), F32),
        jax.ShapeDtypeStruct((n_rows, 2 * D_MODEL), BF16),
        jax.ShapeDtypeStruct((n_seq, CONV_W - 1, D_CONV), F32),
    )
    kern = functools.partial(_proj_kernel, n_seg=n_seq, seg_len=seg_len, tiles_per_seq=1,
                             shifted=False)
    return pl.pallas_call(
        kern, grid=(1,),
        in_specs=[full(*x2d.shape), full(*left.shape)]
        + [_const_spec(w.shape) for w in wts + extra],
        out_specs=tuple(full(*s.shape) for s in out_shape), out_shape=out_shape,
        scratch_shapes=[pltpu.VMEM((n_seq, seg_len + ZPAD, D_CONV), F32)],
        compiler_params=_params(1), name="proj_small",
    )(x2d, left, *wts, *extra)


def _project_prompt(x2d, left, wts, extra, *, b, seq):
    rows = PROJ_ROWS
    tiles = seq // rows
    steps = tiles + 1
    length = N_META + seq

    def tile_of(i):
        return (i // steps) * tiles + jnp.minimum(i % steps, tiles - 1)

    row_spec = lambda width: pl.BlockSpec((rows, width), lambda i: (tile_of(i), 0))
    seq_spec = pl.BlockSpec((1, CONV_W - 1, D_CONV), lambda i: (i // steps, 0, 0))
    tok_spec = lambda feat: pl.BlockSpec(
        (None, feat, rows), lambda i: (i // steps, 0, jnp.minimum(i % steps, tiles - 1)))
    pos_spec = lambda feat: pl.BlockSpec(
        (None, feat, rows), lambda i: (i // steps, 0, i % steps))
    out_shape = (
        jax.ShapeDtypeStruct((b * seq, D_CONV), BF16),
        jax.ShapeDtypeStruct((b, D_ATTN, seq), BF16),
        jax.ShapeDtypeStruct((b, D_ATTN, length), F32),
        jax.ShapeDtypeStruct((b, D_ATTN, length), F32),
        jax.ShapeDtypeStruct((b, N_HEADS, length), F32),
        jax.ShapeDtypeStruct((b, N_HEADS, seq), F32),
        jax.ShapeDtypeStruct((b * seq, LANES), F32),
        jax.ShapeDtypeStruct((b * seq, 2 * D_MODEL), BF16),
        jax.ShapeDtypeStruct((b, CONV_W - 1, D_CONV), F32),
    )
    out_specs = (row_spec(D_CONV), tok_spec(D_ATTN), pos_spec(D_ATTN),
                 pos_spec(D_ATTN), pos_spec(N_HEADS), tok_spec(N_HEADS), row_spec(LANES),
                 row_spec(2 * D_MODEL), seq_spec)
    kern = functools.partial(_proj_kernel, n_seg=1, seg_len=rows, tiles_per_seq=tiles,
                             shifted=True)
    return pl.pallas_call(
        kern, grid=(b * steps,),
        in_specs=[row_spec(D_MODEL), seq_spec] + [_const_spec(w.shape) for w in wts + extra],
        out_specs=out_specs, out_shape=out_shape,
        scratch_shapes=[pltpu.VMEM((1, rows + ZPAD, D_CONV), F32),
                        pltpu.VMEM((D_ATTN, LANES), F32),
                        pltpu.VMEM((D_ATTN, LANES), F32),
                        pltpu.VMEM((N_HEADS, LANES), F32)],
        compiler_params=_params(1), name="proj",
    )(x2d, left, *wts, *extra)


def _prompt_attn_kernel(qt_ref, kt_ref, vt_ref, lft_ref, lftm_ref, lfp_ref, lfpm_ref,
                        o_ref, kpos, vb, kbias, crow, cend, qcat, m_s, acc_s, sbuf, tail,
                        *, seq):
    blk = ATTN_BLOCK
    n_blk = seq // blk
    length = N_META + seq
    n_pos = kpos.shape[0]
    n_bias = N_SPLIT * N_HEADS

    for j in range(n_blk):
        kpos[j * blk:(j + 1) * blk, :] = kt_ref[:, j * blk:(j + 1) * blk].T.astype(BF16)
    tail[...] = jnp.zeros(tail.shape, F32)
    tail[:, 0:N_META] = kt_ref[:, seq:length]
    kpos[seq:seq + LANES, :] = tail[...].T.astype(BF16)
    kpos[seq + LANES:, :] = jnp.zeros((n_pos - seq - LANES, D_ATTN), BF16)

    ones_row = (lax.broadcasted_iota(jnp.int32, (V_SLAB - HEAD_DIM, blk), 0) == 0).astype(BF16)
    vb[n_blk] = jnp.zeros((N_HEADS * V_SLAB, blk), BF16)
    head_col = lax.broadcasted_iota(jnp.int32, (N_HEADS, 1), 0)
    kn2 = jnp.zeros((N_HEADS, 1), F32)
    for h in range(N_HEADS):
        rows = slice(h * HEAD_DIM, (h + 1) * HEAD_DIM)
        slab = slice(h * V_SLAB, h * V_SLAB + HEAD_DIM)
        for j in range(n_blk + 1):
            if j < n_blk:
                vb[j, slab, :] = vt_ref[rows, j * blk:(j + 1) * blk].astype(BF16)
            else:
                vb[j, slab, 0:N_META] = vt_ref[rows, seq:length].astype(BF16)
            vb[j, h * V_SLAB + HEAD_DIM:(h + 1) * V_SLAB, :] = ones_row
        k_h = kt_ref[rows, :]
        kn2 = jnp.where(
            head_col == h,
            jnp.max(jnp.sum(k_h * k_h, axis=0, keepdims=True), axis=1, keepdims=True), kn2)
    k_norm = jnp.sqrt(kn2)

    upper, lower = _triangles(blk)

    def store_kbias(rows, c_col):
        hi, mid, lo = _split3(c_col * LOG2E)
        lane = lax.broadcasted_iota(jnp.int32, c_col.shape, 1)
        grp = lane // N_HEADS
        part = jnp.where(grp == 0, hi, jnp.where(grp == 1, mid, lo))
        kbias[rows, :] = jnp.where(lane < n_bias, -part,
                                   jnp.where(lane < 2 * n_bias, 1.0, 0.0)).astype(BF16)

    off_c = jnp.zeros((1, LANES), F32)
    for j in range(n_blk):
        if j == 0:
            lf_blk = jnp.concatenate([lfpm_ref[...], lfp_ref[0:blk - N_META, :]], axis=0)
        else:
            lf_blk = lfp_ref[j * blk - N_META:(j + 1) * blk - N_META, :]
        c_col = _dot_exact(lower, lf_blk) + off_c
        store_kbias(slice(j * blk, (j + 1) * blk), c_col)
        off_c = c_col[blk - 1:blk, :]
    c_col = _cumsum_few(lfp_ref[seq - N_META:seq, :], axis=0) + off_c
    store_kbias(slice(seq, length), c_col)
    kbias[length:, :] = jnp.zeros((n_pos - length, LANES), BF16)

    meta_r = _cumsum_few(lftm_ref[...], axis=1)
    off_r = meta_r[:, N_META - 1:N_META]
    lane_h = lax.broadcasted_iota(jnp.int32, (N_HEADS, LANES), 1)
    c_end = jnp.zeros((N_HEADS, LANES), F32)
    for j in range(n_blk):
        c_row = _dot_exact(lft_ref[:, j * blk:(j + 1) * blk], upper) + off_r
        crow[j] = c_row * LOG2E
        off_r = c_row[:, blk - 1:blk]
        c_end = jnp.where(lane_h == j, off_r * LOG2E, c_end)
    cend[...] = c_end

    row128 = lax.broadcasted_iota(jnp.int32, (LANES, blk), 0)
    krow = lax.broadcasted_iota(jnp.int32, (blk, blk), 0)
    qcol = lax.broadcasted_iota(jnp.int32, (blk, blk), 1)
    row_s = lax.broadcasted_iota(jnp.int32, (N_META, blk), 0)
    col_s = lax.broadcasted_iota(jnp.int32, (N_META, blk), 1)

    def q_block(t, _):
        tok0 = pl.multiple_of(t * blk, blk)
        c_q = crow[t]
        hi, mid, lo = _split3(c_q)
        bias_rows = jnp.concatenate(
            [jnp.ones((n_bias, blk), F32), hi, mid, lo,
             jnp.zeros((LANES - 2 * n_bias, blk), F32)], axis=0)
        qn2 = jnp.zeros((N_HEADS, 1), F32)
        for h in range(N_HEADS):
            pair = qt_ref[(h // 2) * LANES:(h // 2 + 1) * LANES, pl.ds(tok0, blk)]
            in_head = (row128 // HEAD_DIM) == (h % 2)
            q_h = jnp.where(in_head, pair, jnp.zeros_like(pair))
            qcat[h, 0:LANES, :] = q_h
            qcat[h, LANES:, :] = jnp.where(row128 % N_HEADS == h, bias_rows, 0.0).astype(BF16)
            q_f = q_h.astype(F32)
            qn2 = jnp.where(
                head_col == h,
                jnp.max(jnp.sum(q_f * q_f, axis=0, keepdims=True), axis=1, keepdims=True), qn2)
        m_s[...] = jnp.full(m_s.shape, -jnp.inf, F32)
        acc_s[...] = jnp.zeros(acc_s.shape, F32)

        q_norm = jnp.sqrt(qn2)
        gap = NORM_SLACK * 2.0 * q_norm * k_norm + c_q[:, 0:1] - cend[...]
        needed = jnp.logical_and(gap >= -SKIP_LOG2, lane_h < t)
        n_needed = jnp.max(
            jnp.sum(jnp.where(needed, 1.0, 0.0), axis=1, keepdims=True)).astype(jnp.int32)

        def key_block(pos0, n_rows, v_blk, visible):
            m_blk = []
            for h in range(N_HEADS):
                g = h // 2
                kc = jnp.concatenate([kpos[pl.ds(pos0, n_rows), g * LANES:(g + 1) * LANES],
                                      kbias[pl.ds(pos0, n_rows), :]], axis=1)
                s = _dot(kc, qcat[h])
                if visible is not None:
                    s = jnp.where(visible, s, -jnp.inf)
                sbuf[h, 0:n_rows, :] = s
                m_blk.append(jnp.max(s, axis=0, keepdims=True))
            for h in range(N_HEADS):
                slab = slice(h * V_SLAB, (h + 1) * V_SLAB)
                m_old = m_s[h:h + 1, :]
                m_new = jnp.maximum(m_old, m_blk[h])
                alpha = jnp.exp2(m_old - m_new)
                p = jnp.exp2(sbuf[h, 0:n_rows, :] - m_new)
                m_s[h:h + 1, :] = m_new
                acc_s[slab, :] = alpha * acc_s[slab, :] + _dot(v_blk(slab), p.astype(BF16))

        def full_block(j, _):
            key_block(pl.multiple_of(j * blk, blk), blk, lambda slab: vb[j, slab, :], None)
            return 0

        key_block(tok0, blk, lambda slab: vb[t, slab, :], krow <= qcol + N_META)
        key_block(pl.multiple_of(tok0 + blk, blk), N_META,
                  lambda slab: vb[t + 1, slab, 0:N_META], row_s <= col_s + (N_META - blk))
        lax.fori_loop(t - n_needed, t, full_block, 0)

        o_t = []
        for h in range(N_HEADS):
            norm = acc_s[h * V_SLAB + HEAD_DIM:h * V_SLAB + HEAD_DIM + 1, :]
            o_t.append(acc_s[h * V_SLAB:h * V_SLAB + HEAD_DIM, :] * (1.0 / norm))
        o_ref[pl.ds(tok0, blk), :] = jnp.concatenate(o_t, axis=0).T.astype(BF16)
        return 0

    lax.fori_loop(0, n_blk, q_block, 0)


def _prompt_attention(qt, kt, vt, lft, lft_meta, lfp, lfp_meta):
    b, _, seq = qt.shape
    length = vt.shape[2]
    blk = ATTN_BLOCK
    n_blk = seq // blk
    n_pos = (n_blk + 1) * blk
    per_b = lambda *shape: pl.BlockSpec((None,) + shape, lambda i: (i,) + (0,) * len(shape))
    kern = functools.partial(_prompt_attn_kernel, seq=seq)
    return pl.pallas_call(
        kern, grid=(b,),
        in_specs=[per_b(D_ATTN, seq), per_b(D_ATTN, length), per_b(D_ATTN, length),
                  per_b(N_HEADS, seq), _const_spec(lft_meta.shape),
                  per_b(seq, LANES), _const_spec(lfp_meta.shape)],
        out_specs=per_b(seq, D_ATTN),
        out_shape=jax.ShapeDtypeStruct((b, seq, D_ATTN), BF16),
        scratch_shapes=[pltpu.VMEM((n_pos, D_ATTN), BF16),
                        pltpu.VMEM((n_blk + 1, N_HEADS * V_SLAB, blk), BF16),
                        pltpu.VMEM((n_pos, LANES), BF16),
                        pltpu.VMEM((n_blk, N_HEADS, blk), F32),
                        pltpu.VMEM((N_HEADS, LANES), F32),
                        pltpu.VMEM((N_HEADS, 2 * LANES, blk), BF16),
                        pltpu.VMEM((N_HEADS, blk), F32),
                        pltpu.VMEM((N_HEADS * V_SLAB, blk), F32),
                        pltpu.VMEM((N_HEADS, blk, blk), F32),
                        pltpu.VMEM((D_ATTN, LANES), F32)],
        compiler_params=_params(1), name="prompt_attn",
    )(qt, kt, vt, lft, lft_meta, lfp, lfp_meta)


def _sample_attn_kernel(q_ref, kn_ref, vn_ref, lf_ref, lft_ref, ckt_ref, cvt_ref, clft_ref,
                        o_ref, crow, *, past, dec):
    blk = ATTN_BLOCK
    n_keys = past + LANES

    upper, _ = _triangles(blk)
    off = jnp.zeros((N_HEADS, 1), F32)
    for j in range(past // blk):
        loc = _dot_exact(clft_ref[:, j * blk:(j + 1) * blk], upper) + off
        crow[:, j * blk:(j + 1) * blk] = loc
        off = loc[:, blk - 1:blk]
    crow[:, 0:past] = crow[:, 0:past] - off

    cq_c = _cumsum_few(lf_ref[:, 0:N_HEADS], axis=0)
    cq_r = _cumsum_few(lft_ref[...], axis=1)
    crow[:, past:] = jnp.full((N_HEADS, LANES), MASKED_BIAS, F32)
    crow[:, past:past + dec] = cq_r

    q = q_ref[...]
    lane_head = lax.broadcasted_iota(jnp.int32, (dec, D_ATTN), 1) // HEAD_DIM
    q_exp = jnp.concatenate(
        [jnp.where(lane_head == h, q, jnp.zeros_like(q)) for h in range(N_HEADS)], axis=0)
    pad_rows = jnp.zeros((LANES - dec, D_ATTN), BF16)
    k_new = jnp.concatenate([kn_ref[...].astype(BF16), pad_rows], axis=0)
    v_new = jnp.concatenate([vn_ref[...].astype(BF16), pad_rows], axis=0)
    s_all = jnp.concatenate(
        [_dot(q_exp, ckt_ref[...].astype(BF16)), _dot_nt(q_exp, k_new)], axis=1)

    kpos = lax.broadcasted_iota(jnp.int32, (dec, n_keys), 1)
    qpos = past + lax.broadcasted_iota(jnp.int32, (dec, n_keys), 0)
    visible = kpos <= qpos
    probs = []
    norms = []
    for h in range(N_HEADS):
        s = s_all[h * dec:(h + 1) * dec, :] + cq_c[:, h:h + 1] - crow[h:h + 1, :]
        s = jnp.where(visible, s, -jnp.inf)
        p = jnp.exp(s - jnp.max(s, axis=-1, keepdims=True))
        norms.append(jnp.sum(p, axis=-1, keepdims=True))
        probs.append(p.astype(BF16))
    p_all = jnp.concatenate(probs, axis=0)
    o_all = (_dot_nt(p_all[:, 0:past], cvt_ref[...].astype(BF16))
             + _dot(p_all[:, past:], v_new))
    out = jnp.zeros((dec, D_ATTN), F32)
    for h in range(N_HEADS):
        o = o_all[h * dec:(h + 1) * dec, :] / norms[h]
        out = out + jnp.where(lane_head == h, o, 0.0)
    o_ref[...] = out.astype(BF16)


def _sample_attention(q, k_new, v_new, lfp, lft, cache_kt, cache_vt, cache_lft):
    b, dec, _ = q.shape
    past = cache_kt.shape[2]
    per_b = lambda *shape: pl.BlockSpec((None,) + shape, lambda i: (i,) + (0,) * len(shape))
    kern = functools.partial(_sample_attn_kernel, past=past, dec=dec)
    return pl.pallas_call(
        kern, grid=(b,),
        in_specs=[per_b(dec, D_ATTN), per_b(dec, D_ATTN), per_b(dec, D_ATTN),
                  per_b(dec, LANES), per_b(N_HEADS, dec),
                  per_b(D_ATTN, past), per_b(D_ATTN, past), per_b(N_HEADS, past)],
        out_specs=per_b(dec, D_ATTN),
        out_shape=jax.ShapeDtypeStruct((b, dec, D_ATTN), BF16),
        scratch_shapes=[pltpu.VMEM((N_HEADS, past + LANES), F32)],
        compiler_params=_params(1), name="sample_attn",
    )(q, k_new, v_new, lfp, lft, cache_kt, cache_vt, cache_lft)


def _mlp_kernel(x_ref, conv_ref, attn_ref, gate_ref, wbc_ref, wba_ref, wo_ref, g2_ref,
                wup_ref, wdn_ref, y_ref):
    ya = _dot(conv_ref[...], wbc_ref[...])
    yb = _dot(attn_ref[...], wba_ref[...])
    merged = (gate_ref[:, 0:D_MODEL].astype(F32) * ya
              + gate_ref[:, D_MODEL:2 * D_MODEL].astype(F32) * yb)
    h = x_ref[...] + _dot(merged.astype(BF16), wo_ref[...])
    ms = jnp.mean(h * h, axis=-1, keepdims=True)
    hn = (h * lax.rsqrt(ms + EPS) * g2_ref[...]).astype(BF16)
    acc = h
    for c in range(D_FF // D_MODEL):
        cols = slice(c * D_MODEL, (c + 1) * D_MODEL)
        a = jnp.maximum(_dot(hn, wup_ref[:, cols]), 0.0)
        acc = acc + _dot((a * a).astype(BF16), wdn_ref[cols, :])
    y_ref[...] = acc


def _merge_mlp(x2d, conv, attn, gates, wts, *, rows):
    n_rows = x2d.shape[0]
    row_spec = lambda width: pl.BlockSpec((rows, width), lambda i: (i, 0))
    return pl.pallas_call(
        _mlp_kernel, grid=(n_rows // rows,),
        in_specs=[row_spec(D_MODEL), row_spec(D_CONV), row_spec(D_ATTN), row_spec(2 * D_MODEL)]
        + [_const_spec(w.shape) for w in wts],
        out_specs=row_spec(D_MODEL),
        out_shape=jax.ShapeDtypeStruct((n_rows, D_MODEL), F32),
        compiler_params=_params(1), name="merge_mlp",
    )(x2d, conv, attn, gates, *wts)


def kernel(x_prompt, x_sample, cache_k, cache_v, cache_logf, state_conv, meta,
           norm1_g, w_in, b_f, conv_w, conv_b, q_norm_g, k_norm_g,
           w_br_conv, w_br_attn, w_out, norm2_g, w_up, w_down):
    b, seq, _ = x_prompt.shape
    db, dec, _ = x_sample.shape
    past = cache_k.shape[2]
    length = N_META + seq
    n_main = 3 * D_CONV + 3 * D_ATTN
    q0, k0, v0 = 3 * D_CONV, 3 * D_CONV + D_ATTN, 3 * D_CONV + 2 * D_ATTN

    w = w_in[0].astype(BF16)
    w_fl = w[:, n_main:n_main + N_HEADS]
    head_of = jnp.arange(D_ATTN) // HEAD_DIM
    qg = jnp.tile(q_norm_g[0], N_HEADS)
    kg = jnp.tile(k_norm_g[0], N_HEADS)
    proj_wts = (
        norm1_g[0][None, :],
        w[:, 0:3 * D_CONV], w[:, n_main + N_HEADS:],
        w[:, q0:n_main].T,
        jnp.tile(w_fl, (1, LANES // N_HEADS)), w_fl.T,
        jnp.tile(b_f[0], LANES // N_HEADS)[None, :], b_f[0][:, None],
        conv_w[0], conv_b[0][None, :],
        qg[:, None], kg[:, None],
    )
    mlp_wts = (w_br_conv[0].astype(BF16), w_br_attn[0].astype(BF16), w_out[0].astype(BF16),
               norm2_g[0][None, :], w_up[0].astype(BF16), w_down[0].astype(BF16))

    x_small = jnp.concatenate([meta, x_sample.reshape(db * dec, D_MODEL)], axis=0)
    left_small = jnp.concatenate(
        [jnp.zeros((1, CONV_W - 1, D_CONV), F32), state_conv[0]], axis=0)
    (conv_s, q_s, k_s, v_s, kt_s, vt_s, lft_s, lfp_s, gate_s, zlast_s) = _project_small(
        x_small, left_small, proj_wts,
        (w[:, q0:q0 + D_ATTN], w[:, k0:k0 + D_ATTN], w[:, v0:v0 + D_ATTN], qg[None, :],
         kg[None, :], (head_of[:, None] == head_of[None, :]).astype(BF16)))

    left_p = jnp.broadcast_to(zlast_s[0:1], (b, CONV_W - 1, D_CONV))
    (conv_p, qt_p, kt_p, vt_p, lftp_p, lft_p, lfp_p, gate_p, zlast_p) = _project_prompt(
        x_prompt.reshape(b * seq, D_MODEL), left_p, proj_wts, (kt_s, vt_s, lft_s),
        b=b, seq=seq)

    attn_p = _prompt_attention(
        qt_p, kt_p, vt_p, lft_p, lft_s[:, :N_META], lfp_p.reshape(b, seq, LANES),
        lfp_s[:N_META])
    y_prompt = _merge_mlp(x_prompt.reshape(b * seq, D_MODEL), conv_p,
                          attn_p.reshape(b * seq, D_ATTN), gate_p, mlp_wts, rows=MLP_ROWS)

    k_new = k_s[N_META:].reshape(db, dec, D_ATTN)
    v_new = v_s[N_META:].reshape(db, dec, D_ATTN)
    lf_new = lfp_s[N_META:, :N_HEADS].reshape(db, dec, N_HEADS)
    cache_kt = jnp.transpose(cache_k[0], (0, 2, 3, 1)).reshape(db, D_ATTN, past)
    cache_vt = jnp.transpose(cache_v[0], (0, 2, 3, 1)).reshape(db, D_ATTN, past)
    attn_s = _sample_attention(
        q_s[N_META:].reshape(db, dec, D_ATTN), k_new, v_new,
        lfp_s[N_META:].reshape(db, dec, LANES), jnp.swapaxes(lf_new, 1, 2),
        cache_kt, cache_vt, jnp.swapaxes(cache_logf[0], 1, 2))
    y_sample = _merge_mlp(x_sample.reshape(db * dec, D_MODEL), conv_s[N_META:],
                          attn_s.reshape(db * dec, D_ATTN), gate_s[N_META:], mlp_wts,
                          rows=db * dec)

    def heads_last(t):
        return jnp.transpose(t.reshape(b, N_HEADS, HEAD_DIM, length), (0, 3, 1, 2))[None]

    return (y_prompt.reshape(b, seq, D_MODEL),
            y_sample.reshape(db, dec, D_MODEL),
            heads_last(kt_p),
            heads_last(vt_p),
            jnp.swapaxes(lftp_p, 1, 2)[None],
            zlast_p[None],
            k_new.reshape(1, db, dec, N_HEADS, HEAD_DIM),
            v_new.reshape(1, db, dec, N_HEADS, HEAD_DIM),
            lf_new[None],
            zlast_s[1:][None])
```

```python
import functools

import jax
import jax.numpy as jnp
from jax import lax
from jax.experimental import pallas as pl
from jax.experimental.pallas import tpu as pltpu

D_MODEL = 1024
D_CONV = D_MODEL // 2
CONV_W = 3
N_HEADS = 8
HEAD_DIM = 64
D_ATTN = N_HEADS * HEAD_DIM
D_FF = 4 * D_MODEL
N_META = 16
EPS = 1e-6
ATTN_SCALE = HEAD_DIM ** -0.5

F32 = jnp.float32
BF16 = jnp.bfloat16

VMEM_LIMIT_BYTES = 56 * 1024 * 1024
LANES = 128
PROJ_ROWS = 1024
MLP_ROWS = 512
ATTN_BLOCK = 256
MASKED_BIAS = 1e30
ZPAD = 8
N_SPLIT = 3
V_SLAB = HEAD_DIM + 16
LOG2E = 1.4426950408889634
SKIP_LOG2 = 40.0
NORM_SLACK = 1.02


def _dot(a, b):
    return jnp.dot(a, b, preferred_element_type=F32)


def _dot_nt(a, b):
    return lax.dot_general(a, b, (((1,), (1,)), ((), ())), preferred_element_type=F32)


def _log_sigmoid(x):
    return jnp.minimum(x, 0.0) - jnp.log1p(jnp.exp(-jnp.abs(x)))


def _cumsum_few(x, axis):
    n = x.shape[axis]
    idx = lax.broadcasted_iota(jnp.int32, x.shape, axis)
    out = jnp.zeros(x.shape, F32)
    for i in range(n):
        term = x[i:i + 1, :] if axis == 0 else x[:, i:i + 1]
        out = out + jnp.where(idx >= i, term, 0.0)
    return out


def _triangles(n):
    r = lax.broadcasted_iota(jnp.int32, (n, n), 0)
    c = lax.broadcasted_iota(jnp.int32, (n, n), 1)
    upper = jnp.where(r <= c, 1.0, 0.0).astype(BF16)
    lower = jnp.where(r >= c, 1.0, 0.0).astype(BF16)
    return upper, lower


def _split3(c):
    hi = c.astype(BF16).astype(F32)
    r1 = c - hi
    mid = r1.astype(BF16).astype(F32)
    return hi, mid, r1 - mid


def _cumsum_rows(tri_lower, x):
    w = x.shape[1]
    pieces = jnp.concatenate(_split3(x), axis=1).astype(BF16)
    y = _dot(tri_lower, pieces)
    return y[:, 0:w] + y[:, w:2 * w] + y[:, 2 * w:3 * w]


def _cumsum_lanes(x, tri_upper):
    h = x.shape[0]
    pieces = jnp.concatenate(_split3(x), axis=0).astype(BF16)
    y = _dot(pieces, tri_upper)
    return y[0:h] + y[h:2 * h] + y[2 * h:3 * h]


def _const_spec(shape):
    nd = len(shape)
    return pl.BlockSpec(shape, lambda *_: (0,) * nd, pipeline_mode=pl.Buffered(1))


def _params(n_axes):
    return pltpu.CompilerParams(
        dimension_semantics=("arbitrary",) * n_axes,
        vmem_limit_bytes=VMEM_LIMIT_BYTES)


def _head_norm_t(ut, g_col):
    out = []
    for h in range(N_HEADS):
        blk = ut[h * HEAD_DIM:(h + 1) * HEAD_DIM, :]
        ms = jnp.mean(blk * blk, axis=0, keepdims=True)
        out.append(blk * lax.rsqrt(ms + EPS) * g_col[h * HEAD_DIM:(h + 1) * HEAD_DIM, :])
    return jnp.concatenate(out, axis=0)


def _proj_kernel(*refs, n_seg, seg_len, tiles_per_seq, shifted):
    (x_ref, left_ref, g1_ref, wa_ref, wgl_ref, wqkv_ref, wfl_ref,
     bfr_ref, cw_ref, cb_ref, qgc_ref, kgc_ref) = refs[:12]
    if shifted:
        ktm_ref, vtm_ref, lftm_ref = refs[12:15]
        (conv_ref, qt_ref, kt_ref, vt_ref, lftp_ref, lft_ref, lfp_ref, gate_ref,
         zlast_ref, zbuf, kcar, vcar, lcar) = refs[15:]
    else:
        wt_ref, wflt_ref, bfc_ref, qgr_ref, kgr_ref, bd_ref = refs[12:18]
        (conv_ref, q_ref, k_ref, v_ref, kt_ref, vt_ref, lft_ref, lfp_ref, gate_ref,
         zlast_ref, zbuf) = refs[18:]

    steps_per_seq = tiles_per_seq + (1 if shifted else 0)
    step = pl.program_id(0) % steps_per_seq
    first = step == 0

    def shifted_store(out_ref, car_ref, tile):
        rolled = pltpu.roll(tile, N_META, axis=1)
        lane = lax.broadcasted_iota(jnp.int32, (tile.shape[0], LANES), 1)
        out_ref[:, 0:LANES] = jnp.where(lane < N_META, car_ref[...], rolled[:, 0:LANES])
        out_ref[:, LANES:] = rolled[:, LANES:]
        car_ref[...] = rolled[:, 0:LANES]

    if shifted:
        @pl.when(first)
        def _():
            kcar[...] = ktm_ref[:, 0:LANES]
            vcar[...] = vtm_ref[:, 0:LANES]
            lcar[...] = lftm_ref[:, 0:LANES]

        @pl.when(step == tiles_per_seq)
        def _():
            for out_ref, car_ref in ((kt_ref, kcar), (vt_ref, vcar), (lftp_ref, lcar)):
                out_ref[...] = jnp.zeros(out_ref.shape, F32)
                out_ref[:, 0:LANES] = car_ref[...]

    @pl.when(first)
    def _():
        zbuf[:, ZPAD - 2:ZPAD, :] = left_ref[...]

    @pl.when(step < tiles_per_seq)
    def _():
        x = x_ref[...]
        ms = jnp.mean(x * x, axis=-1, keepdims=True)
        xn = (x * lax.rsqrt(ms + EPS) * g1_ref[...]).astype(BF16)

        cb = _dot(xn, wa_ref[:, 0:D_CONV])
        z = _dot(xn, wa_ref[:, D_CONV:2 * D_CONV]) * _dot(xn, wa_ref[:, 2 * D_CONV:3 * D_CONV])

        w0 = cw_ref[0:1, :]
        w1 = cw_ref[1:2, :]
        w2 = cw_ref[2:3, :]
        for s in range(n_seg):
            r0 = s * seg_len
            zs = z[r0:r0 + seg_len]
            zbuf[s, ZPAD:ZPAD + seg_len, :] = zs
            zm1 = zbuf[s, ZPAD - 1:ZPAD - 1 + seg_len, :]
            zm2 = zbuf[s, ZPAD - 2:ZPAD - 2 + seg_len, :]
            y = zm2 * w0 + zm1 * w1 + zs * w2 + cb_ref[...]
            conv_ref[r0:r0 + seg_len, :] = (cb[r0:r0 + seg_len] * y).astype(BF16)
            tail = zbuf[s, ZPAD + seg_len - 2:ZPAD + seg_len, :]
            zlast_ref[s] = tail
            zbuf[s, ZPAD - 2:ZPAD, :] = tail

        def rows_major(j):
            return _dot(xn, wqkv_ref[:, j * D_ATTN:(j + 1) * D_ATTN])

        def feature_major(j):
            if shifted:
                return rows_major(j).T
            return _dot_nt(wt_ref[j * D_ATTN:(j + 1) * D_ATTN, :], xn)

        lfp = _log_sigmoid(_dot(xn, wfl_ref[...]) + bfr_ref[...])
        lfp_ref[...] = lfp
        kt = _head_norm_t(feature_major(1), kgc_ref[...])
        vt = feature_major(2)

        if shifted:
            lft = lfp.T[0:N_HEADS, :]
            qt = _head_norm_t(feature_major(0), qgc_ref[...])
            qt_ref[...] = (qt * (ATTN_SCALE * LOG2E)).astype(BF16)
            lft_ref[...] = lft
            shifted_store(kt_ref, kcar, kt)
            shifted_store(vt_ref, vcar, vt)
            shifted_store(lftp_ref, lcar, lft)
        else:
            def head_norm(u, g_row):
                ssq = _dot((u * u).astype(BF16), bd_ref[...])
                return u * lax.rsqrt(ssq * (1.0 / HEAD_DIM) + EPS) * g_row

            k_ref[...] = head_norm(rows_major(1), kgr_ref[...])
            q_ref[...] = (head_norm(rows_major(0), qgr_ref[...]) * ATTN_SCALE).astype(BF16)
            v_ref[...] = rows_major(2)
            kt_ref[...] = kt
            vt_ref[...] = vt
            lft_ref[...] = _log_sigmoid(_dot_nt(wflt_ref[...], xn) + bfc_ref[...])

        for c in range(2 * D_MODEL // D_CONV):
            gl = _dot(xn, wgl_ref[:, c * D_CONV:(c + 1) * D_CONV])
            gate_ref[:, c * D_CONV:(c + 1) * D_CONV] = jax.nn.sigmoid(gl).astype(BF16)


def _project_small(x2d, left, wts, extra):
    n_rows = x2d.shape[0]
    n_seq = left.shape[0]
    seg_len = n_rows // n_seq
    full = lambda *shape: pl.BlockSpec(shape, lambda i: (0,) * len(shape))
    out_shape = (
        jax.ShapeDtypeStruct((n_rows, D_CONV), BF16),
        jax.ShapeDtypeStruct((n_rows, D_ATTN), BF16),
        jax.ShapeDtypeStruct((n_rows, D_ATTN), F32),
        jax.ShapeDtypeStruct((n_rows, D_ATTN), F32),
        jax.ShapeDtypeStruct((D_ATTN, n_rows), F32),
        jax.ShapeDtypeStruct((D_ATTN, n_rows), F32),
        jax.ShapeDtypeStruct((N_HEADS, n_rows), F32),
        jax.ShapeDtypeStruct((n_rows, LANES), F32),
        jax.ShapeDtypeStruct((n_rows, 2 * D_MODEL), BF16),
        jax.ShapeDtypeStruct((n_seq, CONV_W - 1, D_CONV), F32),
    )
    kern = functools.partial(_proj_kernel, n_seg=n_seq, seg_len=seg_len, tiles_per_seq=1,
                             shifted=False)
    return pl.pallas_call(
        kern, grid=(1,),
        in_specs=[full(*x2d.shape), full(*left.shape)]
        + [_const_spec(w.shape) for w in wts + extra],
        out_specs=tuple(full(*s.shape) for s in out_shape), out_shape=out_shape,
        scratch_shapes=[pltpu.VMEM((n_seq, seg_len + ZPAD, D_CONV), F32)],
        compiler_params=_params(1), name="proj_small",
    )(x2d, left, *wts, *extra)


def _project_prompt(x2d, left, wts, extra, *, b, seq):
    rows = PROJ_ROWS
    tiles = seq // rows
    steps = tiles + 1
    length = N_META + seq

    def tile_of(i):
        return (i // steps) * tiles + jnp.minimum(i % steps, tiles - 1)

    row_spec = lambda width: pl.BlockSpec((rows, width), lambda i: (tile_of(i), 0))
    seq_spec = pl.BlockSpec((1, CONV_W - 1, D_CONV), lambda i: (i // steps, 0, 0))
    tok_spec = lambda feat: pl.BlockSpec(
        (None, feat, rows), lambda i: (i // steps, 0, jnp.minimum(i % steps, tiles - 1)))
    pos_spec = lambda feat: pl.BlockSpec(
        (None, feat, rows), lambda i: (i // steps, 0, i % steps))
    out_shape = (
        jax.ShapeDtypeStruct((b * seq, D_CONV), BF16),
        jax.ShapeDtypeStruct((b, D_ATTN, seq), BF16),
        jax.ShapeDtypeStruct((b, D_ATTN, length), F32),
        jax.ShapeDtypeStruct((b, D_ATTN, length), F32),
        jax.ShapeDtypeStruct((b, N_HEADS, length), F32),
        jax.ShapeDtypeStruct((b, N_HEADS, seq), F32),
        jax.ShapeDtypeStruct((b * seq, LANES), F32),
        jax.ShapeDtypeStruct((b * seq, 2 * D_MODEL), BF16),
        jax.ShapeDtypeStruct((b, CONV_W - 1, D_CONV), F32),
    )
    out_specs = (row_spec(D_CONV), tok_spec(D_ATTN), pos_spec(D_ATTN),
                 pos_spec(D_ATTN), pos_spec(N_HEADS), tok_spec(N_HEADS), row_spec(LANES),
                 row_spec(2 * D_MODEL), seq_spec)
    kern = functools.partial(_proj_kernel, n_seg=1, seg_len=rows, tiles_per_seq=tiles,
                             shifted=True)
    return pl.pallas_call(
        kern, grid=(b * steps,),
        in_specs=[row_spec(D_MODEL), seq_spec] + [_const_spec(w.shape) for w in wts + extra],
        out_specs=out_specs, out_shape=out_shape,
        scratch_shapes=[pltpu.VMEM((1, rows + ZPAD, D_CONV), F32),
                        pltpu.VMEM((D_ATTN, LANES), F32),
                        pltpu.VMEM((D_ATTN, LANES), F32),
                        pltpu.VMEM((N_HEADS, LANES), F32)],
        compiler_params=_params(1), name="proj",
    )(x2d, left, *wts, *extra)


def _prompt_attn_kernel(qt_ref, kt_ref, vt_ref, lft_ref, lftm_ref, lfp_ref, lfpm_ref,
                        qg_ref, kg_ref, o_ref, kpos, vb, kbias, crow, cend, qcat, m_s, acc_s, sbuf, tail,
                        *, seq):
    blk = ATTN_BLOCK
    n_blk = seq // blk
    length = N_META + seq
    n_pos = kpos.shape[0]
    n_bias = N_SPLIT * N_HEADS

    for j in range(n_blk):
        kpos[j * blk:(j + 1) * blk, :] = kt_ref[:, j * blk:(j + 1) * blk].T.astype(BF16)
    tail[...] = jnp.zeros(tail.shape, F32)
    tail[:, 0:N_META] = kt_ref[:, seq:length]
    kpos[seq:seq + LANES, :] = tail[...].T.astype(BF16)
    kpos[seq + LANES:, :] = jnp.zeros((n_pos - seq - LANES, D_ATTN), BF16)

    ones_row = (lax.broadcasted_iota(jnp.int32, (V_SLAB - HEAD_DIM, blk), 0) == 0).astype(BF16)
    vb[n_blk] = jnp.zeros((N_HEADS * V_SLAB, blk), BF16)
    for h in range(N_HEADS):
        rows = slice(h * HEAD_DIM, (h + 1) * HEAD_DIM)
        slab = slice(h * V_SLAB, h * V_SLAB + HEAD_DIM)
        for j in range(n_blk + 1):
            if j < n_blk:
                vb[j, slab, :] = vt_ref[rows, j * blk:(j + 1) * blk].astype(BF16)
            else:
                vb[j, slab, 0:N_META] = vt_ref[rows, seq:length].astype(BF16)
            vb[j, h * V_SLAB + HEAD_DIM:(h + 1) * V_SLAB, :] = ones_row

    qk_bound = (NORM_SLACK * HEAD_DIM * ATTN_SCALE * LOG2E
                * jnp.max(jnp.abs(qg_ref[...]), axis=1, keepdims=True)
                * jnp.max(jnp.abs(kg_ref[...]), axis=1, keepdims=True))

    upper, lower = _triangles(blk)

    def store_kbias(rows, c_col):
        hi, mid, lo = _split3(c_col * LOG2E)
        lane = lax.broadcasted_iota(jnp.int32, c_col.shape, 1)
        grp = lane // N_HEADS
        part = jnp.where(grp == 0, hi, jnp.where(grp == 1, mid, lo))
        kbias[rows, :] = jnp.where(lane < n_bias, -part,
                                   jnp.where(lane < 2 * n_bias, 1.0, 0.0)).astype(BF16)

    off_c = jnp.zeros((1, LANES), F32)
    for j in range(n_blk):
        if j == 0:
            lf_blk = jnp.concatenate([lfpm_ref[...], lfp_ref[0:blk - N_META, :]], axis=0)
        else:
            lf_blk = lfp_ref[j * blk - N_META:(j + 1) * blk - N_META, :]
        c_col = _cumsum_rows(lower, lf_blk) + off_c
        store_kbias(slice(j * blk, (j + 1) * blk), c_col)
        off_c = c_col[blk - 1:blk, :]
    c_col = _cumsum_few(lfp_ref[seq - N_META:seq, :], axis=0) + off_c
    store_kbias(slice(seq, length), c_col)
    kbias[length:, :] = jnp.zeros((n_pos - length, LANES), BF16)

    meta_r = _cumsum_few(lftm_ref[...], axis=1)
    off_r = meta_r[:, N_META - 1:N_META]
    lane_h = lax.broadcasted_iota(jnp.int32, (N_HEADS, LANES), 1)
    c_end = jnp.zeros((N_HEADS, LANES), F32)
    for j in range(n_blk):
        c_row = _cumsum_lanes(lft_ref[:, j * blk:(j + 1) * blk], upper) + off_r
        crow[j] = c_row * LOG2E
        off_r = c_row[:, blk - 1:blk]
        c_end = jnp.where(lane_h == j, off_r * LOG2E, c_end)
    cend[...] = c_end

    row128 = lax.broadcasted_iota(jnp.int32, (LANES, blk), 0)
    krow = lax.broadcasted_iota(jnp.int32, (blk + N_META, blk), 0)
    qcol = lax.broadcasted_iota(jnp.int32, (blk + N_META, blk), 1)

    def q_block(t, _):
        tok0 = pl.multiple_of(t * blk, blk)
        c_q = crow[t]
        hi, mid, lo = _split3(c_q)
        bias_rows = jnp.concatenate(
            [jnp.ones((n_bias, blk), F32), hi, mid, lo,
             jnp.zeros((LANES - 2 * n_bias, blk), F32)], axis=0)
        for h in range(N_HEADS):
            pair = qt_ref[(h // 2) * LANES:(h // 2 + 1) * LANES, pl.ds(tok0, blk)]
            in_head = (row128 // HEAD_DIM) == (h % 2)
            qcat[h, 0:LANES, :] = jnp.where(in_head, pair, jnp.zeros_like(pair))
            qcat[h, LANES:, :] = jnp.where(row128 % N_HEADS == h, bias_rows, 0.0).astype(BF16)
        m_s[...] = jnp.full(m_s.shape, -jnp.inf, F32)
        acc_s[...] = jnp.zeros(acc_s.shape, F32)

        gap = 2.0 * qk_bound + c_q[:, 0:1] - cend[...]
        needed = jnp.logical_and(gap >= -SKIP_LOG2, lane_h < t)
        n_needed = jnp.max(
            jnp.sum(jnp.where(needed, 1.0, 0.0), axis=1, keepdims=True)).astype(jnp.int32)

        def key_block(pos0, n_rows, pv, visible):
            m_blk = []
            for h in range(N_HEADS):
                g = h // 2
                kc = jnp.concatenate([kpos[pl.ds(pos0, n_rows), g * LANES:(g + 1) * LANES],
                                      kbias[pl.ds(pos0, n_rows), :]], axis=1)
                s = _dot(kc, qcat[h])
                if visible is not None:
                    s = jnp.where(visible, s, -jnp.inf)
                sbuf[h, 0:n_rows, :] = s
                m_blk.append(jnp.max(s, axis=0, keepdims=True))
            for h in range(N_HEADS):
                slab = slice(h * V_SLAB, (h + 1) * V_SLAB)
                m_old = m_s[h:h + 1, :]
                m_new = jnp.maximum(m_old, m_blk[h])
                alpha = jnp.exp2(m_old - m_new)
                p = jnp.exp2(sbuf[h, 0:n_rows, :] - m_new)
                m_s[h:h + 1, :] = m_new
                acc_s[slab, :] = alpha * acc_s[slab, :] + pv(slab, p.astype(BF16))

        def full_block(j, _):
            key_block(pl.multiple_of(j * blk, blk), blk,
                      lambda slab, p: _dot(vb[j, slab, :], p), None)
            return 0

        def own_pv(slab, p):
            spill = jnp.concatenate([p[blk:], jnp.zeros((LANES - N_META, blk), BF16)], axis=0)
            return _dot(vb[t, slab, :], p[0:blk]) + _dot(vb[t + 1, slab, 0:LANES], spill)

        key_block(tok0, blk + N_META, own_pv, krow <= qcol + N_META)
        lax.fori_loop(t - n_needed, t, full_block, 0)

        o_t = []
        for h in range(N_HEADS):
            norm = acc_s[h * V_SLAB + HEAD_DIM:h * V_SLAB + HEAD_DIM + 1, :]
            o_t.append(acc_s[h * V_SLAB:h * V_SLAB + HEAD_DIM, :] * (1.0 / norm))
        o_ref[pl.ds(tok0, blk), :] = jnp.concatenate(o_t, axis=0).T.astype(BF16)
        return 0

    lax.fori_loop(0, n_blk, q_block, 0)


def _prompt_attention(qt, kt, vt, lft, lft_meta, lfp, lfp_meta, q_gain, k_gain):
    b, _, seq = qt.shape
    length = vt.shape[2]
    blk = ATTN_BLOCK
    n_blk = seq // blk
    n_pos = (n_blk + 1) * blk
    per_b = lambda *shape: pl.BlockSpec((None,) + shape, lambda i: (i,) + (0,) * len(shape))
    kern = functools.partial(_prompt_attn_kernel, seq=seq)
    return pl.pallas_call(
        kern, grid=(b,),
        in_specs=[per_b(D_ATTN, seq), per_b(D_ATTN, length), per_b(D_ATTN, length),
                  per_b(N_HEADS, seq), _const_spec(lft_meta.shape),
                  per_b(seq, LANES), _const_spec(lfp_meta.shape),
                  _const_spec(q_gain.shape), _const_spec(k_gain.shape)],
        out_specs=per_b(seq, D_ATTN),
        out_shape=jax.ShapeDtypeStruct((b, seq, D_ATTN), BF16),
        scratch_shapes=[pltpu.VMEM((n_pos, D_ATTN), BF16),
                        pltpu.VMEM((n_blk + 1, N_HEADS * V_SLAB, blk), BF16),
                        pltpu.VMEM((n_pos, LANES), BF16),
                        pltpu.VMEM((n_blk, N_HEADS, blk), F32),
                        pltpu.VMEM((N_HEADS, LANES), F32),
                        pltpu.VMEM((N_HEADS, 2 * LANES, blk), BF16),
                        pltpu.VMEM((N_HEADS, blk), F32),
                        pltpu.VMEM((N_HEADS * V_SLAB, blk), F32),
                        pltpu.VMEM((N_HEADS, blk + N_META, blk), F32),
                        pltpu.VMEM((D_ATTN, LANES), F32)],
        compiler_params=_params(1), name="prompt_attn",
    )(qt, kt, vt, lft, lft_meta, lfp, lfp_meta, q_gain, k_gain)


def _sample_attn_kernel(q_ref, kn_ref, vn_ref, lf_ref, lft_ref, ckt_ref, cvt_ref, clft_ref,
                        o_ref, crow, *, past, dec):
    blk = ATTN_BLOCK
    n_keys = past + LANES

    upper, _ = _triangles(blk)
    off = jnp.zeros((N_HEADS, 1), F32)
    for j in range(past // blk):
        loc = _cumsum_lanes(clft_ref[:, j * blk:(j + 1) * blk], upper) + off
        crow[:, j * blk:(j + 1) * blk] = loc
        off = loc[:, blk - 1:blk]
    crow[:, 0:past] = crow[:, 0:past] - off

    cq_c = _cumsum_few(lf_ref[:, 0:N_HEADS], axis=0)
    cq_r = _cumsum_few(lft_ref[...], axis=1)
    crow[:, past:] = jnp.full((N_HEADS, LANES), MASKED_BIAS, F32)
    crow[:, past:past + dec] = cq_r

    q = q_ref[...]
    lane_head = lax.broadcasted_iota(jnp.int32, (dec, D_ATTN), 1) // HEAD_DIM
    q_exp = jnp.concatenate(
        [jnp.where(lane_head == h, q, jnp.zeros_like(q)) for h in range(N_HEADS)], axis=0)
    pad_rows = jnp.zeros((LANES - dec, D_ATTN), BF16)
    k_new = jnp.concatenate([kn_ref[...].astype(BF16), pad_rows], axis=0)
    v_new = jnp.concatenate([vn_ref[...].astype(BF16), pad_rows], axis=0)
    s_all = jnp.concatenate(
        [_dot(q_exp, ckt_ref[...].astype(BF16)), _dot_nt(q_exp, k_new)], axis=1)

    kpos = lax.broadcasted_iota(jnp.int32, (dec, n_keys), 1)
    qpos = past + lax.broadcasted_iota(jnp.int32, (dec, n_keys), 0)
    visible = kpos <= qpos
    probs = []
    norms = []
    for h in range(N_HEADS):
        s = s_all[h * dec:(h + 1) * dec, :] + cq_c[:, h:h + 1] - crow[h:h + 1, :]
        s = jnp.where(visible, s, -jnp.inf)
        p = jnp.exp(s - jnp.max(s, axis=-1, keepdims=True))
        norms.append(jnp.sum(p, axis=-1, keepdims=True))
        probs.append(p.astype(BF16))
    p_all = jnp.concatenate(probs, axis=0)
    o_all = (_dot_nt(p_all[:, 0:past], cvt_ref[...].astype(BF16))
             + _dot(p_all[:, past:], v_new))
    out = jnp.zeros((dec, D_ATTN), F32)
    for h in range(N_HEADS):
        o = o_all[h * dec:(h + 1) * dec, :] / norms[h]
        out = out + jnp.where(lane_head == h, o, 0.0)
    o_ref[...] = out.astype(BF16)


def _sample_attention(q, k_new, v_new, lfp, lft, cache_kt, cache_vt, cache_lft):
    b, dec, _ = q.shape
    past = cache_kt.shape[2]
    per_b = lambda *shape: pl.BlockSpec((None,) + shape, lambda i: (i,) + (0,) * len(shape))
    kern = functools.partial(_sample_attn_kernel, past=past, dec=dec)
    return pl.pallas_call(
        kern, grid=(b,),
        in_specs=[per_b(dec, D_ATTN), per_b(dec, D_ATTN), per_b(dec, D_ATTN),
                  per_b(dec, LANES), per_b(N_HEADS, dec),
                  per_b(D_ATTN, past), per_b(D_ATTN, past), per_b(N_HEADS, past)],
        out_specs=per_b(dec, D_ATTN),
        out_shape=jax.ShapeDtypeStruct((b, dec, D_ATTN), BF16),
        scratch_shapes=[pltpu.VMEM((N_HEADS, past + LANES), F32)],
        compiler_params=_params(1), name="sample_attn",
    )(q, k_new, v_new, lfp, lft, cache_kt, cache_vt, cache_lft)


def _mlp_kernel(x_ref, conv_ref, attn_ref, gate_ref, wbc_ref, wba_ref, wo_ref, g2_ref,
                wup_ref, wdn_ref, y_ref):
    ya = _dot(conv_ref[...], wbc_ref[...])
    yb = _dot(attn_ref[...], wba_ref[...])
    merged = (gate_ref[:, 0:D_MODEL].astype(F32) * ya
              + gate_ref[:, D_MODEL:2 * D_MODEL].astype(F32) * yb)
    h = x_ref[...] + _dot(merged.astype(BF16), wo_ref[...])
    ms = jnp.mean(h * h, axis=-1, keepdims=True)
    hn = (h * lax.rsqrt(ms + EPS) * g2_ref[...]).astype(BF16)
    acc = h
    for c in range(D_FF // D_MODEL):
        cols = slice(c * D_MODEL, (c + 1) * D_MODEL)
        a = jnp.maximum(_dot(hn, wup_ref[:, cols]), 0.0)
        acc = acc + _dot((a * a).astype(BF16), wdn_ref[cols, :])
    y_ref[...] = acc


def _merge_mlp(x2d, conv, attn, gates, wts, *, rows):
    n_rows = x2d.shape[0]
    row_spec = lambda width: pl.BlockSpec((rows, width), lambda i: (i, 0))
    return pl.pallas_call(
        _mlp_kernel, grid=(n_rows // rows,),
        in_specs=[row_spec(D_MODEL), row_spec(D_CONV), row_spec(D_ATTN), row_spec(2 * D_MODEL)]
        + [_const_spec(w.shape) for w in wts],
        out_specs=row_spec(D_MODEL),
        out_shape=jax.ShapeDtypeStruct((n_rows, D_MODEL), F32),
        compiler_params=_params(1), name="merge_mlp",
    )(x2d, conv, attn, gates, *wts)


def kernel(x_prompt, x_sample, cache_k, cache_v, cache_logf, state_conv, meta,
           norm1_g, w_in, b_f, conv_w, conv_b, q_norm_g, k_norm_g,
           w_br_conv, w_br_attn, w_out, norm2_g, w_up, w_down):
    b, seq, _ = x_prompt.shape
    db, dec, _ = x_sample.shape
    past = cache_k.shape[2]
    length = N_META + seq
    n_main = 3 * D_CONV + 3 * D_ATTN
    q0 = 3 * D_CONV

    w = w_in[0].astype(BF16)
    w_fl = w[:, n_main:n_main + N_HEADS]
    head_of = jnp.arange(D_ATTN) // HEAD_DIM
    qg = jnp.tile(q_norm_g[0], N_HEADS)
    kg = jnp.tile(k_norm_g[0], N_HEADS)
    proj_wts = (
        norm1_g[0][None, :],
        w[:, 0:3 * D_CONV], w[:, n_main + N_HEADS:], w[:, q0:n_main],
        jnp.tile(w_fl, (1, LANES // N_HEADS)),
        jnp.tile(b_f[0], LANES // N_HEADS)[None, :],
        conv_w[0], conv_b[0][None, :],
        qg[:, None], kg[:, None],
    )
    mlp_wts = (w_br_conv[0].astype(BF16), w_br_attn[0].astype(BF16), w_out[0].astype(BF16),
               norm2_g[0][None, :], w_up[0].astype(BF16), w_down[0].astype(BF16))

    x_small = jnp.concatenate([meta, x_sample.reshape(db * dec, D_MODEL)], axis=0)
    left_small = jnp.concatenate(
        [jnp.zeros((1, CONV_W - 1, D_CONV), F32), state_conv[0]], axis=0)
    (conv_s, q_s, k_s, v_s, kt_s, vt_s, lft_s, lfp_s, gate_s, zlast_s) = _project_small(
        x_small, left_small, proj_wts,
        (w[:, q0:n_main].T, w_fl.T, b_f[0][:, None], qg[None, :], kg[None, :],
         (head_of[:, None] == head_of[None, :]).astype(BF16)))

    left_p = jnp.broadcast_to(zlast_s[0:1], (b, CONV_W - 1, D_CONV))
    (conv_p, qt_p, kt_p, vt_p, lftp_p, lft_p, lfp_p, gate_p, zlast_p) = _project_prompt(
        x_prompt.reshape(b * seq, D_MODEL), left_p, proj_wts, (kt_s, vt_s, lft_s),
        b=b, seq=seq)

    attn_p = _prompt_attention(
        qt_p, kt_p, vt_p, lft_p, lft_s[:, :N_META], lfp_p.reshape(b, seq, LANES),
        lfp_s[:N_META], q_norm_g, k_norm_g)
    y_prompt = _merge_mlp(x_prompt.reshape(b * seq, D_MODEL), conv_p,
                          attn_p.reshape(b * seq, D_ATTN), gate_p, mlp_wts, rows=MLP_ROWS)

    k_new = k_s[N_META:].reshape(db, dec, D_ATTN)
    v_new = v_s[N_META:].reshape(db, dec, D_ATTN)
    lf_new = lfp_s[N_META:, :N_HEADS].reshape(db, dec, N_HEADS)
    cache_kt = jnp.transpose(cache_k[0], (0, 2, 3, 1)).reshape(db, D_ATTN, past)
    cache_vt = jnp.transpose(cache_v[0], (0, 2, 3, 1)).reshape(db, D_ATTN, past)
    attn_s = _sample_attention(
        q_s[N_META:].reshape(db, dec, D_ATTN), k_new, v_new,
        lfp_s[N_META:].reshape(db, dec, LANES), jnp.swapaxes(lf_new, 1, 2),
        cache_kt, cache_vt, jnp.swapaxes(cache_logf[0], 1, 2))
    y_sample = _merge_mlp(x_sample.reshape(db * dec, D_MODEL), conv_s[N_META:],
                          attn_s.reshape(db * dec, D_ATTN), gate_s[N_META:], mlp_wts,
                          rows=db * dec)

    def heads_last(t):
        return jnp.transpose(t.reshape(b, N_HEADS, HEAD_DIM, length), (0, 3, 1, 2))[None]

    return (y_prompt.reshape(b, seq, D_MODEL),
            y_sample.reshape(db, dec, D_MODEL),
            heads_last(kt_p),
            heads_last(vt_p),
            jnp.swapaxes(lftp_p, 1, 2)[None],
            zlast_p[None],
            k_new.reshape(1, db, dec, N_HEADS, HEAD_DIM),
            v_new.reshape(1, db, dec, N_HEADS, HEAD_DIM),
            lf_new[None],
            zlast_s[1:][None])
```

```python
import functools

import jax
import jax.numpy as jnp
from jax import lax
from jax.experimental import pallas as pl
from jax.experimental.pallas import tpu as pltpu

D_MODEL = 1024
D_CONV = D_MODEL // 2
CONV_W = 3
N_HEADS = 8
HEAD_DIM = 64
D_ATTN = N_HEADS * HEAD_DIM
D_FF = 4 * D_MODEL
N_META = 16
EPS = 1e-6
ATTN_SCALE = HEAD_DIM ** -0.5

F32 = jnp.float32
BF16 = jnp.bfloat16

VMEM_LIMIT_BYTES = 56 * 1024 * 1024
LANES = 128
PROJ_ROWS = 1024
MLP_ROWS = 512
ATTN_BLOCK = 256
MASKED_BIAS = 1e30
ZPAD = 8
N_SPLIT = 3
V_SLAB = HEAD_DIM + 16
LOG2E = 1.4426950408889634
SKIP_LOG2 = 40.0
NORM_SLACK = 1.02


def _dot(a, b):
    return jnp.dot(a, b, preferred_element_type=F32)


def _dot_nt(a, b):
    return lax.dot_general(a, b, (((1,), (1,)), ((), ())), preferred_element_type=F32)


def _log_sigmoid(x):
    return jnp.minimum(x, 0.0) - jnp.log1p(jnp.exp(-jnp.abs(x)))


def _cumsum_few(x, axis):
    n = x.shape[axis]
    idx = lax.broadcasted_iota(jnp.int32, x.shape, axis)
    out = jnp.zeros(x.shape, F32)
    for i in range(n):
        term = x[i:i + 1, :] if axis == 0 else x[:, i:i + 1]
        out = out + jnp.where(idx >= i, term, 0.0)
    return out


def _triangles(n):
    r = lax.broadcasted_iota(jnp.int32, (n, n), 0)
    c = lax.broadcasted_iota(jnp.int32, (n, n), 1)
    upper = jnp.where(r <= c, 1.0, 0.0).astype(BF16)
    lower = jnp.where(r >= c, 1.0, 0.0).astype(BF16)
    return upper, lower


def _split3(c):
    hi = c.astype(BF16).astype(F32)
    r1 = c - hi
    mid = r1.astype(BF16).astype(F32)
    return hi, mid, r1 - mid


def _cumsum_rows(tri_lower, x):
    w = x.shape[1]
    pieces = jnp.concatenate(_split3(x), axis=1).astype(BF16)
    y = _dot(tri_lower, pieces)
    return y[:, 0:w] + y[:, w:2 * w] + y[:, 2 * w:3 * w]


def _cumsum_lanes(x, tri_upper):
    h = x.shape[0]
    pieces = jnp.concatenate(_split3(x), axis=0).astype(BF16)
    y = _dot(pieces, tri_upper)
    return y[0:h] + y[h:2 * h] + y[2 * h:3 * h]


def _const_spec(shape):
    nd = len(shape)
    return pl.BlockSpec(shape, lambda *_: (0,) * nd, pipeline_mode=pl.Buffered(1))


def _params(n_axes):
    return pltpu.CompilerParams(
        dimension_semantics=("arbitrary",) * n_axes,
        vmem_limit_bytes=VMEM_LIMIT_BYTES)


def _head_norm_t(ut, g_col):
    out = []
    for h in range(N_HEADS):
        blk = ut[h * HEAD_DIM:(h + 1) * HEAD_DIM, :]
        ms = jnp.mean(blk * blk, axis=0, keepdims=True)
        out.append(blk * lax.rsqrt(ms + EPS) * g_col[h * HEAD_DIM:(h + 1) * HEAD_DIM, :])
    return jnp.concatenate(out, axis=0)


def _proj_kernel(*refs, n_seg, seg_len, tiles_per_seq, shifted):
    (x_ref, left_ref, g1_ref, wa_ref, wgl_ref, wqkv_ref, wfl_ref,
     bfr_ref, cw_ref, cb_ref, qgc_ref, kgc_ref) = refs[:12]
    if shifted:
        ktm_ref, vtm_ref, lftm_ref = refs[12:15]
        (conv_ref, qt_ref, kt_ref, vt_ref, lftp_ref, lft_ref, lfp_ref, gate_ref,
         zlast_ref, zbuf, kcar, vcar, lcar) = refs[15:]
    else:
        wflt_ref, bfc_ref, qgr_ref, kgr_ref, bd_ref = refs[12:17]
        (conv_ref, q_ref, k_ref, v_ref, kt_ref, vt_ref, lft_ref, lfp_ref, gate_ref,
         zlast_ref, zbuf) = refs[17:]

    steps_per_seq = tiles_per_seq + (1 if shifted else 0)
    step = pl.program_id(0) % steps_per_seq
    first = step == 0

    def shifted_store(out_ref, car_ref, tile):
        rolled = pltpu.roll(tile, N_META, axis=1)
        lane = lax.broadcasted_iota(jnp.int32, (tile.shape[0], LANES), 1)
        out_ref[:, 0:LANES] = jnp.where(lane < N_META, car_ref[...], rolled[:, 0:LANES])
        out_ref[:, LANES:] = rolled[:, LANES:]
        car_ref[...] = rolled[:, 0:LANES]

    if shifted:
        @pl.when(first)
        def _():
            kcar[...] = ktm_ref[:, 0:LANES]
            vcar[...] = vtm_ref[:, 0:LANES]
            lcar[...] = lftm_ref[:, 0:LANES]

        @pl.when(step == tiles_per_seq)
        def _():
            for out_ref, car_ref in ((kt_ref, kcar), (vt_ref, vcar), (lftp_ref, lcar)):
                out_ref[...] = jnp.zeros(out_ref.shape, F32)
                out_ref[:, 0:LANES] = car_ref[...]

    @pl.when(first)
    def _():
        zbuf[:, ZPAD - 2:ZPAD, :] = left_ref[...]

    @pl.when(step < tiles_per_seq)
    def _():
        x = x_ref[...]
        ms = jnp.mean(x * x, axis=-1, keepdims=True)
        xn = (x * lax.rsqrt(ms + EPS) * g1_ref[...]).astype(BF16)

        cb = _dot_nt(xn, wa_ref[0:D_CONV, :])
        z = (_dot_nt(xn, wa_ref[D_CONV:2 * D_CONV, :])
             * _dot_nt(xn, wa_ref[2 * D_CONV:3 * D_CONV, :]))

        w0 = cw_ref[0:1, :]
        w1 = cw_ref[1:2, :]
        w2 = cw_ref[2:3, :]
        for s in range(n_seg):
            r0 = s * seg_len
            zs = z[r0:r0 + seg_len]
            zbuf[s, ZPAD:ZPAD + seg_len, :] = zs
            zm1 = zbuf[s, ZPAD - 1:ZPAD - 1 + seg_len, :]
            zm2 = zbuf[s, ZPAD - 2:ZPAD - 2 + seg_len, :]
            y = zm2 * w0 + zm1 * w1 + zs * w2 + cb_ref[...]
            conv_ref[r0:r0 + seg_len, :] = (cb[r0:r0 + seg_len] * y).astype(BF16)
            tail = zbuf[s, ZPAD + seg_len - 2:ZPAD + seg_len, :]
            zlast_ref[s] = tail
            zbuf[s, ZPAD - 2:ZPAD, :] = tail

        def rows_major(j):
            return _dot_nt(xn, wqkv_ref[j * D_ATTN:(j + 1) * D_ATTN, :])

        def feature_major(j):
            if shifted:
                return rows_major(j).T
            return _dot_nt(wqkv_ref[j * D_ATTN:(j + 1) * D_ATTN, :], xn)

        lfp = _log_sigmoid(_dot_nt(xn, wfl_ref[...]) + bfr_ref[...])
        lfp_ref[...] = lfp
        kt = _head_norm_t(feature_major(1), kgc_ref[...])
        vt = feature_major(2)

        if shifted:
            lft = lfp.T[0:N_HEADS, :]
            qt = _head_norm_t(feature_major(0), qgc_ref[...])
            qt_ref[...] = (qt * (ATTN_SCALE * LOG2E)).astype(BF16)
            lft_ref[...] = lft
            shifted_store(kt_ref, kcar, kt)
            shifted_store(vt_ref, vcar, vt)
            shifted_store(lftp_ref, lcar, lft)
        else:
            def head_norm(u, g_row):
                ssq = _dot((u * u).astype(BF16), bd_ref[...])
                return u * lax.rsqrt(ssq * (1.0 / HEAD_DIM) + EPS) * g_row

            k_ref[...] = head_norm(rows_major(1), kgr_ref[...])
            q_ref[...] = (head_norm(rows_major(0), qgr_ref[...]) * ATTN_SCALE).astype(BF16)
            v_ref[...] = rows_major(2)
            kt_ref[...] = kt
            vt_ref[...] = vt
            lft_ref[...] = _log_sigmoid(_dot_nt(wflt_ref[...], xn) + bfc_ref[...])

        for c in range(2 * D_MODEL // D_CONV):
            gl = _dot_nt(xn, wgl_ref[c * D_CONV:(c + 1) * D_CONV, :])
            gate_ref[:, c * D_CONV:(c + 1) * D_CONV] = jax.nn.sigmoid(gl).astype(BF16)


def _project_small(x2d, left, wts, extra):
    n_rows = x2d.shape[0]
    n_seq = left.shape[0]
    seg_len = n_rows // n_seq
    full = lambda *shape: pl.BlockSpec(shape, lambda i: (0,) * len(shape))
    out_shape = (
        jax.ShapeDtypeStruct((n_rows, D_CONV), BF16),
        jax.ShapeDtypeStruct((n_rows, D_ATTN), BF16),
        jax.ShapeDtypeStruct((n_rows, D_ATTN), F32),
        jax.ShapeDtypeStruct((n_rows, D_ATTN), F32),
        jax.ShapeDtypeStruct((D_ATTN, n_rows), F32),
        jax.ShapeDtypeStruct((D_ATTN, n_rows), F32),
        jax.ShapeDtypeStruct((N_HEADS, n_rows), F32),
        jax.ShapeDtypeStruct((n_rows, LANES), F32),
        jax.ShapeDtypeStruct((n_rows, 2 * D_MODEL), BF16),
        jax.ShapeDtypeStruct((n_seq, CONV_W - 1, D_CONV), F32),
    )
    kern = functools.partial(_proj_kernel, n_seg=n_seq, seg_len=seg_len, tiles_per_seq=1,
                             shifted=False)
    return pl.pallas_call(
        kern, grid=(1,),
        in_specs=[full(*x2d.shape), full(*left.shape)]
        + [_const_spec(w.shape) for w in wts + extra],
        out_specs=tuple(full(*s.shape) for s in out_shape), out_shape=out_shape,
        scratch_shapes=[pltpu.VMEM((n_seq, seg_len + ZPAD, D_CONV), F32)],
        compiler_params=_params(1), name="proj_small",
    )(x2d, left, *wts, *extra)


def _project_prompt(x2d, left, wts, extra, *, b, seq):
    rows = PROJ_ROWS
    tiles = seq // rows
    steps = tiles + 1
    length = N_META + seq

    def tile_of(i):
        return (i // steps) * tiles + jnp.minimum(i % steps, tiles - 1)

    row_spec = lambda width: pl.BlockSpec((rows, width), lambda i: (tile_of(i), 0))
    seq_spec = pl.BlockSpec((1, CONV_W - 1, D_CONV), lambda i: (i // steps, 0, 0))
    tok_spec = lambda feat: pl.BlockSpec(
        (None, feat, rows), lambda i: (i // steps, 0, jnp.minimum(i % steps, tiles - 1)))
    pos_spec = lambda feat: pl.BlockSpec(
        (None, feat, rows), lambda i: (i // steps, 0, i % steps))
    out_shape = (
        jax.ShapeDtypeStruct((b * seq, D_CONV), BF16),
        jax.ShapeDtypeStruct((b, D_ATTN, seq), BF16),
        jax.ShapeDtypeStruct((b, D_ATTN, length), F32),
        jax.ShapeDtypeStruct((b, D_ATTN, length), F32),
        jax.ShapeDtypeStruct((b, N_HEADS, length), F32),
        jax.ShapeDtypeStruct((b, N_HEADS, seq), F32),
        jax.ShapeDtypeStruct((b * seq, LANES), F32),
        jax.ShapeDtypeStruct((b * seq, 2 * D_MODEL), BF16),
        jax.ShapeDtypeStruct((b, CONV_W - 1, D_CONV), F32),
    )
    out_specs = (row_spec(D_CONV), tok_spec(D_ATTN), pos_spec(D_ATTN),
                 pos_spec(D_ATTN), pos_spec(N_HEADS), tok_spec(N_HEADS), row_spec(LANES),
                 row_spec(2 * D_MODEL), seq_spec)
    kern = functools.partial(_proj_kernel, n_seg=1, seg_len=rows, tiles_per_seq=tiles,
                             shifted=True)
    return pl.pallas_call(
        kern, grid=(b * steps,),
        in_specs=[row_spec(D_MODEL), seq_spec] + [_const_spec(w.shape) for w in wts + extra],
        out_specs=out_specs, out_shape=out_shape,
        scratch_shapes=[pltpu.VMEM((1, rows + ZPAD, D_CONV), F32),
                        pltpu.VMEM((D_ATTN, LANES), F32),
                        pltpu.VMEM((D_ATTN, LANES), F32),
                        pltpu.VMEM((N_HEADS, LANES), F32)],
        compiler_params=_params(1), name="proj",
    )(x2d, left, *wts, *extra)


def _prompt_attn_kernel(qt_ref, kt_ref, vt_ref, lft_ref, lftm_ref, lfp_ref, lfpm_ref,
                        qg_ref, kg_ref, o_ref, kpos, vb, kbias, crow, cend, qcat, m_s, acc_s, sbuf, tail,
                        *, seq):
    blk = ATTN_BLOCK
    n_blk = seq // blk
    length = N_META + seq
    n_pos = kpos.shape[0]
    n_bias = N_SPLIT * N_HEADS

    for j in range(n_blk):
        kpos[j * blk:(j + 1) * blk, :] = kt_ref[:, j * blk:(j + 1) * blk].T.astype(BF16)
    tail[...] = jnp.zeros(tail.shape, F32)
    tail[:, 0:N_META] = kt_ref[:, seq:length]
    kpos[seq:seq + LANES, :] = tail[...].T.astype(BF16)
    kpos[seq + LANES:, :] = jnp.zeros((n_pos - seq - LANES, D_ATTN), BF16)

    ones_row = (lax.broadcasted_iota(jnp.int32, (V_SLAB - HEAD_DIM, blk), 0) == 0).astype(BF16)
    vb[n_blk] = jnp.zeros((N_HEADS * V_SLAB, blk), BF16)
    for h in range(N_HEADS):
        rows = slice(h * HEAD_DIM, (h + 1) * HEAD_DIM)
        slab = slice(h * V_SLAB, h * V_SLAB + HEAD_DIM)
        for j in range(n_blk + 1):
            if j < n_blk:
                vb[j, slab, :] = vt_ref[rows, j * blk:(j + 1) * blk].astype(BF16)
            else:
                vb[j, slab, 0:N_META] = vt_ref[rows, seq:length].astype(BF16)
            vb[j, h * V_SLAB + HEAD_DIM:(h + 1) * V_SLAB, :] = ones_row

    qk_bound = (NORM_SLACK * HEAD_DIM * ATTN_SCALE * LOG2E
                * jnp.max(jnp.abs(qg_ref[...]), axis=1, keepdims=True)
                * jnp.max(jnp.abs(kg_ref[...]), axis=1, keepdims=True))

    upper, lower = _triangles(blk)

    def store_kbias(rows, c_col):
        hi, mid, lo = _split3(c_col * LOG2E)
        lane = lax.broadcasted_iota(jnp.int32, c_col.shape, 1)
        grp = lane // N_HEADS
        part = jnp.where(grp == 0, hi, jnp.where(grp == 1, mid, lo))
        kbias[rows, :] = jnp.where(lane < n_bias, -part,
                                   jnp.where(lane < 2 * n_bias, 1.0, 0.0)).astype(BF16)

    off_c = jnp.zeros((1, LANES), F32)
    for j in range(n_blk):
        if j == 0:
            lf_blk = jnp.concatenate([lfpm_ref[...], lfp_ref[0:blk - N_META, :]], axis=0)
        else:
            lf_blk = lfp_ref[j * blk - N_META:(j + 1) * blk - N_META, :]
        c_col = _cumsum_rows(lower, lf_blk) + off_c
        store_kbias(slice(j * blk, (j + 1) * blk), c_col)
        off_c = c_col[blk - 1:blk, :]
    c_col = _cumsum_few(lfp_ref[seq - N_META:seq, :], axis=0) + off_c
    store_kbias(slice(seq, length), c_col)
    kbias[length:, :] = jnp.zeros((n_pos - length, LANES), BF16)

    meta_r = _cumsum_few(lftm_ref[...], axis=1)
    off_r = meta_r[:, N_META - 1:N_META]
    lane_h = lax.broadcasted_iota(jnp.int32, (N_HEADS, LANES), 1)
    c_end = jnp.zeros((N_HEADS, LANES), F32)
    for j in range(n_blk):
        c_row = _cumsum_lanes(lft_ref[:, j * blk:(j + 1) * blk], upper) + off_r
        crow[j] = c_row * LOG2E
        off_r = c_row[:, blk - 1:blk]
        c_end = jnp.where(lane_h == j, off_r * LOG2E, c_end)
    cend[...] = c_end

    row128 = lax.broadcasted_iota(jnp.int32, (LANES, blk), 0)
    krow = lax.broadcasted_iota(jnp.int32, (blk + N_META, blk), 0)
    qcol = lax.broadcasted_iota(jnp.int32, (blk + N_META, blk), 1)

    def q_block(t, _):
        tok0 = pl.multiple_of(t * blk, blk)
        c_q = crow[t]
        hi, mid, lo = _split3(c_q)
        bias_rows = jnp.concatenate(
            [jnp.ones((n_bias, blk), F32), hi, mid, lo,
             jnp.zeros((LANES - 2 * n_bias, blk), F32)], axis=0)
        for h in range(N_HEADS):
            pair = qt_ref[(h // 2) * LANES:(h // 2 + 1) * LANES, pl.ds(tok0, blk)]
            in_head = (row128 // HEAD_DIM) == (h % 2)
            qcat[h, 0:LANES, :] = jnp.where(in_head, pair, jnp.zeros_like(pair))
            qcat[h, LANES:, :] = jnp.where(row128 % N_HEADS == h, bias_rows, 0.0).astype(BF16)
        m_s[...] = jnp.full(m_s.shape, -jnp.inf, F32)
        acc_s[...] = jnp.zeros(acc_s.shape, F32)

        gap = 2.0 * qk_bound + c_q[:, 0:1] - cend[...]
        needed = jnp.logical_and(gap >= -SKIP_LOG2, lane_h < t)
        n_needed = jnp.max(
            jnp.sum(jnp.where(needed, 1.0, 0.0), axis=1, keepdims=True)).astype(jnp.int32)

        def key_block(pos0, n_rows, pv, visible):
            m_blk = []
            for h in range(N_HEADS):
                g = h // 2
                kc = jnp.concatenate([kpos[pl.ds(pos0, n_rows), g * LANES:(g + 1) * LANES],
                                      kbias[pl.ds(pos0, n_rows), :]], axis=1)
                s = _dot(kc, qcat[h])
                if visible is not None:
                    s = jnp.where(visible, s, -jnp.inf)
                sbuf[h, 0:n_rows, :] = s
                m_blk.append(jnp.max(s, axis=0, keepdims=True))
            for h in range(N_HEADS):
                slab = slice(h * V_SLAB, (h + 1) * V_SLAB)
                m_old = m_s[h:h + 1, :]
                m_new = jnp.maximum(m_old, m_blk[h])
                alpha = jnp.exp2(m_old - m_new)
                p = jnp.exp2(sbuf[h, 0:n_rows, :] - m_new)
                m_s[h:h + 1, :] = m_new
                acc_s[slab, :] = alpha * acc_s[slab, :] + pv(slab, p.astype(BF16))

        def full_block(j, _):
            key_block(pl.multiple_of(j * blk, blk), blk,
                      lambda slab, p: _dot(vb[j, slab, :], p), None)
            return 0

        def own_pv(slab, p):
            spill = jnp.concatenate([p[blk:], jnp.zeros((LANES - N_META, blk), BF16)], axis=0)
            return _dot(vb[t, slab, :], p[0:blk]) + _dot(vb[t + 1, slab, 0:LANES], spill)

        key_block(tok0, blk + N_META, own_pv, krow <= qcol + N_META)
        lax.fori_loop(t - n_needed, t, full_block, 0)

        o_t = []
        for h in range(N_HEADS):
            norm = acc_s[h * V_SLAB + HEAD_DIM:h * V_SLAB + HEAD_DIM + 1, :]
            o_t.append(acc_s[h * V_SLAB:h * V_SLAB + HEAD_DIM, :] * (1.0 / norm))
        o_ref[pl.ds(tok0, blk), :] = jnp.concatenate(o_t, axis=0).T.astype(BF16)
        return 0

    lax.fori_loop(0, n_blk, q_block, 0)


def _prompt_attention(qt, kt, vt, lft, lft_meta, lfp, lfp_meta, q_gain, k_gain):
    b, _, seq = qt.shape
    length = vt.shape[2]
    blk = ATTN_BLOCK
    n_blk = seq // blk
    n_pos = (n_blk + 1) * blk
    per_b = lambda *shape: pl.BlockSpec((None,) + shape, lambda i: (i,) + (0,) * len(shape))
    kern = functools.partial(_prompt_attn_kernel, seq=seq)
    return pl.pallas_call(
        kern, grid=(b,),
        in_specs=[per_b(D_ATTN, seq), per_b(D_ATTN, length), per_b(D_ATTN, length),
                  per_b(N_HEADS, seq), _const_spec(lft_meta.shape),
                  per_b(seq, LANES), _const_spec(lfp_meta.shape),
                  _const_spec(q_gain.shape), _const_spec(k_gain.shape)],
        out_specs=per_b(seq, D_ATTN),
        out_shape=jax.ShapeDtypeStruct((b, seq, D_ATTN), BF16),
        scratch_shapes=[pltpu.VMEM((n_pos, D_ATTN), BF16),
                        pltpu.VMEM((n_blk + 1, N_HEADS * V_SLAB, blk), BF16),
                        pltpu.VMEM((n_pos, LANES), BF16),
                        pltpu.VMEM((n_blk, N_HEADS, blk), F32),
                        pltpu.VMEM((N_HEADS, LANES), F32),
                        pltpu.VMEM((N_HEADS, 2 * LANES, blk), BF16),
                        pltpu.VMEM((N_HEADS, blk), F32),
                        pltpu.VMEM((N_HEADS * V_SLAB, blk), F32),
                        pltpu.VMEM((N_HEADS, blk + N_META, blk), F32),
                        pltpu.VMEM((D_ATTN, LANES), F32)],
        compiler_params=_params(1), name="prompt_attn",
    )(qt, kt, vt, lft, lft_meta, lfp, lfp_meta, q_gain, k_gain)


def _sample_attn_kernel(q_ref, kn_ref, vn_ref, lf_ref, lft_ref, ckt_ref, cvt_ref, clft_ref,
                        o_ref, crow, *, past, dec):
    blk = ATTN_BLOCK
    n_keys = past + LANES

    upper, _ = _triangles(blk)
    off = jnp.zeros((N_HEADS, 1), F32)
    for j in range(past // blk):
        loc = _cumsum_lanes(clft_ref[:, j * blk:(j + 1) * blk], upper) + off
        crow[:, j * blk:(j + 1) * blk] = loc
        off = loc[:, blk - 1:blk]
    crow[:, 0:past] = crow[:, 0:past] - off

    cq_c = _cumsum_few(lf_ref[:, 0:N_HEADS], axis=0)
    cq_r = _cumsum_few(lft_ref[...], axis=1)
    crow[:, past:] = jnp.full((N_HEADS, LANES), MASKED_BIAS, F32)
    crow[:, past:past + dec] = cq_r

    q = q_ref[...]
    lane_head = lax.broadcasted_iota(jnp.int32, (dec, D_ATTN), 1) // HEAD_DIM
    q_exp = jnp.concatenate(
        [jnp.where(lane_head == h, q, jnp.zeros_like(q)) for h in range(N_HEADS)], axis=0)
    pad_rows = jnp.zeros((LANES - dec, D_ATTN), BF16)
    k_new = jnp.concatenate([kn_ref[...].astype(BF16), pad_rows], axis=0)
    v_new = jnp.concatenate([vn_ref[...].astype(BF16), pad_rows], axis=0)
    s_all = jnp.concatenate(
        [_dot(q_exp, ckt_ref[...].astype(BF16)), _dot_nt(q_exp, k_new)], axis=1)

    kpos = lax.broadcasted_iota(jnp.int32, (dec, n_keys), 1)
    qpos = past + lax.broadcasted_iota(jnp.int32, (dec, n_keys), 0)
    visible = kpos <= qpos
    probs = []
    norms = []
    for h in range(N_HEADS):
        s = s_all[h * dec:(h + 1) * dec, :] + cq_c[:, h:h + 1] - crow[h:h + 1, :]
        s = jnp.where(visible, s, -jnp.inf)
        p = jnp.exp(s - jnp.max(s, axis=-1, keepdims=True))
        norms.append(jnp.sum(p, axis=-1, keepdims=True))
        probs.append(p.astype(BF16))
    p_all = jnp.concatenate(probs, axis=0)
    o_all = (_dot_nt(p_all[:, 0:past], cvt_ref[...].astype(BF16))
             + _dot(p_all[:, past:], v_new))
    out = jnp.zeros((dec, D_ATTN), F32)
    for h in range(N_HEADS):
        o = o_all[h * dec:(h + 1) * dec, :] / norms[h]
        out = out + jnp.where(lane_head == h, o, 0.0)
    o_ref[...] = out.astype(BF16)


def _sample_attention(q, k_new, v_new, lfp, lft, cache_kt, cache_vt, cache_lft):
    b, dec, _ = q.shape
    past = cache_kt.shape[2]
    per_b = lambda *shape: pl.BlockSpec((None,) + shape, lambda i: (i,) + (0,) * len(shape))
    kern = functools.partial(_sample_attn_kernel, past=past, dec=dec)
    return pl.pallas_call(
        kern, grid=(b,),
        in_specs=[per_b(dec, D_ATTN), per_b(dec, D_ATTN), per_b(dec, D_ATTN),
                  per_b(dec, LANES), per_b(N_HEADS, dec),
                  per_b(D_ATTN, past), per_b(D_ATTN, past), per_b(N_HEADS, past)],
        out_specs=per_b(dec, D_ATTN),
        out_shape=jax.ShapeDtypeStruct((b, dec, D_ATTN), BF16),
        scratch_shapes=[pltpu.VMEM((N_HEADS, past + LANES), F32)],
        compiler_params=_params(1), name="sample_attn",
    )(q, k_new, v_new, lfp, lft, cache_kt, cache_vt, cache_lft)


def _mlp_kernel(x_ref, conv_ref, attn_ref, gate_ref, wbc_ref, wba_ref, wo_ref, g2_ref,
                wup_ref, wdn_ref, y_ref):
    ya = _dot(conv_ref[...], wbc_ref[...])
    yb = _dot(attn_ref[...], wba_ref[...])
    merged = (gate_ref[:, 0:D_MODEL].astype(F32) * ya
              + gate_ref[:, D_MODEL:2 * D_MODEL].astype(F32) * yb)
    h = x_ref[...] + _dot(merged.astype(BF16), wo_ref[...])
    ms = jnp.mean(h * h, axis=-1, keepdims=True)
    hn = (h * lax.rsqrt(ms + EPS) * g2_ref[...]).astype(BF16)
    acc = h
    for c in range(D_FF // D_MODEL):
        cols = slice(c * D_MODEL, (c + 1) * D_MODEL)
        a = jnp.maximum(_dot(hn, wup_ref[:, cols]), 0.0)
        acc = acc + _dot((a * a).astype(BF16), wdn_ref[cols, :])
    y_ref[...] = acc


def _merge_mlp(x2d, conv, attn, gates, wts, *, rows):
    n_rows = x2d.shape[0]
    row_spec = lambda width: pl.BlockSpec((rows, width), lambda i: (i, 0))
    return pl.pallas_call(
        _mlp_kernel, grid=(n_rows // rows,),
        in_specs=[row_spec(D_MODEL), row_spec(D_CONV), row_spec(D_ATTN), row_spec(2 * D_MODEL)]
        + [_const_spec(w.shape) for w in wts],
        out_specs=row_spec(D_MODEL),
        out_shape=jax.ShapeDtypeStruct((n_rows, D_MODEL), F32),
        compiler_params=_params(1), name="merge_mlp",
    )(x2d, conv, attn, gates, *wts)


def kernel(x_prompt, x_sample, cache_k, cache_v, cache_logf, state_conv, meta,
           norm1_g, w_in, b_f, conv_w, conv_b, q_norm_g, k_norm_g,
           w_br_conv, w_br_attn, w_out, norm2_g, w_up, w_down):
    b, seq, _ = x_prompt.shape
    db, dec, _ = x_sample.shape
    past = cache_k.shape[2]
    length = N_META + seq
    n_main = 3 * D_CONV + 3 * D_ATTN
    q0 = 3 * D_CONV

    wt = w_in[0].T
    wt_fl = wt[n_main:n_main + N_HEADS].astype(BF16)
    head_of = jnp.arange(D_ATTN) // HEAD_DIM
    qg = jnp.tile(q_norm_g[0], N_HEADS)
    kg = jnp.tile(k_norm_g[0], N_HEADS)
    proj_wts = (
        norm1_g[0][None, :],
        wt[0:q0].astype(BF16), wt[n_main + N_HEADS:].astype(BF16), wt[q0:n_main].astype(BF16),
        jnp.tile(wt_fl, (LANES // N_HEADS, 1)),
        jnp.tile(b_f[0], LANES // N_HEADS)[None, :],
        conv_w[0], conv_b[0][None, :],
        qg[:, None], kg[:, None],
    )
    mlp_wts = (w_br_conv[0].astype(BF16), w_br_attn[0].astype(BF16), w_out[0].astype(BF16),
               norm2_g[0][None, :], w_up[0].astype(BF16), w_down[0].astype(BF16))

    x_small = jnp.concatenate([meta, x_sample.reshape(db * dec, D_MODEL)], axis=0)
    left_small = jnp.concatenate(
        [jnp.zeros((1, CONV_W - 1, D_CONV), F32), state_conv[0]], axis=0)
    (conv_s, q_s, k_s, v_s, kt_s, vt_s, lft_s, lfp_s, gate_s, zlast_s) = _project_small(
        x_small, left_small, proj_wts,
        (wt_fl, b_f[0][:, None], qg[None, :], kg[None, :],
         (head_of[:, None] == head_of[None, :]).astype(BF16)))

    left_p = jnp.broadcast_to(zlast_s[0:1], (b, CONV_W - 1, D_CONV))
    (conv_p, qt_p, kt_p, vt_p, lftp_p, lft_p, lfp_p, gate_p, zlast_p) = _project_prompt(
        x_prompt.reshape(b * seq, D_MODEL), left_p, proj_wts, (kt_s, vt_s, lft_s),
        b=b, seq=seq)

    attn_p = _prompt_attention(
        qt_p, kt_p, vt_p, lft_p, lft_s[:, :N_META], lfp_p.reshape(b, seq, LANES),
        lfp_s[:N_META], q_norm_g, k_norm_g)
    y_prompt = _merge_mlp(x_prompt.reshape(b * seq, D_MODEL), conv_p,
                          attn_p.reshape(b * seq, D_ATTN), gate_p, mlp_wts, rows=MLP_ROWS)

    k_new = k_s[N_META:].reshape(db, dec, D_ATTN)
    v_new = v_s[N_META:].reshape(db, dec, D_ATTN)
    lf_new = lfp_s[N_META:, :N_HEADS].reshape(db, dec, N_HEADS)
    cache_kt = jnp.transpose(cache_k[0], (0, 2, 3, 1)).reshape(db, D_ATTN, past)
    cache_vt = jnp.transpose(cache_v[0], (0, 2, 3, 1)).reshape(db, D_ATTN, past)
    attn_s = _sample_attention(
        q_s[N_META:].reshape(db, dec, D_ATTN), k_new, v_new,
        lfp_s[N_META:].reshape(db, dec, LANES), jnp.swapaxes(lf_new, 1, 2),
        cache_kt, cache_vt, jnp.swapaxes(cache_logf[0], 1, 2))
    y_sample = _merge_mlp(x_sample.reshape(db * dec, D_MODEL), conv_s[N_META:],
                          attn_s.reshape(db * dec, D_ATTN), gate_s[N_META:], mlp_wts,
                          rows=db * dec)

    def heads_last(t):
        return jnp.transpose(t.reshape(b, N_HEADS, HEAD_DIM, length), (0, 3, 1, 2))[None]

    return (y_prompt.reshape(b, seq, D_MODEL),
            y_sample.reshape(db, dec, D_MODEL),
            heads_last(kt_p),
            heads_last(vt_p),
            jnp.swapaxes(lftp_p, 1, 2)[None],
            zlast_p[None],
            k_new.reshape(1, db, dec, N_HEADS, HEAD_DIM),
            v_new.reshape(1, db, dec, N_HEADS, HEAD_DIM),
            lf_new[None],
            zlast_s[1:][None])
```

```python
import functools

import jax
import jax.numpy as jnp
from jax import lax
from jax.experimental import pallas as pl
from jax.experimental.pallas import tpu as pltpu

D_MODEL = 1024
D_CONV = D_MODEL // 2
CONV_W = 3
N_HEADS = 8
HEAD_DIM = 64
D_ATTN = N_HEADS * HEAD_DIM
D_FF = 4 * D_MODEL
N_META = 16
EPS = 1e-6
ATTN_SCALE = HEAD_DIM ** -0.5

F32 = jnp.float32
BF16 = jnp.bfloat16

VMEM_LIMIT_BYTES = 56 * 1024 * 1024
LANES = 128
PROJ_ROWS = 1024
MLP_ROWS = 512
ATTN_BLOCK = 256
MASKED_BIAS = 1e30
ZPAD = 8
N_SPLIT = 3
V_SLAB = HEAD_DIM + 16
LOG2E = 1.4426950408889634
SKIP_LOG2 = 40.0
NORM_SLACK = 1.02


def _dot(a, b):
    return jnp.dot(a, b, preferred_element_type=F32)


def _dot_nt(a, b):
    return lax.dot_general(a, b, (((1,), (1,)), ((), ())), preferred_element_type=F32)


def _log_sigmoid(x):
    return jnp.minimum(x, 0.0) - jnp.log1p(jnp.exp(-jnp.abs(x)))


def _cumsum_few(x, axis):
    n = x.shape[axis]
    idx = lax.broadcasted_iota(jnp.int32, x.shape, axis)
    out = jnp.zeros(x.shape, F32)
    for i in range(n):
        term = x[i:i + 1, :] if axis == 0 else x[:, i:i + 1]
        out = out + jnp.where(idx >= i, term, 0.0)
    return out


def _triangles(n):
    r = lax.broadcasted_iota(jnp.int32, (n, n), 0)
    c = lax.broadcasted_iota(jnp.int32, (n, n), 1)
    upper = jnp.where(r <= c, 1.0, 0.0).astype(BF16)
    lower = jnp.where(r >= c, 1.0, 0.0).astype(BF16)
    return upper, lower


def _split3(c):
    hi = c.astype(BF16).astype(F32)
    r1 = c - hi
    mid = r1.astype(BF16).astype(F32)
    return hi, mid, r1 - mid


def _cumsum_rows(tri_lower, x):
    w = x.shape[1]
    pieces = jnp.concatenate(_split3(x), axis=1).astype(BF16)
    y = _dot(tri_lower, pieces)
    return y[:, 0:w] + y[:, w:2 * w] + y[:, 2 * w:3 * w]


def _cumsum_lanes(x, tri_upper):
    h = x.shape[0]
    pieces = jnp.concatenate(_split3(x), axis=0).astype(BF16)
    y = _dot(pieces, tri_upper)
    return y[0:h] + y[h:2 * h] + y[2 * h:3 * h]


def _const_spec(shape):
    nd = len(shape)
    return pl.BlockSpec(shape, lambda *_: (0,) * nd, pipeline_mode=pl.Buffered(1))


def _params(n_axes):
    return pltpu.CompilerParams(
        dimension_semantics=("arbitrary",) * n_axes,
        vmem_limit_bytes=VMEM_LIMIT_BYTES)


def _rms_rows(x, g_row):
    ms = jnp.mean(x * x, axis=-1, keepdims=True)
    return (x * lax.rsqrt(ms + EPS) * g_row).astype(BF16)


def _head_norm_t(ut, g_col):
    out = []
    for h in range(N_HEADS):
        blk = ut[h * HEAD_DIM:(h + 1) * HEAD_DIM, :]
        ms = jnp.mean(blk * blk, axis=0, keepdims=True)
        out.append(blk * lax.rsqrt(ms + EPS) * g_col[h * HEAD_DIM:(h + 1) * HEAD_DIM, :])
    return jnp.concatenate(out, axis=0)


def _short_conv(xn, wa_ref, cw_ref, cb_ref, zbuf, zlast_ref, n_seg, seg_len):
    cb = _dot_nt(xn, wa_ref[0:D_CONV, :])
    z = (_dot_nt(xn, wa_ref[D_CONV:2 * D_CONV, :])
         * _dot_nt(xn, wa_ref[2 * D_CONV:3 * D_CONV, :]))
    w0 = cw_ref[0:1, :]
    w1 = cw_ref[1:2, :]
    w2 = cw_ref[2:3, :]
    out = []
    for s in range(n_seg):
        r0 = s * seg_len
        zs = z[r0:r0 + seg_len]
        zbuf[s, ZPAD:ZPAD + seg_len, :] = zs
        zm1 = zbuf[s, ZPAD - 1:ZPAD - 1 + seg_len, :]
        zm2 = zbuf[s, ZPAD - 2:ZPAD - 2 + seg_len, :]
        y = zm2 * w0 + zm1 * w1 + zs * w2 + cb_ref[...]
        out.append(cb[r0:r0 + seg_len] * y)
        tail = zbuf[s, ZPAD + seg_len - 2:ZPAD + seg_len, :]
        zlast_ref[s] = tail
        zbuf[s, ZPAD - 2:ZPAD, :] = tail
    return out


def _qkv_kernel(x_ref, g1_ref, wqkv_ref, wfl_ref, bfr_ref, qgc_ref, kgc_ref,
                ktm_ref, vtm_ref, lftm_ref,
                qt_ref, kt_ref, vt_ref, lftp_ref, lft_ref, lfp_ref,
                kcar, vcar, lcar, *, tiles_per_seq):
    step = pl.program_id(0) % (tiles_per_seq + 1)

    def shifted_store(out_ref, car_ref, tile):
        rolled = pltpu.roll(tile, N_META, axis=1)
        lane = lax.broadcasted_iota(jnp.int32, (tile.shape[0], LANES), 1)
        out_ref[:, 0:LANES] = jnp.where(lane < N_META, car_ref[...], rolled[:, 0:LANES])
        out_ref[:, LANES:] = rolled[:, LANES:]
        car_ref[...] = rolled[:, 0:LANES]

    @pl.when(step == 0)
    def _():
        kcar[...] = ktm_ref[:, 0:LANES]
        vcar[...] = vtm_ref[:, 0:LANES]
        lcar[...] = lftm_ref[:, 0:LANES]

    @pl.when(step == tiles_per_seq)
    def _():
        for out_ref, car_ref in ((kt_ref, kcar), (vt_ref, vcar), (lftp_ref, lcar)):
            out_ref[...] = jnp.zeros(out_ref.shape, F32)
            out_ref[:, 0:LANES] = car_ref[...]

    @pl.when(step < tiles_per_seq)
    def _():
        xn = _rms_rows(x_ref[...], g1_ref[...])

        def feature_major(j):
            return _dot_nt(wqkv_ref[j * D_ATTN:(j + 1) * D_ATTN, :], xn)

        lfp = _log_sigmoid(_dot_nt(xn, wfl_ref[...]) + bfr_ref[...])
        lfp_ref[...] = lfp
        lft = lfp.T[0:N_HEADS, :]
        lft_ref[...] = lft
        qt = _head_norm_t(feature_major(0), qgc_ref[...])
        qt_ref[...] = (qt * (ATTN_SCALE * LOG2E)).astype(BF16)
        shifted_store(kt_ref, kcar, _head_norm_t(feature_major(1), kgc_ref[...]))
        shifted_store(vt_ref, vcar, feature_major(2))
        shifted_store(lftp_ref, lcar, lft)


def _project_qkv(x2d, wts, meta_cols, *, b, seq):
    rows = PROJ_ROWS
    tiles = seq // rows
    steps = tiles + 1
    length = N_META + seq

    def tok_block(i):
        return jnp.minimum(i % steps, tiles - 1)

    row_spec = lambda width: pl.BlockSpec(
        (rows, width), lambda i: ((i // steps) * tiles + tok_block(i), 0))
    tok_spec = lambda feat: pl.BlockSpec(
        (None, feat, rows), lambda i: (i // steps, 0, tok_block(i)))
    pos_spec = lambda feat: pl.BlockSpec(
        (None, feat, rows), lambda i: (i // steps, 0, i % steps))
    out_shape = (
        jax.ShapeDtypeStruct((b, D_ATTN, seq), BF16),
        jax.ShapeDtypeStruct((b, D_ATTN, length), F32),
        jax.ShapeDtypeStruct((b, D_ATTN, length), F32),
        jax.ShapeDtypeStruct((b, N_HEADS, length), F32),
        jax.ShapeDtypeStruct((b, N_HEADS, seq), F32),
        jax.ShapeDtypeStruct((b * seq, LANES), F32),
    )
    out_specs = (tok_spec(D_ATTN), pos_spec(D_ATTN), pos_spec(D_ATTN), pos_spec(N_HEADS),
                 tok_spec(N_HEADS), row_spec(LANES))
    kern = functools.partial(_qkv_kernel, tiles_per_seq=tiles)
    return pl.pallas_call(
        kern, grid=(b * steps,),
        in_specs=[row_spec(D_MODEL)] + [_const_spec(w.shape) for w in wts + meta_cols],
        out_specs=out_specs, out_shape=out_shape,
        scratch_shapes=[pltpu.VMEM((D_ATTN, LANES), F32),
                        pltpu.VMEM((D_ATTN, LANES), F32),
                        pltpu.VMEM((N_HEADS, LANES), F32)],
        compiler_params=_params(1), name="proj_qkv",
    )(x2d, *wts, *meta_cols)


def _proj_small_kernel(x_ref, left_ref, g1_ref, wqkv_ref, wfl_ref, bfr_ref, qgc_ref, kgc_ref,
                       wa_ref, wgl_ref, cw_ref, cb_ref, wflt_ref, bfc_ref, qgr_ref, kgr_ref,
                       bd_ref,
                       conv_ref, q_ref, k_ref, v_ref, kt_ref, vt_ref, lft_ref, lfp_ref,
                       gate_ref, zlast_ref, zbuf, *, n_seg, seg_len):
    xn = _rms_rows(x_ref[...], g1_ref[...])
    zbuf[:, ZPAD - 2:ZPAD, :] = left_ref[...]
    conv = _short_conv(xn, wa_ref, cw_ref, cb_ref, zbuf, zlast_ref, n_seg, seg_len)
    for s in range(n_seg):
        conv_ref[s * seg_len:(s + 1) * seg_len, :] = conv[s].astype(BF16)

    def rows_major(j):
        return _dot_nt(xn, wqkv_ref[j * D_ATTN:(j + 1) * D_ATTN, :])

    def feature_major(j):
        return _dot_nt(wqkv_ref[j * D_ATTN:(j + 1) * D_ATTN, :], xn)

    def head_norm(u, g_row):
        ssq = _dot((u * u).astype(BF16), bd_ref[...])
        return u * lax.rsqrt(ssq * (1.0 / HEAD_DIM) + EPS) * g_row

    q_ref[...] = (head_norm(rows_major(0), qgr_ref[...]) * ATTN_SCALE).astype(BF16)
    k_ref[...] = head_norm(rows_major(1), kgr_ref[...])
    v_ref[...] = rows_major(2)
    kt_ref[...] = _head_norm_t(feature_major(1), kgc_ref[...])
    vt_ref[...] = feature_major(2)
    lfp_ref[...] = _log_sigmoid(_dot_nt(xn, wfl_ref[...]) + bfr_ref[...])
    lft_ref[...] = _log_sigmoid(_dot_nt(wflt_ref[...], xn) + bfc_ref[...])
    gate_ref[...] = jax.nn.sigmoid(_dot_nt(xn, wgl_ref[...])).astype(BF16)


def _project_small(x2d, left, wts):
    n_rows = x2d.shape[0]
    n_seq = left.shape[0]
    seg_len = n_rows // n_seq
    full = lambda *shape: pl.BlockSpec(shape, lambda i: (0,) * len(shape))
    out_shape = (
        jax.ShapeDtypeStruct((n_rows, D_CONV), BF16),
        jax.ShapeDtypeStruct((n_rows, D_ATTN), BF16),
        jax.ShapeDtypeStruct((n_rows, D_ATTN), F32),
        jax.ShapeDtypeStruct((n_rows, D_ATTN), F32),
        jax.ShapeDtypeStruct((D_ATTN, n_rows), F32),
        jax.ShapeDtypeStruct((D_ATTN, n_rows), F32),
        jax.ShapeDtypeStruct((N_HEADS, n_rows), F32),
        jax.ShapeDtypeStruct((n_rows, LANES), F32),
        jax.ShapeDtypeStruct((n_rows, 2 * D_MODEL), BF16),
        jax.ShapeDtypeStruct((n_seq, CONV_W - 1, D_CONV), F32),
    )
    kern = functools.partial(_proj_small_kernel, n_seg=n_seq, seg_len=seg_len)
    return pl.pallas_call(
        kern, grid=(1,),
        in_specs=[full(*x2d.shape), full(*left.shape)] + [_const_spec(w.shape) for w in wts],
        out_specs=tuple(full(*s.shape) for s in out_shape), out_shape=out_shape,
        scratch_shapes=[pltpu.VMEM((n_seq, seg_len + ZPAD, D_CONV), F32)],
        compiler_params=_params(1), name="proj_small",
    )(x2d, left, *wts)


def _prompt_attn_kernel(qt_ref, kt_ref, vt_ref, lft_ref, lftm_ref, lfp_ref, lfpm_ref,
                        qg_ref, kg_ref, o_ref, kpos, vb, kbias, crow, cend, qcat, m_s, acc_s,
                        sbuf, tail, *, seq):
    blk = ATTN_BLOCK
    n_blk = seq // blk
    length = N_META + seq
    n_pos = kpos.shape[0]
    n_bias = N_SPLIT * N_HEADS

    for j in range(n_blk):
        kpos[j * blk:(j + 1) * blk, :] = kt_ref[:, j * blk:(j + 1) * blk].T.astype(BF16)
    tail[...] = jnp.zeros(tail.shape, F32)
    tail[:, 0:N_META] = kt_ref[:, seq:length]
    kpos[seq:seq + LANES, :] = tail[...].T.astype(BF16)
    kpos[seq + LANES:, :] = jnp.zeros((n_pos - seq - LANES, D_ATTN), BF16)

    ones_row = (lax.broadcasted_iota(jnp.int32, (V_SLAB - HEAD_DIM, blk), 0) == 0).astype(BF16)
    vb[n_blk] = jnp.zeros((N_HEADS * V_SLAB, blk), BF16)
    for h in range(N_HEADS):
        rows = slice(h * HEAD_DIM, (h + 1) * HEAD_DIM)
        slab = slice(h * V_SLAB, h * V_SLAB + HEAD_DIM)
        for j in range(n_blk + 1):
            if j < n_blk:
                vb[j, slab, :] = vt_ref[rows, j * blk:(j + 1) * blk].astype(BF16)
            else:
                vb[j, slab, 0:N_META] = vt_ref[rows, seq:length].astype(BF16)
            vb[j, h * V_SLAB + HEAD_DIM:(h + 1) * V_SLAB, :] = ones_row

    qk_bound = (NORM_SLACK * HEAD_DIM * ATTN_SCALE * LOG2E
                * jnp.max(jnp.abs(qg_ref[...]), axis=1, keepdims=True)
                * jnp.max(jnp.abs(kg_ref[...]), axis=1, keepdims=True))

    upper, lower = _triangles(blk)

    def store_kbias(rows, c_col):
        hi, mid, lo = _split3(c_col * LOG2E)
        lane = lax.broadcasted_iota(jnp.int32, c_col.shape, 1)
        grp = lane // N_HEADS
        part = jnp.where(grp == 0, hi, jnp.where(grp == 1, mid, lo))
        kbias[rows, :] = jnp.where(lane < n_bias, -part,
                                   jnp.where(lane < 2 * n_bias, 1.0, 0.0)).astype(BF16)

    off_c = jnp.zeros((1, LANES), F32)
    for j in range(n_blk):
        if j == 0:
            lf_blk = jnp.concatenate([lfpm_ref[...], lfp_ref[0:blk - N_META, :]], axis=0)
        else:
            lf_blk = lfp_ref[j * blk - N_META:(j + 1) * blk - N_META, :]
        c_col = _cumsum_rows(lower, lf_blk) + off_c
        store_kbias(slice(j * blk, (j + 1) * blk), c_col)
        off_c = c_col[blk - 1:blk, :]
    c_col = _cumsum_few(lfp_ref[seq - N_META:seq, :], axis=0) + off_c
    store_kbias(slice(seq, length), c_col)
    kbias[length:, :] = jnp.zeros((n_pos - length, LANES), BF16)

    meta_r = _cumsum_few(lftm_ref[...], axis=1)
    off_r = meta_r[:, N_META - 1:N_META]
    lane_h = lax.broadcasted_iota(jnp.int32, (N_HEADS, LANES), 1)
    c_end = jnp.zeros((N_HEADS, LANES), F32)
    for j in range(n_blk):
        c_row = _cumsum_lanes(lft_ref[:, j * blk:(j + 1) * blk], upper) + off_r
        crow[j] = c_row * LOG2E
        off_r = c_row[:, blk - 1:blk]
        c_end = jnp.where(lane_h == j, off_r * LOG2E, c_end)
    cend[...] = c_end

    row128 = lax.broadcasted_iota(jnp.int32, (LANES, blk), 0)
    krow = lax.broadcasted_iota(jnp.int32, (blk + N_META, blk), 0)
    qcol = lax.broadcasted_iota(jnp.int32, (blk + N_META, blk), 1)

    def q_block(t, _):
        tok0 = pl.multiple_of(t * blk, blk)
        c_q = crow[t]
        hi, mid, lo = _split3(c_q)
        bias_rows = jnp.concatenate(
            [jnp.ones((n_bias, blk), F32), hi, mid, lo,
             jnp.zeros((LANES - 2 * n_bias, blk), F32)], axis=0)
        for h in range(N_HEADS):
            pair = qt_ref[(h // 2) * LANES:(h // 2 + 1) * LANES, pl.ds(tok0, blk)]
            in_head = (row128 // HEAD_DIM) == (h % 2)
            qcat[h, 0:LANES, :] = jnp.where(in_head, pair, jnp.zeros_like(pair))
            qcat[h, LANES:, :] = jnp.where(row128 % N_HEADS == h, bias_rows, 0.0).astype(BF16)
        m_s[...] = jnp.full(m_s.shape, -jnp.inf, F32)
        acc_s[...] = jnp.zeros(acc_s.shape, F32)

        gap = 2.0 * qk_bound + c_q[:, 0:1] - cend[...]
        needed = jnp.logical_and(gap >= -SKIP_LOG2, lane_h < t)
        n_needed = jnp.max(
            jnp.sum(jnp.where(needed, 1.0, 0.0), axis=1, keepdims=True)).astype(jnp.int32)

        def key_block(pos0, n_rows, pv, visible):
            m_blk = []
            for h in range(N_HEADS):
                g = h // 2
                kc = jnp.concatenate([kpos[pl.ds(pos0, n_rows), g * LANES:(g + 1) * LANES],
                                      kbias[pl.ds(pos0, n_rows), :]], axis=1)
                s = _dot(kc, qcat[h])
                if visible is not None:
                    s = jnp.where(visible, s, -jnp.inf)
                sbuf[h, 0:n_rows, :] = s
                m_blk.append(jnp.max(s, axis=0, keepdims=True))
            for h in range(N_HEADS):
                slab = slice(h * V_SLAB, (h + 1) * V_SLAB)
                m_old = m_s[h:h + 1, :]
                m_new = jnp.maximum(m_old, m_blk[h])
                alpha = jnp.exp2(m_old - m_new)
                p = jnp.exp2(sbuf[h, 0:n_rows, :] - m_new)
                m_s[h:h + 1, :] = m_new
                acc_s[slab, :] = alpha * acc_s[slab, :] + pv(slab, p.astype(BF16))

        def full_block(j, _):
            key_block(pl.multiple_of(j * blk, blk), blk,
                      lambda slab, p: _dot(vb[j, slab, :], p), None)
            return 0

        def own_pv(slab, p):
            spill = jnp.concatenate([p[blk:], jnp.zeros((LANES - N_META, blk), BF16)], axis=0)
            return _dot(vb[t, slab, :], p[0:blk]) + _dot(vb[t + 1, slab, 0:LANES], spill)

        key_block(tok0, blk + N_META, own_pv, krow <= qcol + N_META)
        lax.fori_loop(t - n_needed, t, full_block, 0)

        o_t = []
        for h in range(N_HEADS):
            norm = acc_s[h * V_SLAB + HEAD_DIM:h * V_SLAB + HEAD_DIM + 1, :]
            o_t.append(acc_s[h * V_SLAB:h * V_SLAB + HEAD_DIM, :] * (1.0 / norm))
        o_ref[pl.ds(tok0, blk), :] = jnp.concatenate(o_t, axis=0).T.astype(BF16)
        return 0

    lax.fori_loop(0, n_blk, q_block, 0)


def _prompt_attention(qt, kt, vt, lft, lft_meta, lfp, lfp_meta, q_gain, k_gain):
    b, _, seq = qt.shape
    length = vt.shape[2]
    blk = ATTN_BLOCK
    n_blk = seq // blk
    n_pos = (n_blk + 1) * blk
    per_b = lambda *shape: pl.BlockSpec((None,) + shape, lambda i: (i,) + (0,) * len(shape))
    kern = functools.partial(_prompt_attn_kernel, seq=seq)
    return pl.pallas_call(
        kern, grid=(b,),
        in_specs=[per_b(D_ATTN, seq), per_b(D_ATTN, length), per_b(D_ATTN, length),
                  per_b(N_HEADS, seq), _const_spec(lft_meta.shape),
                  per_b(seq, LANES), _const_spec(lfp_meta.shape),
                  _const_spec(q_gain.shape), _const_spec(k_gain.shape)],
        out_specs=per_b(seq, D_ATTN),
        out_shape=jax.ShapeDtypeStruct((b, seq, D_ATTN), BF16),
        scratch_shapes=[pltpu.VMEM((n_pos, D_ATTN), BF16),
                        pltpu.VMEM((n_blk + 1, N_HEADS * V_SLAB, blk), BF16),
                        pltpu.VMEM((n_pos, LANES), BF16),
                        pltpu.VMEM((n_blk, N_HEADS, blk), F32),
                        pltpu.VMEM((N_HEADS, LANES), F32),
                        pltpu.VMEM((N_HEADS, 2 * LANES, blk), BF16),
                        pltpu.VMEM((N_HEADS, blk), F32),
                        pltpu.VMEM((N_HEADS * V_SLAB, blk), F32),
                        pltpu.VMEM((N_HEADS, blk + N_META, blk), F32),
                        pltpu.VMEM((D_ATTN, LANES), F32)],
        compiler_params=_params(1), name="prompt_attn",
    )(qt, kt, vt, lft, lft_meta, lfp, lfp_meta, q_gain, k_gain)


def _sample_attn_kernel(q_ref, kn_ref, vn_ref, lf_ref, lft_ref, ckt_ref, cvt_ref, clft_ref,
                        o_ref, crow, *, past, dec):
    blk = ATTN_BLOCK
    n_keys = past + LANES

    upper, _ = _triangles(blk)
    off = jnp.zeros((N_HEADS, 1), F32)
    for j in range(past // blk):
        loc = _cumsum_lanes(clft_ref[:, j * blk:(j + 1) * blk], upper) + off
        crow[:, j * blk:(j + 1) * blk] = loc
        off = loc[:, blk - 1:blk]
    crow[:, 0:past] = crow[:, 0:past] - off

    cq_c = _cumsum_few(lf_ref[:, 0:N_HEADS], axis=0)
    cq_r = _cumsum_few(lft_ref[...], axis=1)
    crow[:, past:] = jnp.full((N_HEADS, LANES), MASKED_BIAS, F32)
    crow[:, past:past + dec] = cq_r

    q = q_ref[...]
    lane_head = lax.broadcasted_iota(jnp.int32, (dec, D_ATTN), 1) // HEAD_DIM
    q_exp = jnp.concatenate(
        [jnp.where(lane_head == h, q, jnp.zeros_like(q)) for h in range(N_HEADS)], axis=0)
    pad_rows = jnp.zeros((LANES - dec, D_ATTN), BF16)
    k_new = jnp.concatenate([kn_ref[...].astype(BF16), pad_rows], axis=0)
    v_new = jnp.concatenate([vn_ref[...].astype(BF16), pad_rows], axis=0)
    s_all = jnp.concatenate(
        [_dot(q_exp, ckt_ref[...].astype(BF16)), _dot_nt(q_exp, k_new)], axis=1)

    kpos = lax.broadcasted_iota(jnp.int32, (dec, n_keys), 1)
    qpos = past + lax.broadcasted_iota(jnp.int32, (dec, n_keys), 0)
    visible = kpos <= qpos
    probs = []
    norms = []
    for h in range(N_HEADS):
        s = s_all[h * dec:(h + 1) * dec, :] + cq_c[:, h:h + 1] - crow[h:h + 1, :]
        s = jnp.where(visible, s, -jnp.inf)
        p = jnp.exp(s - jnp.max(s, axis=-1, keepdims=True))
        norms.append(jnp.sum(p, axis=-1, keepdims=True))
        probs.append(p.astype(BF16))
    p_all = jnp.concatenate(probs, axis=0)
    o_all = (_dot_nt(p_all[:, 0:past], cvt_ref[...].astype(BF16))
             + _dot(p_all[:, past:], v_new))
    out = jnp.zeros((dec, D_ATTN), F32)
    for h in range(N_HEADS):
        o = o_all[h * dec:(h + 1) * dec, :] / norms[h]
        out = out + jnp.where(lane_head == h, o, 0.0)
    o_ref[...] = out.astype(BF16)


def _sample_attention(q, k_new, v_new, lfp, lft, cache_kt, cache_vt, cache_lft):
    b, dec, _ = q.shape
    past = cache_kt.shape[2]
    per_b = lambda *shape: pl.BlockSpec((None,) + shape, lambda i: (i,) + (0,) * len(shape))
    kern = functools.partial(_sample_attn_kernel, past=past, dec=dec)
    return pl.pallas_call(
        kern, grid=(b,),
        in_specs=[per_b(dec, D_ATTN), per_b(dec, D_ATTN), per_b(dec, D_ATTN),
                  per_b(dec, LANES), per_b(N_HEADS, dec),
                  per_b(D_ATTN, past), per_b(D_ATTN, past), per_b(N_HEADS, past)],
        out_specs=per_b(dec, D_ATTN),
        out_shape=jax.ShapeDtypeStruct((b, dec, D_ATTN), BF16),
        scratch_shapes=[pltpu.VMEM((N_HEADS, past + LANES), F32)],
        compiler_params=_params(1), name="sample_attn",
    )(q, k_new, v_new, lfp, lft, cache_kt, cache_vt, cache_lft)


def _merge_mlp_tail(x, conv_bf16, attn_ref, g_conv, g_attn, wbc_ref, wba_ref, wo_ref, g2_ref,
                    wup_ref, wdn_ref, y_ref):
    merged = g_conv * _dot(conv_bf16, wbc_ref[...]) + g_attn * _dot(attn_ref[...], wba_ref[...])
    h = x + _dot(merged.astype(BF16), wo_ref[...])
    hn = _rms_rows(h, g2_ref[...])
    acc = h
    for c in range(D_FF // D_MODEL):
        cols = slice(c * D_MODEL, (c + 1) * D_MODEL)
        a = jnp.maximum(_dot(hn, wup_ref[:, cols]), 0.0)
        acc = acc + _dot((a * a).astype(BF16), wdn_ref[cols, :])
    y_ref[...] = acc


def _branch_mlp_kernel(x_ref, left_ref, attn_ref, g1_ref, wa_ref, wgl_ref, cw_ref, cb_ref,
                       wbc_ref, wba_ref, wo_ref, g2_ref, wup_ref, wdn_ref,
                       y_ref, zlast_ref, zbuf, *, tiles_per_seq):
    @pl.when(pl.program_id(0) % tiles_per_seq == 0)
    def _():
        zbuf[:, ZPAD - 2:ZPAD, :] = left_ref[...]

    x = x_ref[...]
    xn = _rms_rows(x, g1_ref[...])
    conv = _short_conv(xn, wa_ref, cw_ref, cb_ref, zbuf, zlast_ref, 1, x.shape[0])[0]
    g_conv = jax.nn.sigmoid(_dot_nt(xn, wgl_ref[0:D_MODEL, :]))
    g_attn = jax.nn.sigmoid(_dot_nt(xn, wgl_ref[D_MODEL:2 * D_MODEL, :]))
    _merge_mlp_tail(x, conv.astype(BF16), attn_ref, g_conv, g_attn, wbc_ref, wba_ref, wo_ref,
                    g2_ref, wup_ref, wdn_ref, y_ref)


def _branch_mlp(x2d, left, attn, wts, *, rows, seq):
    n_rows = x2d.shape[0]
    tiles = seq // rows
    row_spec = lambda width: pl.BlockSpec((rows, width), lambda i: (i, 0))
    seq_spec = pl.BlockSpec((1, CONV_W - 1, D_CONV), lambda i: (i // tiles, 0, 0))
    kern = functools.partial(_branch_mlp_kernel, tiles_per_seq=tiles)
    return pl.pallas_call(
        kern, grid=(n_rows // rows,),
        in_specs=[row_spec(D_MODEL), seq_spec, row_spec(D_ATTN)]
        + [_const_spec(w.shape) for w in wts],
        out_specs=(row_spec(D_MODEL), seq_spec),
        out_shape=(jax.ShapeDtypeStruct((n_rows, D_MODEL), F32),
                   jax.ShapeDtypeStruct((n_rows // seq, CONV_W - 1, D_CONV), F32)),
        scratch_shapes=[pltpu.VMEM((1, rows + ZPAD, D_CONV), F32)],
        compiler_params=_params(1), name="branch_mlp",
    )(x2d, left, attn, *wts)


def _merge_mlp_kernel(x_ref, conv_ref, attn_ref, gate_ref, wbc_ref, wba_ref, wo_ref, g2_ref,
                      wup_ref, wdn_ref, y_ref):
    _merge_mlp_tail(x_ref[...], conv_ref[...], attn_ref,
                    gate_ref[:, 0:D_MODEL].astype(F32),
                    gate_ref[:, D_MODEL:2 * D_MODEL].astype(F32),
                    wbc_ref, wba_ref, wo_ref, g2_ref, wup_ref, wdn_ref, y_ref)


def _merge_mlp(x2d, conv, attn, gates, wts):
    full = lambda a: pl.BlockSpec(a.shape, lambda i: (0,) * a.ndim)
    return pl.pallas_call(
        _merge_mlp_kernel, grid=(1,),
        in_specs=[full(x2d), full(conv), full(attn), full(gates)]
        + [_const_spec(w.shape) for w in wts],
        out_specs=full(x2d),
        out_shape=jax.ShapeDtypeStruct(x2d.shape, F32),
        compiler_params=_params(1), name="merge_mlp",
    )(x2d, conv, attn, gates, *wts)


def kernel(x_prompt, x_sample, cache_k, cache_v, cache_logf, state_conv, meta,
           norm1_g, w_in, b_f, conv_w, conv_b, q_norm_g, k_norm_g,
           w_br_conv, w_br_attn, w_out, norm2_g, w_up, w_down):
    b, seq, _ = x_prompt.shape
    db, dec, _ = x_sample.shape
    past = cache_k.shape[2]
    length = N_META + seq
    n_main = 3 * D_CONV + 3 * D_ATTN
    q0 = 3 * D_CONV

    wt = w_in[0].T
    wt_conv = wt[0:q0].astype(BF16)
    wt_qkv = wt[q0:n_main].astype(BF16)
    wt_fl = wt[n_main:n_main + N_HEADS].astype(BF16)
    wt_gate = wt[n_main + N_HEADS:].astype(BF16)
    head_of = jnp.arange(D_ATTN) // HEAD_DIM
    qg = jnp.tile(q_norm_g[0], N_HEADS)
    kg = jnp.tile(k_norm_g[0], N_HEADS)
    g1 = norm1_g[0][None, :]
    qkv_wts = (g1, wt_qkv, jnp.tile(wt_fl, (LANES // N_HEADS, 1)),
               jnp.tile(b_f[0], LANES // N_HEADS)[None, :], qg[:, None], kg[:, None])
    conv_wts = (conv_w[0], conv_b[0][None, :])
    mlp_wts = (w_br_conv[0].astype(BF16), w_br_attn[0].astype(BF16), w_out[0].astype(BF16),
               norm2_g[0][None, :], w_up[0].astype(BF16), w_down[0].astype(BF16))

    x_small = jnp.concatenate([meta, x_sample.reshape(db * dec, D_MODEL)], axis=0)
    left_small = jnp.concatenate(
        [jnp.zeros((1, CONV_W - 1, D_CONV), F32), state_conv[0]], axis=0)
    (conv_s, q_s, k_s, v_s, kt_s, vt_s, lft_s, lfp_s, gate_s, zlast_s) = _project_small(
        x_small, left_small,
        qkv_wts + (wt_conv, wt_gate) + conv_wts
        + (wt_fl, b_f[0][:, None], qg[None, :], kg[None, :],
           (head_of[:, None] == head_of[None, :]).astype(BF16)))

    x_rows = x_prompt.reshape(b * seq, D_MODEL)
    (qt_p, kt_p, vt_p, lftp_p, lft_p, lfp_p) = _project_qkv(
        x_rows, qkv_wts, (kt_s, vt_s, lft_s), b=b, seq=seq)
    attn_p = _prompt_attention(
        qt_p, kt_p, vt_p, lft_p, lft_s[:, :N_META], lfp_p.reshape(b, seq, LANES),
        lfp_s[:N_META], q_norm_g, k_norm_g)
    left_p = jnp.broadcast_to(zlast_s[0:1], (b, CONV_W - 1, D_CONV))
    y_prompt, zlast_p = _branch_mlp(
        x_rows, left_p, attn_p.reshape(b * seq, D_ATTN),
        (g1, wt_conv, wt_gate) + conv_wts + mlp_wts, rows=MLP_ROWS, seq=seq)

    k_new = k_s[N_META:].reshape(db, dec, D_ATTN)
    v_new = v_s[N_META:].reshape(db, dec, D_ATTN)
    lf_new = lfp_s[N_META:, :N_HEADS].reshape(db, dec, N_HEADS)
    cache_kt = jnp.transpose(cache_k[0], (0, 2, 3, 1)).reshape(db, D_ATTN, past)
    cache_vt = jnp.transpose(cache_v[0], (0, 2, 3, 1)).reshape(db, D_ATTN, past)
    attn_s = _sample_attention(
        q_s[N_META:].reshape(db, dec, D_ATTN), k_new, v_new,
        lfp_s[N_META:].reshape(db, dec, LANES), jnp.swapaxes(lf_new, 1, 2),
        cache_kt, cache_vt, jnp.swapaxes(cache_logf[0], 1, 2))
    y_sample = _merge_mlp(x_sample.reshape(db * dec, D_MODEL), conv_s[N_META:],
                          attn_s.reshape(db * dec, D_ATTN), gate_s[N_META:], mlp_wts)

    def heads_last(t):
        return jnp.transpose(t.reshape(b, N_HEADS, HEAD_DIM, length), (0, 3, 1, 2))[None]

    return (y_prompt.reshape(b, seq, D_MODEL),
            y_sample.reshape(db, dec, D_MODEL),
            heads_last(kt_p),
            heads_last(vt_p),
            jnp.swapaxes(lftp_p, 1, 2)[None],
            zlast_p[None],
            k_new.reshape(1, db, dec, N_HEADS, HEAD_DIM),
            v_new.reshape(1, db, dec, N_HEADS, HEAD_DIM),
            lf_new[None],
            zlast_s[1:][None])
```

```python
import functools

import jax
import jax.numpy as jnp
from jax import lax
from jax.experimental import pallas as pl
from jax.experimental.pallas import tpu as pltpu

D_MODEL = 1024
D_CONV = D_MODEL // 2
CONV_W = 3
N_HEADS = 8
HEAD_DIM = 64
D_ATTN = N_HEADS * HEAD_DIM
D_FF = 4 * D_MODEL
N_META = 16
EPS = 1e-6
ATTN_SCALE = HEAD_DIM ** -0.5

F32 = jnp.float32
BF16 = jnp.bfloat16

VMEM_LIMIT_BYTES = 56 * 1024 * 1024
LANES = 128
PROJ_ROWS = 1024
MLP_ROWS = 512
ATTN_BLOCK = 256
MASKED_BIAS = 1e30
ZPAD = 8
N_SPLIT = 3
V_SLAB = HEAD_DIM + 16
LOG2E = 1.4426950408889634
SKIP_LOG2 = 40.0
NORM_SLACK = 1.02


def _dot(a, b):
    return jnp.dot(a, b, preferred_element_type=F32)


def _dot_nt(a, b):
    return lax.dot_general(a, b, (((1,), (1,)), ((), ())), preferred_element_type=F32)


def _log_sigmoid(x):
    return jnp.minimum(x, 0.0) - jnp.log1p(jnp.exp(-jnp.abs(x)))


def _cumsum_few(x, axis):
    n = x.shape[axis]
    idx = lax.broadcasted_iota(jnp.int32, x.shape, axis)
    out = jnp.zeros(x.shape, F32)
    for i in range(n):
        term = x[i:i + 1, :] if axis == 0 else x[:, i:i + 1]
        out = out + jnp.where(idx >= i, term, 0.0)
    return out


def _triangles(n):
    r = lax.broadcasted_iota(jnp.int32, (n, n), 0)
    c = lax.broadcasted_iota(jnp.int32, (n, n), 1)
    upper = jnp.where(r <= c, 1.0, 0.0).astype(BF16)
    lower = jnp.where(r >= c, 1.0, 0.0).astype(BF16)
    return upper, lower


def _split3(c):
    hi = c.astype(BF16).astype(F32)
    r1 = c - hi
    mid = r1.astype(BF16).astype(F32)
    return hi, mid, r1 - mid


def _cumsum_rows(tri_lower, x):
    w = x.shape[1]
    pieces = jnp.concatenate(_split3(x), axis=1).astype(BF16)
    y = _dot(tri_lower, pieces)
    return y[:, 0:w] + y[:, w:2 * w] + y[:, 2 * w:3 * w]


def _cumsum_lanes(x, tri_upper):
    h = x.shape[0]
    pieces = jnp.concatenate(_split3(x), axis=0).astype(BF16)
    y = _dot(pieces, tri_upper)
    return y[0:h] + y[h:2 * h] + y[2 * h:3 * h]


def _const_spec(shape):
    nd = len(shape)
    return pl.BlockSpec(shape, lambda *_: (0,) * nd, pipeline_mode=pl.Buffered(1))


def _params(n_axes):
    return pltpu.CompilerParams(
        dimension_semantics=("arbitrary",) * n_axes,
        vmem_limit_bytes=VMEM_LIMIT_BYTES)


def _rms_rows(x, g_row):
    ms = jnp.mean(x * x, axis=-1, keepdims=True)
    return (x * lax.rsqrt(ms + EPS) * g_row).astype(BF16)


def _head_norm_t(ut, g_col):
    out = []
    for h in range(N_HEADS):
        blk = ut[h * HEAD_DIM:(h + 1) * HEAD_DIM, :]
        ms = jnp.mean(blk * blk, axis=0, keepdims=True)
        out.append(blk * lax.rsqrt(ms + EPS) * g_col[h * HEAD_DIM:(h + 1) * HEAD_DIM, :])
    return jnp.concatenate(out, axis=0)


def _short_conv(xn, wa_ref, cw_ref, cb_ref, zbuf, zlast_ref, n_seg, seg_len):
    cb = _dot_nt(xn, wa_ref[0:D_CONV, :])
    z = (_dot_nt(xn, wa_ref[D_CONV:2 * D_CONV, :])
         * _dot_nt(xn, wa_ref[2 * D_CONV:3 * D_CONV, :]))
    w0 = cw_ref[0:1, :]
    w1 = cw_ref[1:2, :]
    w2 = cw_ref[2:3, :]
    out = []
    for s in range(n_seg):
        r0 = s * seg_len
        zs = z[r0:r0 + seg_len]
        zbuf[s, ZPAD:ZPAD + seg_len, :] = zs
        zm1 = zbuf[s, ZPAD - 1:ZPAD - 1 + seg_len, :]
        zm2 = zbuf[s, ZPAD - 2:ZPAD - 2 + seg_len, :]
        y = zm2 * w0 + zm1 * w1 + zs * w2 + cb_ref[...]
        out.append(cb[r0:r0 + seg_len] * y)
        tail = zbuf[s, ZPAD + seg_len - 2:ZPAD + seg_len, :]
        zlast_ref[s] = tail
        zbuf[s, ZPAD - 2:ZPAD, :] = tail
    return out


def _qkv_kernel(x_ref, g1_ref, wqkv_ref, wfl_ref, bfr_ref, qgc_ref, kgc_ref,
                ktm_ref, vtm_ref, lftm_ref,
                qt_ref, kt_ref, vt_ref, lftp_ref, lft_ref, lfp_ref, kcar, vcar, lcar,
                *, tiles_per_seq):
    def shifted_store(out_ref, car_ref, tile):
        rolled = pltpu.roll(tile, N_META, axis=1)
        lane = lax.broadcasted_iota(jnp.int32, (tile.shape[0], LANES), 1)
        out_ref[:, 0:LANES] = jnp.where(lane < N_META, car_ref[...], rolled[:, 0:LANES])
        out_ref[:, LANES:] = rolled[:, LANES:]
        car_ref[...] = rolled[:, 0:LANES]

    @pl.when(pl.program_id(0) % tiles_per_seq == 0)
    def _():
        kcar[...] = ktm_ref[:, 0:LANES]
        vcar[...] = vtm_ref[:, 0:LANES]
        lcar[...] = lftm_ref[:, 0:LANES]

    xn = _rms_rows(x_ref[...], g1_ref[...])

    def feature_major(j):
        return _dot_nt(wqkv_ref[j * D_ATTN:(j + 1) * D_ATTN, :], xn)

    lfp = _log_sigmoid(_dot_nt(xn, wfl_ref[...]) + bfr_ref[...])
    lfp_ref[...] = lfp
    lft = lfp.T[0:N_HEADS, :]
    lft_ref[...] = lft
    qt = _head_norm_t(feature_major(0), qgc_ref[...])
    qt_ref[...] = (qt * (ATTN_SCALE * LOG2E)).astype(BF16)
    shifted_store(kt_ref, kcar, _head_norm_t(feature_major(1), kgc_ref[...]))
    shifted_store(vt_ref, vcar, feature_major(2))
    shifted_store(lftp_ref, lcar, lft)


def _project_qkv(x2d, wts, meta_cols, *, b, seq):
    rows = PROJ_ROWS
    tiles = seq // rows
    length = N_META + seq
    row_spec = lambda width: pl.BlockSpec((rows, width), lambda i: (i, 0))
    col_spec = lambda feat: pl.BlockSpec(
        (None, feat, rows), lambda i: (i // tiles, 0, i % tiles))
    seq_spec = lambda feat: pl.BlockSpec((None, feat, LANES), lambda i: (i // tiles, 0, 0))
    out_shape = (
        jax.ShapeDtypeStruct((b, D_ATTN, seq), BF16),
        jax.ShapeDtypeStruct((b, D_ATTN, length), F32),
        jax.ShapeDtypeStruct((b, D_ATTN, length), F32),
        jax.ShapeDtypeStruct((b, N_HEADS, length), F32),
        jax.ShapeDtypeStruct((b, N_HEADS, seq), F32),
        jax.ShapeDtypeStruct((b * seq, LANES), F32),
        jax.ShapeDtypeStruct((b, D_ATTN, LANES), F32),
        jax.ShapeDtypeStruct((b, D_ATTN, LANES), F32),
        jax.ShapeDtypeStruct((b, N_HEADS, LANES), F32),
    )
    out_specs = (col_spec(D_ATTN), col_spec(D_ATTN), col_spec(D_ATTN), col_spec(N_HEADS),
                 col_spec(N_HEADS), row_spec(LANES),
                 seq_spec(D_ATTN), seq_spec(D_ATTN), seq_spec(N_HEADS))
    kern = functools.partial(_qkv_kernel, tiles_per_seq=tiles)
    return pl.pallas_call(
        kern, grid=(b * tiles,),
        in_specs=[row_spec(D_MODEL)] + [_const_spec(w.shape) for w in wts + meta_cols],
        out_specs=out_specs, out_shape=out_shape,
        compiler_params=_params(1), name="proj_qkv",
    )(x2d, *wts, *meta_cols)


def _tail_kernel(kt_hbm, vt_hbm, lft_hbm, ktail_ref, vtail_ref, ltail_ref,
                 kt_ref, vt_ref, lft_ref):
    del kt_hbm, vt_hbm, lft_hbm
    kt_ref[...] = ktail_ref[...]
    vt_ref[...] = vtail_ref[...]
    lft_ref[...] = ltail_ref[...]


def _write_tails(kt, vt, lft, ktail, vtail, ltail):
    b, _, length = kt.shape
    last = (length - 1) // LANES
    src = lambda feat: pl.BlockSpec((None, feat, LANES), lambda i: (i, 0, 0))
    dst = lambda feat: pl.BlockSpec((None, feat, LANES), lambda i: (i, 0, last))
    hbm = pl.BlockSpec(memory_space=pl.ANY)
    return pl.pallas_call(
        _tail_kernel, grid=(b,),
        in_specs=[hbm, hbm, hbm, src(D_ATTN), src(D_ATTN), src(N_HEADS)],
        out_specs=(dst(D_ATTN), dst(D_ATTN), dst(N_HEADS)),
        out_shape=(jax.ShapeDtypeStruct(kt.shape, F32), jax.ShapeDtypeStruct(vt.shape, F32),
                   jax.ShapeDtypeStruct(lft.shape, F32)),
        input_output_aliases={0: 0, 1: 1, 2: 2},
        compiler_params=_params(1), name="write_tails",
    )(kt, vt, lft, ktail, vtail, ltail)


def _proj_small_kernel(x_ref, left_ref, g1_ref, wqkv_ref, wfl_ref, bfr_ref, qgc_ref, kgc_ref,
                       wa_ref, wgl_ref, cw_ref, cb_ref, wflt_ref, bfc_ref, qgr_ref, kgr_ref,
                       bd_ref,
                       conv_ref, q_ref, k_ref, v_ref, kt_ref, vt_ref, lft_ref, lfp_ref,
                       gate_ref, zlast_ref, zbuf, *, n_seg, seg_len):
    xn = _rms_rows(x_ref[...], g1_ref[...])
    zbuf[:, ZPAD - 2:ZPAD, :] = left_ref[...]
    conv = _short_conv(xn, wa_ref, cw_ref, cb_ref, zbuf, zlast_ref, n_seg, seg_len)
    for s in range(n_seg):
        conv_ref[s * seg_len:(s + 1) * seg_len, :] = conv[s].astype(BF16)

    def rows_major(j):
        return _dot_nt(xn, wqkv_ref[j * D_ATTN:(j + 1) * D_ATTN, :])

    def feature_major(j):
        return _dot_nt(wqkv_ref[j * D_ATTN:(j + 1) * D_ATTN, :], xn)

    def head_norm(u, g_row):
        ssq = _dot((u * u).astype(BF16), bd_ref[...])
        return u * lax.rsqrt(ssq * (1.0 / HEAD_DIM) + EPS) * g_row

    q_ref[...] = (head_norm(rows_major(0), qgr_ref[...]) * ATTN_SCALE).astype(BF16)
    k_ref[...] = head_norm(rows_major(1), kgr_ref[...])
    v_ref[...] = rows_major(2)
    kt_ref[...] = _head_norm_t(feature_major(1), kgc_ref[...])
    vt_ref[...] = feature_major(2)
    lfp_ref[...] = _log_sigmoid(_dot_nt(xn, wfl_ref[...]) + bfr_ref[...])
    lft_ref[...] = _log_sigmoid(_dot_nt(wflt_ref[...], xn) + bfc_ref[...])
    gate_ref[...] = jax.nn.sigmoid(_dot_nt(xn, wgl_ref[...])).astype(BF16)


def _project_small(x2d, left, wts):
    n_rows = x2d.shape[0]
    n_seq = left.shape[0]
    seg_len = n_rows // n_seq
    full = lambda *shape: pl.BlockSpec(shape, lambda i: (0,) * len(shape))
    out_shape = (
        jax.ShapeDtypeStruct((n_rows, D_CONV), BF16),
        jax.ShapeDtypeStruct((n_rows, D_ATTN), BF16),
        jax.ShapeDtypeStruct((n_rows, D_ATTN), F32),
        jax.ShapeDtypeStruct((n_rows, D_ATTN), F32),
        jax.ShapeDtypeStruct((D_ATTN, n_rows), F32),
        jax.ShapeDtypeStruct((D_ATTN, n_rows), F32),
        jax.ShapeDtypeStruct((N_HEADS, n_rows), F32),
        jax.ShapeDtypeStruct((n_rows, LANES), F32),
        jax.ShapeDtypeStruct((n_rows, 2 * D_MODEL), BF16),
        jax.ShapeDtypeStruct((n_seq, CONV_W - 1, D_CONV), F32),
    )
    kern = functools.partial(_proj_small_kernel, n_seg=n_seq, seg_len=seg_len)
    return pl.pallas_call(
        kern, grid=(1,),
        in_specs=[full(*x2d.shape), full(*left.shape)] + [_const_spec(w.shape) for w in wts],
        out_specs=tuple(full(*s.shape) for s in out_shape), out_shape=out_shape,
        scratch_shapes=[pltpu.VMEM((n_seq, seg_len + ZPAD, D_CONV), F32)],
        compiler_params=_params(1), name="proj_small",
    )(x2d, left, *wts)


def _prompt_attn_kernel(qt_ref, kt_ref, vt_ref, lft_ref, lftm_ref, lfp_ref, lfpm_ref,
                        qg_ref, kg_ref, o_ref, kpos, vb, kbias, crow, cend, qcat, m_s, acc_s,
                        sbuf, tail, *, seq):
    blk = ATTN_BLOCK
    n_blk = seq // blk
    length = N_META + seq
    n_pos = kpos.shape[0]
    n_bias = N_SPLIT * N_HEADS

    for j in range(n_blk):
        kpos[j * blk:(j + 1) * blk, :] = kt_ref[:, j * blk:(j + 1) * blk].T.astype(BF16)
    tail[...] = jnp.zeros(tail.shape, F32)
    tail[:, 0:N_META] = kt_ref[:, seq:length]
    kpos[seq:seq + LANES, :] = tail[...].T.astype(BF16)
    kpos[seq + LANES:, :] = jnp.zeros((n_pos - seq - LANES, D_ATTN), BF16)

    ones_row = (lax.broadcasted_iota(jnp.int32, (V_SLAB - HEAD_DIM, blk), 0) == 0).astype(BF16)
    vb[n_blk] = jnp.zeros((N_HEADS * V_SLAB, blk), BF16)
    for h in range(N_HEADS):
        rows = slice(h * HEAD_DIM, (h + 1) * HEAD_DIM)
        slab = slice(h * V_SLAB, h * V_SLAB + HEAD_DIM)
        for j in range(n_blk + 1):
            if j < n_blk:
                vb[j, slab, :] = vt_ref[rows, j * blk:(j + 1) * blk].astype(BF16)
            else:
                vb[j, slab, 0:N_META] = vt_ref[rows, seq:length].astype(BF16)
            vb[j, h * V_SLAB + HEAD_DIM:(h + 1) * V_SLAB, :] = ones_row

    qk_bound = (NORM_SLACK * HEAD_DIM * ATTN_SCALE * LOG2E
                * jnp.max(jnp.abs(qg_ref[...]), axis=1, keepdims=True)
                * jnp.max(jnp.abs(kg_ref[...]), axis=1, keepdims=True))

    upper, lower = _triangles(blk)

    def store_kbias(rows, c_col):
        hi, mid, lo = _split3(c_col * LOG2E)
        lane = lax.broadcasted_iota(jnp.int32, c_col.shape, 1)
        grp = lane // N_HEADS
        part = jnp.where(grp == 0, hi, jnp.where(grp == 1, mid, lo))
        kbias[rows, :] = jnp.where(lane < n_bias, -part,
                                   jnp.where(lane < 2 * n_bias, 1.0, 0.0)).astype(BF16)

    off_c = jnp.zeros((1, LANES), F32)
    for j in range(n_blk):
        if j == 0:
            lf_blk = jnp.concatenate([lfpm_ref[...], lfp_ref[0:blk - N_META, :]], axis=0)
        else:
            lf_blk = lfp_ref[j * blk - N_META:(j + 1) * blk - N_META, :]
        c_col = _cumsum_rows(lower, lf_blk) + off_c
        store_kbias(slice(j * blk, (j + 1) * blk), c_col)
        off_c = c_col[blk - 1:blk, :]
    c_col = _cumsum_few(lfp_ref[seq - N_META:seq, :], axis=0) + off_c
    store_kbias(slice(seq, length), c_col)
    kbias[length:, :] = jnp.zeros((n_pos - length, LANES), BF16)

    meta_r = _cumsum_few(lftm_ref[...], axis=1)
    off_r = meta_r[:, N_META - 1:N_META]
    lane_h = lax.broadcasted_iota(jnp.int32, (N_HEADS, LANES), 1)
    c_end = jnp.zeros((N_HEADS, LANES), F32)
    for j in range(n_blk):
        c_row = _cumsum_lanes(lft_ref[:, j * blk:(j + 1) * blk], upper) + off_r
        crow[j] = c_row * LOG2E
        off_r = c_row[:, blk - 1:blk]
        c_end = jnp.where(lane_h == j, off_r * LOG2E, c_end)
    cend[...] = c_end

    row128 = lax.broadcasted_iota(jnp.int32, (LANES, blk), 0)
    krow = lax.broadcasted_iota(jnp.int32, (blk + N_META, blk), 0)
    qcol = lax.broadcasted_iota(jnp.int32, (blk + N_META, blk), 1)

    def q_block(t, _):
        tok0 = pl.multiple_of(t * blk, blk)
        c_q = crow[t]
        hi, mid, lo = _split3(c_q)
        bias_rows = jnp.concatenate(
            [jnp.ones((n_bias, blk), F32), hi, mid, lo,
             jnp.zeros((LANES - 2 * n_bias, blk), F32)], axis=0)
        for h in range(N_HEADS):
            pair = qt_ref[(h // 2) * LANES:(h // 2 + 1) * LANES, pl.ds(tok0, blk)]
            in_head = (row128 // HEAD_DIM) == (h % 2)
            qcat[h, 0:LANES, :] = jnp.where(in_head, pair, jnp.zeros_like(pair))
            qcat[h, LANES:, :] = jnp.where(row128 % N_HEADS == h, bias_rows, 0.0).astype(BF16)
        m_s[...] = jnp.full(m_s.shape, -jnp.inf, F32)
        acc_s[...] = jnp.zeros(acc_s.shape, F32)

        gap = 2.0 * qk_bound + c_q[:, 0:1] - cend[...]
        needed = jnp.logical_and(gap >= -SKIP_LOG2, lane_h < t)
        n_needed = jnp.max(
            jnp.sum(jnp.where(needed, 1.0, 0.0), axis=1, keepdims=True)).astype(jnp.int32)

        def key_block(pos0, n_rows, pv, visible):
            m_blk = []
            for h in range(N_HEADS):
                g = h // 2
                kc = jnp.concatenate([kpos[pl.ds(pos0, n_rows), g * LANES:(g + 1) * LANES],
                                      kbias[pl.ds(pos0, n_rows), :]], axis=1)
                s = _dot(kc, qcat[h])
                if visible is not None:
                    s = jnp.where(visible, s, -jnp.inf)
                sbuf[h, 0:n_rows, :] = s
                m_blk.append(jnp.max(s, axis=0, keepdims=True))
            for h in range(N_HEADS):
                slab = slice(h * V_SLAB, (h + 1) * V_SLAB)
                m_old = m_s[h:h + 1, :]
                m_new = jnp.maximum(m_old, m_blk[h])
                alpha = jnp.exp2(m_old - m_new)
                p = jnp.exp2(sbuf[h, 0:n_rows, :] - m_new)
                m_s[h:h + 1, :] = m_new
                acc_s[slab, :] = alpha * acc_s[slab, :] + pv(slab, p.astype(BF16))

        def full_block(j, _):
            key_block(pl.multiple_of(j * blk, blk), blk,
                      lambda slab, p: _dot(vb[j, slab, :], p), None)
            return 0

        def own_pv(slab, p):
            spill = jnp.concatenate([p[blk:], jnp.zeros((LANES - N_META, blk), BF16)], axis=0)
            return _dot(vb[t, slab, :], p[0:blk]) + _dot(vb[t + 1, slab, 0:LANES], spill)

        key_block(tok0, blk + N_META, own_pv, krow <= qcol + N_META)
        lax.fori_loop(t - n_needed, t, full_block, 0)

        o_t = []
        for h in range(N_HEADS):
            norm = acc_s[h * V_SLAB + HEAD_DIM:h * V_SLAB + HEAD_DIM + 1, :]
            o_t.append(acc_s[h * V_SLAB:h * V_SLAB + HEAD_DIM, :] * (1.0 / norm))
        o_ref[pl.ds(tok0, blk), :] = jnp.concatenate(o_t, axis=0).T.astype(BF16)
        return 0

    lax.fori_loop(0, n_blk, q_block, 0)


def _prompt_attention(qt, kt, vt, lft, lft_meta, lfp, lfp_meta, q_gain, k_gain):
    b, _, seq = qt.shape
    length = vt.shape[2]
    blk = ATTN_BLOCK
    n_blk = seq // blk
    n_pos = (n_blk + 1) * blk
    per_b = lambda *shape: pl.BlockSpec((None,) + shape, lambda i: (i,) + (0,) * len(shape))
    kern = functools.partial(_prompt_attn_kernel, seq=seq)
    return pl.pallas_call(
        kern, grid=(b,),
        in_specs=[per_b(D_ATTN, seq), per_b(D_ATTN, length), per_b(D_ATTN, length),
                  per_b(N_HEADS, seq), _const_spec(lft_meta.shape),
                  per_b(seq, LANES), _const_spec(lfp_meta.shape),
                  _const_spec(q_gain.shape), _const_spec(k_gain.shape)],
        out_specs=per_b(seq, D_ATTN),
        out_shape=jax.ShapeDtypeStruct((b, seq, D_ATTN), BF16),
        scratch_shapes=[pltpu.VMEM((n_pos, D_ATTN), BF16),
                        pltpu.VMEM((n_blk + 1, N_HEADS * V_SLAB, blk), BF16),
                        pltpu.VMEM((n_pos, LANES), BF16),
                        pltpu.VMEM((n_blk, N_HEADS, blk), F32),
                        pltpu.VMEM((N_HEADS, LANES), F32),
                        pltpu.VMEM((N_HEADS, 2 * LANES, blk), BF16),
                        pltpu.VMEM((N_HEADS, blk), F32),
                        pltpu.VMEM((N_HEADS * V_SLAB, blk), F32),
                        pltpu.VMEM((N_HEADS, blk + N_META, blk), F32),
                        pltpu.VMEM((D_ATTN, LANES), F32)],
        compiler_params=_params(1), name="prompt_attn",
    )(qt, kt, vt, lft, lft_meta, lfp, lfp_meta, q_gain, k_gain)


def _sample_attn_kernel(q_ref, kn_ref, vn_ref, lf_ref, lft_ref, ckt_ref, cvt_ref, clft_ref,
                        o_ref, crow, *, past, dec):
    blk = ATTN_BLOCK
    n_keys = past + LANES

    upper, _ = _triangles(blk)
    off = jnp.zeros((N_HEADS, 1), F32)
    for j in range(past // blk):
        loc = _cumsum_lanes(clft_ref[:, j * blk:(j + 1) * blk], upper) + off
        crow[:, j * blk:(j + 1) * blk] = loc
        off = loc[:, blk - 1:blk]
    crow[:, 0:past] = crow[:, 0:past] - off

    cq_c = _cumsum_few(lf_ref[:, 0:N_HEADS], axis=0)
    cq_r = _cumsum_few(lft_ref[...], axis=1)
    crow[:, past:] = jnp.full((N_HEADS, LANES), MASKED_BIAS, F32)
    crow[:, past:past + dec] = cq_r

    q = q_ref[...]
    lane_head = lax.broadcasted_iota(jnp.int32, (dec, D_ATTN), 1) // HEAD_DIM
    q_exp = jnp.concatenate(
        [jnp.where(lane_head == h, q, jnp.zeros_like(q)) for h in range(N_HEADS)], axis=0)
    pad_rows = jnp.zeros((LANES - dec, D_ATTN), BF16)
    k_new = jnp.concatenate([kn_ref[...].astype(BF16), pad_rows], axis=0)
    v_new = jnp.concatenate([vn_ref[...].astype(BF16), pad_rows], axis=0)
    s_all = jnp.concatenate(
        [_dot(q_exp, ckt_ref[...].astype(BF16)), _dot_nt(q_exp, k_new)], axis=1)

    kpos = lax.broadcasted_iota(jnp.int32, (dec, n_keys), 1)
    qpos = past + lax.broadcasted_iota(jnp.int32, (dec, n_keys), 0)
    visible = kpos <= qpos
    probs = []
    norms = []
    for h in range(N_HEADS):
        s = s_all[h * dec:(h + 1) * dec, :] + cq_c[:, h:h + 1] - crow[h:h + 1, :]
        s = jnp.where(visible, s, -jnp.inf)
        p = jnp.exp(s - jnp.max(s, axis=-1, keepdims=True))
        norms.append(jnp.sum(p, axis=-1, keepdims=True))
        probs.append(p.astype(BF16))
    p_all = jnp.concatenate(probs, axis=0)
    o_all = (_dot_nt(p_all[:, 0:past], cvt_ref[...].astype(BF16))
             + _dot(p_all[:, past:], v_new))
    out = jnp.zeros((dec, D_ATTN), F32)
    for h in range(N_HEADS):
        o = o_all[h * dec:(h + 1) * dec, :] / norms[h]
        out = out + jnp.where(lane_head == h, o, 0.0)
    o_ref[...] = out.astype(BF16)


def _sample_attention(q, k_new, v_new, lfp, lft, cache_kt, cache_vt, cache_lft):
    b, dec, _ = q.shape
    past = cache_kt.shape[2]
    per_b = lambda *shape: pl.BlockSpec((None,) + shape, lambda i: (i,) + (0,) * len(shape))
    kern = functools.partial(_sample_attn_kernel, past=past, dec=dec)
    return pl.pallas_call(
        kern, grid=(b,),
        in_specs=[per_b(dec, D_ATTN), per_b(dec, D_ATTN), per_b(dec, D_ATTN),
                  per_b(dec, LANES), per_b(N_HEADS, dec),
                  per_b(D_ATTN, past), per_b(D_ATTN, past), per_b(N_HEADS, past)],
        out_specs=per_b(dec, D_ATTN),
        out_shape=jax.ShapeDtypeStruct((b, dec, D_ATTN), BF16),
        scratch_shapes=[pltpu.VMEM((N_HEADS, past + LANES), F32)],
        compiler_params=_params(1), name="sample_attn",
    )(q, k_new, v_new, lfp, lft, cache_kt, cache_vt, cache_lft)


def _merge_mlp_tail(x, conv_bf16, attn_ref, g_conv, g_attn, wbc_ref, wba_ref, wo_ref, g2_ref,
                    wup_ref, wdn_ref, y_ref):
    merged = g_conv * _dot(conv_bf16, wbc_ref[...]) + g_attn * _dot(attn_ref[...], wba_ref[...])
    h = x + _dot(merged.astype(BF16), wo_ref[...])
    hn = _rms_rows(h, g2_ref[...])
    acc = h
    for c in range(D_FF // D_MODEL):
        cols = slice(c * D_MODEL, (c + 1) * D_MODEL)
        a = jnp.maximum(_dot(hn, wup_ref[:, cols]), 0.0)
        acc = acc + _dot((a * a).astype(BF16), wdn_ref[cols, :])
    y_ref[...] = acc


def _branch_mlp_kernel(x_ref, left_ref, attn_ref, g1_ref, wa_ref, wgl_ref, cw_ref, cb_ref,
                       wbc_ref, wba_ref, wo_ref, g2_ref, wup_ref, wdn_ref,
                       y_ref, zlast_ref, zbuf, *, tiles_per_seq):
    @pl.when(pl.program_id(0) % tiles_per_seq == 0)
    def _():
        zbuf[:, ZPAD - 2:ZPAD, :] = left_ref[...]

    x = x_ref[...]
    xn = _rms_rows(x, g1_ref[...])
    conv = _short_conv(xn, wa_ref, cw_ref, cb_ref, zbuf, zlast_ref, 1, x.shape[0])[0]
    g_conv = jax.nn.sigmoid(_dot_nt(xn, wgl_ref[0:D_MODEL, :]))
    g_attn = jax.nn.sigmoid(_dot_nt(xn, wgl_ref[D_MODEL:2 * D_MODEL, :]))
    _merge_mlp_tail(x, conv.astype(BF16), attn_ref, g_conv, g_attn, wbc_ref, wba_ref, wo_ref,
                    g2_ref, wup_ref, wdn_ref, y_ref)


def _branch_mlp(x2d, left, attn, wts, *, rows, seq):
    n_rows = x2d.shape[0]
    tiles = seq // rows
    row_spec = lambda width: pl.BlockSpec((rows, width), lambda i: (i, 0))
    seq_spec = pl.BlockSpec((1, CONV_W - 1, D_CONV), lambda i: (i // tiles, 0, 0))
    kern = functools.partial(_branch_mlp_kernel, tiles_per_seq=tiles)
    return pl.pallas_call(
        kern, grid=(n_rows // rows,),
        in_specs=[row_spec(D_MODEL), seq_spec, row_spec(D_ATTN)]
        + [_const_spec(w.shape) for w in wts],
        out_specs=(row_spec(D_MODEL), seq_spec),
        out_shape=(jax.ShapeDtypeStruct((n_rows, D_MODEL), F32),
                   jax.ShapeDtypeStruct((n_rows // seq, CONV_W - 1, D_CONV), F32)),
        scratch_shapes=[pltpu.VMEM((1, rows + ZPAD, D_CONV), F32)],
        compiler_params=_params(1), name="branch_mlp",
    )(x2d, left, attn, *wts)


def _merge_mlp_kernel(x_ref, conv_ref, attn_ref, gate_ref, wbc_ref, wba_ref, wo_ref, g2_ref,
                      wup_ref, wdn_ref, y_ref):
    _merge_mlp_tail(x_ref[...], conv_ref[...], attn_ref,
                    gate_ref[:, 0:D_MODEL].astype(F32),
                    gate_ref[:, D_MODEL:2 * D_MODEL].astype(F32),
                    wbc_ref, wba_ref, wo_ref, g2_ref, wup_ref, wdn_ref, y_ref)


def _merge_mlp(x2d, conv, attn, gates, wts):
    full = lambda a: pl.BlockSpec(a.shape, lambda i: (0,) * a.ndim)
    return pl.pallas_call(
        _merge_mlp_kernel, grid=(1,),
        in_specs=[full(x2d), full(conv), full(attn), full(gates)]
        + [_const_spec(w.shape) for w in wts],
        out_specs=full(x2d),
        out_shape=jax.ShapeDtypeStruct(x2d.shape, F32),
        compiler_params=_params(1), name="merge_mlp",
    )(x2d, conv, attn, gates, *wts)


def kernel(x_prompt, x_sample, cache_k, cache_v, cache_logf, state_conv, meta,
           norm1_g, w_in, b_f, conv_w, conv_b, q_norm_g, k_norm_g,
           w_br_conv, w_br_attn, w_out, norm2_g, w_up, w_down):
    b, seq, _ = x_prompt.shape
    db, dec, _ = x_sample.shape
    past = cache_k.shape[2]
    length = N_META + seq
    n_main = 3 * D_CONV + 3 * D_ATTN
    q0 = 3 * D_CONV

    wt = w_in[0].T
    wt_conv = wt[0:q0].astype(BF16)
    wt_qkv = wt[q0:n_main].astype(BF16)
    wt_fl = wt[n_main:n_main + N_HEADS].astype(BF16)
    wt_gate = wt[n_main + N_HEADS:].astype(BF16)
    head_of = jnp.arange(D_ATTN) // HEAD_DIM
    qg = jnp.tile(q_norm_g[0], N_HEADS)
    kg = jnp.tile(k_norm_g[0], N_HEADS)
    g1 = norm1_g[0][None, :]
    qkv_wts = (g1, wt_qkv, jnp.tile(wt_fl, (LANES // N_HEADS, 1)),
               jnp.tile(b_f[0], LANES // N_HEADS)[None, :], qg[:, None], kg[:, None])
    conv_wts = (conv_w[0], conv_b[0][None, :])
    mlp_wts = (w_br_conv[0].astype(BF16), w_br_attn[0].astype(BF16), w_out[0].astype(BF16),
               norm2_g[0][None, :], w_up[0].astype(BF16), w_down[0].astype(BF16))

    x_small = jnp.concatenate([meta, x_sample.reshape(db * dec, D_MODEL)], axis=0)
    left_small = jnp.concatenate(
        [jnp.zeros((1, CONV_W - 1, D_CONV), F32), state_conv[0]], axis=0)
    (conv_s, q_s, k_s, v_s, kt_s, vt_s, lft_s, lfp_s, gate_s, zlast_s) = _project_small(
        x_small, left_small,
        qkv_wts + (wt_conv, wt_gate) + conv_wts
        + (wt_fl, b_f[0][:, None], qg[None, :], kg[None, :],
           (head_of[:, None] == head_of[None, :]).astype(BF16)))

    x_rows = x_prompt.reshape(b * seq, D_MODEL)
    (qt_p, kt_p, vt_p, lftp_p, lft_p, lfp_p, ktail, vtail, ltail) = _project_qkv(
        x_rows, qkv_wts, (kt_s, vt_s, lft_s), b=b, seq=seq)
    kt_p, vt_p, lftp_p = _write_tails(kt_p, vt_p, lftp_p, ktail, vtail, ltail)
    attn_p = _prompt_attention(
        qt_p, kt_p, vt_p, lft_p, lft_s[:, :N_META], lfp_p.reshape(b, seq, LANES),
        lfp_s[:N_META], q_norm_g, k_norm_g)
    left_p = jnp.broadcast_to(zlast_s[0:1], (b, CONV_W - 1, D_CONV))
    y_prompt, zlast_p = _branch_mlp(
        x_rows, left_p, attn_p.reshape(b * seq, D_ATTN),
        (g1, wt_conv, wt_gate) + conv_wts + mlp_wts, rows=MLP_ROWS, seq=seq)

    k_new = k_s[N_META:].reshape(db, dec, D_ATTN)
    v_new = v_s[N_META:].reshape(db, dec, D_ATTN)
    lf_new = lfp_s[N_META:, :N_HEADS].reshape(db, dec, N_HEADS)
    cache_kt = jnp.transpose(cache_k[0], (0, 2, 3, 1)).reshape(db, D_ATTN, past)
    cache_vt = jnp.transpose(cache_v[0], (0, 2, 3, 1)).reshape(db, D_ATTN, past)
    attn_s = _sample_attention(
        q_s[N_META:].reshape(db, dec, D_ATTN), k_new, v_new,
        lfp_s[N_META:].reshape(db, dec, LANES), jnp.swapaxes(lf_new, 1, 2),
        cache_kt, cache_vt, jnp.swapaxes(cache_logf[0], 1, 2))
    y_sample = _merge_mlp(x_sample.reshape(db * dec, D_MODEL), conv_s[N_META:],
                          attn_s.reshape(db * dec, D_ATTN), gate_s[N_META:], mlp_wts)

    def heads_last(t):
        return jnp.transpose(t.reshape(b, N_HEADS, HEAD_DIM, length), (0, 3, 1, 2))[None]

    return (y_prompt.reshape(b, seq, D_MODEL),
            y_sample.reshape(db, dec, D_MODEL),
            heads_last(kt_p),
            heads_last(vt_p),
            jnp.swapaxes(lftp_p, 1, 2)[None],
            zlast_p[None],
            k_new.reshape(1, db, dec, N_HEADS, HEAD_DIM),
            v_new.reshape(1, db, dec, N_HEADS, HEAD_DIM),
            lf_new[None],
            zlast_s[1:][None])
```

```python
import functools

import jax
import jax.numpy as jnp
from jax import lax
from jax.experimental import pallas as pl
from jax.experimental.pallas import tpu as pltpu

D_MODEL = 1024
D_CONV = D_MODEL // 2
CONV_W = 3
N_HEADS = 8
HEAD_DIM = 64
D_ATTN = N_HEADS * HEAD_DIM
D_FF = 4 * D_MODEL
N_META = 16
EPS = 1e-6
ATTN_SCALE = HEAD_DIM ** -0.5

F32 = jnp.float32
BF16 = jnp.bfloat16

VMEM_LIMIT_BYTES = 56 * 1024 * 1024
LANES = 128
PROJ_ROWS = 1024
MLP_ROWS = 512
ATTN_BLOCK = 256
MASKED_BIAS = 1e30
ZPAD = 8
N_SPLIT = 3
V_SLAB = HEAD_DIM + 16
LOG2E = 1.4426950408889634
SKIP_LOG2 = 40.0
NORM_SLACK = 1.02


def _dot(a, b):
    return jnp.dot(a, b, preferred_element_type=F32)


def _dot_nt(a, b):
    return lax.dot_general(a, b, (((1,), (1,)), ((), ())), preferred_element_type=F32)


def _log_sigmoid(x):
    return jnp.minimum(x, 0.0) - jnp.log1p(jnp.exp(-jnp.abs(x)))


def _cumsum_few(x, axis):
    n = x.shape[axis]
    idx = lax.broadcasted_iota(jnp.int32, x.shape, axis)
    out = jnp.zeros(x.shape, F32)
    for i in range(n):
        term = x[i:i + 1, :] if axis == 0 else x[:, i:i + 1]
        out = out + jnp.where(idx >= i, term, 0.0)
    return out


def _triangles(n):
    r = lax.broadcasted_iota(jnp.int32, (n, n), 0)
    c = lax.broadcasted_iota(jnp.int32, (n, n), 1)
    upper = jnp.where(r <= c, 1.0, 0.0).astype(BF16)
    lower = jnp.where(r >= c, 1.0, 0.0).astype(BF16)
    return upper, lower


def _split3(c):
    hi = c.astype(BF16).astype(F32)
    r1 = c - hi
    mid = r1.astype(BF16).astype(F32)
    return hi, mid, r1 - mid


def _cumsum_rows(tri_lower, x):
    w = x.shape[1]
    pieces = jnp.concatenate(_split3(x), axis=1).astype(BF16)
    y = _dot(tri_lower, pieces)
    return y[:, 0:w] + y[:, w:2 * w] + y[:, 2 * w:3 * w]


def _cumsum_lanes(x, tri_upper):
    h = x.shape[0]
    pieces = jnp.concatenate(_split3(x), axis=0).astype(BF16)
    y = _dot(pieces, tri_upper)
    return y[0:h] + y[h:2 * h] + y[2 * h:3 * h]


def _const_spec(shape):
    nd = len(shape)
    return pl.BlockSpec(shape, lambda *_: (0,) * nd, pipeline_mode=pl.Buffered(1))


def _params(n_axes):
    return pltpu.CompilerParams(
        dimension_semantics=("arbitrary",) * n_axes,
        vmem_limit_bytes=VMEM_LIMIT_BYTES)


def _rms_rows(x, g_row):
    ms = jnp.mean(x * x, axis=-1, keepdims=True)
    return (x * lax.rsqrt(ms + EPS) * g_row).astype(BF16)


def _head_norm_t(ut, g_col):
    out = []
    for h in range(N_HEADS):
        blk = ut[h * HEAD_DIM:(h + 1) * HEAD_DIM, :]
        ms = jnp.mean(blk * blk, axis=0, keepdims=True)
        out.append(blk * lax.rsqrt(ms + EPS) * g_col[h * HEAD_DIM:(h + 1) * HEAD_DIM, :])
    return jnp.concatenate(out, axis=0)


def _short_conv(xn, wa_ref, cw_ref, cb_ref, zbuf, zlast_ref, n_seg, seg_len):
    cb = _dot_nt(xn, wa_ref[0:D_CONV, :])
    z = (_dot_nt(xn, wa_ref[D_CONV:2 * D_CONV, :])
         * _dot_nt(xn, wa_ref[2 * D_CONV:3 * D_CONV, :]))
    w0 = cw_ref[0:1, :]
    w1 = cw_ref[1:2, :]
    w2 = cw_ref[2:3, :]
    out = []
    for s in range(n_seg):
        r0 = s * seg_len
        zs = z[r0:r0 + seg_len]
        zbuf[s, ZPAD:ZPAD + seg_len, :] = zs
        zm1 = zbuf[s, ZPAD - 1:ZPAD - 1 + seg_len, :]
        zm2 = zbuf[s, ZPAD - 2:ZPAD - 2 + seg_len, :]
        y = zm2 * w0 + zm1 * w1 + zs * w2 + cb_ref[...]
        out.append(cb[r0:r0 + seg_len] * y)
        tail = zbuf[s, ZPAD + seg_len - 2:ZPAD + seg_len, :]
        zlast_ref[s] = tail
        zbuf[s, ZPAD - 2:ZPAD, :] = tail
    return out


def _qkv_kernel(x_ref, g1_ref, wqkv_ref, wfl_ref, bfr_ref, qgc_ref, kgc_ref,
                ktm_ref, vtm_ref, lftm_ref,
                qt_ref, kt_ref, vt_ref, lftp_ref, lft_ref, lfp_ref, kcar, vcar, lcar,
                *, tiles_per_seq):
    def shifted_store(out_ref, car_ref, tile):
        rolled = pltpu.roll(tile, N_META, axis=1)
        lane = lax.broadcasted_iota(jnp.int32, (tile.shape[0], LANES), 1)
        out_ref[:, 0:LANES] = jnp.where(lane < N_META, car_ref[...], rolled[:, 0:LANES])
        out_ref[:, LANES:] = rolled[:, LANES:]
        car_ref[...] = rolled[:, 0:LANES]

    @pl.when(pl.program_id(0) % tiles_per_seq == 0)
    def _():
        kcar[...] = ktm_ref[:, 0:LANES]
        vcar[...] = vtm_ref[:, 0:LANES]
        lcar[...] = lftm_ref[:, 0:LANES]

    xn = _rms_rows(x_ref[...], g1_ref[...])

    def feature_major(j):
        return _dot_nt(wqkv_ref[j * D_ATTN:(j + 1) * D_ATTN, :], xn)

    lfp = _log_sigmoid(_dot_nt(xn, wfl_ref[...]) + bfr_ref[...])
    lfp_ref[...] = lfp
    lft = lfp.T[0:N_HEADS, :]
    lft_ref[...] = lft
    qt = _head_norm_t(feature_major(0), qgc_ref[...])
    qt_ref[...] = (qt * (ATTN_SCALE * LOG2E)).astype(BF16)
    shifted_store(kt_ref, kcar, _head_norm_t(feature_major(1), kgc_ref[...]))
    shifted_store(vt_ref, vcar, feature_major(2))
    shifted_store(lftp_ref, lcar, lft)


def _project_qkv(x2d, wts, meta_cols, *, b, seq):
    rows = PROJ_ROWS
    tiles = seq // rows
    length = N_META + seq
    row_spec = lambda width: pl.BlockSpec((rows, width), lambda i: (i, 0))
    col_spec = lambda feat: pl.BlockSpec(
        (None, feat, rows), lambda i: (i // tiles, 0, i % tiles))
    seq_spec = lambda feat: pl.BlockSpec((None, feat, LANES), lambda i: (i // tiles, 0, 0))
    out_shape = (
        jax.ShapeDtypeStruct((b, D_ATTN, seq), BF16),
        jax.ShapeDtypeStruct((b, D_ATTN, length), F32),
        jax.ShapeDtypeStruct((b, D_ATTN, length), F32),
        jax.ShapeDtypeStruct((b, N_HEADS, length), F32),
        jax.ShapeDtypeStruct((b, N_HEADS, seq), F32),
        jax.ShapeDtypeStruct((b * seq, LANES), F32),
        jax.ShapeDtypeStruct((b, D_ATTN, LANES), F32),
        jax.ShapeDtypeStruct((b, D_ATTN, LANES), F32),
        jax.ShapeDtypeStruct((b, N_HEADS, LANES), F32),
    )
    out_specs = (col_spec(D_ATTN), col_spec(D_ATTN), col_spec(D_ATTN), col_spec(N_HEADS),
                 col_spec(N_HEADS), row_spec(LANES),
                 seq_spec(D_ATTN), seq_spec(D_ATTN), seq_spec(N_HEADS))
    kern = functools.partial(_qkv_kernel, tiles_per_seq=tiles)
    return pl.pallas_call(
        kern, grid=(b * tiles,),
        in_specs=[row_spec(D_MODEL)] + [_const_spec(w.shape) for w in wts + meta_cols],
        out_specs=out_specs, out_shape=out_shape,
        compiler_params=_params(1), name="proj_qkv",
    )(x2d, *wts, *meta_cols)


def _tail_kernel(kt_hbm, vt_hbm, lft_hbm, ktail_ref, vtail_ref, ltail_ref,
                 kt_ref, vt_ref, lft_ref):
    del kt_hbm, vt_hbm, lft_hbm
    kt_ref[...] = ktail_ref[...]
    vt_ref[...] = vtail_ref[...]
    lft_ref[...] = ltail_ref[...]


def _write_tails(kt, vt, lft, ktail, vtail, ltail):
    b, _, length = kt.shape
    last = (length - 1) // LANES
    src = lambda feat: pl.BlockSpec((None, feat, LANES), lambda i: (i, 0, 0))
    dst = lambda feat: pl.BlockSpec((None, feat, LANES), lambda i: (i, 0, last))
    hbm = pl.BlockSpec(memory_space=pl.ANY)
    return pl.pallas_call(
        _tail_kernel, grid=(b,),
        in_specs=[hbm, hbm, hbm, src(D_ATTN), src(D_ATTN), src(N_HEADS)],
        out_specs=(dst(D_ATTN), dst(D_ATTN), dst(N_HEADS)),
        out_shape=(jax.ShapeDtypeStruct(kt.shape, F32), jax.ShapeDtypeStruct(vt.shape, F32),
                   jax.ShapeDtypeStruct(lft.shape, F32)),
        input_output_aliases={0: 0, 1: 1, 2: 2},
        compiler_params=_params(1), name="write_tails",
    )(kt, vt, lft, ktail, vtail, ltail)


def _proj_small_kernel(x_ref, left_ref, g1_ref, wqkv_ref, wfl_ref, bfr_ref, qgc_ref, kgc_ref,
                       wa_ref, wgl_ref, cw_ref, cb_ref, wflt_ref, bfc_ref, qgr_ref, kgr_ref,
                       bd_ref,
                       conv_ref, q_ref, k_ref, v_ref, kt_ref, vt_ref, lft_ref, lfp_ref,
                       gate_ref, zlast_ref, zbuf, *, n_seg, seg_len):
    xn = _rms_rows(x_ref[...], g1_ref[...])
    zbuf[:, ZPAD - 2:ZPAD, :] = left_ref[...]
    conv = _short_conv(xn, wa_ref, cw_ref, cb_ref, zbuf, zlast_ref, n_seg, seg_len)
    for s in range(n_seg):
        conv_ref[s * seg_len:(s + 1) * seg_len, :] = conv[s].astype(BF16)

    def rows_major(j):
        return _dot_nt(xn, wqkv_ref[j * D_ATTN:(j + 1) * D_ATTN, :])

    def feature_major(j):
        return _dot_nt(wqkv_ref[j * D_ATTN:(j + 1) * D_ATTN, :], xn)

    def head_norm(u, g_row):
        ssq = _dot((u * u).astype(BF16), bd_ref[...])
        return u * lax.rsqrt(ssq * (1.0 / HEAD_DIM) + EPS) * g_row

    q_ref[...] = (head_norm(rows_major(0), qgr_ref[...]) * ATTN_SCALE).astype(BF16)
    k_ref[...] = head_norm(rows_major(1), kgr_ref[...])
    v_ref[...] = rows_major(2)
    kt_ref[...] = _head_norm_t(feature_major(1), kgc_ref[...])
    vt_ref[...] = feature_major(2)
    lfp_ref[...] = _log_sigmoid(_dot_nt(xn, wfl_ref[...]) + bfr_ref[...])
    lft_ref[...] = _log_sigmoid(_dot_nt(wflt_ref[...], xn) + bfc_ref[...])
    gate_ref[...] = jax.nn.sigmoid(_dot_nt(xn, wgl_ref[...])).astype(BF16)


def _project_small(x2d, left, wts):
    n_rows = x2d.shape[0]
    n_seq = left.shape[0]
    seg_len = n_rows // n_seq
    full = lambda *shape: pl.BlockSpec(shape, lambda i: (0,) * len(shape))
    out_shape = (
        jax.ShapeDtypeStruct((n_rows, D_CONV), BF16),
        jax.ShapeDtypeStruct((n_rows, D_ATTN), BF16),
        jax.ShapeDtypeStruct((n_rows, D_ATTN), F32),
        jax.ShapeDtypeStruct((n_rows, D_ATTN), F32),
        jax.ShapeDtypeStruct((D_ATTN, n_rows), F32),
        jax.ShapeDtypeStruct((D_ATTN, n_rows), F32),
        jax.ShapeDtypeStruct((N_HEADS, n_rows), F32),
        jax.ShapeDtypeStruct((n_rows, LANES), F32),
        jax.ShapeDtypeStruct((n_rows, 2 * D_MODEL), BF16),
        jax.ShapeDtypeStruct((n_seq, CONV_W - 1, D_CONV), F32),
    )
    kern = functools.partial(_proj_small_kernel, n_seg=n_seq, seg_len=seg_len)
    return pl.pallas_call(
        kern, grid=(1,),
        in_specs=[full(*x2d.shape), full(*left.shape)] + [_const_spec(w.shape) for w in wts],
        out_specs=tuple(full(*s.shape) for s in out_shape), out_shape=out_shape,
        scratch_shapes=[pltpu.VMEM((n_seq, seg_len + ZPAD, D_CONV), F32)],
        compiler_params=_params(1), name="proj_small",
    )(x2d, left, *wts)


def _prompt_attn_kernel(qt_ref, kt_ref, vt_ref, lft_ref, lftm_ref, lfp_ref, lfpm_ref,
                        qg_ref, kg_ref, o_ref, kpos, vb, kbias, crow, cend, qcat, m_s, acc_s,
                        sbuf, tail, *, seq):
    blk = ATTN_BLOCK
    half = blk // 2
    n_blk = seq // blk
    length = N_META + seq
    n_pos = kpos.shape[0]
    n_bias = N_SPLIT * N_HEADS

    for j in range(n_blk):
        kpos[j * blk:(j + 1) * blk, :] = kt_ref[:, j * blk:(j + 1) * blk].T.astype(BF16)
    tail[...] = jnp.zeros(tail.shape, F32)
    tail[:, 0:N_META] = kt_ref[:, seq:length]
    kpos[seq:seq + LANES, :] = tail[...].T.astype(BF16)
    kpos[seq + LANES:, :] = jnp.zeros((n_pos - seq - LANES, D_ATTN), BF16)

    ones_row = (lax.broadcasted_iota(jnp.int32, (V_SLAB - HEAD_DIM, blk), 0) == 0).astype(BF16)
    vb[n_blk] = jnp.zeros((N_HEADS * V_SLAB, blk), BF16)
    for h in range(N_HEADS):
        rows = slice(h * HEAD_DIM, (h + 1) * HEAD_DIM)
        slab = slice(h * V_SLAB, h * V_SLAB + HEAD_DIM)
        for j in range(n_blk + 1):
            if j < n_blk:
                vb[j, slab, :] = vt_ref[rows, j * blk:(j + 1) * blk].astype(BF16)
            else:
                vb[j, slab, 0:N_META] = vt_ref[rows, seq:length].astype(BF16)
            vb[j, h * V_SLAB + HEAD_DIM:(h + 1) * V_SLAB, :] = ones_row

    qk_bound = (NORM_SLACK * HEAD_DIM * ATTN_SCALE * LOG2E
                * jnp.max(jnp.abs(qg_ref[...]), axis=1, keepdims=True)
                * jnp.max(jnp.abs(kg_ref[...]), axis=1, keepdims=True))

    upper, lower = _triangles(blk)

    def store_kbias(rows, c_col):
        hi, mid, lo = _split3(c_col * LOG2E)
        lane = lax.broadcasted_iota(jnp.int32, c_col.shape, 1)
        grp = lane // N_HEADS
        part = jnp.where(grp == 0, hi, jnp.where(grp == 1, mid, lo))
        kbias[rows, :] = jnp.where(lane < n_bias, -part,
                                   jnp.where(lane < 2 * n_bias, 1.0, 0.0)).astype(BF16)

    off_c = jnp.zeros((1, LANES), F32)
    for j in range(n_blk):
        if j == 0:
            lf_blk = jnp.concatenate([lfpm_ref[...], lfp_ref[0:blk - N_META, :]], axis=0)
        else:
            lf_blk = lfp_ref[j * blk - N_META:(j + 1) * blk - N_META, :]
        c_col = _cumsum_rows(lower, lf_blk) + off_c
        store_kbias(slice(j * blk, (j + 1) * blk), c_col)
        off_c = c_col[blk - 1:blk, :]
    c_col = _cumsum_few(lfp_ref[seq - N_META:seq, :], axis=0) + off_c
    store_kbias(slice(seq, length), c_col)
    kbias[length:, :] = jnp.zeros((n_pos - length, LANES), BF16)

    meta_r = _cumsum_few(lftm_ref[...], axis=1)
    off_r = meta_r[:, N_META - 1:N_META]
    lane_h = lax.broadcasted_iota(jnp.int32, (N_HEADS, LANES), 1)
    c_end = jnp.zeros((N_HEADS, LANES), F32)
    for j in range(n_blk):
        c_row = _cumsum_lanes(lft_ref[:, j * blk:(j + 1) * blk], upper) + off_r
        crow[j] = c_row * LOG2E
        off_r = c_row[:, blk - 1:blk]
        c_end = jnp.where(lane_h == 2 * j, c_row[:, half - 1:half] * LOG2E, c_end)
        c_end = jnp.where(lane_h == 2 * j + 1, off_r * LOG2E, c_end)
    cend[...] = c_end

    row128 = lax.broadcasted_iota(jnp.int32, (LANES, blk), 0)
    krow = lax.broadcasted_iota(jnp.int32, (blk + N_META, blk), 0)
    qcol = lax.broadcasted_iota(jnp.int32, (blk + N_META, blk), 1)

    def q_block(t, _):
        tok0 = pl.multiple_of(t * blk, blk)
        c_q = crow[t]
        hi, mid, lo = _split3(c_q)
        bias_rows = jnp.concatenate(
            [jnp.ones((n_bias, blk), F32), hi, mid, lo,
             jnp.zeros((LANES - 2 * n_bias, blk), F32)], axis=0)
        for h in range(N_HEADS):
            pair = qt_ref[(h // 2) * LANES:(h // 2 + 1) * LANES, pl.ds(tok0, blk)]
            in_head = (row128 // HEAD_DIM) == (h % 2)
            qcat[h, 0:LANES, :] = jnp.where(in_head, pair, jnp.zeros_like(pair))
            qcat[h, LANES:, :] = jnp.where(row128 % N_HEADS == h, bias_rows, 0.0).astype(BF16)
        m_s[...] = jnp.full(m_s.shape, -jnp.inf, F32)
        acc_s[...] = jnp.zeros(acc_s.shape, F32)

        gap = 2.0 * qk_bound + c_q[:, 0:1] - cend[...]
        needed = jnp.logical_and(gap >= -SKIP_LOG2, lane_h < 2 * t)
        n_needed = jnp.max(
            jnp.sum(jnp.where(needed, 1.0, 0.0), axis=1, keepdims=True)).astype(jnp.int32)

        def key_block(pos0, n_rows, pv, visible):
            m_blk = []
            for h in range(N_HEADS):
                g = h // 2
                kc = jnp.concatenate([kpos[pl.ds(pos0, n_rows), g * LANES:(g + 1) * LANES],
                                      kbias[pl.ds(pos0, n_rows), :]], axis=1)
                s = _dot(kc, qcat[h])
                if visible is not None:
                    s = jnp.where(visible, s, -jnp.inf)
                sbuf[h, 0:n_rows, :] = s
                m_blk.append(jnp.max(s, axis=0, keepdims=True))
            for h in range(N_HEADS):
                slab = slice(h * V_SLAB, (h + 1) * V_SLAB)
                m_old = m_s[h:h + 1, :]
                m_new = jnp.maximum(m_old, m_blk[h])
                alpha = jnp.exp2(m_old - m_new)
                p = jnp.exp2(sbuf[h, 0:n_rows, :] - m_new)
                m_s[h:h + 1, :] = m_new
                acc_s[slab, :] = alpha * acc_s[slab, :] + pv(slab, p.astype(BF16))

        def older_half_block(j, _):
            lanes = pl.ds(pl.multiple_of((j % 2) * half, half), half)
            key_block(pl.multiple_of(j * half, half), half,
                      lambda slab, p: _dot(vb[j // 2, slab, lanes], p), None)
            return 0

        def own_pv(slab, p):
            spill = jnp.concatenate([p[blk:], jnp.zeros((LANES - N_META, blk), BF16)], axis=0)
            return _dot(vb[t, slab, :], p[0:blk]) + _dot(vb[t + 1, slab, 0:LANES], spill)

        key_block(tok0, blk + N_META, own_pv, krow <= qcol + N_META)
        lax.fori_loop(2 * t - n_needed, 2 * t, older_half_block, 0)

        o_t = []
        for h in range(N_HEADS):
            norm = acc_s[h * V_SLAB + HEAD_DIM:h * V_SLAB + HEAD_DIM + 1, :]
            o_t.append(acc_s[h * V_SLAB:h * V_SLAB + HEAD_DIM, :] * (1.0 / norm))
        o_ref[pl.ds(tok0, blk), :] = jnp.concatenate(o_t, axis=0).T.astype(BF16)
        return 0

    lax.fori_loop(0, n_blk, q_block, 0)


def _prompt_attention(qt, kt, vt, lft, lft_meta, lfp, lfp_meta, q_gain, k_gain):
    b, _, seq = qt.shape
    length = vt.shape[2]
    blk = ATTN_BLOCK
    n_blk = seq // blk
    n_pos = (n_blk + 1) * blk
    per_b = lambda *shape: pl.BlockSpec((None,) + shape, lambda i: (i,) + (0,) * len(shape))
    kern = functools.partial(_prompt_attn_kernel, seq=seq)
    return pl.pallas_call(
        kern, grid=(b,),
        in_specs=[per_b(D_ATTN, seq), per_b(D_ATTN, length), per_b(D_ATTN, length),
                  per_b(N_HEADS, seq), _const_spec(lft_meta.shape),
                  per_b(seq, LANES), _const_spec(lfp_meta.shape),
                  _const_spec(q_gain.shape), _const_spec(k_gain.shape)],
        out_specs=per_b(seq, D_ATTN),
        out_shape=jax.ShapeDtypeStruct((b, seq, D_ATTN), BF16),
        scratch_shapes=[pltpu.VMEM((n_pos, D_ATTN), BF16),
                        pltpu.VMEM((n_blk + 1, N_HEADS * V_SLAB, blk), BF16),
                        pltpu.VMEM((n_pos, LANES), BF16),
                        pltpu.VMEM((n_blk, N_HEADS, blk), F32),
                        pltpu.VMEM((N_HEADS, LANES), F32),
                        pltpu.VMEM((N_HEADS, 2 * LANES, blk), BF16),
                        pltpu.VMEM((N_HEADS, blk), F32),
                        pltpu.VMEM((N_HEADS * V_SLAB, blk), F32),
                        pltpu.VMEM((N_HEADS, blk + N_META, blk), F32),
                        pltpu.VMEM((D_ATTN, LANES), F32)],
        compiler_params=_params(1), name="prompt_attn",
    )(qt, kt, vt, lft, lft_meta, lfp, lfp_meta, q_gain, k_gain)


def _sample_attn_kernel(q_ref, kn_ref, vn_ref, lf_ref, lft_ref, ckt_ref, cvt_ref, clft_ref,
                        o_ref, crow, *, past, dec):
    blk = ATTN_BLOCK
    n_keys = past + LANES

    upper, _ = _triangles(blk)
    off = jnp.zeros((N_HEADS, 1), F32)
    for j in range(past // blk):
        loc = _cumsum_lanes(clft_ref[:, j * blk:(j + 1) * blk], upper) + off
        crow[:, j * blk:(j + 1) * blk] = loc
        off = loc[:, blk - 1:blk]
    crow[:, 0:past] = crow[:, 0:past] - off

    cq_c = _cumsum_few(lf_ref[:, 0:N_HEADS], axis=0)
    cq_r = _cumsum_few(lft_ref[...], axis=1)
    crow[:, past:] = jnp.full((N_HEADS, LANES), MASKED_BIAS, F32)
    crow[:, past:past + dec] = cq_r

    q = q_ref[...]
    lane_head = lax.broadcasted_iota(jnp.int32, (dec, D_ATTN), 1) // HEAD_DIM
    q_exp = jnp.concatenate(
        [jnp.where(lane_head == h, q, jnp.zeros_like(q)) for h in range(N_HEADS)], axis=0)
    pad_rows = jnp.zeros((LANES - dec, D_ATTN), BF16)
    k_new = jnp.concatenate([kn_ref[...].astype(BF16), pad_rows], axis=0)
    v_new = jnp.concatenate([vn_ref[...].astype(BF16), pad_rows], axis=0)
    s_all = jnp.concatenate(
        [_dot(q_exp, ckt_ref[...].astype(BF16)), _dot_nt(q_exp, k_new)], axis=1)

    kpos = lax.broadcasted_iota(jnp.int32, (dec, n_keys), 1)
    qpos = past + lax.broadcasted_iota(jnp.int32, (dec, n_keys), 0)
    visible = kpos <= qpos
    probs = []
    norms = []
    for h in range(N_HEADS):
        s = s_all[h * dec:(h + 1) * dec, :] + cq_c[:, h:h + 1] - crow[h:h + 1, :]
        s = jnp.where(visible, s, -jnp.inf)
        p = jnp.exp(s - jnp.max(s, axis=-1, keepdims=True))
        norms.append(jnp.sum(p, axis=-1, keepdims=True))
        probs.append(p.astype(BF16))
    p_all = jnp.concatenate(probs, axis=0)
    o_all = (_dot_nt(p_all[:, 0:past], cvt_ref[...].astype(BF16))
             + _dot(p_all[:, past:], v_new))
    out = jnp.zeros((dec, D_ATTN), F32)
    for h in range(N_HEADS):
        o = o_all[h * dec:(h + 1) * dec, :] / norms[h]
        out = out + jnp.where(lane_head == h, o, 0.0)
    o_ref[...] = out.astype(BF16)


def _sample_attention(q, k_new, v_new, lfp, lft, cache_kt, cache_vt, cache_lft):
    b, dec, _ = q.shape
    past = cache_kt.shape[2]
    per_b = lambda *shape: pl.BlockSpec((None,) + shape, lambda i: (i,) + (0,) * len(shape))
    kern = functools.partial(_sample_attn_kernel, past=past, dec=dec)
    return pl.pallas_call(
        kern, grid=(b,),
        in_specs=[per_b(dec, D_ATTN), per_b(dec, D_ATTN), per_b(dec, D_ATTN),
                  per_b(dec, LANES), per_b(N_HEADS, dec),
                  per_b(D_ATTN, past), per_b(D_ATTN, past), per_b(N_HEADS, past)],
        out_specs=per_b(dec, D_ATTN),
        out_shape=jax.ShapeDtypeStruct((b, dec, D_ATTN), BF16),
        scratch_shapes=[pltpu.VMEM((N_HEADS, past + LANES), F32)],
        compiler_params=_params(1), name="sample_attn",
    )(q, k_new, v_new, lfp, lft, cache_kt, cache_vt, cache_lft)


def _merge_mlp_tail(x, conv_bf16, attn_ref, g_conv, g_attn, wbc_ref, wba_ref, wo_ref, g2_ref,
                    wup_ref, wdn_ref, y_ref):
    merged = g_conv * _dot(conv_bf16, wbc_ref[...]) + g_attn * _dot(attn_ref[...], wba_ref[...])
    h = x + _dot(merged.astype(BF16), wo_ref[...])
    hn = _rms_rows(h, g2_ref[...])
    acc = h
    for c in range(D_FF // D_MODEL):
        cols = slice(c * D_MODEL, (c + 1) * D_MODEL)
        a = jnp.maximum(_dot(hn, wup_ref[:, cols]), 0.0)
        acc = acc + _dot((a * a).astype(BF16), wdn_ref[cols, :])
    y_ref[...] = acc


def _branch_mlp_kernel(x_ref, left_ref, attn_ref, xs_ref, convs_ref, attns_ref, gates_ref,
                       g1_ref, wa_ref, wgl_ref, cw_ref, cb_ref,
                       wbc_ref, wba_ref, wo_ref, g2_ref, wup_ref, wdn_ref,
                       y_ref, zlast_ref, ys_ref, zbuf, *, n_tiles, tiles_per_seq):
    step = pl.program_id(0)

    @pl.when(jnp.logical_and(step % tiles_per_seq == 0, step < n_tiles))
    def _():
        zbuf[:, ZPAD - 2:ZPAD, :] = left_ref[...]

    @pl.when(step < n_tiles)
    def _():
        x = x_ref[...]
        xn = _rms_rows(x, g1_ref[...])
        conv = _short_conv(xn, wa_ref, cw_ref, cb_ref, zbuf, zlast_ref, 1, x.shape[0])[0]
        g_conv = jax.nn.sigmoid(_dot_nt(xn, wgl_ref[0:D_MODEL, :]))
        g_attn = jax.nn.sigmoid(_dot_nt(xn, wgl_ref[D_MODEL:2 * D_MODEL, :]))
        _merge_mlp_tail(x, conv.astype(BF16), attn_ref, g_conv, g_attn, wbc_ref, wba_ref,
                        wo_ref, g2_ref, wup_ref, wdn_ref, y_ref)

    @pl.when(step == n_tiles)
    def _():
        _merge_mlp_tail(xs_ref[...], convs_ref[...], attns_ref,
                        gates_ref[:, 0:D_MODEL].astype(F32),
                        gates_ref[:, D_MODEL:2 * D_MODEL].astype(F32),
                        wbc_ref, wba_ref, wo_ref, g2_ref, wup_ref, wdn_ref, ys_ref)


def _branch_mlp(x2d, left, attn, small, wts, *, rows, seq):
    n_rows = x2d.shape[0]
    tiles = seq // rows
    n_tiles = n_rows // rows
    last = n_tiles - 1
    row_spec = lambda width: pl.BlockSpec((rows, width), lambda i: (jnp.minimum(i, last), 0))
    seq_spec = pl.BlockSpec((1, CONV_W - 1, D_CONV),
                            lambda i: (jnp.minimum(i, last) // tiles, 0, 0))
    full = lambda a: pl.BlockSpec(a.shape, lambda i: (0,) * a.ndim)
    kern = functools.partial(_branch_mlp_kernel, n_tiles=n_tiles, tiles_per_seq=tiles)
    return pl.pallas_call(
        kern, grid=(n_tiles + 1,),
        in_specs=[row_spec(D_MODEL), seq_spec, row_spec(D_ATTN)] + [full(a) for a in small]
        + [_const_spec(w.shape) for w in wts],
        out_specs=(row_spec(D_MODEL), seq_spec, full(small[0])),
        out_shape=(jax.ShapeDtypeStruct((n_rows, D_MODEL), F32),
                   jax.ShapeDtypeStruct((n_rows // seq, CONV_W - 1, D_CONV), F32),
                   jax.ShapeDtypeStruct(small[0].shape, F32)),
        scratch_shapes=[pltpu.VMEM((1, rows + ZPAD, D_CONV), F32)],
        compiler_params=_params(1), name="branch_mlp",
    )(x2d, left, attn, *small, *wts)


def kernel(x_prompt, x_sample, cache_k, cache_v, cache_logf, state_conv, meta,
           norm1_g, w_in, b_f, conv_w, conv_b, q_norm_g, k_norm_g,
           w_br_conv, w_br_attn, w_out, norm2_g, w_up, w_down):
    b, seq, _ = x_prompt.shape
    db, dec, _ = x_sample.shape
    past = cache_k.shape[2]
    length = N_META + seq
    n_main = 3 * D_CONV + 3 * D_ATTN
    q0 = 3 * D_CONV

    wt = w_in[0].T
    wt_conv = wt[0:q0].astype(BF16)
    wt_qkv = wt[q0:n_main].astype(BF16)
    wt_fl = wt[n_main:n_main + N_HEADS].astype(BF16)
    wt_gate = wt[n_main + N_HEADS:].astype(BF16)
    head_of = jnp.arange(D_ATTN) // HEAD_DIM
    qg = jnp.tile(q_norm_g[0], N_HEADS)
    kg = jnp.tile(k_norm_g[0], N_HEADS)
    g1 = norm1_g[0][None, :]
    qkv_wts = (g1, wt_qkv, jnp.tile(wt_fl, (LANES // N_HEADS, 1)),
               jnp.tile(b_f[0], LANES // N_HEADS)[None, :], qg[:, None], kg[:, None])
    conv_wts = (conv_w[0], conv_b[0][None, :])
    mlp_wts = (w_br_conv[0].astype(BF16), w_br_attn[0].astype(BF16), w_out[0].astype(BF16),
               norm2_g[0][None, :], w_up[0].astype(BF16), w_down[0].astype(BF16))

    x_small = jnp.concatenate([meta, x_sample.reshape(db * dec, D_MODEL)], axis=0)
    left_small = jnp.concatenate(
        [jnp.zeros((1, CONV_W - 1, D_CONV), F32), state_conv[0]], axis=0)
    (conv_s, q_s, k_s, v_s, kt_s, vt_s, lft_s, lfp_s, gate_s, zlast_s) = _project_small(
        x_small, left_small,
        qkv_wts + (wt_conv, wt_gate) + conv_wts
        + (wt_fl, b_f[0][:, None], qg[None, :], kg[None, :],
           (head_of[:, None] == head_of[None, :]).astype(BF16)))

    x_rows = x_prompt.reshape(b * seq, D_MODEL)
    (qt_p, kt_p, vt_p, lftp_p, lft_p, lfp_p, ktail, vtail, ltail) = _project_qkv(
        x_rows, qkv_wts, (kt_s, vt_s, lft_s), b=b, seq=seq)
    kt_p, vt_p, lftp_p = _write_tails(kt_p, vt_p, lftp_p, ktail, vtail, ltail)
    attn_p = _prompt_attention(
        qt_p, kt_p, vt_p, lft_p, lft_s[:, :N_META], lfp_p.reshape(b, seq, LANES),
        lfp_s[:N_META], q_norm_g, k_norm_g)
    k_new = k_s[N_META:].reshape(db, dec, D_ATTN)
    v_new = v_s[N_META:].reshape(db, dec, D_ATTN)
    lf_new = lfp_s[N_META:, :N_HEADS].reshape(db, dec, N_HEADS)
    cache_kt = jnp.transpose(cache_k[0], (0, 2, 3, 1)).reshape(db, D_ATTN, past)
    cache_vt = jnp.transpose(cache_v[0], (0, 2, 3, 1)).reshape(db, D_ATTN, past)
    attn_s = _sample_attention(
        q_s[N_META:].reshape(db, dec, D_ATTN), k_new, v_new,
        lfp_s[N_META:].reshape(db, dec, LANES), jnp.swapaxes(lf_new, 1, 2),
        cache_kt, cache_vt, jnp.swapaxes(cache_logf[0], 1, 2))

    left_p = jnp.broadcast_to(zlast_s[0:1], (b, CONV_W - 1, D_CONV))
    y_prompt, zlast_p, y_sample = _branch_mlp(
        x_rows, left_p, attn_p.reshape(b * seq, D_ATTN),
        (x_sample.reshape(db * dec, D_MODEL), conv_s[N_META:],
         attn_s.reshape(db * dec, D_ATTN), gate_s[N_META:]),
        (g1, wt_conv, wt_gate) + conv_wts + mlp_wts, rows=MLP_ROWS, seq=seq)

    def heads_last(t):
        return jnp.transpose(t.reshape(b, N_HEADS, HEAD_DIM, length), (0, 3, 1, 2))[None]

    return (y_prompt.reshape(b, seq, D_MODEL),
            y_sample.reshape(db, dec, D_MODEL),
            heads_last(kt_p),
            heads_last(vt_p),
            jnp.swapaxes(lftp_p, 1, 2)[None],
            zlast_p[None],
            k_new.reshape(1, db, dec, N_HEADS, HEAD_DIM),
            v_new.reshape(1, db, dec, N_HEADS, HEAD_DIM),
            lf_new[None],
            zlast_s[1:][None])
```

```python
import functools

import jax
import jax.numpy as jnp
from jax import lax
from jax.experimental import pallas as pl
from jax.experimental.pallas import tpu as pltpu

D_MODEL = 1024
D_CONV = D_MODEL // 2
CONV_W = 3
N_HEADS = 8
HEAD_DIM = 64
D_ATTN = N_HEADS * HEAD_DIM
D_FF = 4 * D_MODEL
N_META = 16
EPS = 1e-6
ATTN_SCALE = HEAD_DIM ** -0.5

F32 = jnp.float32
BF16 = jnp.bfloat16

VMEM_LIMIT_BYTES = 56 * 1024 * 1024
LANES = 128
PROJ_ROWS = 1024
MLP_ROWS = 512
ATTN_BLOCK = 256
MASKED_BIAS = 1e30
ZPAD = 8
N_SPLIT = 3
V_SLAB = HEAD_DIM + 16
LOG2E = 1.4426950408889634
SKIP_LOG2 = 40.0
NORM_SLACK = 1.02


def _dot(a, b):
    return jnp.dot(a, b, preferred_element_type=F32)


def _dot_nt(a, b):
    return lax.dot_general(a, b, (((1,), (1,)), ((), ())), preferred_element_type=F32)


def _log_sigmoid(x):
    return jnp.minimum(x, 0.0) - jnp.log1p(jnp.exp(-jnp.abs(x)))


def _cumsum_few(x, axis):
    n = x.shape[axis]
    idx = lax.broadcasted_iota(jnp.int32, x.shape, axis)
    out = jnp.zeros(x.shape, F32)
    for i in range(n):
        term = x[i:i + 1, :] if axis == 0 else x[:, i:i + 1]
        out = out + jnp.where(idx >= i, term, 0.0)
    return out


def _triangles(n):
    r = lax.broadcasted_iota(jnp.int32, (n, n), 0)
    c = lax.broadcasted_iota(jnp.int32, (n, n), 1)
    upper = jnp.where(r <= c, 1.0, 0.0).astype(BF16)
    lower = jnp.where(r >= c, 1.0, 0.0).astype(BF16)
    return upper, lower


def _split3(c):
    hi = c.astype(BF16).astype(F32)
    r1 = c - hi
    mid = r1.astype(BF16).astype(F32)
    return hi, mid, r1 - mid


def _cumsum_rows(tri_lower, x):
    w = x.shape[1]
    pieces = jnp.concatenate(_split3(x), axis=1).astype(BF16)
    y = _dot(tri_lower, pieces)
    return y[:, 0:w] + y[:, w:2 * w] + y[:, 2 * w:3 * w]


def _cumsum_lanes(x, tri_upper):
    h = x.shape[0]
    pieces = jnp.concatenate(_split3(x), axis=0).astype(BF16)
    y = _dot(pieces, tri_upper)
    return y[0:h] + y[h:2 * h] + y[2 * h:3 * h]


def _const_spec(shape):
    nd = len(shape)
    return pl.BlockSpec(shape, lambda *_: (0,) * nd, pipeline_mode=pl.Buffered(1))


def _params(n_axes):
    return pltpu.CompilerParams(
        dimension_semantics=("arbitrary",) * n_axes,
        vmem_limit_bytes=VMEM_LIMIT_BYTES)


def _rms_rows(x, g_row):
    ms = jnp.mean(x * x, axis=-1, keepdims=True)
    return (x * lax.rsqrt(ms + EPS) * g_row).astype(BF16)


def _head_norm_t(ut, g_col):
    out = []
    for h in range(N_HEADS):
        blk = ut[h * HEAD_DIM:(h + 1) * HEAD_DIM, :]
        ms = jnp.mean(blk * blk, axis=0, keepdims=True)
        out.append(blk * lax.rsqrt(ms + EPS) * g_col[h * HEAD_DIM:(h + 1) * HEAD_DIM, :])
    return jnp.concatenate(out, axis=0)


def _short_conv(xn, wa_ref, cw_ref, cb_ref, zbuf, zlast_ref, n_seg, seg_len):
    cb = _dot_nt(xn, wa_ref[0:D_CONV, :])
    z = (_dot_nt(xn, wa_ref[D_CONV:2 * D_CONV, :])
         * _dot_nt(xn, wa_ref[2 * D_CONV:3 * D_CONV, :]))
    w0 = cw_ref[0:1, :]
    w1 = cw_ref[1:2, :]
    w2 = cw_ref[2:3, :]
    out = []
    for s in range(n_seg):
        r0 = s * seg_len
        zs = z[r0:r0 + seg_len]
        zbuf[s, ZPAD:ZPAD + seg_len, :] = zs
        zm1 = zbuf[s, ZPAD - 1:ZPAD - 1 + seg_len, :]
        zm2 = zbuf[s, ZPAD - 2:ZPAD - 2 + seg_len, :]
        y = zm2 * w0 + zm1 * w1 + zs * w2 + cb_ref[...]
        out.append(cb[r0:r0 + seg_len] * y)
        tail = zbuf[s, ZPAD + seg_len - 2:ZPAD + seg_len, :]
        zlast_ref[s] = tail
        zbuf[s, ZPAD - 2:ZPAD, :] = tail
    return out


def _qkv_kernel(x_ref, g1_ref, wqkv_ref, wfl_ref, bfr_ref, qgc_ref, kgc_ref,
                ktm_ref, vtm_ref, lftm_ref,
                qt_ref, kt_ref, vt_ref, lftp_ref, lft_ref, lfp_ref, kcar, vcar, lcar,
                *, tiles_per_seq):
    rows = x_ref.shape[0]
    step = pl.program_id(0) % tiles_per_seq
    lane0 = pl.multiple_of(step * rows, rows)

    def shifted_store(out_ref, car_ref, tile):
        rolled = pltpu.roll(tile, N_META, axis=1)
        lane = lax.broadcasted_iota(jnp.int32, (tile.shape[0], LANES), 1)
        out_ref[:, pl.ds(lane0, LANES)] = jnp.where(lane < N_META, car_ref[...],
                                                    rolled[:, 0:LANES])
        out_ref[:, pl.ds(lane0 + LANES, rows - LANES)] = rolled[:, LANES:]
        car_ref[...] = rolled[:, 0:LANES]
        out_ref[:, tiles_per_seq * rows:] = rolled[:, 0:N_META]

    @pl.when(step == 0)
    def _():
        kcar[...] = ktm_ref[:, 0:LANES]
        vcar[...] = vtm_ref[:, 0:LANES]
        lcar[...] = lftm_ref[:, 0:LANES]

    xn = _rms_rows(x_ref[...], g1_ref[...])

    def feature_major(j):
        return _dot_nt(wqkv_ref[j * D_ATTN:(j + 1) * D_ATTN, :], xn)

    lfp = _log_sigmoid(_dot_nt(xn, wfl_ref[...]) + bfr_ref[...])
    lfp_ref[...] = lfp
    lft = lfp.T[0:N_HEADS, :]
    lft_ref[...] = lft
    qt = _head_norm_t(feature_major(0), qgc_ref[...])
    qt_ref[...] = (qt * (ATTN_SCALE * LOG2E)).astype(BF16)
    shifted_store(kt_ref, kcar, _head_norm_t(feature_major(1), kgc_ref[...]))
    shifted_store(vt_ref, vcar, feature_major(2))
    shifted_store(lftp_ref, lcar, lft)


def _project_qkv(x2d, wts, meta_cols, *, b, seq):
    rows = PROJ_ROWS
    tiles = seq // rows
    length = N_META + seq
    row_spec = lambda width: pl.BlockSpec((rows, width), lambda i: (i, 0))
    col_spec = lambda feat: pl.BlockSpec(
        (None, feat, rows), lambda i: (i // tiles, 0, i % tiles))
    seq_spec = lambda feat: pl.BlockSpec((None, feat, length), lambda i: (i // tiles, 0, 0))
    out_shape = (
        jax.ShapeDtypeStruct((b, D_ATTN, seq), BF16),
        jax.ShapeDtypeStruct((b, D_ATTN, length), F32),
        jax.ShapeDtypeStruct((b, D_ATTN, length), F32),
        jax.ShapeDtypeStruct((b, N_HEADS, length), F32),
        jax.ShapeDtypeStruct((b, N_HEADS, seq), F32),
        jax.ShapeDtypeStruct((b * seq, LANES), F32),
    )
    out_specs = (col_spec(D_ATTN), seq_spec(D_ATTN), seq_spec(D_ATTN), seq_spec(N_HEADS),
                 col_spec(N_HEADS), row_spec(LANES))
    kern = functools.partial(_qkv_kernel, tiles_per_seq=tiles)
    return pl.pallas_call(
        kern, grid=(b * tiles,),
        in_specs=[row_spec(D_MODEL)] + [_const_spec(w.shape) for w in wts + meta_cols],
        out_specs=out_specs, out_shape=out_shape,
        scratch_shapes=[pltpu.VMEM((D_ATTN, LANES), F32),
                        pltpu.VMEM((D_ATTN, LANES), F32),
                        pltpu.VMEM((N_HEADS, LANES), F32)],
        compiler_params=_params(1), name="proj_qkv",
    )(x2d, *wts, *meta_cols)


def _proj_small_kernel(x_ref, left_ref, g1_ref, wqkv_ref, wfl_ref, bfr_ref, qgc_ref, kgc_ref,
                       wa_ref, wgl_ref, cw_ref, cb_ref, wflt_ref, bfc_ref, qgr_ref, kgr_ref,
                       bd_ref,
                       conv_ref, q_ref, k_ref, v_ref, kt_ref, vt_ref, lft_ref, lfp_ref,
                       gate_ref, zlast_ref, zbuf, *, n_seg, seg_len):
    xn = _rms_rows(x_ref[...], g1_ref[...])
    zbuf[:, ZPAD - 2:ZPAD, :] = left_ref[...]
    conv = _short_conv(xn, wa_ref, cw_ref, cb_ref, zbuf, zlast_ref, n_seg, seg_len)
    for s in range(n_seg):
        conv_ref[s * seg_len:(s + 1) * seg_len, :] = conv[s].astype(BF16)

    def rows_major(j):
        return _dot_nt(xn, wqkv_ref[j * D_ATTN:(j + 1) * D_ATTN, :])

    def feature_major(j):
        return _dot_nt(wqkv_ref[j * D_ATTN:(j + 1) * D_ATTN, :], xn)

    def head_norm(u, g_row):
        ssq = _dot((u * u).astype(BF16), bd_ref[...])
        return u * lax.rsqrt(ssq * (1.0 / HEAD_DIM) + EPS) * g_row

    q_ref[...] = (head_norm(rows_major(0), qgr_ref[...]) * ATTN_SCALE).astype(BF16)
    k_ref[...] = head_norm(rows_major(1), kgr_ref[...])
    v_ref[...] = rows_major(2)
    kt_ref[...] = _head_norm_t(feature_major(1), kgc_ref[...])
    vt_ref[...] = feature_major(2)
    lfp_ref[...] = _log_sigmoid(_dot_nt(xn, wfl_ref[...]) + bfr_ref[...])
    lft_ref[...] = _log_sigmoid(_dot_nt(wflt_ref[...], xn) + bfc_ref[...])
    gate_ref[...] = jax.nn.sigmoid(_dot_nt(xn, wgl_ref[...])).astype(BF16)


def _project_small(x2d, left, wts):
    n_rows = x2d.shape[0]
    n_seq = left.shape[0]
    seg_len = n_rows // n_seq
    full = lambda *shape: pl.BlockSpec(shape, lambda i: (0,) * len(shape))
    out_shape = (
        jax.ShapeDtypeStruct((n_rows, D_CONV), BF16),
        jax.ShapeDtypeStruct((n_rows, D_ATTN), BF16),
        jax.ShapeDtypeStruct((n_rows, D_ATTN), F32),
        jax.ShapeDtypeStruct((n_rows, D_ATTN), F32),
        jax.ShapeDtypeStruct((D_ATTN, n_rows), F32),
        jax.ShapeDtypeStruct((D_ATTN, n_rows), F32),
        jax.ShapeDtypeStruct((N_HEADS, n_rows), F32),
        jax.ShapeDtypeStruct((n_rows, LANES), F32),
        jax.ShapeDtypeStruct((n_rows, 2 * D_MODEL), BF16),
        jax.ShapeDtypeStruct((n_seq, CONV_W - 1, D_CONV), F32),
    )
    kern = functools.partial(_proj_small_kernel, n_seg=n_seq, seg_len=seg_len)
    return pl.pallas_call(
        kern, grid=(1,),
        in_specs=[full(*x2d.shape), full(*left.shape)] + [_const_spec(w.shape) for w in wts],
        out_specs=tuple(full(*s.shape) for s in out_shape), out_shape=out_shape,
        scratch_shapes=[pltpu.VMEM((n_seq, seg_len + ZPAD, D_CONV), F32)],
        compiler_params=_params(1), name="proj_small",
    )(x2d, left, *wts)


def _prompt_attn_kernel(qt_ref, kt_ref, vt_ref, lft_ref, lftm_ref, lfp_ref, lfpm_ref,
                        qg_ref, kg_ref, o_ref, kpos, vb, kbias, crow, cend, qcat, m_s, acc_s,
                        sbuf, tail, *, seq):
    blk = ATTN_BLOCK
    n_blk = seq // blk
    length = N_META + seq
    n_pos = kpos.shape[0]
    n_bias = N_SPLIT * N_HEADS

    for j in range(n_blk):
        kpos[j * blk:(j + 1) * blk, :] = kt_ref[:, j * blk:(j + 1) * blk].T.astype(BF16)
    tail[...] = jnp.zeros(tail.shape, F32)
    tail[:, 0:N_META] = kt_ref[:, seq:length]
    kpos[seq:seq + LANES, :] = tail[...].T.astype(BF16)
    kpos[seq + LANES:, :] = jnp.zeros((n_pos - seq - LANES, D_ATTN), BF16)

    ones_row = (lax.broadcasted_iota(jnp.int32, (V_SLAB - HEAD_DIM, blk), 0) == 0).astype(BF16)
    vb[n_blk] = jnp.zeros((N_HEADS * V_SLAB, blk), BF16)
    for h in range(N_HEADS):
        rows = slice(h * HEAD_DIM, (h + 1) * HEAD_DIM)
        slab = slice(h * V_SLAB, h * V_SLAB + HEAD_DIM)
        for j in range(n_blk + 1):
            if j < n_blk:
                vb[j, slab, :] = vt_ref[rows, j * blk:(j + 1) * blk].astype(BF16)
            else:
                vb[j, slab, 0:N_META] = vt_ref[rows, seq:length].astype(BF16)
            vb[j, h * V_SLAB + HEAD_DIM:(h + 1) * V_SLAB, :] = ones_row

    qk_bound = (NORM_SLACK * HEAD_DIM * ATTN_SCALE * LOG2E
                * jnp.max(jnp.abs(qg_ref[...]), axis=1, keepdims=True)
                * jnp.max(jnp.abs(kg_ref[...]), axis=1, keepdims=True))

    upper, lower = _triangles(blk)

    def store_kbias(rows, c_col):
        hi, mid, lo = _split3(c_col * LOG2E)
        lane = lax.broadcasted_iota(jnp.int32, c_col.shape, 1)
        grp = lane // N_HEADS
        part = jnp.where(grp == 0, hi, jnp.where(grp == 1, mid, lo))
        kbias[rows, :] = jnp.where(lane < n_bias, -part,
                                   jnp.where(lane < 2 * n_bias, 1.0, 0.0)).astype(BF16)

    off_c = jnp.zeros((1, LANES), F32)
    for j in range(n_blk):
        if j == 0:
            lf_blk = jnp.concatenate([lfpm_ref[...], lfp_ref[0:blk - N_META, :]], axis=0)
        else:
            lf_blk = lfp_ref[j * blk - N_META:(j + 1) * blk - N_META, :]
        c_col = _cumsum_rows(lower, lf_blk) + off_c
        store_kbias(slice(j * blk, (j + 1) * blk), c_col)
        off_c = c_col[blk - 1:blk, :]
    c_col = _cumsum_few(lfp_ref[seq - N_META:seq, :], axis=0) + off_c
    store_kbias(slice(seq, length), c_col)
    kbias[length:, :] = jnp.zeros((n_pos - length, LANES), BF16)

    meta_r = _cumsum_few(lftm_ref[...], axis=1)
    off_r = meta_r[:, N_META - 1:N_META]
    lane_h = lax.broadcasted_iota(jnp.int32, (N_HEADS, LANES), 1)
    c_end = jnp.zeros((N_HEADS, LANES), F32)
    for j in range(n_blk):
        c_row = _cumsum_lanes(lft_ref[:, j * blk:(j + 1) * blk], upper) + off_r
        crow[j] = c_row * LOG2E
        off_r = c_row[:, blk - 1:blk]
        c_end = jnp.where(lane_h == j, off_r * LOG2E, c_end)
    cend[...] = c_end

    row128 = lax.broadcasted_iota(jnp.int32, (LANES, blk), 0)
    krow = lax.broadcasted_iota(jnp.int32, (blk + N_META, blk), 0)
    qcol = lax.broadcasted_iota(jnp.int32, (blk + N_META, blk), 1)

    def q_block(t, _):
        tok0 = pl.multiple_of(t * blk, blk)
        c_q = crow[t]
        hi, mid, lo = _split3(c_q)
        bias_rows = jnp.concatenate(
            [jnp.ones((n_bias, blk), F32), hi, mid, lo,
             jnp.zeros((LANES - 2 * n_bias, blk), F32)], axis=0)
        for h in range(N_HEADS):
            pair = qt_ref[(h // 2) * LANES:(h // 2 + 1) * LANES, pl.ds(tok0, blk)]
            in_head = (row128 // HEAD_DIM) == (h % 2)
            qcat[h, 0:LANES, :] = jnp.where(in_head, pair, jnp.zeros_like(pair))
            qcat[h, LANES:, :] = jnp.where(row128 % N_HEADS == h, bias_rows, 0.0).astype(BF16)
        m_s[...] = jnp.full(m_s.shape, -jnp.inf, F32)
        acc_s[...] = jnp.zeros(acc_s.shape, F32)

        gap = 2.0 * qk_bound + c_q[:, 0:1] - cend[...]
        needed = jnp.logical_and(gap >= -SKIP_LOG2, lane_h < t)
        n_needed = jnp.max(
            jnp.sum(jnp.where(needed, 1.0, 0.0), axis=1, keepdims=True)).astype(jnp.int32)

        def key_block(pos0, n_rows, pv, visible):
            m_blk = []
            for h in range(N_HEADS):
                g = h // 2
                kc = jnp.concatenate([kpos[pl.ds(pos0, n_rows), g * LANES:(g + 1) * LANES],
                                      kbias[pl.ds(pos0, n_rows), :]], axis=1)
                s = _dot(kc, qcat[h])
                if visible is not None:
                    s = jnp.where(visible, s, -jnp.inf)
                sbuf[h, 0:n_rows, :] = s
                m_blk.append(jnp.max(s, axis=0, keepdims=True))
            for h in range(N_HEADS):
                slab = slice(h * V_SLAB, (h + 1) * V_SLAB)
                m_old = m_s[h:h + 1, :]
                m_new = jnp.maximum(m_old, m_blk[h])
                alpha = jnp.exp2(m_old - m_new)
                p = jnp.exp2(sbuf[h, 0:n_rows, :] - m_new)
                m_s[h:h + 1, :] = m_new
                acc_s[slab, :] = alpha * acc_s[slab, :] + pv(slab, p.astype(BF16))

        def full_block(j, _):
            key_block(pl.multiple_of(j * blk, blk), blk,
                      lambda slab, p: _dot(vb[j, slab, :], p), None)
            return 0

        def own_pv(slab, p):
            spill = jnp.concatenate([p[blk:], jnp.zeros((LANES - N_META, blk), BF16)], axis=0)
            return _dot(vb[t, slab, :], p[0:blk]) + _dot(vb[t + 1, slab, 0:LANES], spill)

        key_block(tok0, blk + N_META, own_pv, krow <= qcol + N_META)
        lax.fori_loop(t - n_needed, t, full_block, 0)

        o_t = []
        for h in range(N_HEADS):
            norm = acc_s[h * V_SLAB + HEAD_DIM:h * V_SLAB + HEAD_DIM + 1, :]
            o_t.append(acc_s[h * V_SLAB:h * V_SLAB + HEAD_DIM, :] * (1.0 / norm))
        o_ref[pl.ds(tok0, blk), :] = jnp.concatenate(o_t, axis=0).T.astype(BF16)
        return 0

    lax.fori_loop(0, n_blk, q_block, 0)


def _prompt_attention(qt, kt, vt, lft, lft_meta, lfp, lfp_meta, q_gain, k_gain):
    b, _, seq = qt.shape
    length = vt.shape[2]
    blk = ATTN_BLOCK
    n_blk = seq // blk
    n_pos = (n_blk + 1) * blk
    per_b = lambda *shape: pl.BlockSpec((None,) + shape, lambda i: (i,) + (0,) * len(shape))
    kern = functools.partial(_prompt_attn_kernel, seq=seq)
    return pl.pallas_call(
        kern, grid=(b,),
        in_specs=[per_b(D_ATTN, seq), per_b(D_ATTN, length), per_b(D_ATTN, length),
                  per_b(N_HEADS, seq), _const_spec(lft_meta.shape),
                  per_b(seq, LANES), _const_spec(lfp_meta.shape),
                  _const_spec(q_gain.shape), _const_spec(k_gain.shape)],
        out_specs=per_b(seq, D_ATTN),
        out_shape=jax.ShapeDtypeStruct((b, seq, D_ATTN), BF16),
        scratch_shapes=[pltpu.VMEM((n_pos, D_ATTN), BF16),
                        pltpu.VMEM((n_blk + 1, N_HEADS * V_SLAB, blk), BF16),
                        pltpu.VMEM((n_pos, LANES), BF16),
                        pltpu.VMEM((n_blk, N_HEADS, blk), F32),
                        pltpu.VMEM((N_HEADS, LANES), F32),
                        pltpu.VMEM((N_HEADS, 2 * LANES, blk), BF16),
                        pltpu.VMEM((N_HEADS, blk), F32),
                        pltpu.VMEM((N_HEADS * V_SLAB, blk), F32),
                        pltpu.VMEM((N_HEADS, blk + N_META, blk), F32),
                        pltpu.VMEM((D_ATTN, LANES), F32)],
        compiler_params=_params(1), name="prompt_attn",
    )(qt, kt, vt, lft, lft_meta, lfp, lfp_meta, q_gain, k_gain)


def _sample_attn_kernel(q_ref, kn_ref, vn_ref, lf_ref, lft_ref, ckt_ref, cvt_ref, clft_ref,
                        o_ref, crow, *, past, dec):
    blk = ATTN_BLOCK
    n_keys = past + LANES

    upper, _ = _triangles(blk)
    off = jnp.zeros((N_HEADS, 1), F32)
    for j in range(past // blk):
        loc = _cumsum_lanes(clft_ref[:, j * blk:(j + 1) * blk], upper) + off
        crow[:, j * blk:(j + 1) * blk] = loc
        off = loc[:, blk - 1:blk]
    crow[:, 0:past] = crow[:, 0:past] - off

    cq_c = _cumsum_few(lf_ref[:, 0:N_HEADS], axis=0)
    cq_r = _cumsum_few(lft_ref[...], axis=1)
    crow[:, past:] = jnp.full((N_HEADS, LANES), MASKED_BIAS, F32)
    crow[:, past:past + dec] = cq_r

    q = q_ref[...]
    lane_head = lax.broadcasted_iota(jnp.int32, (dec, D_ATTN), 1) // HEAD_DIM
    q_exp = jnp.concatenate(
        [jnp.where(lane_head == h, q, jnp.zeros_like(q)) for h in range(N_HEADS)], axis=0)
    pad_rows = jnp.zeros((LANES - dec, D_ATTN), BF16)
    k_new = jnp.concatenate([kn_ref[...].astype(BF16), pad_rows], axis=0)
    v_new = jnp.concatenate([vn_ref[...].astype(BF16), pad_rows], axis=0)
    s_all = jnp.concatenate(
        [_dot(q_exp, ckt_ref[...].astype(BF16)), _dot_nt(q_exp, k_new)], axis=1)

    kpos = lax.broadcasted_iota(jnp.int32, (dec, n_keys), 1)
    qpos = past + lax.broadcasted_iota(jnp.int32, (dec, n_keys), 0)
    visible = kpos <= qpos
    probs = []
    norms = []
    for h in range(N_HEADS):
        s = s_all[h * dec:(h + 1) * dec, :] + cq_c[:, h:h + 1] - crow[h:h + 1, :]
        s = jnp.where(visible, s, -jnp.inf)
        p = jnp.exp(s - jnp.max(s, axis=-1, keepdims=True))
        norms.append(jnp.sum(p, axis=-1, keepdims=True))
        probs.append(p.astype(BF16))
    p_all = jnp.concatenate(probs, axis=0)
    o_all = (_dot_nt(p_all[:, 0:past], cvt_ref[...].astype(BF16))
             + _dot(p_all[:, past:], v_new))
    out = jnp.zeros((dec, D_ATTN), F32)
    for h in range(N_HEADS):
        o = o_all[h * dec:(h + 1) * dec, :] / norms[h]
        out = out + jnp.where(lane_head == h, o, 0.0)
    o_ref[...] = out.astype(BF16)


def _sample_attention(q, k_new, v_new, lfp, lft, cache_kt, cache_vt, cache_lft):
    b, dec, _ = q.shape
    past = cache_kt.shape[2]
    per_b = lambda *shape: pl.BlockSpec((None,) + shape, lambda i: (i,) + (0,) * len(shape))
    kern = functools.partial(_sample_attn_kernel, past=past, dec=dec)
    return pl.pallas_call(
        kern, grid=(b,),
        in_specs=[per_b(dec, D_ATTN), per_b(dec, D_ATTN), per_b(dec, D_ATTN),
                  per_b(dec, LANES), per_b(N_HEADS, dec),
                  per_b(D_ATTN, past), per_b(D_ATTN, past), per_b(N_HEADS, past)],
        out_specs=per_b(dec, D_ATTN),
        out_shape=jax.ShapeDtypeStruct((b, dec, D_ATTN), BF16),
        scratch_shapes=[pltpu.VMEM((N_HEADS, past + LANES), F32)],
        compiler_params=_params(1), name="sample_attn",
    )(q, k_new, v_new, lfp, lft, cache_kt, cache_vt, cache_lft)


def _merge_mlp_tail(x, conv_bf16, attn_ref, g_conv, g_attn, wbc_ref, wba_ref, wo_ref, g2_ref,
                    wup_ref, wdn_ref, y_ref):
    merged = g_conv * _dot(conv_bf16, wbc_ref[...]) + g_attn * _dot(attn_ref[...], wba_ref[...])
    h = x + _dot(merged.astype(BF16), wo_ref[...])
    hn = _rms_rows(h, g2_ref[...])
    acc = h
    for c in range(D_FF // D_MODEL):
        cols = slice(c * D_MODEL, (c + 1) * D_MODEL)
        a = jnp.maximum(_dot(hn, wup_ref[:, cols]), 0.0)
        acc = acc + _dot((a * a).astype(BF16), wdn_ref[cols, :])
    y_ref[...] = acc


def _branch_mlp_kernel(x_ref, left_ref, attn_ref, xs_ref, convs_ref, attns_ref, gates_ref,
                       g1_ref, wa_ref, wgl_ref, cw_ref, cb_ref,
                       wbc_ref, wba_ref, wo_ref, g2_ref, wup_ref, wdn_ref,
                       y_ref, zlast_ref, ys_ref, zbuf, *, n_tiles, tiles_per_seq):
    step = pl.program_id(0)

    @pl.when(jnp.logical_and(step % tiles_per_seq == 0, step < n_tiles))
    def _():
        zbuf[:, ZPAD - 2:ZPAD, :] = left_ref[...]

    @pl.when(step < n_tiles)
    def _():
        x = x_ref[...]
        xn = _rms_rows(x, g1_ref[...])
        conv = _short_conv(xn, wa_ref, cw_ref, cb_ref, zbuf, zlast_ref, 1, x.shape[0])[0]
        g_conv = jax.nn.sigmoid(_dot_nt(xn, wgl_ref[0:D_MODEL, :]))
        g_attn = jax.nn.sigmoid(_dot_nt(xn, wgl_ref[D_MODEL:2 * D_MODEL, :]))
        _merge_mlp_tail(x, conv.astype(BF16), attn_ref, g_conv, g_attn, wbc_ref, wba_ref,
                        wo_ref, g2_ref, wup_ref, wdn_ref, y_ref)

    @pl.when(step == n_tiles)
    def _():
        _merge_mlp_tail(xs_ref[...], convs_ref[...], attns_ref,
                        gates_ref[:, 0:D_MODEL].astype(F32),
                        gates_ref[:, D_MODEL:2 * D_MODEL].astype(F32),
                        wbc_ref, wba_ref, wo_ref, g2_ref, wup_ref, wdn_ref, ys_ref)


def _branch_mlp(x2d, left, attn, small, wts, *, rows, seq):
    n_rows = x2d.shape[0]
    tiles = seq // rows
    n_tiles = n_rows // rows
    last = n_tiles - 1
    row_spec = lambda width: pl.BlockSpec((rows, width), lambda i: (jnp.minimum(i, last), 0))
    seq_spec = pl.BlockSpec((1, CONV_W - 1, D_CONV),
                            lambda i: (jnp.minimum(i, last) // tiles, 0, 0))
    full = lambda a: pl.BlockSpec(a.shape, lambda i: (0,) * a.ndim)
    kern = functools.partial(_branch_mlp_kernel, n_tiles=n_tiles, tiles_per_seq=tiles)
    return pl.pallas_call(
        kern, grid=(n_tiles + 1,),
        in_specs=[row_spec(D_MODEL), seq_spec, row_spec(D_ATTN)] + [full(a) for a in small]
        + [_const_spec(w.shape) for w in wts],
        out_specs=(row_spec(D_MODEL), seq_spec, full(small[0])),
        out_shape=(jax.ShapeDtypeStruct((n_rows, D_MODEL), F32),
                   jax.ShapeDtypeStruct((n_rows // seq, CONV_W - 1, D_CONV), F32),
                   jax.ShapeDtypeStruct(small[0].shape, F32)),
        scratch_shapes=[pltpu.VMEM((1, rows + ZPAD, D_CONV), F32)],
        compiler_params=_params(1), name="branch_mlp",
    )(x2d, left, attn, *small, *wts)


def kernel(x_prompt, x_sample, cache_k, cache_v, cache_logf, state_conv, meta,
           norm1_g, w_in, b_f, conv_w, conv_b, q_norm_g, k_norm_g,
           w_br_conv, w_br_attn, w_out, norm2_g, w_up, w_down):
    b, seq, _ = x_prompt.shape
    db, dec, _ = x_sample.shape
    past = cache_k.shape[2]
    length = N_META + seq
    n_main = 3 * D_CONV + 3 * D_ATTN
    q0 = 3 * D_CONV

    wt = w_in[0].T
    wt_conv = wt[0:q0].astype(BF16)
    wt_qkv = wt[q0:n_main].astype(BF16)
    wt_fl = wt[n_main:n_main + N_HEADS].astype(BF16)
    wt_gate = wt[n_main + N_HEADS:].astype(BF16)
    head_of = jnp.arange(D_ATTN) // HEAD_DIM
    qg = jnp.tile(q_norm_g[0], N_HEADS)
    kg = jnp.tile(k_norm_g[0], N_HEADS)
    g1 = norm1_g[0][None, :]
    qkv_wts = (g1, wt_qkv, jnp.tile(wt_fl, (LANES // N_HEADS, 1)),
               jnp.tile(b_f[0], LANES // N_HEADS)[None, :], qg[:, None], kg[:, None])
    conv_wts = (conv_w[0], conv_b[0][None, :])
    mlp_wts = (w_br_conv[0].astype(BF16), w_br_attn[0].astype(BF16), w_out[0].astype(BF16),
               norm2_g[0][None, :], w_up[0].astype(BF16), w_down[0].astype(BF16))

    x_small = jnp.concatenate([meta, x_sample.reshape(db * dec, D_MODEL)], axis=0)
    left_small = jnp.concatenate(
        [jnp.zeros((1, CONV_W - 1, D_CONV), F32), state_conv[0]], axis=0)
    (conv_s, q_s, k_s, v_s, kt_s, vt_s, lft_s, lfp_s, gate_s, zlast_s) = _project_small(
        x_small, left_small,
        qkv_wts + (wt_conv, wt_gate) + conv_wts
        + (wt_fl, b_f[0][:, None], qg[None, :], kg[None, :],
           (head_of[:, None] == head_of[None, :]).astype(BF16)))

    x_rows = x_prompt.reshape(b * seq, D_MODEL)
    (qt_p, kt_p, vt_p, lftp_p, lft_p, lfp_p) = _project_qkv(
        x_rows, qkv_wts, (kt_s, vt_s, lft_s), b=b, seq=seq)
    attn_p = _prompt_attention(
        qt_p, kt_p, vt_p, lft_p, lft_s[:, :N_META], lfp_p.reshape(b, seq, LANES),
        lfp_s[:N_META], q_norm_g, k_norm_g)
    k_new = k_s[N_META:].reshape(db, dec, D_ATTN)
    v_new = v_s[N_META:].reshape(db, dec, D_ATTN)
    lf_new = lfp_s[N_META:, :N_HEADS].reshape(db, dec, N_HEADS)
    cache_kt = jnp.transpose(cache_k[0], (0, 2, 3, 1)).reshape(db, D_ATTN, past)
    cache_vt = jnp.transpose(cache_v[0], (0, 2, 3, 1)).reshape(db, D_ATTN, past)
    attn_s = _sample_attention(
        q_s[N_META:].reshape(db, dec, D_ATTN), k_new, v_new,
        lfp_s[N_META:].reshape(db, dec, LANES), jnp.swapaxes(lf_new, 1, 2),
        cache_kt, cache_vt, jnp.swapaxes(cache_logf[0], 1, 2))

    left_p = jnp.broadcast_to(zlast_s[0:1], (b, CONV_W - 1, D_CONV))
    y_prompt, zlast_p, y_sample = _branch_mlp(
        x_rows, left_p, attn_p.reshape(b * seq, D_ATTN),
        (x_sample.reshape(db * dec, D_MODEL), conv_s[N_META:],
         attn_s.reshape(db * dec, D_ATTN), gate_s[N_META:]),
        (g1, wt_conv, wt_gate) + conv_wts + mlp_wts, rows=MLP_ROWS, seq=seq)

    def heads_last(t):
        return jnp.transpose(t.reshape(b, N_HEADS, HEAD_DIM, length), (0, 3, 1, 2))[None]

    return (y_prompt.reshape(b, seq, D_MODEL),
            y_sample.reshape(db, dec, D_MODEL),
            heads_last(kt_p),
            heads_last(vt_p),
            jnp.swapaxes(lftp_p, 1, 2)[None],
            zlast_p[None],
            k_new.reshape(1, db, dec, N_HEADS, HEAD_DIM),
            v_new.reshape(1, db, dec, N_HEADS, HEAD_DIM),
            lf_new[None],
            zlast_s[1:][None])
```

```python
import functools

import jax
import jax.numpy as jnp
from jax import lax
from jax.experimental import pallas as pl
from jax.experimental.pallas import tpu as pltpu

D_MODEL = 1024
D_CONV = D_MODEL // 2
CONV_W = 3
N_HEADS = 8
HEAD_DIM = 64
D_ATTN = N_HEADS * HEAD_DIM
D_FF = 4 * D_MODEL
N_META = 16
EPS = 1e-6
ATTN_SCALE = HEAD_DIM ** -0.5

F32 = jnp.float32
BF16 = jnp.bfloat16

VMEM_LIMIT_BYTES = 56 * 1024 * 1024
LANES = 128
PROJ_ROWS = 1024
MLP_ROWS = 512
ATTN_BLOCK = 256
MASKED_BIAS = 1e30
ZPAD = 8
N_SPLIT = 3
V_SLAB = HEAD_DIM + 16
LOG2E = 1.4426950408889634
SKIP_LOG2 = 40.0
NORM_SLACK = 1.02


def _dot(a, b):
    return jnp.dot(a, b, preferred_element_type=F32)


def _dot_nt(a, b):
    return lax.dot_general(a, b, (((1,), (1,)), ((), ())), preferred_element_type=F32)


def _log_sigmoid(x):
    return jnp.minimum(x, 0.0) - jnp.log1p(jnp.exp(-jnp.abs(x)))


def _cumsum_few(x, axis):
    n = x.shape[axis]
    idx = lax.broadcasted_iota(jnp.int32, x.shape, axis)
    out = jnp.zeros(x.shape, F32)
    for i in range(n):
        term = x[i:i + 1, :] if axis == 0 else x[:, i:i + 1]
        out = out + jnp.where(idx >= i, term, 0.0)
    return out


def _triangles(n):
    r = lax.broadcasted_iota(jnp.int32, (n, n), 0)
    c = lax.broadcasted_iota(jnp.int32, (n, n), 1)
    upper = jnp.where(r <= c, 1.0, 0.0).astype(BF16)
    lower = jnp.where(r >= c, 1.0, 0.0).astype(BF16)
    return upper, lower


def _split3(c):
    hi = c.astype(BF16).astype(F32)
    r1 = c - hi
    mid = r1.astype(BF16).astype(F32)
    return hi, mid, r1 - mid


def _cumsum_rows(tri_lower, x):
    w = x.shape[1]
    pieces = jnp.concatenate(_split3(x), axis=1).astype(BF16)
    y = _dot(tri_lower, pieces)
    return y[:, 0:w] + y[:, w:2 * w] + y[:, 2 * w:3 * w]


def _cumsum_lanes(x, tri_upper):
    h = x.shape[0]
    pieces = jnp.concatenate(_split3(x), axis=0).astype(BF16)
    y = _dot(pieces, tri_upper)
    return y[0:h] + y[h:2 * h] + y[2 * h:3 * h]


def _const_spec(shape):
    nd = len(shape)
    return pl.BlockSpec(shape, lambda *_: (0,) * nd, pipeline_mode=pl.Buffered(1))


def _params(n_axes):
    return pltpu.CompilerParams(
        dimension_semantics=("arbitrary",) * n_axes,
        vmem_limit_bytes=VMEM_LIMIT_BYTES)


def _rms_rows(x, g_row):
    ms = jnp.mean(x * x, axis=-1, keepdims=True)
    return (x * lax.rsqrt(ms + EPS) * g_row).astype(BF16)


def _head_norm_t(ut, g_col):
    out = []
    for h in range(N_HEADS):
        blk = ut[h * HEAD_DIM:(h + 1) * HEAD_DIM, :]
        ms = jnp.mean(blk * blk, axis=0, keepdims=True)
        out.append(blk * lax.rsqrt(ms + EPS) * g_col[h * HEAD_DIM:(h + 1) * HEAD_DIM, :])
    return jnp.concatenate(out, axis=0)


def _short_conv(xn, wa_ref, cw_ref, cb_ref, zbuf, zlast_ref, n_seg, seg_len):
    cb = _dot_nt(xn, wa_ref[0:D_CONV, :])
    z = (_dot_nt(xn, wa_ref[D_CONV:2 * D_CONV, :])
         * _dot_nt(xn, wa_ref[2 * D_CONV:3 * D_CONV, :]))
    w0 = cw_ref[0:1, :]
    w1 = cw_ref[1:2, :]
    w2 = cw_ref[2:3, :]
    out = []
    for s in range(n_seg):
        r0 = s * seg_len
        zs = z[r0:r0 + seg_len]
        zbuf[s, ZPAD:ZPAD + seg_len, :] = zs
        zm1 = zbuf[s, ZPAD - 1:ZPAD - 1 + seg_len, :]
        zm2 = zbuf[s, ZPAD - 2:ZPAD - 2 + seg_len, :]
        y = zm2 * w0 + zm1 * w1 + zs * w2 + cb_ref[...]
        out.append(cb[r0:r0 + seg_len] * y)
        tail = zbuf[s, ZPAD + seg_len - 2:ZPAD + seg_len, :]
        zlast_ref[s] = tail
        zbuf[s, ZPAD - 2:ZPAD, :] = tail
    return out


def _qkv_kernel(x_ref, g1_ref, wqkv_ref, wfl_ref, bfr_ref, qgc_ref, kgc_ref,
                ktm_ref, vtm_ref, lftm_ref,
                qt_ref, kt_ref, vt_ref, lftp_ref, lft_ref, lfp_ref, kcar, vcar, lcar,
                *, tiles_per_seq):
    rows = x_ref.shape[0]
    step = pl.program_id(0) % tiles_per_seq
    lane0 = pl.multiple_of(step * rows, rows)

    def shifted_store(out_ref, car_ref, tile):
        rolled = pltpu.roll(tile, N_META, axis=1)
        lane = lax.broadcasted_iota(jnp.int32, (tile.shape[0], LANES), 1)
        out_ref[:, pl.ds(lane0, LANES)] = jnp.where(lane < N_META, car_ref[...],
                                                    rolled[:, 0:LANES])
        out_ref[:, pl.ds(lane0 + LANES, rows - LANES)] = rolled[:, LANES:]
        car_ref[...] = rolled[:, 0:LANES]
        out_ref[:, tiles_per_seq * rows:] = rolled[:, 0:N_META]

    @pl.when(step == 0)
    def _():
        kcar[...] = ktm_ref[:, 0:LANES]
        vcar[...] = vtm_ref[:, 0:LANES]
        lcar[...] = lftm_ref[:, 0:LANES]

    xn = _rms_rows(x_ref[...], g1_ref[...])

    def feature_major(j):
        return _dot_nt(wqkv_ref[j * D_ATTN:(j + 1) * D_ATTN, :], xn)

    lfp = _log_sigmoid(_dot_nt(xn, wfl_ref[...]) + bfr_ref[...])
    lfp_ref[...] = lfp
    lft = lfp.T[0:N_HEADS, :]
    lft_ref[...] = lft
    qt = _head_norm_t(feature_major(0), qgc_ref[...])
    qt_ref[...] = (qt * (ATTN_SCALE * LOG2E)).astype(BF16)
    shifted_store(kt_ref, kcar, _head_norm_t(feature_major(1), kgc_ref[...]))
    shifted_store(vt_ref, vcar, feature_major(2))
    shifted_store(lftp_ref, lcar, lft)


def _project_qkv(x2d, wts, meta_cols, *, b, seq):
    rows = PROJ_ROWS
    tiles = seq // rows
    length = N_META + seq
    row_spec = lambda width: pl.BlockSpec((rows, width), lambda i: (i, 0))
    col_spec = lambda feat: pl.BlockSpec(
        (None, feat, rows), lambda i: (i // tiles, 0, i % tiles))
    seq_spec = lambda feat: pl.BlockSpec((None, feat, length), lambda i: (i // tiles, 0, 0))
    out_shape = (
        jax.ShapeDtypeStruct((b, D_ATTN, seq), BF16),
        jax.ShapeDtypeStruct((b, D_ATTN, length), F32),
        jax.ShapeDtypeStruct((b, D_ATTN, length), F32),
        jax.ShapeDtypeStruct((b, N_HEADS, length), F32),
        jax.ShapeDtypeStruct((b, N_HEADS, seq), F32),
        jax.ShapeDtypeStruct((b * seq, LANES), F32),
    )
    out_specs = (col_spec(D_ATTN), seq_spec(D_ATTN), seq_spec(D_ATTN), seq_spec(N_HEADS),
                 col_spec(N_HEADS), row_spec(LANES))
    kern = functools.partial(_qkv_kernel, tiles_per_seq=tiles)
    return pl.pallas_call(
        kern, grid=(b * tiles,),
        in_specs=[row_spec(D_MODEL)] + [_const_spec(w.shape) for w in wts + meta_cols],
        out_specs=out_specs, out_shape=out_shape,
        scratch_shapes=[pltpu.VMEM((D_ATTN, LANES), F32),
                        pltpu.VMEM((D_ATTN, LANES), F32),
                        pltpu.VMEM((N_HEADS, LANES), F32)],
        compiler_params=_params(1), name="proj_qkv",
    )(x2d, *wts, *meta_cols)


def _proj_small_kernel(x_ref, left_ref, g1_ref, wqkv_ref, wfl_ref, bfr_ref, qgc_ref, kgc_ref,
                       wa_ref, wgl_ref, cw_ref, cb_ref, wflt_ref, bfc_ref, qgr_ref, kgr_ref,
                       bd_ref,
                       conv_ref, q_ref, k_ref, v_ref, kt_ref, vt_ref, lft_ref, lfp_ref,
                       gate_ref, zlast_ref, zbuf, *, n_seg, seg_len):
    xn = _rms_rows(x_ref[...], g1_ref[...])
    zbuf[:, ZPAD - 2:ZPAD, :] = left_ref[...]
    conv = _short_conv(xn, wa_ref, cw_ref, cb_ref, zbuf, zlast_ref, n_seg, seg_len)
    for s in range(n_seg):
        conv_ref[s * seg_len:(s + 1) * seg_len, :] = conv[s].astype(BF16)

    def rows_major(j):
        return _dot_nt(xn, wqkv_ref[j * D_ATTN:(j + 1) * D_ATTN, :])

    def feature_major(j):
        return _dot_nt(wqkv_ref[j * D_ATTN:(j + 1) * D_ATTN, :], xn)

    def head_norm(u, g_row):
        ssq = _dot((u * u).astype(BF16), bd_ref[...])
        return u * lax.rsqrt(ssq * (1.0 / HEAD_DIM) + EPS) * g_row

    q_ref[...] = (head_norm(rows_major(0), qgr_ref[...]) * ATTN_SCALE).astype(BF16)
    k_ref[...] = head_norm(rows_major(1), kgr_ref[...])
    v_ref[...] = rows_major(2)
    kt_ref[...] = _head_norm_t(feature_major(1), kgc_ref[...])
    vt_ref[...] = feature_major(2)
    lfp_ref[...] = _log_sigmoid(_dot_nt(xn, wfl_ref[...]) + bfr_ref[...])
    lft_ref[...] = _log_sigmoid(_dot_nt(wflt_ref[...], xn) + bfc_ref[...])
    gate_ref[...] = jax.nn.sigmoid(_dot_nt(xn, wgl_ref[...])).astype(BF16)


def _project_small(x2d, left, wts):
    n_rows = x2d.shape[0]
    n_seq = left.shape[0]
    seg_len = n_rows // n_seq
    full = lambda *shape: pl.BlockSpec(shape, lambda i: (0,) * len(shape))
    out_shape = (
        jax.ShapeDtypeStruct((n_rows, D_CONV), BF16),
        jax.ShapeDtypeStruct((n_rows, D_ATTN), BF16),
        jax.ShapeDtypeStruct((n_rows, D_ATTN), F32),
        jax.ShapeDtypeStruct((n_rows, D_ATTN), F32),
        jax.ShapeDtypeStruct((D_ATTN, n_rows), F32),
        jax.ShapeDtypeStruct((D_ATTN, n_rows), F32),
        jax.ShapeDtypeStruct((N_HEADS, n_rows), F32),
        jax.ShapeDtypeStruct((n_rows, LANES), F32),
        jax.ShapeDtypeStruct((n_rows, 2 * D_MODEL), BF16),
        jax.ShapeDtypeStruct((n_seq, CONV_W - 1, D_CONV), F32),
    )
    kern = functools.partial(_proj_small_kernel, n_seg=n_seq, seg_len=seg_len)
    return pl.pallas_call(
        kern, grid=(1,),
        in_specs=[full(*x2d.shape), full(*left.shape)] + [_const_spec(w.shape) for w in wts],
        out_specs=tuple(full(*s.shape) for s in out_shape), out_shape=out_shape,
        scratch_shapes=[pltpu.VMEM((n_seq, seg_len + ZPAD, D_CONV), F32)],
        compiler_params=_params(1), name="proj_small",
    )(x2d, left, *wts)


def _prompt_attn_kernel(qt_ref, kt_ref, vt_ref, lft_ref, lftm_ref, lfp_ref, lfpm_ref,
                        qg_ref, kg_ref, o_ref, kpos, vb, kbias, crow, cend, qcat, m_s, acc_s,
                        sbuf, tail, *, seq):
    blk = ATTN_BLOCK
    n_blk = seq // blk
    length = N_META + seq
    n_pos = kpos.shape[0]
    n_bias = N_SPLIT * N_HEADS

    for j in range(n_blk):
        kpos[j * blk:(j + 1) * blk, :] = kt_ref[:, j * blk:(j + 1) * blk].T.astype(BF16)
    tail[...] = jnp.zeros(tail.shape, F32)
    tail[:, 0:N_META] = kt_ref[:, seq:length]
    kpos[seq:seq + LANES, :] = tail[...].T.astype(BF16)
    kpos[seq + LANES:, :] = jnp.zeros((n_pos - seq - LANES, D_ATTN), BF16)

    ones_row = (lax.broadcasted_iota(jnp.int32, (V_SLAB - HEAD_DIM, blk), 0) == 0).astype(BF16)
    vb[n_blk] = jnp.zeros((N_HEADS * V_SLAB, blk), BF16)
    for h in range(N_HEADS):
        rows = slice(h * HEAD_DIM, (h + 1) * HEAD_DIM)
        slab = slice(h * V_SLAB, h * V_SLAB + HEAD_DIM)
        for j in range(n_blk + 1):
            if j < n_blk:
                vb[j, slab, :] = vt_ref[rows, j * blk:(j + 1) * blk].astype(BF16)
            else:
                vb[j, slab, 0:N_META] = vt_ref[rows, seq:length].astype(BF16)
            vb[j, h * V_SLAB + HEAD_DIM:(h + 1) * V_SLAB, :] = ones_row

    qk_bound = (NORM_SLACK * HEAD_DIM * ATTN_SCALE * LOG2E
                * jnp.max(jnp.abs(qg_ref[...]), axis=1, keepdims=True)
                * jnp.max(jnp.abs(kg_ref[...]), axis=1, keepdims=True))

    upper, lower = _triangles(blk)

    def store_kbias(rows, c_col):
        hi, mid, lo = _split3(c_col * LOG2E)
        lane = lax.broadcasted_iota(jnp.int32, c_col.shape, 1)
        grp = lane // N_HEADS
        part = jnp.where(grp == 0, hi, jnp.where(grp == 1, mid, lo))
        kbias[rows, :] = jnp.where(lane < n_bias, -part,
                                   jnp.where(lane < 2 * n_bias, 1.0, 0.0)).astype(BF16)

    off_c = jnp.zeros((1, LANES), F32)
    for j in range(n_blk):
        if j == 0:
            lf_blk = jnp.concatenate([lfpm_ref[...], lfp_ref[0:blk - N_META, :]], axis=0)
        else:
            lf_blk = lfp_ref[j * blk - N_META:(j + 1) * blk - N_META, :]
        c_col = _cumsum_rows(lower, lf_blk) + off_c
        store_kbias(slice(j * blk, (j + 1) * blk), c_col)
        off_c = c_col[blk - 1:blk, :]
    c_col = _cumsum_few(lfp_ref[seq - N_META:seq, :], axis=0) + off_c
    store_kbias(slice(seq, length), c_col)
    kbias[length:, :] = jnp.zeros((n_pos - length, LANES), BF16)

    meta_r = _cumsum_few(lftm_ref[...], axis=1)
    off_r = meta_r[:, N_META - 1:N_META]
    lane_h = lax.broadcasted_iota(jnp.int32, (N_HEADS, LANES), 1)
    c_end = jnp.zeros((N_HEADS, LANES), F32)
    for j in range(n_blk):
        c_row = _cumsum_lanes(lft_ref[:, j * blk:(j + 1) * blk], upper) + off_r
        crow[j] = c_row * LOG2E
        off_r = c_row[:, blk - 1:blk]
        c_end = jnp.where(lane_h == j, off_r * LOG2E, c_end)
    cend[...] = c_end

    row128 = lax.broadcasted_iota(jnp.int32, (LANES, blk), 0)
    krow = lax.broadcasted_iota(jnp.int32, (blk + N_META, blk), 0)
    qcol = lax.broadcasted_iota(jnp.int32, (blk + N_META, blk), 1)

    def q_block(t, _):
        tok0 = pl.multiple_of(t * blk, blk)
        c_q = crow[t]
        hi, mid, lo = _split3(c_q)
        bias_rows = jnp.concatenate(
            [jnp.ones((n_bias, blk), F32), hi, mid, lo,
             jnp.zeros((LANES - 2 * n_bias, blk), F32)], axis=0)
        for h in range(N_HEADS):
            pair = qt_ref[(h // 2) * LANES:(h // 2 + 1) * LANES, pl.ds(tok0, blk)]
            in_head = (row128 // HEAD_DIM) == (h % 2)
            qcat[h, 0:LANES, :] = jnp.where(in_head, pair, jnp.zeros_like(pair))
            qcat[h, LANES:, :] = jnp.where(row128 % N_HEADS == h, bias_rows, 0.0).astype(BF16)
        m_s[...] = jnp.full(m_s.shape, -jnp.inf, F32)
        acc_s[...] = jnp.zeros(acc_s.shape, F32)

        gap = 2.0 * qk_bound + c_q[:, 0:1] - cend[...]
        needed = jnp.logical_and(gap >= -SKIP_LOG2, lane_h < t)
        n_needed = jnp.max(
            jnp.sum(jnp.where(needed, 1.0, 0.0), axis=1, keepdims=True)).astype(jnp.int32)

        def key_block(pos0, n_rows, pv, visible):
            m_blk = []
            for h in range(N_HEADS):
                g = h // 2
                kc = jnp.concatenate([kpos[pl.ds(pos0, n_rows), g * LANES:(g + 1) * LANES],
                                      kbias[pl.ds(pos0, n_rows), :]], axis=1)
                s = _dot(kc, qcat[h])
                if visible is not None:
                    s = jnp.where(visible, s, -jnp.inf)
                sbuf[h, 0:n_rows, :] = s
                m_blk.append(jnp.max(s, axis=0, keepdims=True))
            for h in range(N_HEADS):
                slab = slice(h * V_SLAB, (h + 1) * V_SLAB)
                m_old = m_s[h:h + 1, :]
                m_new = jnp.maximum(m_old, m_blk[h])
                alpha = jnp.exp2(m_old - m_new)
                p = jnp.exp2(sbuf[h, 0:n_rows, :] - m_new)
                m_s[h:h + 1, :] = m_new
                acc_s[slab, :] = alpha * acc_s[slab, :] + pv(slab, p.astype(BF16))

        def full_block(j, _):
            key_block(pl.multiple_of(j * blk, blk), blk,
                      lambda slab, p: _dot(vb[j, slab, :], p), None)
            return 0

        def own_pv(slab, p):
            spill = jnp.concatenate([p[blk:], jnp.zeros((LANES - N_META, blk), BF16)], axis=0)
            return _dot(vb[t, slab, :], p[0:blk]) + _dot(vb[t + 1, slab, 0:LANES], spill)

        key_block(tok0, blk + N_META, own_pv, krow <= qcol + N_META)
        lax.fori_loop(t - n_needed, t, full_block, 0)

        o_t = []
        for h in range(N_HEADS):
            norm = acc_s[h * V_SLAB + HEAD_DIM:h * V_SLAB + HEAD_DIM + 1, :]
            o_t.append(acc_s[h * V_SLAB:h * V_SLAB + HEAD_DIM, :] * (1.0 / norm))
        o_ref[pl.ds(tok0, blk), :] = jnp.concatenate(o_t, axis=0).T.astype(BF16)
        return 0

    lax.fori_loop(0, n_blk, q_block, 0)


def _sample_attn_kernel(q_ref, kn_ref, vn_ref, lf_ref, lft_ref, ckt_ref, cvt_ref, clft_ref,
                        o_ref, crow, *, past, dec):
    blk = ATTN_BLOCK
    n_keys = past + LANES

    upper, _ = _triangles(blk)
    off = jnp.zeros((N_HEADS, 1), F32)
    for j in range(past // blk):
        loc = _cumsum_lanes(clft_ref[:, j * blk:(j + 1) * blk], upper) + off
        crow[:, j * blk:(j + 1) * blk] = loc
        off = loc[:, blk - 1:blk]
    crow[:, 0:past] = crow[:, 0:past] - off

    cq_c = _cumsum_few(lf_ref[:, 0:N_HEADS], axis=0)
    cq_r = _cumsum_few(lft_ref[...], axis=1)
    crow[:, past:] = jnp.full((N_HEADS, LANES), MASKED_BIAS, F32)
    crow[:, past:past + dec] = cq_r

    q = q_ref[...]
    lane_head = lax.broadcasted_iota(jnp.int32, (dec, D_ATTN), 1) // HEAD_DIM
    q_exp = jnp.concatenate(
        [jnp.where(lane_head == h, q, jnp.zeros_like(q)) for h in range(N_HEADS)], axis=0)
    pad_rows = jnp.zeros((LANES - dec, D_ATTN), BF16)
    k_new = jnp.concatenate([kn_ref[...].astype(BF16), pad_rows], axis=0)
    v_new = jnp.concatenate([vn_ref[...].astype(BF16), pad_rows], axis=0)
    s_all = jnp.concatenate(
        [_dot(q_exp, ckt_ref[...].astype(BF16)), _dot_nt(q_exp, k_new)], axis=1)

    kpos = lax.broadcasted_iota(jnp.int32, (dec, n_keys), 1)
    qpos = past + lax.broadcasted_iota(jnp.int32, (dec, n_keys), 0)
    visible = kpos <= qpos
    probs = []
    norms = []
    for h in range(N_HEADS):
        s = s_all[h * dec:(h + 1) * dec, :] + cq_c[:, h:h + 1] - crow[h:h + 1, :]
        s = jnp.where(visible, s, -jnp.inf)
        p = jnp.exp(s - jnp.max(s, axis=-1, keepdims=True))
        norms.append(jnp.sum(p, axis=-1, keepdims=True))
        probs.append(p.astype(BF16))
    p_all = jnp.concatenate(probs, axis=0)
    o_all = (_dot_nt(p_all[:, 0:past], cvt_ref[...].astype(BF16))
             + _dot(p_all[:, past:], v_new))
    out = jnp.zeros((dec, D_ATTN), F32)
    for h in range(N_HEADS):
        o = o_all[h * dec:(h + 1) * dec, :] / norms[h]
        out = out + jnp.where(lane_head == h, o, 0.0)
    o_ref[...] = out.astype(BF16)


def _attn_kernel(*refs, seq, past, dec):
    n_p, n_s = _N_PROMPT_IN, _N_SAMPLE_IN
    p_in, s_in = refs[0:n_p], refs[n_p:n_p + n_s]
    o_ref, os_ref = refs[n_p + n_s:n_p + n_s + 2]
    scratch = refs[n_p + n_s + 2:]
    _sample_attn_kernel(*s_in, os_ref, scratch[-1], past=past, dec=dec)
    _prompt_attn_kernel(*p_in, o_ref, *scratch[:-1], seq=seq)


_N_PROMPT_IN = 9
_N_SAMPLE_IN = 8


def _attention(prompt_in, sample_in):
    qt, kt, vt, lft, lft_meta, lfp, lfp_meta, q_gain, k_gain = prompt_in
    q_s, _, _, _, _, cache_kt, _, _ = sample_in
    b, _, seq = qt.shape
    db, dec, _ = q_s.shape
    assert b == db, "one new stream and one running stream per grid step"
    length = vt.shape[2]
    past = cache_kt.shape[2]
    blk = ATTN_BLOCK
    n_blk = seq // blk
    n_pos = (n_blk + 1) * blk
    per_b = lambda *shape: pl.BlockSpec((None,) + shape, lambda i: (i,) + (0,) * len(shape))
    in_specs = [per_b(D_ATTN, seq), per_b(D_ATTN, length), per_b(D_ATTN, length),
                per_b(N_HEADS, seq), _const_spec(lft_meta.shape),
                per_b(seq, LANES), _const_spec(lfp_meta.shape),
                _const_spec(q_gain.shape), _const_spec(k_gain.shape),
                per_b(dec, D_ATTN), per_b(dec, D_ATTN), per_b(dec, D_ATTN),
                per_b(dec, LANES), per_b(N_HEADS, dec),
                per_b(D_ATTN, past), per_b(D_ATTN, past), per_b(N_HEADS, past)]
    kern = functools.partial(_attn_kernel, seq=seq, past=past, dec=dec)
    return pl.pallas_call(
        kern, grid=(b,), in_specs=in_specs,
        out_specs=(per_b(seq, D_ATTN), per_b(dec, D_ATTN)),
        out_shape=(jax.ShapeDtypeStruct((b, seq, D_ATTN), BF16),
                   jax.ShapeDtypeStruct((b, dec, D_ATTN), BF16)),
        scratch_shapes=[pltpu.VMEM((n_pos, D_ATTN), BF16),
                        pltpu.VMEM((n_blk + 1, N_HEADS * V_SLAB, blk), BF16),
                        pltpu.VMEM((n_pos, LANES), BF16),
                        pltpu.VMEM((n_blk, N_HEADS, blk), F32),
                        pltpu.VMEM((N_HEADS, LANES), F32),
                        pltpu.VMEM((N_HEADS, 2 * LANES, blk), BF16),
                        pltpu.VMEM((N_HEADS, blk), F32),
                        pltpu.VMEM((N_HEADS * V_SLAB, blk), F32),
                        pltpu.VMEM((N_HEADS, blk + N_META, blk), F32),
                        pltpu.VMEM((D_ATTN, LANES), F32),
                        pltpu.VMEM((N_HEADS, past + LANES), F32)],
        compiler_params=_params(1), name="attention",
    )(*prompt_in, *sample_in)


def _merge_mlp_tail(x, conv_bf16, attn_ref, g_conv, g_attn, wbc_ref, wba_ref, wo_ref, g2_ref,
                    wup_ref, wdn_ref, y_ref):
    merged = g_conv * _dot(conv_bf16, wbc_ref[...]) + g_attn * _dot(attn_ref[...], wba_ref[...])
    h = x + _dot(merged.astype(BF16), wo_ref[...])
    hn = _rms_rows(h, g2_ref[...])
    acc = h
    for c in range(D_FF // D_MODEL):
        cols = slice(c * D_MODEL, (c + 1) * D_MODEL)
        a = jnp.maximum(_dot(hn, wup_ref[:, cols]), 0.0)
        acc = acc + _dot((a * a).astype(BF16), wdn_ref[cols, :])
    y_ref[...] = acc


def _branch_mlp_kernel(x_ref, left_ref, attn_ref, xs_ref, convs_ref, attns_ref, gates_ref,
                       g1_ref, wa_ref, wgl_ref, cw_ref, cb_ref,
                       wbc_ref, wba_ref, wo_ref, g2_ref, wup_ref, wdn_ref,
                       y_ref, zlast_ref, ys_ref, zbuf, *, n_tiles, tiles_per_seq):
    step = pl.program_id(0)

    @pl.when(jnp.logical_and(step % tiles_per_seq == 0, step < n_tiles))
    def _():
        zbuf[:, ZPAD - 2:ZPAD, :] = left_ref[...]

    @pl.when(step < n_tiles)
    def _():
        x = x_ref[...]
        xn = _rms_rows(x, g1_ref[...])
        conv = _short_conv(xn, wa_ref, cw_ref, cb_ref, zbuf, zlast_ref, 1, x.shape[0])[0]
        g_conv = jax.nn.sigmoid(_dot_nt(xn, wgl_ref[0:D_MODEL, :]))
        g_attn = jax.nn.sigmoid(_dot_nt(xn, wgl_ref[D_MODEL:2 * D_MODEL, :]))
        _merge_mlp_tail(x, conv.astype(BF16), attn_ref, g_conv, g_attn, wbc_ref, wba_ref,
                        wo_ref, g2_ref, wup_ref, wdn_ref, y_ref)

    @pl.when(step == n_tiles)
    def _():
        _merge_mlp_tail(xs_ref[...], convs_ref[...], attns_ref,
                        gates_ref[:, 0:D_MODEL].astype(F32),
                        gates_ref[:, D_MODEL:2 * D_MODEL].astype(F32),
                        wbc_ref, wba_ref, wo_ref, g2_ref, wup_ref, wdn_ref, ys_ref)


def _branch_mlp(x2d, left, attn, small, wts, *, rows, seq):
    n_rows = x2d.shape[0]
    tiles = seq // rows
    n_tiles = n_rows // rows
    last = n_tiles - 1
    row_spec = lambda width: pl.BlockSpec((rows, width), lambda i: (jnp.minimum(i, last), 0))
    seq_spec = pl.BlockSpec((1, CONV_W - 1, D_CONV),
                            lambda i: (jnp.minimum(i, last) // tiles, 0, 0))
    full = lambda a: pl.BlockSpec(a.shape, lambda i: (0,) * a.ndim)
    kern = functools.partial(_branch_mlp_kernel, n_tiles=n_tiles, tiles_per_seq=tiles)
    return pl.pallas_call(
        kern, grid=(n_tiles + 1,),
        in_specs=[row_spec(D_MODEL), seq_spec, row_spec(D_ATTN)] + [full(a) for a in small]
        + [_const_spec(w.shape) for w in wts],
        out_specs=(row_spec(D_MODEL), seq_spec, full(small[0])),
        out_shape=(jax.ShapeDtypeStruct((n_rows, D_MODEL), F32),
                   jax.ShapeDtypeStruct((n_rows // seq, CONV_W - 1, D_CONV), F32),
                   jax.ShapeDtypeStruct(small[0].shape, F32)),
        scratch_shapes=[pltpu.VMEM((1, rows + ZPAD, D_CONV), F32)],
        compiler_params=_params(1), name="branch_mlp",
    )(x2d, left, attn, *small, *wts)


def kernel(x_prompt, x_sample, cache_k, cache_v, cache_logf, state_conv, meta,
           norm1_g, w_in, b_f, conv_w, conv_b, q_norm_g, k_norm_g,
           w_br_conv, w_br_attn, w_out, norm2_g, w_up, w_down):
    b, seq, _ = x_prompt.shape
    db, dec, _ = x_sample.shape
    past = cache_k.shape[2]
    length = N_META + seq
    n_main = 3 * D_CONV + 3 * D_ATTN
    q0 = 3 * D_CONV

    wt = w_in[0].T
    wt_conv = wt[0:q0].astype(BF16)
    wt_qkv = wt[q0:n_main].astype(BF16)
    wt_fl = wt[n_main:n_main + N_HEADS].astype(BF16)
    wt_gate = wt[n_main + N_HEADS:].astype(BF16)
    head_of = jnp.arange(D_ATTN) // HEAD_DIM
    qg = jnp.tile(q_norm_g[0], N_HEADS)
    kg = jnp.tile(k_norm_g[0], N_HEADS)
    g1 = norm1_g[0][None, :]
    qkv_wts = (g1, wt_qkv, jnp.tile(wt_fl, (LANES // N_HEADS, 1)),
               jnp.tile(b_f[0], LANES // N_HEADS)[None, :], qg[:, None], kg[:, None])
    conv_wts = (conv_w[0], conv_b[0][None, :])
    mlp_wts = (w_br_conv[0].astype(BF16), w_br_attn[0].astype(BF16), w_out[0].astype(BF16),
               norm2_g[0][None, :], w_up[0].astype(BF16), w_down[0].astype(BF16))

    x_small = jnp.concatenate([meta, x_sample.reshape(db * dec, D_MODEL)], axis=0)
    left_small = jnp.concatenate(
        [jnp.zeros((1, CONV_W - 1, D_CONV), F32), state_conv[0]], axis=0)
    (conv_s, q_s, k_s, v_s, kt_s, vt_s, lft_s, lfp_s, gate_s, zlast_s) = _project_small(
        x_small, left_small,
        qkv_wts + (wt_conv, wt_gate) + conv_wts
        + (wt_fl, b_f[0][:, None], qg[None, :], kg[None, :],
           (head_of[:, None] == head_of[None, :]).astype(BF16)))

    x_rows = x_prompt.reshape(b * seq, D_MODEL)
    (qt_p, kt_p, vt_p, lftp_p, lft_p, lfp_p) = _project_qkv(
        x_rows, qkv_wts, (kt_s, vt_s, lft_s), b=b, seq=seq)
    k_new = k_s[N_META:].reshape(db, dec, D_ATTN)
    v_new = v_s[N_META:].reshape(db, dec, D_ATTN)
    lf_new = lfp_s[N_META:, :N_HEADS].reshape(db, dec, N_HEADS)
    cache_kt = jnp.transpose(cache_k[0], (0, 2, 3, 1)).reshape(db, D_ATTN, past)
    cache_vt = jnp.transpose(cache_v[0], (0, 2, 3, 1)).reshape(db, D_ATTN, past)
    attn_p, attn_s = _attention(
        (qt_p, kt_p, vt_p, lft_p, lft_s[:, :N_META], lfp_p.reshape(b, seq, LANES),
         lfp_s[:N_META], q_norm_g, k_norm_g),
        (q_s[N_META:].reshape(db, dec, D_ATTN), k_new, v_new,
         lfp_s[N_META:].reshape(db, dec, LANES), jnp.swapaxes(lf_new, 1, 2),
         cache_kt, cache_vt, jnp.swapaxes(cache_logf[0], 1, 2)))

    left_p = jnp.broadcast_to(zlast_s[0:1], (b, CONV_W - 1, D_CONV))
    y_prompt, zlast_p, y_sample = _branch_mlp(
        x_rows, left_p, attn_p.reshape(b * seq, D_ATTN),
        (x_sample.reshape(db * dec, D_MODEL), conv_s[N_META:],
         attn_s.reshape(db * dec, D_ATTN), gate_s[N_META:]),
        (g1, wt_conv, wt_gate) + conv_wts + mlp_wts, rows=MLP_ROWS, seq=seq)

    def heads_last(t):
        return jnp.transpose(t.reshape(b, N_HEADS, HEAD_DIM, length), (0, 3, 1, 2))[None]

    return (y_prompt.reshape(b, seq, D_MODEL),
            y_sample.reshape(db, dec, D_MODEL),
            heads_last(kt_p),
            heads_last(vt_p),
            jnp.swapaxes(lftp_p, 1, 2)[None],
            zlast_p[None],
            k_new.reshape(1, db, dec, N_HEADS, HEAD_DIM),
            v_new.reshape(1, db, dec, N_HEADS, HEAD_DIM),
            lf_new[None],
            zlast_s[1:][None])
```

```python
import functools

import jax
import jax.numpy as jnp
from jax import lax
from jax.experimental import pallas as pl
from jax.experimental.pallas import tpu as pltpu

D_MODEL = 1024
D_CONV = D_MODEL // 2
CONV_W = 3
N_HEADS = 8
HEAD_DIM = 64
D_ATTN = N_HEADS * HEAD_DIM
D_FF = 4 * D_MODEL
N_META = 16
EPS = 1e-6
ATTN_SCALE = HEAD_DIM ** -0.5

F32 = jnp.float32
BF16 = jnp.bfloat16

VMEM_LIMIT_BYTES = 56 * 1024 * 1024
LANES = 128
SUBLANES = 8
BF16_ROWS = 16
PROJ_ROWS = 1024
MLP_ROWS = 512
ATTN_BLOCK = 256
MASKED_BIAS = 1e30
CONV_HALO = CONV_W - 1
ZPAD = SUBLANES
N_SPLIT = 3
V_SLAB = HEAD_DIM + BF16_ROWS
LOG2E = 1.4426950408889634
SKIP_LOG2 = 40.0
NORM_SLACK = 1.02


def _dot(a, b):
    return jnp.dot(a, b, preferred_element_type=F32)


def _dot_nt(a, b):
    return lax.dot_general(a, b, (((1,), (1,)), ((), ())), preferred_element_type=F32)


def _log_sigmoid(x):
    return jnp.minimum(x, 0.0) - jnp.log1p(jnp.exp(-jnp.abs(x)))


def _cumsum_few(x, axis):
    n = x.shape[axis]
    idx = lax.broadcasted_iota(jnp.int32, x.shape, axis)
    out = jnp.zeros(x.shape, F32)
    for i in range(n):
        term = x[i:i + 1, :] if axis == 0 else x[:, i:i + 1]
        out = out + jnp.where(idx >= i, term, 0.0)
    return out


def _triangles(n):
    r = lax.broadcasted_iota(jnp.int32, (n, n), 0)
    c = lax.broadcasted_iota(jnp.int32, (n, n), 1)
    upper = jnp.where(r <= c, 1.0, 0.0).astype(BF16)
    lower = jnp.where(r >= c, 1.0, 0.0).astype(BF16)
    return upper, lower


def _split3(c):
    hi = c.astype(BF16).astype(F32)
    r1 = c - hi
    mid = r1.astype(BF16).astype(F32)
    return hi, mid, r1 - mid


def _cumsum_rows(tri_lower, x):
    w = x.shape[1]
    pieces = jnp.concatenate(_split3(x), axis=1).astype(BF16)
    y = _dot(tri_lower, pieces)
    return y[:, 0:w] + y[:, w:2 * w] + y[:, 2 * w:3 * w]


def _cumsum_lanes(x, tri_upper):
    h = x.shape[0]
    pieces = jnp.concatenate(_split3(x), axis=0).astype(BF16)
    y = _dot(pieces, tri_upper)
    return y[0:h] + y[h:2 * h] + y[2 * h:3 * h]


def _const_spec(shape):
    nd = len(shape)
    return pl.BlockSpec(shape, lambda *_: (0,) * nd, pipeline_mode=pl.Buffered(1))


def _params(n_axes):
    return pltpu.CompilerParams(
        dimension_semantics=("arbitrary",) * n_axes,
        vmem_limit_bytes=VMEM_LIMIT_BYTES)


def _rms_rows(x, g_row):
    ms = jnp.mean(x * x, axis=-1, keepdims=True)
    return (x * lax.rsqrt(ms + EPS) * g_row).astype(BF16)


def _head_norm_t(ut, g_col):
    out = []
    for h in range(N_HEADS):
        blk = ut[h * HEAD_DIM:(h + 1) * HEAD_DIM, :]
        ms = jnp.mean(blk * blk, axis=0, keepdims=True)
        out.append(blk * lax.rsqrt(ms + EPS) * g_col[h * HEAD_DIM:(h + 1) * HEAD_DIM, :])
    return jnp.concatenate(out, axis=0)


def _short_conv(xn, wa_ref, cw_ref, cb_ref, zbuf, zlast_ref, n_seg, seg_len):
    cb = _dot_nt(xn, wa_ref[0:D_CONV, :])
    z = (_dot_nt(xn, wa_ref[D_CONV:2 * D_CONV, :])
         * _dot_nt(xn, wa_ref[2 * D_CONV:3 * D_CONV, :]))
    out = []
    for s in range(n_seg):
        r0 = s * seg_len
        zs = z[r0:r0 + seg_len]
        zbuf[s, ZPAD:ZPAD + seg_len, :] = zs
        y = None
        for i in range(CONV_W):
            lo = ZPAD - CONV_HALO + i
            tap = zs if i == CONV_HALO else zbuf[s, lo:lo + seg_len, :]
            term = tap * cw_ref[i:i + 1, :]
            y = term if y is None else y + term
        out.append(cb[r0:r0 + seg_len] * (y + cb_ref[...]))
        tail = zbuf[s, ZPAD + seg_len - CONV_HALO:ZPAD + seg_len, :]
        zlast_ref[s] = tail
        zbuf[s, ZPAD - CONV_HALO:ZPAD, :] = tail
    return out


def _qkv_kernel(x_ref, g1_ref, wqkv_ref, wfl_ref, bfr_ref, qgc_ref, kgc_ref,
                ktm_ref, vtm_ref, lftm_ref,
                qt_ref, kt_ref, vt_ref, lftp_ref, lft_ref, lfp_ref, kcar, vcar, lcar,
                *, tiles_per_seq):
    rows = x_ref.shape[0]
    step = pl.program_id(0) % tiles_per_seq

    def tile_body(tile_idx):
        lane0 = tile_idx * rows

        def shifted_store(out_ref, car_ref, meta_ref, tile):
            left = meta_ref[:, 0:LANES] if tile_idx == 0 else car_ref[...]
            rolled = pltpu.roll(tile, N_META, axis=1)
            lane = lax.broadcasted_iota(jnp.int32, (tile.shape[0], LANES), 1)
            out_ref[:, lane0:lane0 + LANES] = jnp.where(lane < N_META, left, rolled[:, 0:LANES])
            out_ref[:, lane0 + LANES:lane0 + rows] = rolled[:, LANES:]
            if tile_idx < tiles_per_seq - 1:
                car_ref[...] = rolled[:, 0:LANES]
            else:
                out_ref[:, tiles_per_seq * rows:] = rolled[:, 0:N_META]

        xn = _rms_rows(x_ref[...], g1_ref[...])

        def feature_major(j):
            return _dot_nt(wqkv_ref[j * D_ATTN:(j + 1) * D_ATTN, :], xn)

        lfp = _log_sigmoid(_dot_nt(xn, wfl_ref[...]) + bfr_ref[...])
        lfp_ref[...] = lfp
        lft = lfp.T[0:N_HEADS, :]
        lft_ref[...] = lft
        qt = _head_norm_t(feature_major(0), qgc_ref[...])
        qt_ref[...] = (qt * (ATTN_SCALE * LOG2E)).astype(BF16)
        shifted_store(kt_ref, kcar, ktm_ref, _head_norm_t(feature_major(1), kgc_ref[...]))
        shifted_store(vt_ref, vcar, vtm_ref, feature_major(2))
        shifted_store(lftp_ref, lcar, lftm_ref, lft)

    for tile_idx in range(tiles_per_seq):
        pl.when(step == tile_idx)(functools.partial(tile_body, tile_idx))


def _project_qkv(x2d, wts, meta_cols, *, b, seq):
    rows = PROJ_ROWS
    tiles = seq // rows
    length = N_META + seq
    row_spec = lambda width: pl.BlockSpec((rows, width), lambda i: (i, 0))
    col_spec = lambda feat: pl.BlockSpec(
        (None, feat, rows), lambda i: (i // tiles, 0, i % tiles))
    seq_spec = lambda feat: pl.BlockSpec((None, feat, length), lambda i: (i // tiles, 0, 0))
    out_shape = (
        jax.ShapeDtypeStruct((b, D_ATTN, seq), BF16),
        jax.ShapeDtypeStruct((b, D_ATTN, length), F32),
        jax.ShapeDtypeStruct((b, D_ATTN, length), F32),
        jax.ShapeDtypeStruct((b, N_HEADS, length), F32),
        jax.ShapeDtypeStruct((b, N_HEADS, seq), F32),
        jax.ShapeDtypeStruct((b * seq, LANES), F32),
    )
    out_specs = (col_spec(D_ATTN), seq_spec(D_ATTN), seq_spec(D_ATTN), seq_spec(N_HEADS),
                 col_spec(N_HEADS), row_spec(LANES))
    kern = functools.partial(_qkv_kernel, tiles_per_seq=tiles)
    return pl.pallas_call(
        kern, grid=(b * tiles,),
        in_specs=[row_spec(D_MODEL)] + [_const_spec(w.shape) for w in wts + meta_cols],
        out_specs=out_specs, out_shape=out_shape,
        scratch_shapes=[pltpu.VMEM((D_ATTN, LANES), F32),
                        pltpu.VMEM((D_ATTN, LANES), F32),
                        pltpu.VMEM((N_HEADS, LANES), F32)],
        compiler_params=_params(1), name="proj_qkv",
    )(x2d, *wts, *meta_cols)


def _proj_small_kernel(x_ref, left_ref, g1_ref, wqkv_ref, wfl_ref, bfr_ref, qgc_ref, kgc_ref,
                       wa_ref, wgl_ref, cw_ref, cb_ref, wflt_ref, bfc_ref, qgr_ref, kgr_ref,
                       bd_ref,
                       conv_ref, q_ref, k_ref, v_ref, kt_ref, vt_ref, lft_ref, lfp_ref,
                       gate_ref, zlast_ref, zbuf, *, n_seg, seg_len):
    xn = _rms_rows(x_ref[...], g1_ref[...])
    zbuf[:, ZPAD - CONV_HALO:ZPAD, :] = left_ref[...]
    conv = _short_conv(xn, wa_ref, cw_ref, cb_ref, zbuf, zlast_ref, n_seg, seg_len)
    for s in range(n_seg):
        conv_ref[s * seg_len:(s + 1) * seg_len, :] = conv[s].astype(BF16)

    def rows_major(j):
        return _dot_nt(xn, wqkv_ref[j * D_ATTN:(j + 1) * D_ATTN, :])

    def feature_major(j):
        return _dot_nt(wqkv_ref[j * D_ATTN:(j + 1) * D_ATTN, :], xn)

    def head_norm(u, g_row):
        ssq = _dot((u * u).astype(BF16), bd_ref[...])
        return u * lax.rsqrt(ssq * (1.0 / HEAD_DIM) + EPS) * g_row

    q_ref[...] = (head_norm(rows_major(0), qgr_ref[...]) * ATTN_SCALE).astype(BF16)
    k_ref[...] = head_norm(rows_major(1), kgr_ref[...])
    v_ref[...] = rows_major(2)
    kt_ref[...] = _head_norm_t(feature_major(1), kgc_ref[...])
    vt_ref[...] = feature_major(2)
    lfp_ref[...] = _log_sigmoid(_dot_nt(xn, wfl_ref[...]) + bfr_ref[...])
    lft_ref[...] = _log_sigmoid(_dot_nt(wflt_ref[...], xn) + bfc_ref[...])
    gate_ref[...] = jax.nn.sigmoid(_dot_nt(xn, wgl_ref[...])).astype(BF16)


def _project_small(x2d, left, wts):
    n_rows = x2d.shape[0]
    n_seq = left.shape[0]
    seg_len = n_rows // n_seq
    full = lambda *shape: pl.BlockSpec(shape, lambda i: (0,) * len(shape))
    out_shape = (
        jax.ShapeDtypeStruct((n_rows, D_CONV), BF16),
        jax.ShapeDtypeStruct((n_rows, D_ATTN), BF16),
        jax.ShapeDtypeStruct((n_rows, D_ATTN), F32),
        jax.ShapeDtypeStruct((n_rows, D_ATTN), F32),
        jax.ShapeDtypeStruct((D_ATTN, n_rows), F32),
        jax.ShapeDtypeStruct((D_ATTN, n_rows), F32),
        jax.ShapeDtypeStruct((N_HEADS, n_rows), F32),
        jax.ShapeDtypeStruct((n_rows, LANES), F32),
        jax.ShapeDtypeStruct((n_rows, 2 * D_MODEL), BF16),
        jax.ShapeDtypeStruct((n_seq, CONV_HALO, D_CONV), F32),
    )
    kern = functools.partial(_proj_small_kernel, n_seg=n_seq, seg_len=seg_len)
    return pl.pallas_call(
        kern, grid=(1,),
        in_specs=[full(*x2d.shape), full(*left.shape)] + [_const_spec(w.shape) for w in wts],
        out_specs=tuple(full(*s.shape) for s in out_shape), out_shape=out_shape,
        scratch_shapes=[pltpu.VMEM((n_seq, seg_len + ZPAD, D_CONV), F32)],
        compiler_params=_params(1), name="proj_small",
    )(x2d, left, *wts)


def _prompt_attn_kernel(qt_ref, kt_ref, vt_ref, lft_ref, lftm_ref, lfp_ref, lfpm_ref,
                        qg_ref, kg_ref, o_ref, kpos, vb, kbias, crow, cend, qcat, m_s, acc_s,
                        sbuf, tail, *, seq):
    blk = ATTN_BLOCK
    n_blk = seq // blk
    length = N_META + seq
    n_pos = kpos.shape[0]
    n_bias = N_SPLIT * N_HEADS

    for j in range(n_blk):
        kpos[j * blk:(j + 1) * blk, :] = kt_ref[:, j * blk:(j + 1) * blk].T.astype(BF16)
    tail[...] = jnp.zeros(tail.shape, F32)
    tail[:, 0:N_META] = kt_ref[:, seq:length]
    kpos[seq:seq + LANES, :] = tail[...].T.astype(BF16)
    kpos[seq + LANES:, :] = jnp.zeros((n_pos - seq - LANES, D_ATTN), BF16)

    ones_row = (lax.broadcasted_iota(jnp.int32, (V_SLAB - HEAD_DIM, blk), 0) == 0).astype(BF16)
    vb[n_blk] = jnp.zeros((N_HEADS * V_SLAB, blk), BF16)
    for h in range(N_HEADS):
        rows = slice(h * HEAD_DIM, (h + 1) * HEAD_DIM)
        slab = slice(h * V_SLAB, h * V_SLAB + HEAD_DIM)
        for j in range(n_blk + 1):
            if j < n_blk:
                vb[j, slab, :] = vt_ref[rows, j * blk:(j + 1) * blk].astype(BF16)
            else:
                vb[j, slab, 0:N_META] = vt_ref[rows, seq:length].astype(BF16)
            vb[j, h * V_SLAB + HEAD_DIM:(h + 1) * V_SLAB, :] = ones_row

    qk_bound = (NORM_SLACK * HEAD_DIM * ATTN_SCALE * LOG2E
                * jnp.max(jnp.abs(qg_ref[...]), axis=1, keepdims=True)
                * jnp.max(jnp.abs(kg_ref[...]), axis=1, keepdims=True))

    upper, lower = _triangles(blk)

    def store_kbias(rows, c_col):
        hi, mid, lo = _split3(c_col * LOG2E)
        lane = lax.broadcasted_iota(jnp.int32, c_col.shape, 1)
        grp = lane // N_HEADS
        part = jnp.where(grp == 0, hi, jnp.where(grp == 1, mid, lo))
        kbias[rows, :] = jnp.where(lane < n_bias, -part,
                                   jnp.where(lane < 2 * n_bias, 1.0, 0.0)).astype(BF16)

    off_c = jnp.zeros((1, LANES), F32)
    for j in range(n_blk):
        if j == 0:
            lf_blk = jnp.concatenate([lfpm_ref[...], lfp_ref[0:blk - N_META, :]], axis=0)
        else:
            lf_blk = lfp_ref[j * blk - N_META:(j + 1) * blk - N_META, :]
        c_col = _cumsum_rows(lower, lf_blk) + off_c
        store_kbias(slice(j * blk, (j + 1) * blk), c_col)
        off_c = c_col[blk - 1:blk, :]
    c_col = _cumsum_few(lfp_ref[seq - N_META:seq, :], axis=0) + off_c
    store_kbias(slice(seq, length), c_col)
    kbias[length:, :] = jnp.zeros((n_pos - length, LANES), BF16)

    meta_r = _cumsum_few(lftm_ref[...], axis=1)
    off_r = meta_r[:, N_META - 1:N_META]
    lane_h = lax.broadcasted_iota(jnp.int32, (N_HEADS, LANES), 1)
    c_end = jnp.zeros((N_HEADS, LANES), F32)
    for j in range(n_blk):
        c_row = _cumsum_lanes(lft_ref[:, j * blk:(j + 1) * blk], upper) + off_r
        crow[j] = c_row * LOG2E
        off_r = c_row[:, blk - 1:blk]
        c_end = jnp.where(lane_h == j, off_r * LOG2E, c_end)
    cend[...] = c_end

    row128 = lax.broadcasted_iota(jnp.int32, (LANES, blk), 0)
    krow = lax.broadcasted_iota(jnp.int32, (blk + N_META, blk), 0)
    qcol = lax.broadcasted_iota(jnp.int32, (blk + N_META, blk), 1)

    def q_block(t, _):
        tok0 = pl.multiple_of(t * blk, blk)
        c_q = crow[t]
        hi, mid, lo = _split3(c_q)
        bias_rows = jnp.concatenate(
            [jnp.ones((n_bias, blk), F32), hi, mid, lo,
             jnp.zeros((LANES - 2 * n_bias, blk), F32)], axis=0)
        for h in range(N_HEADS):
            pair = qt_ref[(h // 2) * LANES:(h // 2 + 1) * LANES, pl.ds(tok0, blk)]
            in_head = (row128 // HEAD_DIM) == (h % 2)
            qcat[h, 0:LANES, :] = jnp.where(in_head, pair, jnp.zeros_like(pair))
            qcat[h, LANES:, :] = jnp.where(row128 % N_HEADS == h, bias_rows, 0.0).astype(BF16)
        m_s[...] = jnp.full(m_s.shape, -jnp.inf, F32)
        acc_s[...] = jnp.zeros(acc_s.shape, F32)

        gap = 2.0 * qk_bound + c_q[:, 0:1] - cend[...]
        needed = jnp.logical_and(gap >= -SKIP_LOG2, lane_h < t)
        n_needed = jnp.max(
            jnp.sum(jnp.where(needed, 1.0, 0.0), axis=1, keepdims=True)).astype(jnp.int32)

        def key_block(pos0, n_rows, pv, visible):
            m_blk = []
            for h in range(N_HEADS):
                g = h // 2
                kc = jnp.concatenate([kpos[pl.ds(pos0, n_rows), g * LANES:(g + 1) * LANES],
                                      kbias[pl.ds(pos0, n_rows), :]], axis=1)
                s = _dot(kc, qcat[h])
                if visible is not None:
                    s = jnp.where(visible, s, -jnp.inf)
                sbuf[h, 0:n_rows, :] = s
                m_blk.append(jnp.max(s, axis=0, keepdims=True))
            for h in range(N_HEADS):
                slab = slice(h * V_SLAB, (h + 1) * V_SLAB)
                m_old = m_s[h:h + 1, :]
                m_new = jnp.maximum(m_old, m_blk[h])
                alpha = jnp.exp2(m_old - m_new)
                p = jnp.exp2(sbuf[h, 0:n_rows, :] - m_new)
                m_s[h:h + 1, :] = m_new
                acc_s[slab, :] = alpha * acc_s[slab, :] + pv(slab, p.astype(BF16))

        def full_block(j, _):
            key_block(pl.multiple_of(j * blk, blk), blk,
                      lambda slab, p: _dot(vb[j, slab, :], p), None)
            return 0

        def own_pv(slab, p):
            spill = jnp.concatenate([p[blk:], jnp.zeros((LANES - N_META, blk), BF16)], axis=0)
            return _dot(vb[t, slab, :], p[0:blk]) + _dot(vb[t + 1, slab, 0:LANES], spill)

        key_block(tok0, blk + N_META, own_pv, krow <= qcol + N_META)
        lax.fori_loop(t - n_needed, t, full_block, 0)

        o_t = []
        for h in range(N_HEADS):
            norm = acc_s[h * V_SLAB + HEAD_DIM:h * V_SLAB + HEAD_DIM + 1, :]
            o_t.append(acc_s[h * V_SLAB:h * V_SLAB + HEAD_DIM, :] * (1.0 / norm))
        o_ref[pl.ds(tok0, blk), :] = jnp.concatenate(o_t, axis=0).T.astype(BF16)
        return 0

    lax.fori_loop(0, n_blk, q_block, 0)


def _sample_attn_kernel(q_ref, kn_ref, vn_ref, lf_ref, lft_ref, ckt_ref, cvt_ref, clft_ref,
                        o_ref, crow, *, past, dec):
    blk = ATTN_BLOCK
    n_keys = past + LANES

    upper, _ = _triangles(blk)
    off = jnp.zeros((N_HEADS, 1), F32)
    for j in range(past // blk):
        loc = _cumsum_lanes(clft_ref[:, j * blk:(j + 1) * blk], upper) + off
        crow[:, j * blk:(j + 1) * blk] = loc
        off = loc[:, blk - 1:blk]
    crow[:, 0:past] = crow[:, 0:past] - off

    cq_c = _cumsum_few(lf_ref[:, 0:N_HEADS], axis=0)
    cq_r = _cumsum_few(lft_ref[...], axis=1)
    crow[:, past:] = jnp.full((N_HEADS, LANES), MASKED_BIAS, F32)
    crow[:, past:past + dec] = cq_r

    q = q_ref[...]
    lane_head = lax.broadcasted_iota(jnp.int32, (dec, D_ATTN), 1) // HEAD_DIM
    q_exp = jnp.concatenate(
        [jnp.where(lane_head == h, q, jnp.zeros_like(q)) for h in range(N_HEADS)], axis=0)
    pad_rows = jnp.zeros((LANES - dec, D_ATTN), BF16)
    k_new = jnp.concatenate([kn_ref[...].astype(BF16), pad_rows], axis=0)
    v_new = jnp.concatenate([vn_ref[...].astype(BF16), pad_rows], axis=0)
    s_all = jnp.concatenate(
        [_dot(q_exp, ckt_ref[...].astype(BF16)), _dot_nt(q_exp, k_new)], axis=1)

    kpos = lax.broadcasted_iota(jnp.int32, (dec, n_keys), 1)
    qpos = past + lax.broadcasted_iota(jnp.int32, (dec, n_keys), 0)
    visible = kpos <= qpos
    probs = []
    norms = []
    for h in range(N_HEADS):
        s = s_all[h * dec:(h + 1) * dec, :] + cq_c[:, h:h + 1] - crow[h:h + 1, :]
        s = jnp.where(visible, s, -jnp.inf)
        p = jnp.exp(s - jnp.max(s, axis=-1, keepdims=True))
        norms.append(jnp.sum(p, axis=-1, keepdims=True))
        probs.append(p.astype(BF16))
    p_all = jnp.concatenate(probs, axis=0)
    o_all = (_dot_nt(p_all[:, 0:past], cvt_ref[...].astype(BF16))
             + _dot(p_all[:, past:], v_new))
    out = jnp.zeros((dec, D_ATTN), F32)
    for h in range(N_HEADS):
        o = o_all[h * dec:(h + 1) * dec, :] / norms[h]
        out = out + jnp.where(lane_head == h, o, 0.0)
    o_ref[...] = out.astype(BF16)


def _attn_kernel(*refs, seq, past, dec):
    n_p, n_s = _N_PROMPT_IN, _N_SAMPLE_IN
    p_in, s_in = refs[0:n_p], refs[n_p:n_p + n_s]
    o_ref, os_ref = refs[n_p + n_s:n_p + n_s + 2]
    scratch = refs[n_p + n_s + 2:]
    _sample_attn_kernel(*s_in, os_ref, scratch[-1], past=past, dec=dec)
    _prompt_attn_kernel(*p_in, o_ref, *scratch[:-1], seq=seq)


_N_PROMPT_IN = 9
_N_SAMPLE_IN = 8


def _attention(prompt_in, sample_in):
    qt, kt, vt, lft, lft_meta, lfp, lfp_meta, q_gain, k_gain = prompt_in
    q_s, _, _, _, _, cache_kt, _, _ = sample_in
    b, _, seq = qt.shape
    db, dec, _ = q_s.shape
    assert b == db, "one new stream and one running stream per grid step"
    length = vt.shape[2]
    past = cache_kt.shape[2]
    blk = ATTN_BLOCK
    n_blk = seq // blk
    n_pos = (n_blk + 1) * blk
    per_b = lambda *shape: pl.BlockSpec((None,) + shape, lambda i: (i,) + (0,) * len(shape))
    in_specs = [per_b(D_ATTN, seq), per_b(D_ATTN, length), per_b(D_ATTN, length),
                per_b(N_HEADS, seq), _const_spec(lft_meta.shape),
                per_b(seq, LANES), _const_spec(lfp_meta.shape),
                _const_spec(q_gain.shape), _const_spec(k_gain.shape),
                per_b(dec, D_ATTN), per_b(dec, D_ATTN), per_b(dec, D_ATTN),
                per_b(dec, LANES), per_b(N_HEADS, dec),
                per_b(D_ATTN, past), per_b(D_ATTN, past), per_b(N_HEADS, past)]
    kern = functools.partial(_attn_kernel, seq=seq, past=past, dec=dec)
    return pl.pallas_call(
        kern, grid=(b,), in_specs=in_specs,
        out_specs=(per_b(seq, D_ATTN), per_b(dec, D_ATTN)),
        out_shape=(jax.ShapeDtypeStruct((b, seq, D_ATTN), BF16),
                   jax.ShapeDtypeStruct((b, dec, D_ATTN), BF16)),
        scratch_shapes=[pltpu.VMEM((n_pos, D_ATTN), BF16),
                        pltpu.VMEM((n_blk + 1, N_HEADS * V_SLAB, blk), BF16),
                        pltpu.VMEM((n_pos, LANES), BF16),
                        pltpu.VMEM((n_blk, N_HEADS, blk), F32),
                        pltpu.VMEM((N_HEADS, LANES), F32),
                        pltpu.VMEM((N_HEADS, 2 * LANES, blk), BF16),
                        pltpu.VMEM((N_HEADS, blk), F32),
                        pltpu.VMEM((N_HEADS * V_SLAB, blk), F32),
                        pltpu.VMEM((N_HEADS, blk + N_META, blk), F32),
                        pltpu.VMEM((D_ATTN, LANES), F32),
                        pltpu.VMEM((N_HEADS, past + LANES), F32)],
        compiler_params=_params(1), name="attention",
    )(*prompt_in, *sample_in)


def _merge_mlp_tail(x, conv_bf16, attn_ref, g_conv, g_attn, wbc_ref, wba_ref, wo_ref, g2_ref,
                    wup_ref, wdn_ref, y_ref):
    merged = g_conv * _dot(conv_bf16, wbc_ref[...]) + g_attn * _dot(attn_ref[...], wba_ref[...])
    h = x + _dot(merged.astype(BF16), wo_ref[...])
    hn = _rms_rows(h, g2_ref[...])
    acc = h
    for c in range(D_FF // D_MODEL):
        cols = slice(c * D_MODEL, (c + 1) * D_MODEL)
        a = jnp.maximum(_dot(hn, wup_ref[:, cols]), 0.0)
        acc = acc + _dot((a * a).astype(BF16), wdn_ref[cols, :])
    y_ref[...] = acc


def _branch_mlp_kernel(x_ref, left_ref, attn_ref, xs_ref, convs_ref, attns_ref, gates_ref,
                       g1_ref, wa_ref, wgl_ref, cw_ref, cb_ref,
                       wbc_ref, wba_ref, wo_ref, g2_ref, wup_ref, wdn_ref,
                       y_ref, zlast_ref, ys_ref, zbuf, *, n_tiles, tiles_per_seq):
    step = pl.program_id(0)

    @pl.when(jnp.logical_and(step % tiles_per_seq == 0, step < n_tiles))
    def _():
        zbuf[:, ZPAD - CONV_HALO:ZPAD, :] = left_ref[...]

    @pl.when(step < n_tiles)
    def _():
        x = x_ref[...]
        xn = _rms_rows(x, g1_ref[...])
        conv = _short_conv(xn, wa_ref, cw_ref, cb_ref, zbuf, zlast_ref, 1, x.shape[0])[0]
        g_conv = jax.nn.sigmoid(_dot_nt(xn, wgl_ref[0:D_MODEL, :]))
        g_attn = jax.nn.sigmoid(_dot_nt(xn, wgl_ref[D_MODEL:2 * D_MODEL, :]))
        _merge_mlp_tail(x, conv.astype(BF16), attn_ref, g_conv, g_attn, wbc_ref, wba_ref,
                        wo_ref, g2_ref, wup_ref, wdn_ref, y_ref)

    @pl.when(step == n_tiles)
    def _():
        _merge_mlp_tail(xs_ref[...], convs_ref[...], attns_ref,
                        gates_ref[:, 0:D_MODEL].astype(F32),
                        gates_ref[:, D_MODEL:2 * D_MODEL].astype(F32),
                        wbc_ref, wba_ref, wo_ref, g2_ref, wup_ref, wdn_ref, ys_ref)


def _branch_mlp(x2d, left, attn, small, wts, *, rows, seq):
    n_rows = x2d.shape[0]
    tiles = seq // rows
    n_tiles = n_rows // rows
    last = n_tiles - 1
    row_spec = lambda width: pl.BlockSpec((rows, width), lambda i: (jnp.minimum(i, last), 0))
    seq_spec = pl.BlockSpec((1, CONV_HALO, D_CONV),
                            lambda i: (jnp.minimum(i, last) // tiles, 0, 0))
    full = lambda a: pl.BlockSpec(a.shape, lambda i: (0,) * a.ndim)
    kern = functools.partial(_branch_mlp_kernel, n_tiles=n_tiles, tiles_per_seq=tiles)
    return pl.pallas_call(
        kern, grid=(n_tiles + 1,),
        in_specs=[row_spec(D_MODEL), seq_spec, row_spec(D_ATTN)] + [full(a) for a in small]
        + [_const_spec(w.shape) for w in wts],
        out_specs=(row_spec(D_MODEL), seq_spec, full(small[0])),
        out_shape=(jax.ShapeDtypeStruct((n_rows, D_MODEL), F32),
                   jax.ShapeDtypeStruct((n_rows // seq, CONV_HALO, D_CONV), F32),
                   jax.ShapeDtypeStruct(small[0].shape, F32)),
        scratch_shapes=[pltpu.VMEM((1, rows + ZPAD, D_CONV), F32)],
        compiler_params=_params(1), name="branch_mlp",
    )(x2d, left, attn, *small, *wts)


def kernel(x_prompt, x_sample, cache_k, cache_v, cache_logf, state_conv, meta,
           norm1_g, w_in, b_f, conv_w, conv_b, q_norm_g, k_norm_g,
           w_br_conv, w_br_attn, w_out, norm2_g, w_up, w_down):
    b, seq, _ = x_prompt.shape
    db, dec, _ = x_sample.shape
    past = cache_k.shape[2]
    length = N_META + seq
    n_main = 3 * D_CONV + 3 * D_ATTN
    q0 = 3 * D_CONV

    wt = w_in[0].T
    wt_conv = wt[0:q0].astype(BF16)
    wt_qkv = wt[q0:n_main].astype(BF16)
    wt_fl = wt[n_main:n_main + N_HEADS].astype(BF16)
    wt_gate = wt[n_main + N_HEADS:].astype(BF16)
    head_of = jnp.arange(D_ATTN) // HEAD_DIM
    qg = jnp.tile(q_norm_g[0], N_HEADS)
    kg = jnp.tile(k_norm_g[0], N_HEADS)
    g1 = norm1_g[0][None, :]
    qkv_wts = (g1, wt_qkv, jnp.tile(wt_fl, (LANES // N_HEADS, 1)),
               jnp.tile(b_f[0], LANES // N_HEADS)[None, :], qg[:, None], kg[:, None])
    conv_wts = (conv_w[0], conv_b[0][None, :])
    mlp_wts = (w_br_conv[0].astype(BF16), w_br_attn[0].astype(BF16), w_out[0].astype(BF16),
               norm2_g[0][None, :], w_up[0].astype(BF16), w_down[0].astype(BF16))

    x_small = jnp.concatenate([meta, x_sample.reshape(db * dec, D_MODEL)], axis=0)
    left_small = jnp.concatenate(
        [jnp.zeros((1, CONV_HALO, D_CONV), F32), state_conv[0]], axis=0)
    (conv_s, q_s, k_s, v_s, kt_s, vt_s, lft_s, lfp_s, gate_s, zlast_s) = _project_small(
        x_small, left_small,
        qkv_wts + (wt_conv, wt_gate) + conv_wts
        + (wt_fl, b_f[0][:, None], qg[None, :], kg[None, :],
           (head_of[:, None] == head_of[None, :]).astype(BF16)))

    x_rows = x_prompt.reshape(b * seq, D_MODEL)
    (qt_p, kt_p, vt_p, lftp_p, lft_p, lfp_p) = _project_qkv(
        x_rows, qkv_wts, (kt_s, vt_s, lft_s), b=b, seq=seq)
    k_new = k_s[N_META:].reshape(db, dec, D_ATTN)
    v_new = v_s[N_META:].reshape(db, dec, D_ATTN)
    lf_new = lfp_s[N_META:, :N_HEADS].reshape(db, dec, N_HEADS)
    cache_kt = jnp.transpose(cache_k[0], (0, 2, 3, 1)).reshape(db, D_ATTN, past)
    cache_vt = jnp.transpose(cache_v[0], (0, 2, 3, 1)).reshape(db, D_ATTN, past)
    attn_p, attn_s = _attention(
        (qt_p, kt_p, vt_p, lft_p, lft_s[:, :N_META], lfp_p.reshape(b, seq, LANES),
         lfp_s[:N_META], q_norm_g, k_norm_g),
        (q_s[N_META:].reshape(db, dec, D_ATTN), k_new, v_new,
         lfp_s[N_META:].reshape(db, dec, LANES), jnp.swapaxes(lf_new, 1, 2),
         cache_kt, cache_vt, jnp.swapaxes(cache_logf[0], 1, 2)))

    left_p = jnp.broadcast_to(zlast_s[0:1], (b, CONV_HALO, D_CONV))
    y_prompt, zlast_p, y_sample = _branch_mlp(
        x_rows, left_p, attn_p.reshape(b * seq, D_ATTN),
        (x_sample.reshape(db * dec, D_MODEL), conv_s[N_META:],
         attn_s.reshape(db * dec, D_ATTN), gate_s[N_META:]),
        (g1, wt_conv, wt_gate) + conv_wts + mlp_wts, rows=MLP_ROWS, seq=seq)

    def heads_last(t):
        return jnp.transpose(t.reshape(b, N_HEADS, HEAD_DIM, length), (0, 3, 1, 2))[None]

    return (y_prompt.reshape(b, seq, D_MODEL),
            y_sample.reshape(db, dec, D_MODEL),
            heads_last(kt_p),
            heads_last(vt_p),
            jnp.swapaxes(lftp_p, 1, 2)[None],
            zlast_p[None],
            k_new.reshape(1, db, dec, N_HEADS, HEAD_DIM),
            v_new.reshape(1, db, dec, N_HEADS, HEAD_DIM),
            lf_new[None],
            zlast_s[1:][None])
```

```python
import functools

import jax
import jax.numpy as jnp
from jax import lax
from jax.experimental import pallas as pl
from jax.experimental.pallas import tpu as pltpu

D_MODEL = 1024
D_CONV = D_MODEL // 2
CONV_W = 3
N_HEADS = 8
HEAD_DIM = 64
D_ATTN = N_HEADS * HEAD_DIM
D_FF = 4 * D_MODEL
N_META = 16
EPS = 1e-6
ATTN_SCALE = HEAD_DIM ** -0.5

F32 = jnp.float32
BF16 = jnp.bfloat16

VMEM_LIMIT_BYTES = 56 * 1024 * 1024
LANES = 128
SUBLANES = 8
BF16_ROWS = 16
PROJ_ROWS = 1024
MLP_ROWS = 512
ATTN_BLOCK = 256
MASKED_BIAS = 1e30
CONV_HALO = CONV_W - 1
ZPAD = SUBLANES
N_SPLIT = 3
V_SLAB = HEAD_DIM + BF16_ROWS
LOG2E = 1.4426950408889634
SKIP_LOG2 = 40.0
NORM_SLACK = 1.02


def _dot(a, b):
    return jnp.dot(a, b, preferred_element_type=F32)


def _dot_nt(a, b):
    return lax.dot_general(a, b, (((1,), (1,)), ((), ())), preferred_element_type=F32)


def _log_sigmoid(x):
    return jnp.minimum(x, 0.0) - jnp.log1p(jnp.exp(-jnp.abs(x)))


def _cumsum_few(x, axis):
    n = x.shape[axis]
    idx = lax.broadcasted_iota(jnp.int32, x.shape, axis)
    out = jnp.zeros(x.shape, F32)
    for i in range(n):
        term = x[i:i + 1, :] if axis == 0 else x[:, i:i + 1]
        out = out + jnp.where(idx >= i, term, 0.0)
    return out


def _triangles(n):
    r = lax.broadcasted_iota(jnp.int32, (n, n), 0)
    c = lax.broadcasted_iota(jnp.int32, (n, n), 1)
    upper = jnp.where(r <= c, 1.0, 0.0).astype(BF16)
    lower = jnp.where(r >= c, 1.0, 0.0).astype(BF16)
    return upper, lower


def _split3(c):
    hi = c.astype(BF16).astype(F32)
    r1 = c - hi
    mid = r1.astype(BF16).astype(F32)
    return hi, mid, r1 - mid


def _cumsum_rows(tri_lower, x):
    w = x.shape[1]
    pieces = jnp.concatenate(_split3(x), axis=1).astype(BF16)
    y = _dot(tri_lower, pieces)
    return y[:, 0:w] + y[:, w:2 * w] + y[:, 2 * w:3 * w]


def _cumsum_lanes(x, tri_upper):
    h = x.shape[0]
    pieces = jnp.concatenate(_split3(x), axis=0).astype(BF16)
    y = _dot(pieces, tri_upper)
    return y[0:h] + y[h:2 * h] + y[2 * h:3 * h]


def _const_spec(shape):
    nd = len(shape)
    return pl.BlockSpec(shape, lambda *_: (0,) * nd, pipeline_mode=pl.Buffered(1))


def _params(n_axes):
    return pltpu.CompilerParams(
        dimension_semantics=("arbitrary",) * n_axes,
        vmem_limit_bytes=VMEM_LIMIT_BYTES)


def _rms_rows(x, g_row):
    ms = jnp.mean(x * x, axis=-1, keepdims=True)
    return (x * lax.rsqrt(ms + EPS) * g_row).astype(BF16)


def _head_norm_t(ut, g_col):
    out = []
    for h in range(N_HEADS):
        blk = ut[h * HEAD_DIM:(h + 1) * HEAD_DIM, :]
        ms = jnp.mean(blk * blk, axis=0, keepdims=True)
        out.append(blk * lax.rsqrt(ms + EPS) * g_col[h * HEAD_DIM:(h + 1) * HEAD_DIM, :])
    return jnp.concatenate(out, axis=0)


def _short_conv(xn, wa_ref, cw_ref, cb_ref, zbuf, zlast_ref, n_seg, seg_len):
    cb = _dot_nt(xn, wa_ref[0:D_CONV, :])
    z = (_dot_nt(xn, wa_ref[D_CONV:2 * D_CONV, :])
         * _dot_nt(xn, wa_ref[2 * D_CONV:3 * D_CONV, :]))
    out = []
    for s in range(n_seg):
        r0 = s * seg_len
        zs = z[r0:r0 + seg_len]
        zbuf[s, ZPAD:ZPAD + seg_len, :] = zs
        y = None
        for i in range(CONV_W):
            lo = ZPAD - CONV_HALO + i
            tap = zs if i == CONV_HALO else zbuf[s, lo:lo + seg_len, :]
            term = tap * cw_ref[i:i + 1, :]
            y = term if y is None else y + term
        out.append(cb[r0:r0 + seg_len] * (y + cb_ref[...]))
        tail = zbuf[s, ZPAD + seg_len - CONV_HALO:ZPAD + seg_len, :]
        zlast_ref[s] = tail
        zbuf[s, ZPAD - CONV_HALO:ZPAD, :] = tail
    return out


def _qkv_kernel(x_ref, g1_ref, wqkv_ref, wfl_ref, bfr_ref, qgc_ref, kgc_ref,
                ktm_ref, vtm_ref, lftm_ref,
                qt_ref, kt_ref, vt_ref, lftp_ref, lft_ref, lfp_ref, kcar, vcar, lcar,
                *, tiles_per_seq):
    rows = x_ref.shape[0]
    step = pl.program_id(0) % tiles_per_seq

    def tile_body(tile_idx):
        lane0 = tile_idx * rows

        def shifted_store(out_ref, car_ref, meta_ref, tile):
            left = meta_ref[:, 0:LANES] if tile_idx == 0 else car_ref[...]
            rolled = pltpu.roll(tile, N_META, axis=1)
            lane = lax.broadcasted_iota(jnp.int32, (tile.shape[0], LANES), 1)
            out_ref[:, lane0:lane0 + LANES] = jnp.where(lane < N_META, left, rolled[:, 0:LANES])
            out_ref[:, lane0 + LANES:lane0 + rows] = rolled[:, LANES:]
            if tile_idx < tiles_per_seq - 1:
                car_ref[...] = rolled[:, 0:LANES]
            else:
                out_ref[:, tiles_per_seq * rows:] = rolled[:, 0:N_META]

        xn = _rms_rows(x_ref[...], g1_ref[...])

        def feature_major(j):
            return _dot_nt(wqkv_ref[j * D_ATTN:(j + 1) * D_ATTN, :], xn)

        lfp = _log_sigmoid(_dot_nt(xn, wfl_ref[...]) + bfr_ref[...])
        lfp_ref[...] = lfp
        lft = lfp.T[0:N_HEADS, :]
        lft_ref[...] = lft
        qt = _head_norm_t(feature_major(0), qgc_ref[...])
        qt_ref[...] = (qt * (ATTN_SCALE * LOG2E)).astype(BF16)
        shifted_store(kt_ref, kcar, ktm_ref, _head_norm_t(feature_major(1), kgc_ref[...]))
        shifted_store(vt_ref, vcar, vtm_ref, feature_major(2))
        shifted_store(lftp_ref, lcar, lftm_ref, lft)

    for tile_idx in range(tiles_per_seq):
        pl.when(step == tile_idx)(functools.partial(tile_body, tile_idx))


def _project_qkv(x2d, wts, meta_cols, *, b, seq):
    rows = PROJ_ROWS
    tiles = seq // rows
    length = N_META + seq
    row_spec = lambda width: pl.BlockSpec((rows, width), lambda i: (i, 0))
    col_spec = lambda feat: pl.BlockSpec(
        (None, feat, rows), lambda i: (i // tiles, 0, i % tiles))
    seq_spec = lambda feat: pl.BlockSpec((None, feat, length), lambda i: (i // tiles, 0, 0))
    out_shape = (
        jax.ShapeDtypeStruct((b, D_ATTN, seq), BF16),
        jax.ShapeDtypeStruct((b, D_ATTN, length), F32),
        jax.ShapeDtypeStruct((b, D_ATTN, length), F32),
        jax.ShapeDtypeStruct((b, N_HEADS, length), F32),
        jax.ShapeDtypeStruct((b, N_HEADS, seq), F32),
        jax.ShapeDtypeStruct((b * seq, LANES), F32),
    )
    out_specs = (col_spec(D_ATTN), seq_spec(D_ATTN), seq_spec(D_ATTN), seq_spec(N_HEADS),
                 col_spec(N_HEADS), row_spec(LANES))
    kern = functools.partial(_qkv_kernel, tiles_per_seq=tiles)
    return pl.pallas_call(
        kern, grid=(b * tiles,),
        in_specs=[row_spec(D_MODEL)] + [_const_spec(w.shape) for w in wts + meta_cols],
        out_specs=out_specs, out_shape=out_shape,
        scratch_shapes=[pltpu.VMEM((D_ATTN, LANES), F32),
                        pltpu.VMEM((D_ATTN, LANES), F32),
                        pltpu.VMEM((N_HEADS, LANES), F32)],
        compiler_params=_params(1), name="proj_qkv",
    )(x2d, *wts, *meta_cols)


def _proj_small_kernel(x_ref, left_ref, g1_ref, wqkv_ref, wfl_ref, bfr_ref, qgc_ref, kgc_ref,
                       wa_ref, wgl_ref, cw_ref, cb_ref, wflt_ref, bfc_ref, qgr_ref, kgr_ref,
                       bd_ref,
                       conv_ref, q_ref, k_ref, v_ref, kt_ref, vt_ref, lft_ref, lfp_ref,
                       gate_ref, zlast_ref, zbuf, *, n_seg, seg_len):
    xn = _rms_rows(x_ref[...], g1_ref[...])
    zbuf[:, ZPAD - CONV_HALO:ZPAD, :] = left_ref[...]
    conv = _short_conv(xn, wa_ref, cw_ref, cb_ref, zbuf, zlast_ref, n_seg, seg_len)
    for s in range(n_seg):
        conv_ref[s * seg_len:(s + 1) * seg_len, :] = conv[s].astype(BF16)

    def rows_major(j):
        return _dot_nt(xn, wqkv_ref[j * D_ATTN:(j + 1) * D_ATTN, :])

    def feature_major(j):
        return _dot_nt(wqkv_ref[j * D_ATTN:(j + 1) * D_ATTN, :], xn)

    def head_norm(u, g_row):
        ssq = _dot((u * u).astype(BF16), bd_ref[...])
        return u * lax.rsqrt(ssq * (1.0 / HEAD_DIM) + EPS) * g_row

    q_ref[...] = (head_norm(rows_major(0), qgr_ref[...]) * ATTN_SCALE).astype(BF16)
    k_ref[...] = head_norm(rows_major(1), kgr_ref[...])
    v_ref[...] = rows_major(2)
    kt_ref[...] = _head_norm_t(feature_major(1), kgc_ref[...])
    vt_ref[...] = feature_major(2)
    lfp_ref[...] = _log_sigmoid(_dot_nt(xn, wfl_ref[...]) + bfr_ref[...])
    lft_ref[...] = _log_sigmoid(_dot_nt(wflt_ref[...], xn) + bfc_ref[...])
    gate_ref[...] = jax.nn.sigmoid(_dot_nt(xn, wgl_ref[...])).astype(BF16)


def _project_small(x2d, left, wts):
    n_rows = x2d.shape[0]
    n_seq = left.shape[0]
    seg_len = n_rows // n_seq
    full = lambda *shape: pl.BlockSpec(shape, lambda i: (0,) * len(shape))
    out_shape = (
        jax.ShapeDtypeStruct((n_rows, D_CONV), BF16),
        jax.ShapeDtypeStruct((n_rows, D_ATTN), BF16),
        jax.ShapeDtypeStruct((n_rows, D_ATTN), F32),
        jax.ShapeDtypeStruct((n_rows, D_ATTN), F32),
        jax.ShapeDtypeStruct((D_ATTN, n_rows), F32),
        jax.ShapeDtypeStruct((D_ATTN, n_rows), F32),
        jax.ShapeDtypeStruct((N_HEADS, n_rows), F32),
        jax.ShapeDtypeStruct((n_rows, LANES), F32),
        jax.ShapeDtypeStruct((n_rows, 2 * D_MODEL), BF16),
        jax.ShapeDtypeStruct((n_seq, CONV_HALO, D_CONV), F32),
    )
    kern = functools.partial(_proj_small_kernel, n_seg=n_seq, seg_len=seg_len)
    return pl.pallas_call(
        kern, grid=(1,),
        in_specs=[full(*x2d.shape), full(*left.shape)] + [_const_spec(w.shape) for w in wts],
        out_specs=tuple(full(*s.shape) for s in out_shape), out_shape=out_shape,
        scratch_shapes=[pltpu.VMEM((n_seq, seg_len + ZPAD, D_CONV), F32)],
        compiler_params=_params(1), name="proj_small",
    )(x2d, left, *wts)


def _prompt_attn_kernel(qt_ref, kt_ref, vt_ref, lft_ref, lftm_ref, lfp_ref, lfpm_ref,
                        qg_ref, kg_ref, o_ref, kpos, vb, kbias, crow, cend, qcat, m_s, acc_s,
                        sbuf, tail, *, seq):
    blk = ATTN_BLOCK
    half = blk // 2
    n_blk = seq // blk
    length = N_META + seq
    n_pos = kpos.shape[0]
    n_bias = N_SPLIT * N_HEADS

    for j in range(n_blk):
        kpos[j * blk:(j + 1) * blk, :] = kt_ref[:, j * blk:(j + 1) * blk].T.astype(BF16)
    tail[...] = jnp.zeros(tail.shape, F32)
    tail[:, 0:N_META] = kt_ref[:, seq:length]
    kpos[seq:seq + LANES, :] = tail[...].T.astype(BF16)
    kpos[seq + LANES:, :] = jnp.zeros((n_pos - seq - LANES, D_ATTN), BF16)

    ones_row = (lax.broadcasted_iota(jnp.int32, (V_SLAB - HEAD_DIM, half), 0) == 0).astype(BF16)
    vb[2 * n_blk] = jnp.zeros((N_HEADS * V_SLAB, half), BF16)
    for h in range(N_HEADS):
        rows = slice(h * HEAD_DIM, (h + 1) * HEAD_DIM)
        slab = slice(h * V_SLAB, h * V_SLAB + HEAD_DIM)
        for j in range(2 * n_blk + 1):
            if j < 2 * n_blk:
                vb[j, slab, :] = vt_ref[rows, j * half:(j + 1) * half].astype(BF16)
            else:
                vb[j, slab, 0:N_META] = vt_ref[rows, seq:length].astype(BF16)
            vb[j, h * V_SLAB + HEAD_DIM:(h + 1) * V_SLAB, :] = ones_row

    qk_bound = (NORM_SLACK * HEAD_DIM * ATTN_SCALE * LOG2E
                * jnp.max(jnp.abs(qg_ref[...]), axis=1, keepdims=True)
                * jnp.max(jnp.abs(kg_ref[...]), axis=1, keepdims=True))

    upper, lower = _triangles(blk)

    def store_kbias(rows, c_col):
        hi, mid, lo = _split3(c_col * LOG2E)
        lane = lax.broadcasted_iota(jnp.int32, c_col.shape, 1)
        grp = lane // N_HEADS
        part = jnp.where(grp == 0, hi, jnp.where(grp == 1, mid, lo))
        kbias[rows, :] = jnp.where(lane < n_bias, -part,
                                   jnp.where(lane < 2 * n_bias, 1.0, 0.0)).astype(BF16)

    off_c = jnp.zeros((1, LANES), F32)
    for j in range(n_blk):
        if j == 0:
            lf_blk = jnp.concatenate([lfpm_ref[...], lfp_ref[0:blk - N_META, :]], axis=0)
        else:
            lf_blk = lfp_ref[j * blk - N_META:(j + 1) * blk - N_META, :]
        c_col = _cumsum_rows(lower, lf_blk) + off_c
        store_kbias(slice(j * blk, (j + 1) * blk), c_col)
        off_c = c_col[blk - 1:blk, :]
    c_col = _cumsum_few(lfp_ref[seq - N_META:seq, :], axis=0) + off_c
    store_kbias(slice(seq, length), c_col)
    kbias[length:, :] = jnp.zeros((n_pos - length, LANES), BF16)

    meta_r = _cumsum_few(lftm_ref[...], axis=1)
    off_r = meta_r[:, N_META - 1:N_META]
    lane_h = lax.broadcasted_iota(jnp.int32, (N_HEADS, LANES), 1)
    c_end = jnp.zeros((N_HEADS, LANES), F32)
    for j in range(n_blk):
        c_row = _cumsum_lanes(lft_ref[:, j * blk:(j + 1) * blk], upper) + off_r
        crow[j] = c_row * LOG2E
        off_r = c_row[:, blk - 1:blk]
        c_end = jnp.where(lane_h == 2 * j, c_row[:, half - 1:half] * LOG2E, c_end)
        c_end = jnp.where(lane_h == 2 * j + 1, off_r * LOG2E, c_end)
    cend[...] = c_end

    row128 = lax.broadcasted_iota(jnp.int32, (LANES, blk), 0)
    krow = lax.broadcasted_iota(jnp.int32, (blk + N_META, blk), 0)
    qcol = lax.broadcasted_iota(jnp.int32, (blk + N_META, blk), 1)

    def q_block(t, _):
        tok0 = pl.multiple_of(t * blk, blk)
        c_q = crow[t]
        hi, mid, lo = _split3(c_q)
        bias_rows = jnp.concatenate(
            [jnp.ones((n_bias, blk), F32), hi, mid, lo,
             jnp.zeros((LANES - 2 * n_bias, blk), F32)], axis=0)
        for h in range(N_HEADS):
            pair = qt_ref[(h // 2) * LANES:(h // 2 + 1) * LANES, pl.ds(tok0, blk)]
            in_head = (row128 // HEAD_DIM) == (h % 2)
            qcat[h, 0:LANES, :] = jnp.where(in_head, pair, jnp.zeros_like(pair))
            qcat[h, LANES:, :] = jnp.where(row128 % N_HEADS == h, bias_rows, 0.0).astype(BF16)
        m_s[...] = jnp.full(m_s.shape, -jnp.inf, F32)
        acc_s[...] = jnp.zeros(acc_s.shape, F32)

        gap = 2.0 * qk_bound + c_q[:, 0:1] - cend[...]
        needed = jnp.logical_and(gap >= -SKIP_LOG2, lane_h < 2 * t)
        n_needed = jnp.max(
            jnp.sum(jnp.where(needed, 1.0, 0.0), axis=1, keepdims=True)).astype(jnp.int32)

        def key_block(pos0, n_rows, pv, visible):
            m_blk = []
            for h in range(N_HEADS):
                g = h // 2
                kc = jnp.concatenate([kpos[pl.ds(pos0, n_rows), g * LANES:(g + 1) * LANES],
                                      kbias[pl.ds(pos0, n_rows), :]], axis=1)
                s = _dot(kc, qcat[h])
                if visible is not None:
                    s = jnp.where(visible, s, -jnp.inf)
                sbuf[h, 0:n_rows, :] = s
                m_blk.append(jnp.max(s, axis=0, keepdims=True))
            for h in range(N_HEADS):
                slab = slice(h * V_SLAB, (h + 1) * V_SLAB)
                m_old = m_s[h:h + 1, :]
                m_new = jnp.maximum(m_old, m_blk[h])
                alpha = jnp.exp2(m_old - m_new)
                p = jnp.exp2(sbuf[h, 0:n_rows, :] - m_new)
                m_s[h:h + 1, :] = m_new
                acc_s[slab, :] = alpha * acc_s[slab, :] + pv(slab, p.astype(BF16))

        def older_half_block(j, _):
            key_block(pl.multiple_of(j * half, half), half,
                      lambda slab, p: _dot(vb[j, slab, :], p), None)
            return 0

        def own_pv(slab, p):
            spill = jnp.concatenate([p[blk:], jnp.zeros((half - N_META, blk), BF16)], axis=0)
            return (_dot(vb[2 * t, slab, :], p[0:half]) + _dot(vb[2 * t + 1, slab, :], p[half:blk])
                    + _dot(vb[2 * t + 2, slab, :], spill))

        key_block(tok0, blk + N_META, own_pv, krow <= qcol + N_META)
        lax.fori_loop(2 * t - n_needed, 2 * t, older_half_block, 0)

        o_t = []
        for h in range(N_HEADS):
            norm = acc_s[h * V_SLAB + HEAD_DIM:h * V_SLAB + HEAD_DIM + 1, :]
            o_t.append(acc_s[h * V_SLAB:h * V_SLAB + HEAD_DIM, :] * (1.0 / norm))
        o_ref[pl.ds(tok0, blk), :] = jnp.concatenate(o_t, axis=0).T.astype(BF16)
        return 0

    lax.fori_loop(0, n_blk, q_block, 0)


def _sample_attn_kernel(q_ref, kn_ref, vn_ref, lf_ref, lft_ref, ckt_ref, cvt_ref, clft_ref,
                        o_ref, crow, *, past, dec):
    blk = ATTN_BLOCK
    n_keys = past + LANES

    upper, _ = _triangles(blk)
    off = jnp.zeros((N_HEADS, 1), F32)
    for j in range(past // blk):
        loc = _cumsum_lanes(clft_ref[:, j * blk:(j + 1) * blk], upper) + off
        crow[:, j * blk:(j + 1) * blk] = loc
        off = loc[:, blk - 1:blk]
    crow[:, 0:past] = crow[:, 0:past] - off

    cq_c = _cumsum_few(lf_ref[:, 0:N_HEADS], axis=0)
    cq_r = _cumsum_few(lft_ref[...], axis=1)
    crow[:, past:] = jnp.full((N_HEADS, LANES), MASKED_BIAS, F32)
    crow[:, past:past + dec] = cq_r

    q = q_ref[...]
    lane_head = lax.broadcasted_iota(jnp.int32, (dec, D_ATTN), 1) // HEAD_DIM
    q_exp = jnp.concatenate(
        [jnp.where(lane_head == h, q, jnp.zeros_like(q)) for h in range(N_HEADS)], axis=0)
    pad_rows = jnp.zeros((LANES - dec, D_ATTN), BF16)
    k_new = jnp.concatenate([kn_ref[...].astype(BF16), pad_rows], axis=0)
    v_new = jnp.concatenate([vn_ref[...].astype(BF16), pad_rows], axis=0)
    s_all = jnp.concatenate(
        [_dot(q_exp, ckt_ref[...].astype(BF16)), _dot_nt(q_exp, k_new)], axis=1)

    kpos = lax.broadcasted_iota(jnp.int32, (dec, n_keys), 1)
    qpos = past + lax.broadcasted_iota(jnp.int32, (dec, n_keys), 0)
    visible = kpos <= qpos
    probs = []
    norms = []
    for h in range(N_HEADS):
        s = s_all[h * dec:(h + 1) * dec, :] + cq_c[:, h:h + 1] - crow[h:h + 1, :]
        s = jnp.where(visible, s, -jnp.inf)
        p = jnp.exp(s - jnp.max(s, axis=-1, keepdims=True))
        norms.append(jnp.sum(p, axis=-1, keepdims=True))
        probs.append(p.astype(BF16))
    p_all = jnp.concatenate(probs, axis=0)
    o_all = (_dot_nt(p_all[:, 0:past], cvt_ref[...].astype(BF16))
             + _dot(p_all[:, past:], v_new))
    out = jnp.zeros((dec, D_ATTN), F32)
    for h in range(N_HEADS):
        o = o_all[h * dec:(h + 1) * dec, :] / norms[h]
        out = out + jnp.where(lane_head == h, o, 0.0)
    o_ref[...] = out.astype(BF16)


def _attn_kernel(*refs, seq, past, dec):
    n_p, n_s = _N_PROMPT_IN, _N_SAMPLE_IN
    p_in, s_in = refs[0:n_p], refs[n_p:n_p + n_s]
    o_ref, os_ref = refs[n_p + n_s:n_p + n_s + 2]
    scratch = refs[n_p + n_s + 2:]
    _sample_attn_kernel(*s_in, os_ref, scratch[-1], past=past, dec=dec)
    _prompt_attn_kernel(*p_in, o_ref, *scratch[:-1], seq=seq)


_N_PROMPT_IN = 9
_N_SAMPLE_IN = 8


def _attention(prompt_in, sample_in):
    qt, kt, vt, lft, lft_meta, lfp, lfp_meta, q_gain, k_gain = prompt_in
    q_s, _, _, _, _, cache_kt, _, _ = sample_in
    b, _, seq = qt.shape
    db, dec, _ = q_s.shape
    assert b == db, "one new stream and one running stream per grid step"
    length = vt.shape[2]
    past = cache_kt.shape[2]
    blk = ATTN_BLOCK
    n_blk = seq // blk
    n_pos = (n_blk + 1) * blk
    per_b = lambda *shape: pl.BlockSpec((None,) + shape, lambda i: (i,) + (0,) * len(shape))
    in_specs = [per_b(D_ATTN, seq), per_b(D_ATTN, length), per_b(D_ATTN, length),
                per_b(N_HEADS, seq), _const_spec(lft_meta.shape),
                per_b(seq, LANES), _const_spec(lfp_meta.shape),
                _const_spec(q_gain.shape), _const_spec(k_gain.shape),
                per_b(dec, D_ATTN), per_b(dec, D_ATTN), per_b(dec, D_ATTN),
                per_b(dec, LANES), per_b(N_HEADS, dec),
                per_b(D_ATTN, past), per_b(D_ATTN, past), per_b(N_HEADS, past)]
    kern = functools.partial(_attn_kernel, seq=seq, past=past, dec=dec)
    return pl.pallas_call(
        kern, grid=(b,), in_specs=in_specs,
        out_specs=(per_b(seq, D_ATTN), per_b(dec, D_ATTN)),
        out_shape=(jax.ShapeDtypeStruct((b, seq, D_ATTN), BF16),
                   jax.ShapeDtypeStruct((b, dec, D_ATTN), BF16)),
        scratch_shapes=[pltpu.VMEM((n_pos, D_ATTN), BF16),
                        pltpu.VMEM((2 * n_blk + 1, N_HEADS * V_SLAB, blk // 2), BF16),
                        pltpu.VMEM((n_pos, LANES), BF16),
                        pltpu.VMEM((n_blk, N_HEADS, blk), F32),
                        pltpu.VMEM((N_HEADS, LANES), F32),
                        pltpu.VMEM((N_HEADS, 2 * LANES, blk), BF16),
                        pltpu.VMEM((N_HEADS, blk), F32),
                        pltpu.VMEM((N_HEADS * V_SLAB, blk), F32),
                        pltpu.VMEM((N_HEADS, blk + N_META, blk), F32),
                        pltpu.VMEM((D_ATTN, LANES), F32),
                        pltpu.VMEM((N_HEADS, past + LANES), F32)],
        compiler_params=_params(1), name="attention",
    )(*prompt_in, *sample_in)


def _merge_mlp_tail(x, conv_bf16, attn_ref, g_conv, g_attn, wbc_ref, wba_ref, wo_ref, g2_ref,
                    wup_ref, wdn_ref, y_ref):
    merged = g_conv * _dot(conv_bf16, wbc_ref[...]) + g_attn * _dot(attn_ref[...], wba_ref[...])
    h = x + _dot(merged.astype(BF16), wo_ref[...])
    hn = _rms_rows(h, g2_ref[...])
    acc = h
    for c in range(D_FF // D_MODEL):
        cols = slice(c * D_MODEL, (c + 1) * D_MODEL)
        a = jnp.maximum(_dot(hn, wup_ref[:, cols]), 0.0)
        acc = acc + _dot((a * a).astype(BF16), wdn_ref[cols, :])
    y_ref[...] = acc


def _branch_mlp_kernel(x_ref, left_ref, attn_ref, xs_ref, convs_ref, attns_ref, gates_ref,
                       g1_ref, wa_ref, wgl_ref, cw_ref, cb_ref,
                       wbc_ref, wba_ref, wo_ref, g2_ref, wup_ref, wdn_ref,
                       y_ref, zlast_ref, ys_ref, zbuf, *, n_tiles, tiles_per_seq):
    step = pl.program_id(0)

    @pl.when(jnp.logical_and(step % tiles_per_seq == 0, step < n_tiles))
    def _():
        zbuf[:, ZPAD - CONV_HALO:ZPAD, :] = left_ref[...]

    @pl.when(step < n_tiles)
    def _():
        x = x_ref[...]
        xn = _rms_rows(x, g1_ref[...])
        conv = _short_conv(xn, wa_ref, cw_ref, cb_ref, zbuf, zlast_ref, 1, x.shape[0])[0]
        g_conv = jax.nn.sigmoid(_dot_nt(xn, wgl_ref[0:D_MODEL, :]))
        g_attn = jax.nn.sigmoid(_dot_nt(xn, wgl_ref[D_MODEL:2 * D_MODEL, :]))
        _merge_mlp_tail(x, conv.astype(BF16), attn_ref, g_conv, g_attn, wbc_ref, wba_ref,
                        wo_ref, g2_ref, wup_ref, wdn_ref, y_ref)

    @pl.when(step == n_tiles)
    def _():
        _merge_mlp_tail(xs_ref[...], convs_ref[...], attns_ref,
                        gates_ref[:, 0:D_MODEL].astype(F32),
                        gates_ref[:, D_MODEL:2 * D_MODEL].astype(F32),
                        wbc_ref, wba_ref, wo_ref, g2_ref, wup_ref, wdn_ref, ys_ref)


def _branch_mlp(x2d, left, attn, small, wts, *, rows, seq):
    n_rows = x2d.shape[0]
    tiles = seq // rows
    n_tiles = n_rows // rows
    last = n_tiles - 1
    row_spec = lambda width: pl.BlockSpec((rows, width), lambda i: (jnp.minimum(i, last), 0))
    seq_spec = pl.BlockSpec((1, CONV_HALO, D_CONV),
                            lambda i: (jnp.minimum(i, last) // tiles, 0, 0))
    full = lambda a: pl.BlockSpec(a.shape, lambda i: (0,) * a.ndim)
    kern = functools.partial(_branch_mlp_kernel, n_tiles=n_tiles, tiles_per_seq=tiles)
    return pl.pallas_call(
        kern, grid=(n_tiles + 1,),
        in_specs=[row_spec(D_MODEL), seq_spec, row_spec(D_ATTN)] + [full(a) for a in small]
        + [_const_spec(w.shape) for w in wts],
        out_specs=(row_spec(D_MODEL), seq_spec, full(small[0])),
        out_shape=(jax.ShapeDtypeStruct((n_rows, D_MODEL), F32),
                   jax.ShapeDtypeStruct((n_rows // seq, CONV_HALO, D_CONV), F32),
                   jax.ShapeDtypeStruct(small[0].shape, F32)),
        scratch_shapes=[pltpu.VMEM((1, rows + ZPAD, D_CONV), F32)],
        compiler_params=_params(1), name="branch_mlp",
    )(x2d, left, attn, *small, *wts)


def kernel(x_prompt, x_sample, cache_k, cache_v, cache_logf, state_conv, meta,
           norm1_g, w_in, b_f, conv_w, conv_b, q_norm_g, k_norm_g,
           w_br_conv, w_br_attn, w_out, norm2_g, w_up, w_down):
    b, seq, _ = x_prompt.shape
    db, dec, _ = x_sample.shape
    past = cache_k.shape[2]
    length = N_META + seq
    n_main = 3 * D_CONV + 3 * D_ATTN
    q0 = 3 * D_CONV

    wt = w_in[0].T
    wt_conv = wt[0:q0].astype(BF16)
    wt_qkv = wt[q0:n_main].astype(BF16)
    wt_fl = wt[n_main:n_main + N_HEADS].astype(BF16)
    wt_gate = wt[n_main + N_HEADS:].astype(BF16)
    head_of = jnp.arange(D_ATTN) // HEAD_DIM
    qg = jnp.tile(q_norm_g[0], N_HEADS)
    kg = jnp.tile(k_norm_g[0], N_HEADS)
    g1 = norm1_g[0][None, :]
    qkv_wts = (g1, wt_qkv, jnp.tile(wt_fl, (LANES // N_HEADS, 1)),
               jnp.tile(b_f[0], LANES // N_HEADS)[None, :], qg[:, None], kg[:, None])
    conv_wts = (conv_w[0], conv_b[0][None, :])
    mlp_wts = (w_br_conv[0].astype(BF16), w_br_attn[0].astype(BF16), w_out[0].astype(BF16),
               norm2_g[0][None, :], w_up[0].astype(BF16), w_down[0].astype(BF16))

    x_small = jnp.concatenate([meta, x_sample.reshape(db * dec, D_MODEL)], axis=0)
    left_small = jnp.concatenate(
        [jnp.zeros((1, CONV_HALO, D_CONV), F32), state_conv[0]], axis=0)
    (conv_s, q_s, k_s, v_s, kt_s, vt_s, lft_s, lfp_s, gate_s, zlast_s) = _project_small(
        x_small, left_small,
        qkv_wts + (wt_conv, wt_gate) + conv_wts
        + (wt_fl, b_f[0][:, None], qg[None, :], kg[None, :],
           (head_of[:, None] == head_of[None, :]).astype(BF16)))

    x_rows = x_prompt.reshape(b * seq, D_MODEL)
    (qt_p, kt_p, vt_p, lftp_p, lft_p, lfp_p) = _project_qkv(
        x_rows, qkv_wts, (kt_s, vt_s, lft_s), b=b, seq=seq)
    k_new = k_s[N_META:].reshape(db, dec, D_ATTN)
    v_new = v_s[N_META:].reshape(db, dec, D_ATTN)
    lf_new = lfp_s[N_META:, :N_HEADS].reshape(db, dec, N_HEADS)
    cache_kt = jnp.transpose(cache_k[0], (0, 2, 3, 1)).reshape(db, D_ATTN, past)
    cache_vt = jnp.transpose(cache_v[0], (0, 2, 3, 1)).reshape(db, D_ATTN, past)
    attn_p, attn_s = _attention(
        (qt_p, kt_p, vt_p, lft_p, lft_s[:, :N_META], lfp_p.reshape(b, seq, LANES),
         lfp_s[:N_META], q_norm_g, k_norm_g),
        (q_s[N_META:].reshape(db, dec, D_ATTN), k_new, v_new,
         lfp_s[N_META:].reshape(db, dec, LANES), jnp.swapaxes(lf_new, 1, 2),
         cache_kt, cache_vt, jnp.swapaxes(cache_logf[0], 1, 2)))

    left_p = jnp.broadcast_to(zlast_s[0:1], (b, CONV_HALO, D_CONV))
    y_prompt, zlast_p, y_sample = _branch_mlp(
        x_rows, left_p, attn_p.reshape(b * seq, D_ATTN),
        (x_sample.reshape(db * dec, D_MODEL), conv_s[N_META:],
         attn_s.reshape(db * dec, D_ATTN), gate_s[N_META:]),
        (g1, wt_conv, wt_gate) + conv_wts + mlp_wts, rows=MLP_ROWS, seq=seq)

    def heads_last(t):
        return jnp.transpose(t.reshape(b, N_HEADS, HEAD_DIM, length), (0, 3, 1, 2))[None]

    return (y_prompt.reshape(b, seq, D_MODEL),
            y_sample.reshape(db, dec, D_MODEL),
            heads_last(kt_p),
            heads_last(vt_p),
            jnp.swapaxes(lftp_p, 1, 2)[None],
            zlast_p[None],
            k_new.reshape(1, db, dec, N_HEADS, HEAD_DIM),
            v_new.reshape(1, db, dec, N_HEADS, HEAD_DIM),
            lf_new[None],
            zlast_s[1:][None])
```

```python
import functools

import jax
import jax.numpy as jnp
from jax import lax
from jax.experimental import pallas as pl
from jax.experimental.pallas import tpu as pltpu

D_MODEL = 1024
D_CONV = D_MODEL // 2
CONV_W = 3
N_HEADS = 8
HEAD_DIM = 64
D_ATTN = N_HEADS * HEAD_DIM
D_FF = 4 * D_MODEL
N_META = 16
EPS = 1e-6
ATTN_SCALE = HEAD_DIM ** -0.5

F32 = jnp.float32
BF16 = jnp.bfloat16

VMEM_LIMIT_BYTES = 56 * 1024 * 1024
LANES = 128
SUBLANES = 8
BF16_ROWS = 16
PROJ_ROWS = 1024
MLP_ROWS = 512
ATTN_BLOCK = 256
MASKED_BIAS = 1e30
CONV_HALO = CONV_W - 1
ZPAD = SUBLANES
CAST_CHUNK = (512, 1024)
N_SPLIT = 3
V_SLAB = HEAD_DIM + BF16_ROWS
LOG2E = 1.4426950408889634
SKIP_LOG2 = 40.0
NORM_SLACK = 1.02


def _dot(a, b):
    return jnp.dot(a, b, preferred_element_type=F32)


def _dot_nt(a, b):
    return lax.dot_general(a, b, (((1,), (1,)), ((), ())), preferred_element_type=F32)


def _log_sigmoid(x):
    return jnp.minimum(x, 0.0) - jnp.log1p(jnp.exp(-jnp.abs(x)))


def _cumsum_few(x, axis):
    n = x.shape[axis]
    idx = lax.broadcasted_iota(jnp.int32, x.shape, axis)
    out = jnp.zeros(x.shape, F32)
    for i in range(n):
        term = x[i:i + 1, :] if axis == 0 else x[:, i:i + 1]
        out = out + jnp.where(idx >= i, term, 0.0)
    return out


def _triangles(n):
    r = lax.broadcasted_iota(jnp.int32, (n, n), 0)
    c = lax.broadcasted_iota(jnp.int32, (n, n), 1)
    upper = jnp.where(r <= c, 1.0, 0.0).astype(BF16)
    lower = jnp.where(r >= c, 1.0, 0.0).astype(BF16)
    return upper, lower


def _split3(c):
    hi = c.astype(BF16).astype(F32)
    r1 = c - hi
    mid = r1.astype(BF16).astype(F32)
    return hi, mid, r1 - mid


def _cumsum_rows(tri_lower, x):
    w = x.shape[1]
    pieces = jnp.concatenate(_split3(x), axis=1).astype(BF16)
    y = _dot(tri_lower, pieces)
    return y[:, 0:w] + y[:, w:2 * w] + y[:, 2 * w:3 * w]


def _cumsum_lanes(x, tri_upper):
    h = x.shape[0]
    pieces = jnp.concatenate(_split3(x), axis=0).astype(BF16)
    y = _dot(pieces, tri_upper)
    return y[0:h] + y[h:2 * h] + y[2 * h:3 * h]


def _const_spec(shape):
    nd = len(shape)
    return pl.BlockSpec(shape, lambda *_: (0,) * nd, pipeline_mode=pl.Buffered(1))


def _params(n_axes):
    return pltpu.CompilerParams(
        dimension_semantics=("arbitrary",) * n_axes,
        vmem_limit_bytes=VMEM_LIMIT_BYTES)


def _rms_rows(x, g_row):
    ms = jnp.mean(x * x, axis=-1, keepdims=True)
    return (x * lax.rsqrt(ms + EPS) * g_row).astype(BF16)


def _head_norm_t(ut, g_col):
    out = []
    for h in range(N_HEADS):
        blk = ut[h * HEAD_DIM:(h + 1) * HEAD_DIM, :]
        ms = jnp.mean(blk * blk, axis=0, keepdims=True)
        out.append(blk * lax.rsqrt(ms + EPS) * g_col[h * HEAD_DIM:(h + 1) * HEAD_DIM, :])
    return jnp.concatenate(out, axis=0)


def _short_conv(xn, wa_ref, cw_ref, cb_ref, zbuf, zlast_ref, n_seg, seg_len):
    cb = _dot_nt(xn, wa_ref[0:D_CONV, :])
    z = (_dot_nt(xn, wa_ref[D_CONV:2 * D_CONV, :])
         * _dot_nt(xn, wa_ref[2 * D_CONV:3 * D_CONV, :]))
    out = []
    for s in range(n_seg):
        r0 = s * seg_len
        zs = z[r0:r0 + seg_len]
        zbuf[s, ZPAD:ZPAD + seg_len, :] = zs
        y = None
        for i in range(CONV_W):
            lo = ZPAD - CONV_HALO + i
            tap = zs if i == CONV_HALO else zbuf[s, lo:lo + seg_len, :]
            term = tap * cw_ref[i:i + 1, :]
            y = term if y is None else y + term
        out.append(cb[r0:r0 + seg_len] * (y + cb_ref[...]))
        tail = zbuf[s, ZPAD + seg_len - CONV_HALO:ZPAD + seg_len, :]
        zlast_ref[s] = tail
        zbuf[s, ZPAD - CONV_HALO:ZPAD, :] = tail
    return out


def _qkv_kernel(x_ref, g1_ref, wqkv_ref, wfl_ref, bfr_ref, qgc_ref, kgc_ref,
                ktm_ref, vtm_ref, lftm_ref,
                qt_ref, kt_ref, vt_ref, lftp_ref, lft_ref, lfp_ref, kcar, vcar, lcar,
                *, tiles_per_seq):
    rows = x_ref.shape[0]
    step = pl.program_id(0) % tiles_per_seq

    def tile_body(tile_idx):
        lane0 = tile_idx * rows

        def shifted_store(out_ref, car_ref, meta_ref, tile):
            left = meta_ref[:, 0:LANES] if tile_idx == 0 else car_ref[...]
            rolled = pltpu.roll(tile, N_META, axis=1)
            lane = lax.broadcasted_iota(jnp.int32, (tile.shape[0], LANES), 1)
            out_ref[:, lane0:lane0 + LANES] = jnp.where(lane < N_META, left, rolled[:, 0:LANES])
            out_ref[:, lane0 + LANES:lane0 + rows] = rolled[:, LANES:]
            if tile_idx < tiles_per_seq - 1:
                car_ref[...] = rolled[:, 0:LANES]
            else:
                out_ref[:, tiles_per_seq * rows:] = rolled[:, 0:N_META]

        xn = _rms_rows(x_ref[...], g1_ref[...])

        def feature_major(j):
            return _dot_nt(wqkv_ref[j * D_ATTN:(j + 1) * D_ATTN, :], xn)

        lfp = _log_sigmoid(_dot_nt(xn, wfl_ref[...]) + bfr_ref[...])
        lfp_ref[...] = lfp
        lft = lfp.T[0:N_HEADS, :]
        lft_ref[...] = lft
        qt = _head_norm_t(feature_major(0), qgc_ref[...])
        qt_ref[...] = (qt * (ATTN_SCALE * LOG2E)).astype(BF16)
        shifted_store(kt_ref, kcar, ktm_ref, _head_norm_t(feature_major(1), kgc_ref[...]))
        shifted_store(vt_ref, vcar, vtm_ref, feature_major(2))
        shifted_store(lftp_ref, lcar, lftm_ref, lft)

    for tile_idx in range(tiles_per_seq):
        pl.when(step == tile_idx)(functools.partial(tile_body, tile_idx))


def _project_qkv(x2d, wts, meta_cols, *, b, seq):
    rows = PROJ_ROWS
    tiles = seq // rows
    length = N_META + seq
    row_spec = lambda width: pl.BlockSpec((rows, width), lambda i: (i, 0))
    col_spec = lambda feat: pl.BlockSpec(
        (None, feat, rows), lambda i: (i // tiles, 0, i % tiles))
    seq_spec = lambda feat: pl.BlockSpec((None, feat, length), lambda i: (i // tiles, 0, 0))
    out_shape = (
        jax.ShapeDtypeStruct((b, D_ATTN, seq), BF16),
        jax.ShapeDtypeStruct((b, D_ATTN, length), F32),
        jax.ShapeDtypeStruct((b, D_ATTN, length), F32),
        jax.ShapeDtypeStruct((b, N_HEADS, length), F32),
        jax.ShapeDtypeStruct((b, N_HEADS, seq), F32),
        jax.ShapeDtypeStruct((b * seq, LANES), F32),
    )
    out_specs = (col_spec(D_ATTN), seq_spec(D_ATTN), seq_spec(D_ATTN), seq_spec(N_HEADS),
                 col_spec(N_HEADS), row_spec(LANES))
    kern = functools.partial(_qkv_kernel, tiles_per_seq=tiles)
    return pl.pallas_call(
        kern, grid=(b * tiles,),
        in_specs=[row_spec(D_MODEL)] + [_const_spec(w.shape) for w in wts + meta_cols],
        out_specs=out_specs, out_shape=out_shape,
        scratch_shapes=[pltpu.VMEM((D_ATTN, LANES), F32),
                        pltpu.VMEM((D_ATTN, LANES), F32),
                        pltpu.VMEM((N_HEADS, LANES), F32)],
        compiler_params=_params(1), name="proj_qkv",
    )(x2d, *wts, *meta_cols)


def _proj_small_kernel(x_ref, left_ref, g1_ref, wqkv_ref, wfl_ref, bfr_ref, qgc_ref, kgc_ref,
                       wa_ref, wgl_ref, cw_ref, cb_ref, wflt_ref, bfc_ref, qgr_ref, kgr_ref,
                       bd_ref,
                       conv_ref, q_ref, k_ref, v_ref, kt_ref, vt_ref, lft_ref, lfp_ref,
                       gate_ref, zlast_ref, zbuf, *, n_seg, seg_len):
    xn = _rms_rows(x_ref[...], g1_ref[...])
    zbuf[:, ZPAD - CONV_HALO:ZPAD, :] = left_ref[...]
    conv = _short_conv(xn, wa_ref, cw_ref, cb_ref, zbuf, zlast_ref, n_seg, seg_len)
    for s in range(n_seg):
        conv_ref[s * seg_len:(s + 1) * seg_len, :] = conv[s].astype(BF16)

    def rows_major(j):
        return _dot_nt(xn, wqkv_ref[j * D_ATTN:(j + 1) * D_ATTN, :])

    def feature_major(j):
        return _dot_nt(wqkv_ref[j * D_ATTN:(j + 1) * D_ATTN, :], xn)

    def head_norm(u, g_row):
        ssq = _dot((u * u).astype(BF16), bd_ref[...])
        return u * lax.rsqrt(ssq * (1.0 / HEAD_DIM) + EPS) * g_row

    q_ref[...] = (head_norm(rows_major(0), qgr_ref[...]) * ATTN_SCALE).astype(BF16)
    k_ref[...] = head_norm(rows_major(1), kgr_ref[...])
    v_ref[...] = rows_major(2)
    kt_ref[...] = _head_norm_t(feature_major(1), kgc_ref[...])
    vt_ref[...] = feature_major(2)
    lfp_ref[...] = _log_sigmoid(_dot_nt(xn, wfl_ref[...]) + bfr_ref[...])
    lft_ref[...] = _log_sigmoid(_dot_nt(wflt_ref[...], xn) + bfc_ref[...])
    gate_ref[...] = jax.nn.sigmoid(_dot_nt(xn, wgl_ref[...])).astype(BF16)


def _project_small(x2d, left, wts):
    n_rows = x2d.shape[0]
    n_seq = left.shape[0]
    seg_len = n_rows // n_seq
    full = lambda *shape: pl.BlockSpec(shape, lambda i: (0,) * len(shape))
    out_shape = (
        jax.ShapeDtypeStruct((n_rows, D_CONV), BF16),
        jax.ShapeDtypeStruct((n_rows, D_ATTN), BF16),
        jax.ShapeDtypeStruct((n_rows, D_ATTN), F32),
        jax.ShapeDtypeStruct((n_rows, D_ATTN), F32),
        jax.ShapeDtypeStruct((D_ATTN, n_rows), F32),
        jax.ShapeDtypeStruct((D_ATTN, n_rows), F32),
        jax.ShapeDtypeStruct((N_HEADS, n_rows), F32),
        jax.ShapeDtypeStruct((n_rows, LANES), F32),
        jax.ShapeDtypeStruct((n_rows, 2 * D_MODEL), BF16),
        jax.ShapeDtypeStruct((n_seq, CONV_HALO, D_CONV), F32),
    )
    kern = functools.partial(_proj_small_kernel, n_seg=n_seq, seg_len=seg_len)
    return pl.pallas_call(
        kern, grid=(1,),
        in_specs=[full(*x2d.shape), full(*left.shape)] + [_const_spec(w.shape) for w in wts],
        out_specs=tuple(full(*s.shape) for s in out_shape), out_shape=out_shape,
        scratch_shapes=[pltpu.VMEM((n_seq, seg_len + ZPAD, D_CONV), F32)],
        compiler_params=_params(1), name="proj_small",
    )(x2d, left, *wts)


def _prompt_attn_kernel(qt_ref, kt_ref, vt_ref, lft_ref, lftm_ref, lfp_ref, lfpm_ref,
                        qg_ref, kg_ref, o_ref, kpos, vb, kbias, crow, cend, qcat, m_s, acc_s,
                        sbuf, tail, *, seq):
    blk = ATTN_BLOCK
    n_blk = seq // blk
    length = N_META + seq
    n_pos = kpos.shape[0]
    n_bias = N_SPLIT * N_HEADS

    for j in range(n_blk):
        kpos[j * blk:(j + 1) * blk, :] = kt_ref[:, j * blk:(j + 1) * blk].T.astype(BF16)
    tail[...] = jnp.zeros(tail.shape, F32)
    tail[:, 0:N_META] = kt_ref[:, seq:length]
    kpos[seq:seq + LANES, :] = tail[...].T.astype(BF16)
    kpos[seq + LANES:, :] = jnp.zeros((n_pos - seq - LANES, D_ATTN), BF16)

    ones_row = (lax.broadcasted_iota(jnp.int32, (V_SLAB - HEAD_DIM, blk), 0) == 0).astype(BF16)
    vb[n_blk] = jnp.zeros((N_HEADS * V_SLAB, blk), BF16)
    for h in range(N_HEADS):
        rows = slice(h * HEAD_DIM, (h + 1) * HEAD_DIM)
        slab = slice(h * V_SLAB, h * V_SLAB + HEAD_DIM)
        for j in range(n_blk + 1):
            if j < n_blk:
                vb[j, slab, :] = vt_ref[rows, j * blk:(j + 1) * blk].astype(BF16)
            else:
                vb[j, slab, 0:N_META] = vt_ref[rows, seq:length].astype(BF16)
            vb[j, h * V_SLAB + HEAD_DIM:(h + 1) * V_SLAB, :] = ones_row

    qk_bound = (NORM_SLACK * HEAD_DIM * ATTN_SCALE * LOG2E
                * jnp.max(jnp.abs(qg_ref[...]), axis=1, keepdims=True)
                * jnp.max(jnp.abs(kg_ref[...]), axis=1, keepdims=True))

    upper, lower = _triangles(blk)

    def store_kbias(rows, c_col):
        hi, mid, lo = _split3(c_col * LOG2E)
        lane = lax.broadcasted_iota(jnp.int32, c_col.shape, 1)
        grp = lane // N_HEADS
        part = jnp.where(grp == 0, hi, jnp.where(grp == 1, mid, lo))
        kbias[rows, :] = jnp.where(lane < n_bias, -part,
                                   jnp.where(lane < 2 * n_bias, 1.0, 0.0)).astype(BF16)

    off_c = jnp.zeros((1, LANES), F32)
    for j in range(n_blk):
        if j == 0:
            lf_blk = jnp.concatenate([lfpm_ref[...], lfp_ref[0:blk - N_META, :]], axis=0)
        else:
            lf_blk = lfp_ref[j * blk - N_META:(j + 1) * blk - N_META, :]
        c_col = _cumsum_rows(lower, lf_blk) + off_c
        store_kbias(slice(j * blk, (j + 1) * blk), c_col)
        off_c = c_col[blk - 1:blk, :]
    c_col = _cumsum_few(lfp_ref[seq - N_META:seq, :], axis=0) + off_c
    store_kbias(slice(seq, length), c_col)
    kbias[length:, :] = jnp.zeros((n_pos - length, LANES), BF16)

    meta_r = _cumsum_few(lftm_ref[...], axis=1)
    off_r = meta_r[:, N_META - 1:N_META]
    lane_h = lax.broadcasted_iota(jnp.int32, (N_HEADS, LANES), 1)
    c_end = jnp.zeros((N_HEADS, LANES), F32)
    for j in range(n_blk):
        c_row = _cumsum_lanes(lft_ref[:, j * blk:(j + 1) * blk], upper) + off_r
        crow[j] = c_row * LOG2E
        off_r = c_row[:, blk - 1:blk]
        c_end = jnp.where(lane_h == j, off_r * LOG2E, c_end)
    cend[...] = c_end

    row128 = lax.broadcasted_iota(jnp.int32, (LANES, blk), 0)
    krow = lax.broadcasted_iota(jnp.int32, (blk + N_META, blk), 0)
    qcol = lax.broadcasted_iota(jnp.int32, (blk + N_META, blk), 1)

    def q_block(t, _):
        tok0 = pl.multiple_of(t * blk, blk)
        c_q = crow[t]
        hi, mid, lo = _split3(c_q)
        bias_rows = jnp.concatenate(
            [jnp.ones((n_bias, blk), F32), hi, mid, lo,
             jnp.zeros((LANES - 2 * n_bias, blk), F32)], axis=0)
        for h in range(N_HEADS):
            pair = qt_ref[(h // 2) * LANES:(h // 2 + 1) * LANES, pl.ds(tok0, blk)]
            in_head = (row128 // HEAD_DIM) == (h % 2)
            qcat[h, 0:LANES, :] = jnp.where(in_head, pair, jnp.zeros_like(pair))
            qcat[h, LANES:, :] = jnp.where(row128 % N_HEADS == h, bias_rows, 0.0).astype(BF16)
        m_s[...] = jnp.full(m_s.shape, -jnp.inf, F32)
        acc_s[...] = jnp.zeros(acc_s.shape, F32)

        gap = 2.0 * qk_bound + c_q[:, 0:1] - cend[...]
        needed = jnp.logical_and(gap >= -SKIP_LOG2, lane_h < t)
        n_needed = jnp.max(
            jnp.sum(jnp.where(needed, 1.0, 0.0), axis=1, keepdims=True)).astype(jnp.int32)

        def key_block(pos0, n_rows, pv, visible):
            m_blk = []
            for h in range(N_HEADS):
                g = h // 2
                kc = jnp.concatenate([kpos[pl.ds(pos0, n_rows), g * LANES:(g + 1) * LANES],
                                      kbias[pl.ds(pos0, n_rows), :]], axis=1)
                s = _dot(kc, qcat[h])
                if visible is not None:
                    s = jnp.where(visible, s, -jnp.inf)
                sbuf[h, 0:n_rows, :] = s
                m_blk.append(jnp.max(s, axis=0, keepdims=True))
            for h in range(N_HEADS):
                slab = slice(h * V_SLAB, (h + 1) * V_SLAB)
                m_old = m_s[h:h + 1, :]
                m_new = jnp.maximum(m_old, m_blk[h])
                alpha = jnp.exp2(m_old - m_new)
                p = jnp.exp2(sbuf[h, 0:n_rows, :] - m_new)
                m_s[h:h + 1, :] = m_new
                acc_s[slab, :] = alpha * acc_s[slab, :] + pv(slab, p.astype(BF16))

        def full_block(j, _):
            key_block(pl.multiple_of(j * blk, blk), blk,
                      lambda slab, p: _dot(vb[j, slab, :], p), None)
            return 0

        def own_pv(slab, p):
            spill = jnp.concatenate([p[blk:], jnp.zeros((LANES - N_META, blk), BF16)], axis=0)
            return _dot(vb[t, slab, :], p[0:blk]) + _dot(vb[t + 1, slab, 0:LANES], spill)

        key_block(tok0, blk + N_META, own_pv, krow <= qcol + N_META)
        lax.fori_loop(t - n_needed, t, full_block, 0)

        o_t = []
        for h in range(N_HEADS):
            norm = acc_s[h * V_SLAB + HEAD_DIM:h * V_SLAB + HEAD_DIM + 1, :]
            o_t.append(acc_s[h * V_SLAB:h * V_SLAB + HEAD_DIM, :] * (1.0 / norm))
        o_ref[pl.ds(tok0, blk), :] = jnp.concatenate(o_t, axis=0).T.astype(BF16)
        return 0

    lax.fori_loop(0, n_blk, q_block, 0)


def _sample_attn_kernel(q_ref, kn_ref, vn_ref, lf_ref, lft_ref, ckt_ref, cvt_ref, clft_ref,
                        o_ref, crow, *, past, dec):
    blk = ATTN_BLOCK
    n_keys = past + LANES

    upper, _ = _triangles(blk)
    off = jnp.zeros((N_HEADS, 1), F32)
    for j in range(past // blk):
        loc = _cumsum_lanes(clft_ref[:, j * blk:(j + 1) * blk], upper) + off
        crow[:, j * blk:(j + 1) * blk] = loc
        off = loc[:, blk - 1:blk]
    crow[:, 0:past] = crow[:, 0:past] - off

    cq_c = _cumsum_few(lf_ref[:, 0:N_HEADS], axis=0)
    cq_r = _cumsum_few(lft_ref[...], axis=1)
    crow[:, past:] = jnp.full((N_HEADS, LANES), MASKED_BIAS, F32)
    crow[:, past:past + dec] = cq_r

    q = q_ref[...]
    lane_head = lax.broadcasted_iota(jnp.int32, (dec, D_ATTN), 1) // HEAD_DIM
    q_exp = jnp.concatenate(
        [jnp.where(lane_head == h, q, jnp.zeros_like(q)) for h in range(N_HEADS)], axis=0)
    pad_rows = jnp.zeros((LANES - dec, D_ATTN), BF16)
    k_new = jnp.concatenate([kn_ref[...].astype(BF16), pad_rows], axis=0)
    v_new = jnp.concatenate([vn_ref[...].astype(BF16), pad_rows], axis=0)
    s_all = jnp.concatenate(
        [_dot(q_exp, ckt_ref[...].astype(BF16)), _dot_nt(q_exp, k_new)], axis=1)

    kpos = lax.broadcasted_iota(jnp.int32, (dec, n_keys), 1)
    qpos = past + lax.broadcasted_iota(jnp.int32, (dec, n_keys), 0)
    visible = kpos <= qpos
    probs = []
    norms = []
    for h in range(N_HEADS):
        s = s_all[h * dec:(h + 1) * dec, :] + cq_c[:, h:h + 1] - crow[h:h + 1, :]
        s = jnp.where(visible, s, -jnp.inf)
        p = jnp.exp(s - jnp.max(s, axis=-1, keepdims=True))
        norms.append(jnp.sum(p, axis=-1, keepdims=True))
        probs.append(p.astype(BF16))
    p_all = jnp.concatenate(probs, axis=0)
    o_all = (_dot_nt(p_all[:, 0:past], cvt_ref[...].astype(BF16))
             + _dot(p_all[:, past:], v_new))
    out = jnp.zeros((dec, D_ATTN), F32)
    for h in range(N_HEADS):
        o = o_all[h * dec:(h + 1) * dec, :] / norms[h]
        out = out + jnp.where(lane_head == h, o, 0.0)
    o_ref[...] = out.astype(BF16)


def _attn_kernel(*refs, seq, past, dec):
    n_p, n_s = _N_PROMPT_IN, _N_SAMPLE_IN
    p_in, s_in = refs[0:n_p], refs[n_p:n_p + n_s]
    o_ref, os_ref = refs[n_p + n_s:n_p + n_s + 2]
    scratch = refs[n_p + n_s + 2:]
    _sample_attn_kernel(*s_in, os_ref, scratch[-1], past=past, dec=dec)
    _prompt_attn_kernel(*p_in, o_ref, *scratch[:-1], seq=seq)


_N_PROMPT_IN = 9
_N_SAMPLE_IN = 8


def _attention(prompt_in, sample_in):
    qt, kt, vt, lft, lft_meta, lfp, lfp_meta, q_gain, k_gain = prompt_in
    q_s, _, _, _, _, cache_kt, _, _ = sample_in
    b, _, seq = qt.shape
    db, dec, _ = q_s.shape
    assert b == db, "one new stream and one running stream per grid step"
    length = vt.shape[2]
    past = cache_kt.shape[2]
    blk = ATTN_BLOCK
    n_blk = seq // blk
    n_pos = (n_blk + 1) * blk
    per_b = lambda *shape: pl.BlockSpec((None,) + shape, lambda i: (i,) + (0,) * len(shape))
    in_specs = [per_b(D_ATTN, seq), per_b(D_ATTN, length), per_b(D_ATTN, length),
                per_b(N_HEADS, seq), _const_spec(lft_meta.shape),
                per_b(seq, LANES), _const_spec(lfp_meta.shape),
                _const_spec(q_gain.shape), _const_spec(k_gain.shape),
                per_b(dec, D_ATTN), per_b(dec, D_ATTN), per_b(dec, D_ATTN),
                per_b(dec, LANES), per_b(N_HEADS, dec),
                per_b(D_ATTN, past), per_b(D_ATTN, past), per_b(N_HEADS, past)]
    kern = functools.partial(_attn_kernel, seq=seq, past=past, dec=dec)
    return pl.pallas_call(
        kern, grid=(b,), in_specs=in_specs,
        out_specs=(per_b(seq, D_ATTN), per_b(dec, D_ATTN)),
        out_shape=(jax.ShapeDtypeStruct((b, seq, D_ATTN), BF16),
                   jax.ShapeDtypeStruct((b, dec, D_ATTN), BF16)),
        scratch_shapes=[pltpu.VMEM((n_pos, D_ATTN), BF16),
                        pltpu.VMEM((n_blk + 1, N_HEADS * V_SLAB, blk), BF16),
                        pltpu.VMEM((n_pos, LANES), BF16),
                        pltpu.VMEM((n_blk, N_HEADS, blk), F32),
                        pltpu.VMEM((N_HEADS, LANES), F32),
                        pltpu.VMEM((N_HEADS, 2 * LANES, blk), BF16),
                        pltpu.VMEM((N_HEADS, blk), F32),
                        pltpu.VMEM((N_HEADS * V_SLAB, blk), F32),
                        pltpu.VMEM((N_HEADS, blk + N_META, blk), F32),
                        pltpu.VMEM((D_ATTN, LANES), F32),
                        pltpu.VMEM((N_HEADS, past + LANES), F32)],
        compiler_params=_params(1), name="attention",
    )(*prompt_in, *sample_in)


def _merge_mlp_tail(x, conv_bf16, attn_ref, g_conv, g_attn, wbc_ref, wba_ref, wo_ref, g2_ref,
                    wup_ref, wdn_ref, y_ref):
    merged = g_conv * _dot(conv_bf16, wbc_ref[...]) + g_attn * _dot(attn_ref[...], wba_ref[...])
    h = x + _dot(merged.astype(BF16), wo_ref[...])
    hn = _rms_rows(h, g2_ref[...])
    acc = h
    for c in range(D_FF // D_MODEL):
        cols = slice(c * D_MODEL, (c + 1) * D_MODEL)
        a = jnp.maximum(_dot(hn, wup_ref[:, cols]), 0.0)
        acc = acc + _dot((a * a).astype(BF16), wdn_ref[cols, :])
    y_ref[...] = acc


def _load_bf16_weights(pairs, stage, sem):
    rows, cols = stage.shape[1:]
    chunks = []
    for src, dst in pairs:
        for r in range(0, src.shape[0], rows):
            for c in range(0, src.shape[1], cols):
                window = (slice(r, r + rows), slice(c, c + cols))
                chunks.append((src.at[window], dst.at[window]))
    copies = [pltpu.make_async_copy(src, stage.at[i % 2], sem.at[i % 2])
              for i, (src, _) in enumerate(chunks)]
    copies[0].start()
    for i, (_, dst) in enumerate(chunks):
        if i + 1 < len(chunks):
            copies[i + 1].start()
        copies[i].wait()
        dst[...] = stage[i % 2].astype(BF16)


def _branch_mlp_kernel(x_ref, left_ref, attn_ref, xs_ref, convs_ref, attns_ref, gates_ref,
                       g1_ref, wa_ref, wgl_ref, cw_ref, cb_ref, g2_ref,
                       wbc_hbm, wba_hbm, wo_hbm, wup_hbm, wdn_hbm,
                       y_ref, zlast_ref, ys_ref,
                       zbuf, wbc_ref, wba_ref, wo_ref, wup_ref, wdn_ref, stage, sem,
                       *, n_tiles, tiles_per_seq):
    step = pl.program_id(0)

    @pl.when(step == 0)
    def _():
        _load_bf16_weights([(wbc_hbm, wbc_ref), (wba_hbm, wba_ref), (wo_hbm, wo_ref),
                            (wup_hbm, wup_ref), (wdn_hbm, wdn_ref)], stage, sem)

    @pl.when(jnp.logical_and(step % tiles_per_seq == 0, step < n_tiles))
    def _():
        zbuf[:, ZPAD - CONV_HALO:ZPAD, :] = left_ref[...]

    @pl.when(step < n_tiles)
    def _():
        x = x_ref[...]
        xn = _rms_rows(x, g1_ref[...])
        conv = _short_conv(xn, wa_ref, cw_ref, cb_ref, zbuf, zlast_ref, 1, x.shape[0])[0]
        g_conv = jax.nn.sigmoid(_dot_nt(xn, wgl_ref[0:D_MODEL, :]))
        g_attn = jax.nn.sigmoid(_dot_nt(xn, wgl_ref[D_MODEL:2 * D_MODEL, :]))
        _merge_mlp_tail(x, conv.astype(BF16), attn_ref, g_conv, g_attn, wbc_ref, wba_ref,
                        wo_ref, g2_ref, wup_ref, wdn_ref, y_ref)

    @pl.when(step == n_tiles)
    def _():
        _merge_mlp_tail(xs_ref[...], convs_ref[...], attns_ref,
                        gates_ref[:, 0:D_MODEL].astype(F32),
                        gates_ref[:, D_MODEL:2 * D_MODEL].astype(F32),
                        wbc_ref, wba_ref, wo_ref, g2_ref, wup_ref, wdn_ref, ys_ref)


def _branch_mlp(x2d, left, attn, small, wts, f32_wts, *, rows, seq):
    n_rows = x2d.shape[0]
    tiles = seq // rows
    n_tiles = n_rows // rows
    last = n_tiles - 1
    row_spec = lambda width: pl.BlockSpec((rows, width), lambda i: (jnp.minimum(i, last), 0))
    seq_spec = pl.BlockSpec((1, CONV_HALO, D_CONV),
                            lambda i: (jnp.minimum(i, last) // tiles, 0, 0))
    full = lambda a: pl.BlockSpec(a.shape, lambda i: (0,) * a.ndim)
    kern = functools.partial(_branch_mlp_kernel, n_tiles=n_tiles, tiles_per_seq=tiles)
    hbm = pl.BlockSpec(memory_space=pl.ANY)
    return pl.pallas_call(
        kern, grid=(n_tiles + 1,),
        in_specs=[row_spec(D_MODEL), seq_spec, row_spec(D_ATTN)] + [full(a) for a in small]
        + [_const_spec(w.shape) for w in wts] + [hbm] * len(f32_wts),
        out_specs=(row_spec(D_MODEL), seq_spec, full(small[0])),
        out_shape=(jax.ShapeDtypeStruct((n_rows, D_MODEL), F32),
                   jax.ShapeDtypeStruct((n_rows // seq, CONV_HALO, D_CONV), F32),
                   jax.ShapeDtypeStruct(small[0].shape, F32)),
        scratch_shapes=[pltpu.VMEM((1, rows + ZPAD, D_CONV), F32)]
        + [pltpu.VMEM(w.shape, BF16) for w in f32_wts]
        + [pltpu.VMEM((2,) + CAST_CHUNK, F32), pltpu.SemaphoreType.DMA((2,))],
        compiler_params=_params(1), name="branch_mlp",
    )(x2d, left, attn, *small, *wts, *f32_wts)


def kernel(x_prompt, x_sample, cache_k, cache_v, cache_logf, state_conv, meta,
           norm1_g, w_in, b_f, conv_w, conv_b, q_norm_g, k_norm_g,
           w_br_conv, w_br_attn, w_out, norm2_g, w_up, w_down):
    b, seq, _ = x_prompt.shape
    db, dec, _ = x_sample.shape
    past = cache_k.shape[2]
    length = N_META + seq
    n_main = 3 * D_CONV + 3 * D_ATTN
    q0 = 3 * D_CONV

    wt = w_in[0].T
    wt_conv = wt[0:q0].astype(BF16)
    wt_qkv = wt[q0:n_main].astype(BF16)
    wt_fl = wt[n_main:n_main + N_HEADS].astype(BF16)
    wt_gate = wt[n_main + N_HEADS:].astype(BF16)
    head_of = jnp.arange(D_ATTN) // HEAD_DIM
    qg = jnp.tile(q_norm_g[0], N_HEADS)
    kg = jnp.tile(k_norm_g[0], N_HEADS)
    g1 = norm1_g[0][None, :]
    qkv_wts = (g1, wt_qkv, jnp.tile(wt_fl, (LANES // N_HEADS, 1)),
               jnp.tile(b_f[0], LANES // N_HEADS)[None, :], qg[:, None], kg[:, None])
    conv_wts = (conv_w[0], conv_b[0][None, :])
    mlp_wts = (w_br_conv[0], w_br_attn[0], w_out[0], w_up[0], w_down[0])

    x_small = jnp.concatenate([meta, x_sample.reshape(db * dec, D_MODEL)], axis=0)
    left_small = jnp.concatenate(
        [jnp.zeros((1, CONV_HALO, D_CONV), F32), state_conv[0]], axis=0)
    (conv_s, q_s, k_s, v_s, kt_s, vt_s, lft_s, lfp_s, gate_s, zlast_s) = _project_small(
        x_small, left_small,
        qkv_wts + (wt_conv, wt_gate) + conv_wts
        + (wt_fl, b_f[0][:, None], qg[None, :], kg[None, :],
           (head_of[:, None] == head_of[None, :]).astype(BF16)))

    x_rows = x_prompt.reshape(b * seq, D_MODEL)
    (qt_p, kt_p, vt_p, lftp_p, lft_p, lfp_p) = _project_qkv(
        x_rows, qkv_wts, (kt_s, vt_s, lft_s), b=b, seq=seq)
    k_new = k_s[N_META:].reshape(db, dec, D_ATTN)
    v_new = v_s[N_META:].reshape(db, dec, D_ATTN)
    lf_new = lfp_s[N_META:, :N_HEADS].reshape(db, dec, N_HEADS)
    cache_kt = jnp.transpose(cache_k[0], (0, 2, 3, 1)).reshape(db, D_ATTN, past)
    cache_vt = jnp.transpose(cache_v[0], (0, 2, 3, 1)).reshape(db, D_ATTN, past)
    attn_p, attn_s = _attention(
        (qt_p, kt_p, vt_p, lft_p, lft_s[:, :N_META], lfp_p.reshape(b, seq, LANES),
         lfp_s[:N_META], q_norm_g, k_norm_g),
        (q_s[N_META:].reshape(db, dec, D_ATTN), k_new, v_new,
         lfp_s[N_META:].reshape(db, dec, LANES), jnp.swapaxes(lf_new, 1, 2),
         cache_kt, cache_vt, jnp.swapaxes(cache_logf[0], 1, 2)))

    left_p = jnp.broadcast_to(zlast_s[0:1], (b, CONV_HALO, D_CONV))
    y_prompt, zlast_p, y_sample = _branch_mlp(
        x_rows, left_p, attn_p.reshape(b * seq, D_ATTN),
        (x_sample.reshape(db * dec, D_MODEL), conv_s[N_META:],
         attn_s.reshape(db * dec, D_ATTN), gate_s[N_META:]),
        (g1, wt_conv, wt_gate) + conv_wts + (norm2_g[0][None, :],), mlp_wts,
        rows=MLP_ROWS, seq=seq)

    def heads_last(t):
        return jnp.transpose(t.reshape(b, N_HEADS, HEAD_DIM, length), (0, 3, 1, 2))[None]

    return (y_prompt.reshape(b, seq, D_MODEL),
            y_sample.reshape(db, dec, D_MODEL),
            heads_last(kt_p),
            heads_last(vt_p),
            jnp.swapaxes(lftp_p, 1, 2)[None],
            zlast_p[None],
            k_new.reshape(1, db, dec, N_HEADS, HEAD_DIM),
            v_new.reshape(1, db, dec, N_HEADS, HEAD_DIM),
            lf_new[None],
            zlast_s[1:][None])
```

```python
import functools

import jax
import jax.numpy as jnp
from jax import lax
from jax.experimental import pallas as pl
from jax.experimental.pallas import tpu as pltpu

D_MODEL = 1024
D_CONV = D_MODEL // 2
CONV_W = 3
N_HEADS = 8
HEAD_DIM = 64
D_ATTN = N_HEADS * HEAD_DIM
D_FF = 4 * D_MODEL
N_META = 16
EPS = 1e-6
ATTN_SCALE = HEAD_DIM ** -0.5

F32 = jnp.float32
BF16 = jnp.bfloat16

VMEM_LIMIT_BYTES = 56 * 1024 * 1024
LANES = 128
SUBLANES = 8
BF16_ROWS = 16
PROJ_ROWS = 1024
MLP_ROWS = 512
ATTN_BLOCK = 256
MASKED_BIAS = 1e30
CONV_HALO = CONV_W - 1
ZPAD = SUBLANES
N_SPLIT = 3
V_SLAB = HEAD_DIM + BF16_ROWS
LOG2E = 1.4426950408889634
SKIP_LOG2 = 40.0
NORM_SLACK = 1.02


def _dot(a, b):
    return jnp.dot(a, b, preferred_element_type=F32)


def _dot_nt(a, b):
    return lax.dot_general(a, b, (((1,), (1,)), ((), ())), preferred_element_type=F32)


def _log_sigmoid(x):
    return jnp.minimum(x, 0.0) - jnp.log1p(jnp.exp(-jnp.abs(x)))


def _cumsum_few(x, axis):
    n = x.shape[axis]
    idx = lax.broadcasted_iota(jnp.int32, x.shape, axis)
    out = jnp.zeros(x.shape, F32)
    for i in range(n):
        term = x[i:i + 1, :] if axis == 0 else x[:, i:i + 1]
        out = out + jnp.where(idx >= i, term, 0.0)
    return out


def _triangles(n):
    r = lax.broadcasted_iota(jnp.int32, (n, n), 0)
    c = lax.broadcasted_iota(jnp.int32, (n, n), 1)
    upper = jnp.where(r <= c, 1.0, 0.0).astype(BF16)
    lower = jnp.where(r >= c, 1.0, 0.0).astype(BF16)
    return upper, lower


def _split3(c):
    hi = c.astype(BF16).astype(F32)
    r1 = c - hi
    mid = r1.astype(BF16).astype(F32)
    return hi, mid, r1 - mid


def _cumsum_rows(tri_lower, x):
    w = x.shape[1]
    pieces = jnp.concatenate(_split3(x), axis=1).astype(BF16)
    y = _dot(tri_lower, pieces)
    return y[:, 0:w] + y[:, w:2 * w] + y[:, 2 * w:3 * w]


def _cumsum_lanes(x, tri_upper):
    h = x.shape[0]
    pieces = jnp.concatenate(_split3(x), axis=0).astype(BF16)
    y = _dot(pieces, tri_upper)
    return y[0:h] + y[h:2 * h] + y[2 * h:3 * h]


def _const_spec(shape):
    nd = len(shape)
    return pl.BlockSpec(shape, lambda *_: (0,) * nd, pipeline_mode=pl.Buffered(1))


def _params(n_axes):
    return pltpu.CompilerParams(
        dimension_semantics=("arbitrary",) * n_axes,
        vmem_limit_bytes=VMEM_LIMIT_BYTES)


def _rms_rows(x, g_row):
    ms = jnp.mean(x * x, axis=-1, keepdims=True)
    return (x * lax.rsqrt(ms + EPS) * g_row).astype(BF16)


def _head_norm_t(ut, g_col):
    out = []
    for h in range(N_HEADS):
        blk = ut[h * HEAD_DIM:(h + 1) * HEAD_DIM, :]
        ms = jnp.mean(blk * blk, axis=0, keepdims=True)
        out.append(blk * lax.rsqrt(ms + EPS) * g_col[h * HEAD_DIM:(h + 1) * HEAD_DIM, :])
    return jnp.concatenate(out, axis=0)


def _short_conv(xn, wa_ref, cw_ref, cb_ref, zbuf, zlast_ref, n_seg, seg_len):
    cb = _dot_nt(xn, wa_ref[0:D_CONV, :])
    z = (_dot_nt(xn, wa_ref[D_CONV:2 * D_CONV, :])
         * _dot_nt(xn, wa_ref[2 * D_CONV:3 * D_CONV, :]))
    out = []
    for s in range(n_seg):
        r0 = s * seg_len
        zs = z[r0:r0 + seg_len]
        zbuf[s, ZPAD:ZPAD + seg_len, :] = zs
        y = None
        for i in range(CONV_W):
            lo = ZPAD - CONV_HALO + i
            tap = zs if i == CONV_HALO else zbuf[s, lo:lo + seg_len, :]
            term = tap * cw_ref[i:i + 1, :]
            y = term if y is None else y + term
        out.append(cb[r0:r0 + seg_len] * (y + cb_ref[...]))
        tail = zbuf[s, ZPAD + seg_len - CONV_HALO:ZPAD + seg_len, :]
        zlast_ref[s] = tail
        zbuf[s, ZPAD - CONV_HALO:ZPAD, :] = tail
    return out


def _qkv_kernel(x_ref, g1_ref, wqkv_ref, wfl_ref, bfr_ref, qgc_ref, kgc_ref,
                ktm_ref, vtm_ref, lftm_ref,
                qt_ref, kt_ref, vt_ref, lftp_ref, lft_ref, lfp_ref, kb_ref, vtb_ref,
                kcar, vcar, lcar, *, tiles_per_seq):
    rows = x_ref.shape[0]
    step = pl.program_id(0) % tiles_per_seq

    def tile_body(tile_idx):
        lane0 = tile_idx * rows

        def shifted_store(out_ref, car_ref, meta_ref, tile):
            left = meta_ref[:, 0:LANES] if tile_idx == 0 else car_ref[...]
            rolled = pltpu.roll(tile, N_META, axis=1)
            lane = lax.broadcasted_iota(jnp.int32, (tile.shape[0], LANES), 1)
            out_ref[:, lane0:lane0 + LANES] = jnp.where(lane < N_META, left, rolled[:, 0:LANES])
            out_ref[:, lane0 + LANES:lane0 + rows] = rolled[:, LANES:]
            if tile_idx < tiles_per_seq - 1:
                car_ref[...] = rolled[:, 0:LANES]
            else:
                out_ref[:, tiles_per_seq * rows:] = rolled[:, 0:N_META]

        xn = _rms_rows(x_ref[...], g1_ref[...])

        def feature_major(j):
            return _dot_nt(wqkv_ref[j * D_ATTN:(j + 1) * D_ATTN, :], xn)

        lfp = _log_sigmoid(_dot_nt(xn, wfl_ref[...]) + bfr_ref[...])
        lfp_ref[...] = lfp
        lft = lfp.T[0:N_HEADS, :]
        lft_ref[...] = lft
        qt = _head_norm_t(feature_major(0), qgc_ref[...])
        qt_ref[...] = (qt * (ATTN_SCALE * LOG2E)).astype(BF16)
        kt = _head_norm_t(feature_major(1), kgc_ref[...])
        vt = feature_major(2)
        kb_ref[...] = kt.T.astype(BF16)
        vtb_ref[...] = vt.astype(BF16)
        shifted_store(kt_ref, kcar, ktm_ref, kt)
        shifted_store(vt_ref, vcar, vtm_ref, vt)
        shifted_store(lftp_ref, lcar, lftm_ref, lft)

    for tile_idx in range(tiles_per_seq):
        pl.when(step == tile_idx)(functools.partial(tile_body, tile_idx))


def _project_qkv(x2d, wts, meta_cols, *, b, seq):
    rows = PROJ_ROWS
    tiles = seq // rows
    length = N_META + seq
    row_spec = lambda width: pl.BlockSpec((rows, width), lambda i: (i, 0))
    col_spec = lambda feat: pl.BlockSpec(
        (None, feat, rows), lambda i: (i // tiles, 0, i % tiles))
    seq_spec = lambda feat: pl.BlockSpec((None, feat, length), lambda i: (i // tiles, 0, 0))
    out_shape = (
        jax.ShapeDtypeStruct((b, D_ATTN, seq), BF16),
        jax.ShapeDtypeStruct((b, D_ATTN, length), F32),
        jax.ShapeDtypeStruct((b, D_ATTN, length), F32),
        jax.ShapeDtypeStruct((b, N_HEADS, length), F32),
        jax.ShapeDtypeStruct((b, N_HEADS, seq), F32),
        jax.ShapeDtypeStruct((b * seq, LANES), F32),
        jax.ShapeDtypeStruct((b * seq, D_ATTN), BF16),
        jax.ShapeDtypeStruct((b, D_ATTN, seq), BF16),
    )
    out_specs = (col_spec(D_ATTN), seq_spec(D_ATTN), seq_spec(D_ATTN), seq_spec(N_HEADS),
                 col_spec(N_HEADS), row_spec(LANES), row_spec(D_ATTN), col_spec(D_ATTN))
    kern = functools.partial(_qkv_kernel, tiles_per_seq=tiles)
    return pl.pallas_call(
        kern, grid=(b * tiles,),
        in_specs=[row_spec(D_MODEL)] + [_const_spec(w.shape) for w in wts + meta_cols],
        out_specs=out_specs, out_shape=out_shape,
        scratch_shapes=[pltpu.VMEM((D_ATTN, LANES), F32),
                        pltpu.VMEM((D_ATTN, LANES), F32),
                        pltpu.VMEM((N_HEADS, LANES), F32)],
        compiler_params=_params(1), name="proj_qkv",
    )(x2d, *wts, *meta_cols)


def _proj_small_kernel(x_ref, left_ref, g1_ref, wqkv_ref, wfl_ref, bfr_ref, qgc_ref, kgc_ref,
                       wa_ref, wgl_ref, cw_ref, cb_ref, wflt_ref, bfc_ref, qgr_ref, kgr_ref,
                       bd_ref,
                       conv_ref, q_ref, k_ref, v_ref, kt_ref, vt_ref, lft_ref, lfp_ref,
                       gate_ref, zlast_ref, zbuf, *, n_seg, seg_len):
    xn = _rms_rows(x_ref[...], g1_ref[...])
    zbuf[:, ZPAD - CONV_HALO:ZPAD, :] = left_ref[...]
    conv = _short_conv(xn, wa_ref, cw_ref, cb_ref, zbuf, zlast_ref, n_seg, seg_len)
    for s in range(n_seg):
        conv_ref[s * seg_len:(s + 1) * seg_len, :] = conv[s].astype(BF16)

    def rows_major(j):
        return _dot_nt(xn, wqkv_ref[j * D_ATTN:(j + 1) * D_ATTN, :])

    def feature_major(j):
        return _dot_nt(wqkv_ref[j * D_ATTN:(j + 1) * D_ATTN, :], xn)

    def head_norm(u, g_row):
        ssq = _dot((u * u).astype(BF16), bd_ref[...])
        return u * lax.rsqrt(ssq * (1.0 / HEAD_DIM) + EPS) * g_row

    q_ref[...] = (head_norm(rows_major(0), qgr_ref[...]) * ATTN_SCALE).astype(BF16)
    k_ref[...] = head_norm(rows_major(1), kgr_ref[...])
    v_ref[...] = rows_major(2)
    kt_ref[...] = _head_norm_t(feature_major(1), kgc_ref[...])
    vt_ref[...] = feature_major(2)
    lfp_ref[...] = _log_sigmoid(_dot_nt(xn, wfl_ref[...]) + bfr_ref[...])
    lft_ref[...] = _log_sigmoid(_dot_nt(wflt_ref[...], xn) + bfc_ref[...])
    gate_ref[...] = jax.nn.sigmoid(_dot_nt(xn, wgl_ref[...])).astype(BF16)


def _project_small(x2d, left, wts):
    n_rows = x2d.shape[0]
    n_seq = left.shape[0]
    seg_len = n_rows // n_seq
    full = lambda *shape: pl.BlockSpec(shape, lambda i: (0,) * len(shape))
    out_shape = (
        jax.ShapeDtypeStruct((n_rows, D_CONV), BF16),
        jax.ShapeDtypeStruct((n_rows, D_ATTN), BF16),
        jax.ShapeDtypeStruct((n_rows, D_ATTN), F32),
        jax.ShapeDtypeStruct((n_rows, D_ATTN), F32),
        jax.ShapeDtypeStruct((D_ATTN, n_rows), F32),
        jax.ShapeDtypeStruct((D_ATTN, n_rows), F32),
        jax.ShapeDtypeStruct((N_HEADS, n_rows), F32),
        jax.ShapeDtypeStruct((n_rows, LANES), F32),
        jax.ShapeDtypeStruct((n_rows, 2 * D_MODEL), BF16),
        jax.ShapeDtypeStruct((n_seq, CONV_HALO, D_CONV), F32),
    )
    kern = functools.partial(_proj_small_kernel, n_seg=n_seq, seg_len=seg_len)
    return pl.pallas_call(
        kern, grid=(1,),
        in_specs=[full(*x2d.shape), full(*left.shape)] + [_const_spec(w.shape) for w in wts],
        out_specs=tuple(full(*s.shape) for s in out_shape), out_shape=out_shape,
        scratch_shapes=[pltpu.VMEM((n_seq, seg_len + ZPAD, D_CONV), F32)],
        compiler_params=_params(1), name="proj_small",
    )(x2d, left, *wts)


def _prompt_attn_kernel(qt_ref, kb_ref, kbm_ref, vtb_ref, vtm_ref, lft_ref, lftm_ref, lfp_ref,
                        lfpm_ref, qg_ref, kg_ref, o_ref, kpos, vb, kbias, crow, cend, qcat,
                        m_s, acc_s, sbuf, *, seq):
    blk = ATTN_BLOCK
    n_blk = seq // blk
    n_bias = N_SPLIT * N_HEADS
    pad = blk - N_META

    kpos[0:pad, :] = jnp.zeros((pad, D_ATTN), BF16)
    kpos[pad:blk, :] = kbm_ref[...]
    kpos[blk:, :] = kb_ref[...]

    ones_row = (lax.broadcasted_iota(jnp.int32, (V_SLAB - HEAD_DIM, blk), 0) == 0).astype(BF16)
    lane_m = lax.broadcasted_iota(jnp.int32, (HEAD_DIM, LANES), 1)
    vb[0] = jnp.zeros((N_HEADS * V_SLAB, blk), BF16)
    for h in range(N_HEADS):
        rows = slice(h * HEAD_DIM, (h + 1) * HEAD_DIM)
        slab = slice(h * V_SLAB, h * V_SLAB + HEAD_DIM)
        meta = jnp.where(lane_m < N_META, vtm_ref[rows, 0:LANES], 0.0)
        vb[0, slab, blk - LANES:] = pltpu.roll(meta, LANES - N_META, axis=1).astype(BF16)
        for j in range(n_blk):
            vb[j + 1, slab, :] = vtb_ref[rows, j * blk:(j + 1) * blk]
        for j in range(n_blk + 1):
            vb[j, h * V_SLAB + HEAD_DIM:(h + 1) * V_SLAB, :] = ones_row

    qk_bound = (NORM_SLACK * HEAD_DIM * ATTN_SCALE * LOG2E
                * jnp.max(jnp.abs(qg_ref[...]), axis=1, keepdims=True)
                * jnp.max(jnp.abs(kg_ref[...]), axis=1, keepdims=True))

    upper, lower = _triangles(blk)

    def store_kbias(rows, c_col):
        hi, mid, lo = _split3(c_col * LOG2E)
        lane = lax.broadcasted_iota(jnp.int32, c_col.shape, 1)
        grp = lane // N_HEADS
        part = jnp.where(grp == 0, hi, jnp.where(grp == 1, mid, lo))
        kbias[rows, :] = jnp.where(lane < n_bias, -part,
                                   jnp.where(lane < 2 * n_bias, 1.0, 0.0)).astype(BF16)

    lane_p = lax.broadcasted_iota(jnp.int32, (pad, LANES), 1)
    kbias[0:pad, :] = jnp.where(lane_p < N_HEADS, -MASKED_BIAS,
                                jnp.where(jnp.logical_and(lane_p >= n_bias, lane_p < 2 * n_bias),
                                          1.0, 0.0)).astype(BF16)
    c_col = _cumsum_few(lfpm_ref[...], axis=0)
    store_kbias(slice(pad, blk), c_col)
    off_c = c_col[N_META - 1:N_META, :]
    for j in range(n_blk):
        c_col = _cumsum_rows(lower, lfp_ref[j * blk:(j + 1) * blk, :]) + off_c
        store_kbias(slice((j + 1) * blk, (j + 2) * blk), c_col)
        off_c = c_col[blk - 1:blk, :]

    meta_r = _cumsum_few(lftm_ref[...], axis=1)
    off_r = meta_r[:, N_META - 1:N_META]
    lane_h = lax.broadcasted_iota(jnp.int32, (N_HEADS, LANES), 1)
    c_end = jnp.where(lane_h == 0, off_r * LOG2E, 0.0)
    for j in range(n_blk):
        c_row = _cumsum_lanes(lft_ref[:, j * blk:(j + 1) * blk], upper) + off_r
        crow[j] = c_row * LOG2E
        off_r = c_row[:, blk - 1:blk]
        c_end = jnp.where(lane_h == j + 1, off_r * LOG2E, c_end)
    cend[...] = c_end

    row128 = lax.broadcasted_iota(jnp.int32, (LANES, blk), 0)
    krow = lax.broadcasted_iota(jnp.int32, (2 * blk, blk), 0)
    qcol = lax.broadcasted_iota(jnp.int32, (2 * blk, blk), 1)

    def q_block(t, _):
        tok0 = pl.multiple_of(t * blk, blk)
        c_q = crow[t]
        hi, mid, lo = _split3(c_q)
        bias_rows = jnp.concatenate(
            [jnp.ones((n_bias, blk), F32), hi, mid, lo,
             jnp.zeros((LANES - 2 * n_bias, blk), F32)], axis=0)
        for h in range(N_HEADS):
            pair = qt_ref[(h // 2) * LANES:(h // 2 + 1) * LANES, pl.ds(tok0, blk)]
            in_head = (row128 // HEAD_DIM) == (h % 2)
            qcat[h, 0:LANES, :] = jnp.where(in_head, pair, jnp.zeros_like(pair))
            qcat[h, LANES:, :] = jnp.where(row128 % N_HEADS == h, bias_rows, 0.0).astype(BF16)
        m_s[...] = jnp.full(m_s.shape, -jnp.inf, F32)
        acc_s[...] = jnp.zeros(acc_s.shape, F32)

        gap = 2.0 * qk_bound + c_q[:, 0:1] - cend[...]
        needed = jnp.logical_and(gap >= -SKIP_LOG2, lane_h < t)
        n_needed = jnp.max(
            jnp.sum(jnp.where(needed, 1.0, 0.0), axis=1, keepdims=True)).astype(jnp.int32)

        def key_pass(row0, n_rows, pv, visible):
            m_blk = []
            for h in range(N_HEADS):
                g = h // 2
                kc = jnp.concatenate([kpos[pl.ds(row0, n_rows), g * LANES:(g + 1) * LANES],
                                      kbias[pl.ds(row0, n_rows), :]], axis=1)
                s = _dot(kc, qcat[h])
                if visible is not None:
                    s = jnp.where(visible, s, -jnp.inf)
                sbuf[h, 0:n_rows, :] = s
                m_blk.append(jnp.max(s, axis=0, keepdims=True))
            for h in range(N_HEADS):
                slab = slice(h * V_SLAB, (h + 1) * V_SLAB)
                m_old = m_s[h:h + 1, :]
                m_new = jnp.maximum(m_old, m_blk[h])
                alpha = jnp.exp2(m_old - m_new)
                p = jnp.exp2(sbuf[h, 0:n_rows, :] - m_new)
                m_s[h:h + 1, :] = m_new
                acc_s[slab, :] = alpha * acc_s[slab, :] + pv(slab, p.astype(BF16))

        key_pass(tok0, 2 * blk,
                 lambda slab, p: (_dot(vb[t, slab, :], p[0:blk])
                                  + _dot(vb[t + 1, slab, :], p[blk:])),
                 krow <= qcol + blk)

        def older_block(j, _):
            key_pass(pl.multiple_of(j * blk, blk), blk,
                     lambda slab, p: _dot(vb[j, slab, :], p), None)
            return 0

        lax.fori_loop(t - n_needed, t, older_block, 0)

        o_t = []
        for h in range(N_HEADS):
            norm = acc_s[h * V_SLAB + HEAD_DIM:h * V_SLAB + HEAD_DIM + 1, :]
            o_t.append(acc_s[h * V_SLAB:h * V_SLAB + HEAD_DIM, :] * (1.0 / norm))
        o_ref[pl.ds(tok0, blk), :] = jnp.concatenate(o_t, axis=0).T.astype(BF16)
        return 0

    lax.fori_loop(0, n_blk, q_block, 0)


def _sample_attn_kernel(q_ref, kn_ref, vn_ref, lf_ref, lft_ref, ckt_ref, cvt_ref, clft_ref,
                        o_ref, crow, *, past, dec):
    blk = ATTN_BLOCK
    n_keys = past + LANES

    upper, _ = _triangles(blk)
    off = jnp.zeros((N_HEADS, 1), F32)
    for j in range(past // blk):
        loc = _cumsum_lanes(clft_ref[:, j * blk:(j + 1) * blk], upper) + off
        crow[:, j * blk:(j + 1) * blk] = loc
        off = loc[:, blk - 1:blk]
    crow[:, 0:past] = crow[:, 0:past] - off

    cq_c = _cumsum_few(lf_ref[:, 0:N_HEADS], axis=0)
    cq_r = _cumsum_few(lft_ref[...], axis=1)
    crow[:, past:] = jnp.full((N_HEADS, LANES), MASKED_BIAS, F32)
    crow[:, past:past + dec] = cq_r

    q = q_ref[...]
    lane_head = lax.broadcasted_iota(jnp.int32, (dec, D_ATTN), 1) // HEAD_DIM
    q_exp = jnp.concatenate(
        [jnp.where(lane_head == h, q, jnp.zeros_like(q)) for h in range(N_HEADS)], axis=0)
    pad_rows = jnp.zeros((LANES - dec, D_ATTN), BF16)
    k_new = jnp.concatenate([kn_ref[...].astype(BF16), pad_rows], axis=0)
    v_new = jnp.concatenate([vn_ref[...].astype(BF16), pad_rows], axis=0)
    s_all = jnp.concatenate(
        [_dot(q_exp, ckt_ref[...].astype(BF16)), _dot_nt(q_exp, k_new)], axis=1)

    kpos = lax.broadcasted_iota(jnp.int32, (dec, n_keys), 1)
    qpos = past + lax.broadcasted_iota(jnp.int32, (dec, n_keys), 0)
    visible = kpos <= qpos
    probs = []
    norms = []
    for h in range(N_HEADS):
        s = s_all[h * dec:(h + 1) * dec, :] + cq_c[:, h:h + 1] - crow[h:h + 1, :]
        s = jnp.where(visible, s, -jnp.inf)
        p = jnp.exp(s - jnp.max(s, axis=-1, keepdims=True))
        norms.append(jnp.sum(p, axis=-1, keepdims=True))
        probs.append(p.astype(BF16))
    p_all = jnp.concatenate(probs, axis=0)
    o_all = (_dot_nt(p_all[:, 0:past], cvt_ref[...].astype(BF16))
             + _dot(p_all[:, past:], v_new))
    out = jnp.zeros((dec, D_ATTN), F32)
    for h in range(N_HEADS):
        o = o_all[h * dec:(h + 1) * dec, :] / norms[h]
        out = out + jnp.where(lane_head == h, o, 0.0)
    o_ref[...] = out.astype(BF16)


def _attn_kernel(*refs, seq, past, dec):
    n_p, n_s = _N_PROMPT_IN, _N_SAMPLE_IN
    p_in, s_in = refs[0:n_p], refs[n_p:n_p + n_s]
    o_ref, os_ref = refs[n_p + n_s:n_p + n_s + 2]
    scratch = refs[n_p + n_s + 2:]
    _sample_attn_kernel(*s_in, os_ref, scratch[-1], past=past, dec=dec)
    _prompt_attn_kernel(*p_in, o_ref, *scratch[:-1], seq=seq)


_N_PROMPT_IN = 11
_N_SAMPLE_IN = 8


def _attention(prompt_in, sample_in):
    qt, kb, kb_meta, vtb, vt_meta, lft, lft_meta, lfp, lfp_meta, q_gain, k_gain = prompt_in
    q_s, _, _, _, _, cache_kt, _, _ = sample_in
    b, _, seq = qt.shape
    db, dec, _ = q_s.shape
    assert b == db, "one new stream and one running stream per grid step"
    past = cache_kt.shape[2]
    blk = ATTN_BLOCK
    n_blk = seq // blk
    n_pos = (n_blk + 1) * blk
    per_b = lambda *shape: pl.BlockSpec((None,) + shape, lambda i: (i,) + (0,) * len(shape))
    in_specs = [per_b(D_ATTN, seq), per_b(seq, D_ATTN), _const_spec(kb_meta.shape),
                per_b(D_ATTN, seq), _const_spec(vt_meta.shape),
                per_b(N_HEADS, seq), _const_spec(lft_meta.shape),
                per_b(seq, LANES), _const_spec(lfp_meta.shape),
                _const_spec(q_gain.shape), _const_spec(k_gain.shape),
                per_b(dec, D_ATTN), per_b(dec, D_ATTN), per_b(dec, D_ATTN),
                per_b(dec, LANES), per_b(N_HEADS, dec),
                per_b(D_ATTN, past), per_b(D_ATTN, past), per_b(N_HEADS, past)]
    kern = functools.partial(_attn_kernel, seq=seq, past=past, dec=dec)
    return pl.pallas_call(
        kern, grid=(b,), in_specs=in_specs,
        out_specs=(per_b(seq, D_ATTN), per_b(dec, D_ATTN)),
        out_shape=(jax.ShapeDtypeStruct((b, seq, D_ATTN), BF16),
                   jax.ShapeDtypeStruct((b, dec, D_ATTN), BF16)),
        scratch_shapes=[pltpu.VMEM((n_pos, D_ATTN), BF16),
                        pltpu.VMEM((n_blk + 1, N_HEADS * V_SLAB, blk), BF16),
                        pltpu.VMEM((n_pos, LANES), BF16),
                        pltpu.VMEM((n_blk, N_HEADS, blk), F32),
                        pltpu.VMEM((N_HEADS, LANES), F32),
                        pltpu.VMEM((N_HEADS, 2 * LANES, blk), BF16),
                        pltpu.VMEM((N_HEADS, blk), F32),
                        pltpu.VMEM((N_HEADS * V_SLAB, blk), F32),
                        pltpu.VMEM((N_HEADS, 2 * blk, blk), F32),
                        pltpu.VMEM((N_HEADS, past + LANES), F32)],
        compiler_params=_params(1), name="attention",
    )(*prompt_in, *sample_in)


def _merge_mlp_tail(x, conv_bf16, attn_ref, g_conv, g_attn, wbc_ref, wba_ref, wo_ref, g2_ref,
                    wup_ref, wdn_ref, y_ref):
    merged = g_conv * _dot(conv_bf16, wbc_ref[...]) + g_attn * _dot(attn_ref[...], wba_ref[...])
    h = x + _dot(merged.astype(BF16), wo_ref[...])
    hn = _rms_rows(h, g2_ref[...])
    acc = h
    for c in range(D_FF // D_MODEL):
        cols = slice(c * D_MODEL, (c + 1) * D_MODEL)
        a = jnp.maximum(_dot(hn, wup_ref[:, cols]), 0.0)
        acc = acc + _dot((a * a).astype(BF16), wdn_ref[cols, :])
    y_ref[...] = acc


def _branch_mlp_kernel(x_ref, left_ref, attn_ref, xs_ref, convs_ref, attns_ref, gates_ref,
                       g1_ref, wa_ref, wgl_ref, cw_ref, cb_ref,
                       wbc_ref, wba_ref, wo_ref, g2_ref, wup_ref, wdn_ref,
                       y_ref, zlast_ref, ys_ref, zbuf, *, n_tiles, tiles_per_seq):
    step = pl.program_id(0)

    @pl.when(jnp.logical_and(step % tiles_per_seq == 0, step < n_tiles))
    def _():
        zbuf[:, ZPAD - CONV_HALO:ZPAD, :] = left_ref[...]

    @pl.when(step < n_tiles)
    def _():
        x = x_ref[...]
        xn = _rms_rows(x, g1_ref[...])
        conv = _short_conv(xn, wa_ref, cw_ref, cb_ref, zbuf, zlast_ref, 1, x.shape[0])[0]
        g_conv = jax.nn.sigmoid(_dot_nt(xn, wgl_ref[0:D_MODEL, :]))
        g_attn = jax.nn.sigmoid(_dot_nt(xn, wgl_ref[D_MODEL:2 * D_MODEL, :]))
        _merge_mlp_tail(x, conv.astype(BF16), attn_ref, g_conv, g_attn, wbc_ref, wba_ref,
                        wo_ref, g2_ref, wup_ref, wdn_ref, y_ref)

    @pl.when(step == n_tiles)
    def _():
        _merge_mlp_tail(xs_ref[...], convs_ref[...], attns_ref,
                        gates_ref[:, 0:D_MODEL].astype(F32),
                        gates_ref[:, D_MODEL:2 * D_MODEL].astype(F32),
                        wbc_ref, wba_ref, wo_ref, g2_ref, wup_ref, wdn_ref, ys_ref)


def _branch_mlp(x2d, left, attn, small, wts, *, rows, seq):
    n_rows = x2d.shape[0]
    tiles = seq // rows
    n_tiles = n_rows // rows
    last = n_tiles - 1
    row_spec = lambda width: pl.BlockSpec((rows, width), lambda i: (jnp.minimum(i, last), 0))
    seq_spec = pl.BlockSpec((1, CONV_HALO, D_CONV),
                            lambda i: (jnp.minimum(i, last) // tiles, 0, 0))
    full = lambda a: pl.BlockSpec(a.shape, lambda i: (0,) * a.ndim)
    kern = functools.partial(_branch_mlp_kernel, n_tiles=n_tiles, tiles_per_seq=tiles)
    return pl.pallas_call(
        kern, grid=(n_tiles + 1,),
        in_specs=[row_spec(D_MODEL), seq_spec, row_spec(D_ATTN)] + [full(a) for a in small]
        + [_const_spec(w.shape) for w in wts],
        out_specs=(row_spec(D_MODEL), seq_spec, full(small[0])),
        out_shape=(jax.ShapeDtypeStruct((n_rows, D_MODEL), F32),
                   jax.ShapeDtypeStruct((n_rows // seq, CONV_HALO, D_CONV), F32),
                   jax.ShapeDtypeStruct(small[0].shape, F32)),
        scratch_shapes=[pltpu.VMEM((1, rows + ZPAD, D_CONV), F32)],
        compiler_params=_params(1), name="branch_mlp",
    )(x2d, left, attn, *small, *wts)


def kernel(x_prompt, x_sample, cache_k, cache_v, cache_logf, state_conv, meta,
           norm1_g, w_in, b_f, conv_w, conv_b, q_norm_g, k_norm_g,
           w_br_conv, w_br_attn, w_out, norm2_g, w_up, w_down):
    b, seq, _ = x_prompt.shape
    db, dec, _ = x_sample.shape
    past = cache_k.shape[2]
    length = N_META + seq
    n_main = 3 * D_CONV + 3 * D_ATTN
    q0 = 3 * D_CONV

    wt = w_in[0].T
    wt_conv = wt[0:q0].astype(BF16)
    wt_qkv = wt[q0:n_main].astype(BF16)
    wt_fl = wt[n_main:n_main + N_HEADS].astype(BF16)
    wt_gate = wt[n_main + N_HEADS:].astype(BF16)
    head_of = jnp.arange(D_ATTN) // HEAD_DIM
    qg = jnp.tile(q_norm_g[0], N_HEADS)
    kg = jnp.tile(k_norm_g[0], N_HEADS)
    g1 = norm1_g[0][None, :]
    qkv_wts = (g1, wt_qkv, jnp.tile(wt_fl, (LANES // N_HEADS, 1)),
               jnp.tile(b_f[0], LANES // N_HEADS)[None, :], qg[:, None], kg[:, None])
    conv_wts = (conv_w[0], conv_b[0][None, :])
    mlp_wts = (w_br_conv[0].astype(BF16), w_br_attn[0].astype(BF16), w_out[0].astype(BF16),
               norm2_g[0][None, :], w_up[0].astype(BF16), w_down[0].astype(BF16))

    x_small = jnp.concatenate([meta, x_sample.reshape(db * dec, D_MODEL)], axis=0)
    left_small = jnp.concatenate(
        [jnp.zeros((1, CONV_HALO, D_CONV), F32), state_conv[0]], axis=0)
    (conv_s, q_s, k_s, v_s, kt_s, vt_s, lft_s, lfp_s, gate_s, zlast_s) = _project_small(
        x_small, left_small,
        qkv_wts + (wt_conv, wt_gate) + conv_wts
        + (wt_fl, b_f[0][:, None], qg[None, :], kg[None, :],
           (head_of[:, None] == head_of[None, :]).astype(BF16)))

    x_rows = x_prompt.reshape(b * seq, D_MODEL)
    (qt_p, kt_p, vt_p, lftp_p, lft_p, lfp_p, kb_p, vtb_p) = _project_qkv(
        x_rows, qkv_wts, (kt_s, vt_s, lft_s), b=b, seq=seq)
    k_new = k_s[N_META:].reshape(db, dec, D_ATTN)
    v_new = v_s[N_META:].reshape(db, dec, D_ATTN)
    lf_new = lfp_s[N_META:, :N_HEADS].reshape(db, dec, N_HEADS)
    cache_kt = jnp.transpose(cache_k[0], (0, 2, 3, 1)).reshape(db, D_ATTN, past)
    cache_vt = jnp.transpose(cache_v[0], (0, 2, 3, 1)).reshape(db, D_ATTN, past)
    attn_p, attn_s = _attention(
        (qt_p, kb_p.reshape(b, seq, D_ATTN), k_s[:N_META].astype(BF16), vtb_p, vt_s,
         lft_p, lft_s[:, :N_META], lfp_p.reshape(b, seq, LANES), lfp_s[:N_META],
         q_norm_g, k_norm_g),
        (q_s[N_META:].reshape(db, dec, D_ATTN), k_new, v_new,
         lfp_s[N_META:].reshape(db, dec, LANES), jnp.swapaxes(lf_new, 1, 2),
         cache_kt, cache_vt, jnp.swapaxes(cache_logf[0], 1, 2)))

    left_p = jnp.broadcast_to(zlast_s[0:1], (b, CONV_HALO, D_CONV))
    y_prompt, zlast_p, y_sample = _branch_mlp(
        x_rows, left_p, attn_p.reshape(b * seq, D_ATTN),
        (x_sample.reshape(db * dec, D_MODEL), conv_s[N_META:],
         attn_s.reshape(db * dec, D_ATTN), gate_s[N_META:]),
        (g1, wt_conv, wt_gate) + conv_wts + mlp_wts, rows=MLP_ROWS, seq=seq)

    def heads_last(t):
        return jnp.transpose(t.reshape(b, N_HEADS, HEAD_DIM, length), (0, 3, 1, 2))[None]

    return (y_prompt.reshape(b, seq, D_MODEL),
            y_sample.reshape(db, dec, D_MODEL),
            heads_last(kt_p),
            heads_last(vt_p),
            jnp.swapaxes(lftp_p, 1, 2)[None],
            zlast_p[None],
            k_new.reshape(1, db, dec, N_HEADS, HEAD_DIM),
            v_new.reshape(1, db, dec, N_HEADS, HEAD_DIM),
            lf_new[None],
            zlast_s[1:][None])
```

```python
import functools

import jax
import jax.numpy as jnp
from jax import lax
from jax.experimental import pallas as pl
from jax.experimental.pallas import tpu as pltpu

D_MODEL = 1024
D_CONV = D_MODEL // 2
CONV_W = 3
N_HEADS = 8
HEAD_DIM = 64
D_ATTN = N_HEADS * HEAD_DIM
D_FF = 4 * D_MODEL
N_META = 16
EPS = 1e-6
ATTN_SCALE = HEAD_DIM ** -0.5

F32 = jnp.float32
BF16 = jnp.bfloat16

VMEM_LIMIT_BYTES = 56 * 1024 * 1024
LANES = 128
SUBLANES = 8
BF16_ROWS = 16
PROJ_ROWS = 1024
MLP_ROWS = 512
ATTN_BLOCK = 256
MASKED_BIAS = 1e30
CONV_HALO = CONV_W - 1
ZPAD = SUBLANES
N_SPLIT = 3
V_SLAB = HEAD_DIM + BF16_ROWS
LOG2E = 1.4426950408889634
SKIP_LOG2 = 40.0
NORM_SLACK = 1.02


def _dot(a, b):
    return jnp.dot(a, b, preferred_element_type=F32)


def _dot_nt(a, b):
    return lax.dot_general(a, b, (((1,), (1,)), ((), ())), preferred_element_type=F32)


def _log_sigmoid(x):
    return jnp.minimum(x, 0.0) - jnp.log1p(jnp.exp(-jnp.abs(x)))


def _cumsum_few(x, axis):
    n = x.shape[axis]
    idx = lax.broadcasted_iota(jnp.int32, x.shape, axis)
    out = jnp.zeros(x.shape, F32)
    for i in range(n):
        term = x[i:i + 1, :] if axis == 0 else x[:, i:i + 1]
        out = out + jnp.where(idx >= i, term, 0.0)
    return out


def _triangles(n):
    r = lax.broadcasted_iota(jnp.int32, (n, n), 0)
    c = lax.broadcasted_iota(jnp.int32, (n, n), 1)
    upper = jnp.where(r <= c, 1.0, 0.0).astype(BF16)
    lower = jnp.where(r >= c, 1.0, 0.0).astype(BF16)
    return upper, lower


def _split3(c):
    hi = c.astype(BF16).astype(F32)
    r1 = c - hi
    mid = r1.astype(BF16).astype(F32)
    return hi, mid, r1 - mid


def _cumsum_rows(tri_lower, x):
    w = x.shape[1]
    pieces = jnp.concatenate(_split3(x), axis=1).astype(BF16)
    y = _dot(tri_lower, pieces)
    return y[:, 0:w] + y[:, w:2 * w] + y[:, 2 * w:3 * w]


def _cumsum_lanes(x, tri_upper):
    h = x.shape[0]
    pieces = jnp.concatenate(_split3(x), axis=0).astype(BF16)
    y = _dot(pieces, tri_upper)
    return y[0:h] + y[h:2 * h] + y[2 * h:3 * h]


def _const_spec(shape):
    nd = len(shape)
    return pl.BlockSpec(shape, lambda *_: (0,) * nd, pipeline_mode=pl.Buffered(1))


def _params(n_axes):
    return pltpu.CompilerParams(
        dimension_semantics=("arbitrary",) * n_axes,
        vmem_limit_bytes=VMEM_LIMIT_BYTES)


def _rms_rows(x, g_row):
    ms = jnp.mean(x * x, axis=-1, keepdims=True)
    return (x * lax.rsqrt(ms + EPS) * g_row).astype(BF16)


def _head_norm_t(ut, g_col):
    out = []
    for h in range(N_HEADS):
        blk = ut[h * HEAD_DIM:(h + 1) * HEAD_DIM, :]
        ms = jnp.mean(blk * blk, axis=0, keepdims=True)
        out.append(blk * lax.rsqrt(ms + EPS) * g_col[h * HEAD_DIM:(h + 1) * HEAD_DIM, :])
    return jnp.concatenate(out, axis=0)


def _short_conv(xn, wa_ref, cw_ref, cb_ref, zbuf, zlast_ref, n_seg, seg_len):
    cb = _dot_nt(xn, wa_ref[0:D_CONV, :])
    z = (_dot_nt(xn, wa_ref[D_CONV:2 * D_CONV, :])
         * _dot_nt(xn, wa_ref[2 * D_CONV:3 * D_CONV, :]))
    out = []
    for s in range(n_seg):
        r0 = s * seg_len
        zs = z[r0:r0 + seg_len]
        zbuf[s, ZPAD:ZPAD + seg_len, :] = zs
        y = None
        for i in range(CONV_W):
            lo = ZPAD - CONV_HALO + i
            tap = zs if i == CONV_HALO else zbuf[s, lo:lo + seg_len, :]
            term = tap * cw_ref[i:i + 1, :]
            y = term if y is None else y + term
        out.append(cb[r0:r0 + seg_len] * (y + cb_ref[...]))
        tail = zbuf[s, ZPAD + seg_len - CONV_HALO:ZPAD + seg_len, :]
        zlast_ref[s] = tail
        zbuf[s, ZPAD - CONV_HALO:ZPAD, :] = tail
    return out


def _qkv_kernel(*refs, tiles_per_seq, n_cast):
    (x_ref, g1_ref, wqkv_ref, wfl_ref, bfr_ref, qgc_ref, kgc_ref,
     ktm_ref, vtm_ref, lftm_ref) = refs[:10]
    cast_in = refs[10:10 + n_cast]
    (qt_ref, kt_ref, vt_ref, lftp_ref, lft_ref, lfp_ref, kb_ref,
     vtb_ref) = refs[10 + n_cast:18 + n_cast]
    cast_out = refs[18 + n_cast:18 + 2 * n_cast]
    kcar, vcar, lcar = refs[18 + 2 * n_cast:]
    rows = x_ref.shape[0]
    step = pl.program_id(0) % tiles_per_seq

    def tile_body(tile_idx):
        lane0 = tile_idx * rows

        def shifted_store(out_ref, car_ref, meta_ref, tile):
            left = meta_ref[:, 0:LANES] if tile_idx == 0 else car_ref[...]
            rolled = pltpu.roll(tile, N_META, axis=1)
            lane = lax.broadcasted_iota(jnp.int32, (tile.shape[0], LANES), 1)
            out_ref[:, lane0:lane0 + LANES] = jnp.where(lane < N_META, left, rolled[:, 0:LANES])
            out_ref[:, lane0 + LANES:lane0 + rows] = rolled[:, LANES:]
            if tile_idx < tiles_per_seq - 1:
                car_ref[...] = rolled[:, 0:LANES]
            else:
                out_ref[:, tiles_per_seq * rows:] = rolled[:, 0:N_META]

        xn = _rms_rows(x_ref[...], g1_ref[...])

        def feature_major(j):
            return _dot_nt(wqkv_ref[j * D_ATTN:(j + 1) * D_ATTN, :], xn)

        lfp = _log_sigmoid(_dot_nt(xn, wfl_ref[...]) + bfr_ref[...])
        lfp_ref[...] = lfp
        lft = lfp.T[0:N_HEADS, :]
        lft_ref[...] = lft
        qt = _head_norm_t(feature_major(0), qgc_ref[...])
        qt_ref[...] = (qt * (ATTN_SCALE * LOG2E)).astype(BF16)
        kt = _head_norm_t(feature_major(1), kgc_ref[...])
        vt = feature_major(2)
        kb_ref[...] = kt.T.astype(BF16)
        vtb_ref[...] = vt.astype(BF16)
        shifted_store(kt_ref, kcar, ktm_ref, kt)
        shifted_store(vt_ref, vcar, vtm_ref, vt)
        shifted_store(lftp_ref, lcar, lftm_ref, lft)
        for src, dst in zip(cast_in, cast_out):
            dst[...] = src[...].astype(BF16)

    for tile_idx in range(tiles_per_seq):
        pl.when(step == tile_idx)(functools.partial(tile_body, tile_idx))


def _project_qkv(x2d, wts, meta_cols, f32_wts, *, b, seq):
    rows = PROJ_ROWS
    tiles = seq // rows
    length = N_META + seq
    n_steps = b * tiles
    chunk_spec = lambda w: pl.BlockSpec((w.shape[0] // n_steps, w.shape[1]), lambda i: (i, 0))
    row_spec = lambda width: pl.BlockSpec((rows, width), lambda i: (i, 0))
    col_spec = lambda feat: pl.BlockSpec(
        (None, feat, rows), lambda i: (i // tiles, 0, i % tiles))
    seq_spec = lambda feat: pl.BlockSpec((None, feat, length), lambda i: (i // tiles, 0, 0))
    out_shape = (
        jax.ShapeDtypeStruct((b, D_ATTN, seq), BF16),
        jax.ShapeDtypeStruct((b, D_ATTN, length), F32),
        jax.ShapeDtypeStruct((b, D_ATTN, length), F32),
        jax.ShapeDtypeStruct((b, N_HEADS, length), F32),
        jax.ShapeDtypeStruct((b, N_HEADS, seq), F32),
        jax.ShapeDtypeStruct((b * seq, LANES), F32),
        jax.ShapeDtypeStruct((b * seq, D_ATTN), BF16),
        jax.ShapeDtypeStruct((b, D_ATTN, seq), BF16),
    )
    out_specs = (col_spec(D_ATTN), seq_spec(D_ATTN), seq_spec(D_ATTN), seq_spec(N_HEADS),
                 col_spec(N_HEADS), row_spec(LANES), row_spec(D_ATTN), col_spec(D_ATTN))
    out_shape = out_shape + tuple(jax.ShapeDtypeStruct(w.shape, BF16) for w in f32_wts)
    out_specs = out_specs + tuple(chunk_spec(w) for w in f32_wts)
    kern = functools.partial(_qkv_kernel, tiles_per_seq=tiles, n_cast=len(f32_wts))
    return pl.pallas_call(
        kern, grid=(n_steps,),
        in_specs=[row_spec(D_MODEL)] + [_const_spec(w.shape) for w in wts + meta_cols]
        + [chunk_spec(w) for w in f32_wts],
        out_specs=out_specs, out_shape=out_shape,
        scratch_shapes=[pltpu.VMEM((D_ATTN, LANES), F32),
                        pltpu.VMEM((D_ATTN, LANES), F32),
                        pltpu.VMEM((N_HEADS, LANES), F32)],
        compiler_params=_params(1), name="proj_qkv",
    )(x2d, *wts, *meta_cols, *f32_wts)


def _proj_small_kernel(x_ref, left_ref, g1_ref, wt_ref, bfr_ref, qgc_ref, kgc_ref,
                       cw_ref, cb_ref, bfc_ref, qgr_ref, kgr_ref, bd_ref,
                       conv_ref, q_ref, k_ref, v_ref, kt_ref, vt_ref, lft_ref, lfp_ref,
                       gate_ref, zlast_ref, wa_ref, wqkv_ref, wgl_ref, wfl_ref,
                       zbuf, *, n_seg, seg_len):
    n_main = 3 * D_CONV + 3 * D_ATTN
    wa_ref[...] = wt_ref[0:3 * D_CONV, :].astype(BF16)
    wqkv_ref[...] = wt_ref[3 * D_CONV:n_main, :].astype(BF16)
    wgl_ref[...] = wt_ref[n_main + N_HEADS:, :].astype(BF16)
    w_fl = wt_ref[n_main:n_main + N_HEADS, :]
    wfl_ref[...] = jnp.concatenate([w_fl] * (LANES // N_HEADS), axis=0).astype(BF16)
    wflt_ref = wfl_ref.at[0:BF16_ROWS]
    xn = _rms_rows(x_ref[...], g1_ref[...])
    zbuf[:, ZPAD - CONV_HALO:ZPAD, :] = left_ref[...]
    conv = _short_conv(xn, wa_ref, cw_ref, cb_ref, zbuf, zlast_ref, n_seg, seg_len)
    for s in range(n_seg):
        conv_ref[s * seg_len:(s + 1) * seg_len, :] = conv[s].astype(BF16)

    def rows_major(j):
        return _dot_nt(xn, wqkv_ref[j * D_ATTN:(j + 1) * D_ATTN, :])

    def feature_major(j):
        return _dot_nt(wqkv_ref[j * D_ATTN:(j + 1) * D_ATTN, :], xn)

    def head_norm(u, g_row):
        ssq = _dot((u * u).astype(BF16), bd_ref[...])
        return u * lax.rsqrt(ssq * (1.0 / HEAD_DIM) + EPS) * g_row

    q_ref[...] = (head_norm(rows_major(0), qgr_ref[...]) * ATTN_SCALE).astype(BF16)
    k_ref[...] = head_norm(rows_major(1), kgr_ref[...])
    v_ref[...] = rows_major(2)
    kt_ref[...] = _head_norm_t(feature_major(1), kgc_ref[...])
    vt_ref[...] = feature_major(2)
    lfp_ref[...] = _log_sigmoid(_dot_nt(xn, wfl_ref[...]) + bfr_ref[...])
    lft_ref[...] = _log_sigmoid(_dot_nt(wflt_ref[...], xn)[0:N_HEADS] + bfc_ref[...])
    gate_ref[...] = jax.nn.sigmoid(_dot_nt(xn, wgl_ref[...])).astype(BF16)


def _project_small(x2d, left, wt, wts):
    n_rows = x2d.shape[0]
    n_seq = left.shape[0]
    seg_len = n_rows // n_seq
    n_main = 3 * D_CONV + 3 * D_ATTN
    full = lambda *shape: pl.BlockSpec(shape, lambda i: (0,) * len(shape))
    out_shape = (
        jax.ShapeDtypeStruct((n_rows, D_CONV), BF16),
        jax.ShapeDtypeStruct((n_rows, D_ATTN), BF16),
        jax.ShapeDtypeStruct((n_rows, D_ATTN), F32),
        jax.ShapeDtypeStruct((n_rows, D_ATTN), F32),
        jax.ShapeDtypeStruct((D_ATTN, n_rows), F32),
        jax.ShapeDtypeStruct((D_ATTN, n_rows), F32),
        jax.ShapeDtypeStruct((N_HEADS, n_rows), F32),
        jax.ShapeDtypeStruct((n_rows, LANES), F32),
        jax.ShapeDtypeStruct((n_rows, 2 * D_MODEL), BF16),
        jax.ShapeDtypeStruct((n_seq, CONV_HALO, D_CONV), F32),
        jax.ShapeDtypeStruct((3 * D_CONV, D_MODEL), BF16),
        jax.ShapeDtypeStruct((3 * D_ATTN, D_MODEL), BF16),
        jax.ShapeDtypeStruct((wt.shape[0] - n_main - N_HEADS, D_MODEL), BF16),
        jax.ShapeDtypeStruct((LANES, D_MODEL), BF16),
    )
    kern = functools.partial(_proj_small_kernel, n_seg=n_seq, seg_len=seg_len)
    return pl.pallas_call(
        kern, grid=(1,),
        in_specs=[full(*x2d.shape), full(*left.shape)]
        + [_const_spec(w.shape) for w in (wts[0], wt) + wts[1:]],
        out_specs=tuple(full(*s.shape) for s in out_shape), out_shape=out_shape,
        scratch_shapes=[pltpu.VMEM((n_seq, seg_len + ZPAD, D_CONV), F32)],
        compiler_params=_params(1), name="proj_small",
    )(x2d, left, wts[0], wt, *wts[1:])


def _prompt_attn_kernel(qt_ref, kb_ref, kbm_ref, vtb_ref, vtm_ref, lft_ref, lftm_ref, lfp_ref,
                        lfpm_ref, qg_ref, kg_ref, o_ref, kpos, vb, kbias, crow, cend, qcat,
                        m_s, acc_s, sbuf, *, seq):
    blk = ATTN_BLOCK
    n_blk = seq // blk
    n_bias = N_SPLIT * N_HEADS
    pad = blk - N_META

    kpos[0:pad, :] = jnp.zeros((pad, D_ATTN), BF16)
    kpos[pad:blk, :] = kbm_ref[...]
    kpos[blk:, :] = kb_ref[...]

    ones_row = (lax.broadcasted_iota(jnp.int32, (V_SLAB - HEAD_DIM, blk), 0) == 0).astype(BF16)
    lane_m = lax.broadcasted_iota(jnp.int32, (HEAD_DIM, LANES), 1)
    vb[0] = jnp.zeros((N_HEADS * V_SLAB, blk), BF16)
    for h in range(N_HEADS):
        rows = slice(h * HEAD_DIM, (h + 1) * HEAD_DIM)
        slab = slice(h * V_SLAB, h * V_SLAB + HEAD_DIM)
        meta = jnp.where(lane_m < N_META, vtm_ref[rows, 0:LANES], 0.0)
        vb[0, slab, blk - LANES:] = pltpu.roll(meta, LANES - N_META, axis=1).astype(BF16)
        for j in range(n_blk):
            vb[j + 1, slab, :] = vtb_ref[rows, j * blk:(j + 1) * blk]
        for j in range(n_blk + 1):
            vb[j, h * V_SLAB + HEAD_DIM:(h + 1) * V_SLAB, :] = ones_row

    qk_bound = (NORM_SLACK * HEAD_DIM * ATTN_SCALE * LOG2E
                * jnp.max(jnp.abs(qg_ref[...]), axis=1, keepdims=True)
                * jnp.max(jnp.abs(kg_ref[...]), axis=1, keepdims=True))

    upper, lower = _triangles(blk)

    def store_kbias(rows, c_col):
        hi, mid, lo = _split3(c_col * LOG2E)
        lane = lax.broadcasted_iota(jnp.int32, c_col.shape, 1)
        grp = lane // N_HEADS
        part = jnp.where(grp == 0, hi, jnp.where(grp == 1, mid, lo))
        kbias[rows, :] = jnp.where(lane < n_bias, -part,
                                   jnp.where(lane < 2 * n_bias, 1.0, 0.0)).astype(BF16)

    lane_p = lax.broadcasted_iota(jnp.int32, (pad, LANES), 1)
    kbias[0:pad, :] = jnp.where(lane_p < N_HEADS, -MASKED_BIAS,
                                jnp.where(jnp.logical_and(lane_p >= n_bias, lane_p < 2 * n_bias),
                                          1.0, 0.0)).astype(BF16)
    c_col = _cumsum_few(lfpm_ref[...], axis=0)
    store_kbias(slice(pad, blk), c_col)
    off_c = c_col[N_META - 1:N_META, :]
    for j in range(n_blk):
        c_col = _cumsum_rows(lower, lfp_ref[j * blk:(j + 1) * blk, :]) + off_c
        store_kbias(slice((j + 1) * blk, (j + 2) * blk), c_col)
        off_c = c_col[blk - 1:blk, :]

    meta_r = _cumsum_few(lftm_ref[...], axis=1)
    off_r = meta_r[:, N_META - 1:N_META]
    lane_h = lax.broadcasted_iota(jnp.int32, (N_HEADS, LANES), 1)
    c_end = jnp.where(lane_h == 0, off_r * LOG2E, 0.0)
    for j in range(n_blk):
        c_row = _cumsum_lanes(lft_ref[:, j * blk:(j + 1) * blk], upper) + off_r
        crow[j] = c_row * LOG2E
        off_r = c_row[:, blk - 1:blk]
        c_end = jnp.where(lane_h == j + 1, off_r * LOG2E, c_end)
    cend[...] = c_end

    row128 = lax.broadcasted_iota(jnp.int32, (LANES, blk), 0)
    krow = lax.broadcasted_iota(jnp.int32, (2 * blk, blk), 0)
    qcol = lax.broadcasted_iota(jnp.int32, (2 * blk, blk), 1)

    def q_block(t, _):
        tok0 = pl.multiple_of(t * blk, blk)
        c_q = crow[t]
        hi, mid, lo = _split3(c_q)
        bias_rows = jnp.concatenate(
            [jnp.ones((n_bias, blk), F32), hi, mid, lo,
             jnp.zeros((LANES - 2 * n_bias, blk), F32)], axis=0)
        for h in range(N_HEADS):
            pair = qt_ref[(h // 2) * LANES:(h // 2 + 1) * LANES, pl.ds(tok0, blk)]
            in_head = (row128 // HEAD_DIM) == (h % 2)
            qcat[h, 0:LANES, :] = jnp.where(in_head, pair, jnp.zeros_like(pair))
            qcat[h, LANES:, :] = jnp.where(row128 % N_HEADS == h, bias_rows, 0.0).astype(BF16)
        m_s[...] = jnp.full(m_s.shape, -jnp.inf, F32)
        acc_s[...] = jnp.zeros(acc_s.shape, F32)

        gap = 2.0 * qk_bound + c_q[:, 0:1] - cend[...]
        needed = jnp.logical_and(gap >= -SKIP_LOG2, lane_h < t)
        n_needed = jnp.max(
            jnp.sum(jnp.where(needed, 1.0, 0.0), axis=1, keepdims=True)).astype(jnp.int32)

        def key_pass(row0, n_rows, pv, visible):
            m_blk = []
            for h in range(N_HEADS):
                g = h // 2
                kc = jnp.concatenate([kpos[pl.ds(row0, n_rows), g * LANES:(g + 1) * LANES],
                                      kbias[pl.ds(row0, n_rows), :]], axis=1)
                s = _dot(kc, qcat[h])
                if visible is not None:
                    s = jnp.where(visible, s, -jnp.inf)
                sbuf[h, 0:n_rows, :] = s
                m_blk.append(jnp.max(s, axis=0, keepdims=True))
            for h in range(N_HEADS):
                slab = slice(h * V_SLAB, (h + 1) * V_SLAB)
                m_old = m_s[h:h + 1, :]
                m_new = jnp.maximum(m_old, m_blk[h])
                alpha = jnp.exp2(m_old - m_new)
                p = jnp.exp2(sbuf[h, 0:n_rows, :] - m_new)
                m_s[h:h + 1, :] = m_new
                acc_s[slab, :] = alpha * acc_s[slab, :] + pv(slab, p.astype(BF16))

        key_pass(tok0, 2 * blk,
                 lambda slab, p: (_dot(vb[t, slab, :], p[0:blk])
                                  + _dot(vb[t + 1, slab, :], p[blk:])),
                 krow <= qcol + blk)

        def older_block(j, _):
            key_pass(pl.multiple_of(j * blk, blk), blk,
                     lambda slab, p: _dot(vb[j, slab, :], p), None)
            return 0

        lax.fori_loop(t - n_needed, t, older_block, 0)

        o_t = []
        for h in range(N_HEADS):
            norm = acc_s[h * V_SLAB + HEAD_DIM:h * V_SLAB + HEAD_DIM + 1, :]
            o_t.append(acc_s[h * V_SLAB:h * V_SLAB + HEAD_DIM, :] * (1.0 / norm))
        o_ref[pl.ds(tok0, blk), :] = jnp.concatenate(o_t, axis=0).T.astype(BF16)
        return 0

    lax.fori_loop(0, n_blk, q_block, 0)


def _sample_attn_kernel(q_ref, kn_ref, vn_ref, lf_ref, lft_ref, ckt_ref, cvt_ref, clft_ref,
                        o_ref, crow, *, past, dec):
    blk = ATTN_BLOCK
    n_keys = past + LANES

    upper, _ = _triangles(blk)
    off = jnp.zeros((N_HEADS, 1), F32)
    for j in range(past // blk):
        loc = _cumsum_lanes(clft_ref[:, j * blk:(j + 1) * blk], upper) + off
        crow[:, j * blk:(j + 1) * blk] = loc
        off = loc[:, blk - 1:blk]
    crow[:, 0:past] = crow[:, 0:past] - off

    cq_c = _cumsum_few(lf_ref[:, 0:N_HEADS], axis=0)
    cq_r = _cumsum_few(lft_ref[...], axis=1)
    crow[:, past:] = jnp.full((N_HEADS, LANES), MASKED_BIAS, F32)
    crow[:, past:past + dec] = cq_r

    q = q_ref[...]
    lane_head = lax.broadcasted_iota(jnp.int32, (dec, D_ATTN), 1) // HEAD_DIM
    q_exp = jnp.concatenate(
        [jnp.where(lane_head == h, q, jnp.zeros_like(q)) for h in range(N_HEADS)], axis=0)
    pad_rows = jnp.zeros((LANES - dec, D_ATTN), BF16)
    k_new = jnp.concatenate([kn_ref[...].astype(BF16), pad_rows], axis=0)
    v_new = jnp.concatenate([vn_ref[...].astype(BF16), pad_rows], axis=0)
    s_all = jnp.concatenate(
        [_dot(q_exp, ckt_ref[...].astype(BF16)), _dot_nt(q_exp, k_new)], axis=1)

    kpos = lax.broadcasted_iota(jnp.int32, (dec, n_keys), 1)
    qpos = past + lax.broadcasted_iota(jnp.int32, (dec, n_keys), 0)
    visible = kpos <= qpos
    probs = []
    norms = []
    for h in range(N_HEADS):
        s = s_all[h * dec:(h + 1) * dec, :] + cq_c[:, h:h + 1] - crow[h:h + 1, :]
        s = jnp.where(visible, s, -jnp.inf)
        p = jnp.exp(s - jnp.max(s, axis=-1, keepdims=True))
        norms.append(jnp.sum(p, axis=-1, keepdims=True))
        probs.append(p.astype(BF16))
    p_all = jnp.concatenate(probs, axis=0)
    o_all = (_dot_nt(p_all[:, 0:past], cvt_ref[...].astype(BF16))
             + _dot(p_all[:, past:], v_new))
    out = jnp.zeros((dec, D_ATTN), F32)
    for h in range(N_HEADS):
        o = o_all[h * dec:(h + 1) * dec, :] / norms[h]
        out = out + jnp.where(lane_head == h, o, 0.0)
    o_ref[...] = out.astype(BF16)


def _attn_kernel(*refs, seq, past, dec):
    n_p, n_s = _N_PROMPT_IN, _N_SAMPLE_IN
    p_in, s_in = refs[0:n_p], refs[n_p:n_p + n_s]
    o_ref, os_ref = refs[n_p + n_s:n_p + n_s + 2]
    scratch = refs[n_p + n_s + 2:]
    _sample_attn_kernel(*s_in, os_ref, scratch[-1], past=past, dec=dec)
    _prompt_attn_kernel(*p_in, o_ref, *scratch[:-1], seq=seq)


_N_PROMPT_IN = 11
_N_SAMPLE_IN = 8


def _attention(prompt_in, sample_in):
    qt, kb, kb_meta, vtb, vt_meta, lft, lft_meta, lfp, lfp_meta, q_gain, k_gain = prompt_in
    q_s, _, _, _, _, cache_kt, _, _ = sample_in
    b, _, seq = qt.shape
    db, dec, _ = q_s.shape
    assert b == db, "one new stream and one running stream per grid step"
    past = cache_kt.shape[2]
    blk = ATTN_BLOCK
    n_blk = seq // blk
    n_pos = (n_blk + 1) * blk
    per_b = lambda *shape: pl.BlockSpec((None,) + shape, lambda i: (i,) + (0,) * len(shape))
    in_specs = [per_b(D_ATTN, seq), per_b(seq, D_ATTN), _const_spec(kb_meta.shape),
                per_b(D_ATTN, seq), _const_spec(vt_meta.shape),
                per_b(N_HEADS, seq), _const_spec(lft_meta.shape),
                per_b(seq, LANES), _const_spec(lfp_meta.shape),
                _const_spec(q_gain.shape), _const_spec(k_gain.shape),
                per_b(dec, D_ATTN), per_b(dec, D_ATTN), per_b(dec, D_ATTN),
                per_b(dec, LANES), per_b(N_HEADS, dec),
                per_b(D_ATTN, past), per_b(D_ATTN, past), per_b(N_HEADS, past)]
    kern = functools.partial(_attn_kernel, seq=seq, past=past, dec=dec)
    return pl.pallas_call(
        kern, grid=(b,), in_specs=in_specs,
        out_specs=(per_b(seq, D_ATTN), per_b(dec, D_ATTN)),
        out_shape=(jax.ShapeDtypeStruct((b, seq, D_ATTN), BF16),
                   jax.ShapeDtypeStruct((b, dec, D_ATTN), BF16)),
        scratch_shapes=[pltpu.VMEM((n_pos, D_ATTN), BF16),
                        pltpu.VMEM((n_blk + 1, N_HEADS * V_SLAB, blk), BF16),
                        pltpu.VMEM((n_pos, LANES), BF16),
                        pltpu.VMEM((n_blk, N_HEADS, blk), F32),
                        pltpu.VMEM((N_HEADS, LANES), F32),
                        pltpu.VMEM((N_HEADS, 2 * LANES, blk), BF16),
                        pltpu.VMEM((N_HEADS, blk), F32),
                        pltpu.VMEM((N_HEADS * V_SLAB, blk), F32),
                        pltpu.VMEM((N_HEADS, 2 * blk, blk), F32),
                        pltpu.VMEM((N_HEADS, past + LANES), F32)],
        compiler_params=_params(1), name="attention",
    )(*prompt_in, *sample_in)


def _merge_mlp_tail(x, conv_bf16, attn_ref, g_conv, g_attn, wbc_ref, wba_ref, wo_ref, g2_ref,
                    wup_ref, wdn_ref, y_ref):
    merged = g_conv * _dot(conv_bf16, wbc_ref[...]) + g_attn * _dot(attn_ref[...], wba_ref[...])
    h = x + _dot(merged.astype(BF16), wo_ref[...])
    hn = _rms_rows(h, g2_ref[...])
    acc = h
    for c in range(D_FF // D_MODEL):
        cols = slice(c * D_MODEL, (c + 1) * D_MODEL)
        a = jnp.maximum(_dot(hn, wup_ref[:, cols]), 0.0)
        acc = acc + _dot((a * a).astype(BF16), wdn_ref[cols, :])
    y_ref[...] = acc


def _branch_mlp_kernel(x_ref, left_ref, attn_ref, xs_ref, convs_ref, attns_ref, gates_ref,
                       g1_ref, wa_ref, wgl_ref, cw_ref, cb_ref,
                       wbc_ref, wba_ref, wo_ref, g2_ref, wup_ref, wdn_ref,
                       y_ref, zlast_ref, ys_ref, zbuf, *, n_tiles, tiles_per_seq):
    step = pl.program_id(0)

    @pl.when(jnp.logical_and(step % tiles_per_seq == 0, step < n_tiles))
    def _():
        zbuf[:, ZPAD - CONV_HALO:ZPAD, :] = left_ref[...]

    @pl.when(step < n_tiles)
    def _():
        x = x_ref[...]
        xn = _rms_rows(x, g1_ref[...])
        conv = _short_conv(xn, wa_ref, cw_ref, cb_ref, zbuf, zlast_ref, 1, x.shape[0])[0]
        g_conv = jax.nn.sigmoid(_dot_nt(xn, wgl_ref[0:D_MODEL, :]))
        g_attn = jax.nn.sigmoid(_dot_nt(xn, wgl_ref[D_MODEL:2 * D_MODEL, :]))
        _merge_mlp_tail(x, conv.astype(BF16), attn_ref, g_conv, g_attn, wbc_ref, wba_ref,
                        wo_ref, g2_ref, wup_ref, wdn_ref, y_ref)

    @pl.when(step == n_tiles)
    def _():
        _merge_mlp_tail(xs_ref[...], convs_ref[...], attns_ref,
                        gates_ref[:, 0:D_MODEL].astype(F32),
                        gates_ref[:, D_MODEL:2 * D_MODEL].astype(F32),
                        wbc_ref, wba_ref, wo_ref, g2_ref, wup_ref, wdn_ref, ys_ref)


def _branch_mlp(x2d, left, attn, small, wts, *, rows, seq):
    n_rows = x2d.shape[0]
    tiles = seq // rows
    n_tiles = n_rows // rows
    last = n_tiles - 1
    row_spec = lambda width: pl.BlockSpec((rows, width), lambda i: (jnp.minimum(i, last), 0))
    seq_spec = pl.BlockSpec((1, CONV_HALO, D_CONV),
                            lambda i: (jnp.minimum(i, last) // tiles, 0, 0))
    full = lambda a: pl.BlockSpec(a.shape, lambda i: (0,) * a.ndim)
    kern = functools.partial(_branch_mlp_kernel, n_tiles=n_tiles, tiles_per_seq=tiles)
    return pl.pallas_call(
        kern, grid=(n_tiles + 1,),
        in_specs=[row_spec(D_MODEL), seq_spec, row_spec(D_ATTN)] + [full(a) for a in small]
        + [_const_spec(w.shape) for w in wts],
        out_specs=(row_spec(D_MODEL), seq_spec, full(small[0])),
        out_shape=(jax.ShapeDtypeStruct((n_rows, D_MODEL), F32),
                   jax.ShapeDtypeStruct((n_rows // seq, CONV_HALO, D_CONV), F32),
                   jax.ShapeDtypeStruct(small[0].shape, F32)),
        scratch_shapes=[pltpu.VMEM((1, rows + ZPAD, D_CONV), F32)],
        compiler_params=_params(1), name="branch_mlp",
    )(x2d, left, attn, *small, *wts)


def kernel(x_prompt, x_sample, cache_k, cache_v, cache_logf, state_conv, meta,
           norm1_g, w_in, b_f, conv_w, conv_b, q_norm_g, k_norm_g,
           w_br_conv, w_br_attn, w_out, norm2_g, w_up, w_down):
    b, seq, _ = x_prompt.shape
    db, dec, _ = x_sample.shape
    past = cache_k.shape[2]
    length = N_META + seq
    n_main = 3 * D_CONV + 3 * D_ATTN
    q0 = 3 * D_CONV

    wt = w_in[0].T
    head_of = jnp.arange(D_ATTN) // HEAD_DIM
    qg = jnp.tile(q_norm_g[0], N_HEADS)
    kg = jnp.tile(k_norm_g[0], N_HEADS)
    g1 = norm1_g[0][None, :]
    bf_row = jnp.tile(b_f[0], LANES // N_HEADS)[None, :]
    conv_wts = (conv_w[0], conv_b[0][None, :])

    x_small = jnp.concatenate([meta, x_sample.reshape(db * dec, D_MODEL)], axis=0)
    left_small = jnp.concatenate(
        [jnp.zeros((1, CONV_HALO, D_CONV), F32), state_conv[0]], axis=0)
    (conv_s, q_s, k_s, v_s, kt_s, vt_s, lft_s, lfp_s, gate_s, zlast_s,
     wt_conv, wt_qkv, wt_gate, wt_fl) = _project_small(
        x_small, left_small, wt,
        (g1, bf_row, qg[:, None], kg[:, None]) + conv_wts
        + (b_f[0][:, None], qg[None, :], kg[None, :],
           (head_of[:, None] == head_of[None, :]).astype(BF16)))
    qkv_wts = (g1, wt_qkv, wt_fl, bf_row, qg[:, None], kg[:, None])

    x_rows = x_prompt.reshape(b * seq, D_MODEL)
    (qt_p, kt_p, vt_p, lftp_p, lft_p, lfp_p, kb_p, vtb_p,
     wbc_b, wba_b, wo_b, wup_b, wdn_b) = _project_qkv(
        x_rows, qkv_wts, (kt_s, vt_s, lft_s),
        (w_br_conv[0], w_br_attn[0], w_out[0], w_up[0], w_down[0]), b=b, seq=seq)
    mlp_wts = (wbc_b, wba_b, wo_b, norm2_g[0][None, :], wup_b, wdn_b)
    k_new = k_s[N_META:].reshape(db, dec, D_ATTN)
    v_new = v_s[N_META:].reshape(db, dec, D_ATTN)
    lf_new = lfp_s[N_META:, :N_HEADS].reshape(db, dec, N_HEADS)
    cache_kt = jnp.transpose(cache_k[0], (0, 2, 3, 1)).reshape(db, D_ATTN, past)
    cache_vt = jnp.transpose(cache_v[0], (0, 2, 3, 1)).reshape(db, D_ATTN, past)
    attn_p, attn_s = _attention(
        (qt_p, kb_p.reshape(b, seq, D_ATTN), k_s[:N_META].astype(BF16), vtb_p, vt_s,
         lft_p, lft_s[:, :N_META], lfp_p.reshape(b, seq, LANES), lfp_s[:N_META],
         q_norm_g, k_norm_g),
        (q_s[N_META:].reshape(db, dec, D_ATTN), k_new, v_new,
         lfp_s[N_META:].reshape(db, dec, LANES), jnp.swapaxes(lf_new, 1, 2),
         cache_kt, cache_vt, jnp.swapaxes(cache_logf[0], 1, 2)))

    left_p = jnp.broadcast_to(zlast_s[0:1], (b, CONV_HALO, D_CONV))
    y_prompt, zlast_p, y_sample = _branch_mlp(
        x_rows, left_p, attn_p.reshape(b * seq, D_ATTN),
        (x_sample.reshape(db * dec, D_MODEL), conv_s[N_META:],
         attn_s.reshape(db * dec, D_ATTN), gate_s[N_META:]),
        (g1, wt_conv, wt_gate) + conv_wts + mlp_wts, rows=MLP_ROWS, seq=seq)

    def heads_last(t):
        return jnp.transpose(t.reshape(b, N_HEADS, HEAD_DIM, length), (0, 3, 1, 2))[None]

    return (y_prompt.reshape(b, seq, D_MODEL),
            y_sample.reshape(db, dec, D_MODEL),
            heads_last(kt_p),
            heads_last(vt_p),
            jnp.swapaxes(lftp_p, 1, 2)[None],
            zlast_p[None],
            k_new.reshape(1, db, dec, N_HEADS, HEAD_DIM),
            v_new.reshape(1, db, dec, N_HEADS, HEAD_DIM),
            lf_new[None],
            zlast_s[1:][None])
```

```python
import functools

import jax
import jax.numpy as jnp
import numpy as np
from jax import lax
from jax.experimental import pallas as pl
from jax.experimental.pallas import tpu as pltpu

D_MODEL = 1024
D_CONV = D_MODEL // 2
CONV_W = 3
N_HEADS = 8
HEAD_DIM = 64
D_ATTN = N_HEADS * HEAD_DIM
D_FF = 4 * D_MODEL
N_META = 16
EPS = 1e-6
ATTN_SCALE = HEAD_DIM ** -0.5

F32 = jnp.float32
BF16 = jnp.bfloat16

VMEM_LIMIT_BYTES = 56 * 1024 * 1024
LANES = 128
SUBLANES = 8
BF16_ROWS = 16
PROJ_ROWS = 1024
MLP_ROWS = 512
ATTN_BLOCK = 256
MASKED_BIAS = 1e30
CONV_HALO = CONV_W - 1
ZPAD = SUBLANES
N_SPLIT = 3
V_SLAB = HEAD_DIM + BF16_ROWS
LOG2E = 1.4426950408889634
SKIP_LOG2 = 40.0
NORM_SLACK = 1.02


def _dot(a, b):
    return jnp.dot(a, b, preferred_element_type=F32)


def _dot_nt(a, b):
    return lax.dot_general(a, b, (((1,), (1,)), ((), ())), preferred_element_type=F32)


def _log_sigmoid(x):
    return jnp.minimum(x, 0.0) - jnp.log1p(jnp.exp(-jnp.abs(x)))


def _cumsum_few(x, axis):
    n = x.shape[axis]
    idx = lax.broadcasted_iota(jnp.int32, x.shape, axis)
    out = jnp.zeros(x.shape, F32)
    for i in range(n):
        term = x[i:i + 1, :] if axis == 0 else x[:, i:i + 1]
        out = out + jnp.where(idx >= i, term, 0.0)
    return out


def _triangles(n):
    r = lax.broadcasted_iota(jnp.int32, (n, n), 0)
    c = lax.broadcasted_iota(jnp.int32, (n, n), 1)
    upper = jnp.where(r <= c, 1.0, 0.0).astype(BF16)
    lower = jnp.where(r >= c, 1.0, 0.0).astype(BF16)
    return upper, lower


def _split3(c):
    hi = c.astype(BF16).astype(F32)
    r1 = c - hi
    mid = r1.astype(BF16).astype(F32)
    return hi, mid, r1 - mid


def _cumsum_rows(tri_lower, x):
    w = x.shape[1]
    pieces = jnp.concatenate(_split3(x), axis=1).astype(BF16)
    y = _dot(tri_lower, pieces)
    return y[:, 0:w] + y[:, w:2 * w] + y[:, 2 * w:3 * w]


def _cumsum_lanes(x, tri_upper):
    h = x.shape[0]
    pieces = jnp.concatenate(_split3(x), axis=0).astype(BF16)
    y = _dot(pieces, tri_upper)
    return y[0:h] + y[h:2 * h] + y[2 * h:3 * h]


def _const_spec(shape):
    nd = len(shape)
    return pl.BlockSpec(shape, lambda *_: (0,) * nd, pipeline_mode=pl.Buffered(1))


def _params(n_axes):
    return pltpu.CompilerParams(
        dimension_semantics=("arbitrary",) * n_axes,
        vmem_limit_bytes=VMEM_LIMIT_BYTES)


def _rms_rows(x, g_row):
    ms = jnp.mean(x * x, axis=-1, keepdims=True)
    return (x * lax.rsqrt(ms + EPS) * g_row).astype(BF16)


def _head_norm_t(ut, g_col):
    out = []
    for h in range(N_HEADS):
        blk = ut[h * HEAD_DIM:(h + 1) * HEAD_DIM, :]
        ms = jnp.mean(blk * blk, axis=0, keepdims=True)
        out.append(blk * lax.rsqrt(ms + EPS) * g_col[h * HEAD_DIM:(h + 1) * HEAD_DIM, :])
    return jnp.concatenate(out, axis=0)


def _short_conv(xn, wa_ref, cw_ref, cb_ref, zbuf, n_seg, seg_len):
    cb = _dot_nt(xn, wa_ref[0:D_CONV, :])
    z = (_dot_nt(xn, wa_ref[D_CONV:2 * D_CONV, :])
         * _dot_nt(xn, wa_ref[2 * D_CONV:3 * D_CONV, :]))
    out, tails = [], []
    for s in range(n_seg):
        r0 = s * seg_len
        zs = z[r0:r0 + seg_len]
        zbuf[s, ZPAD:ZPAD + seg_len, :] = zs
        y = None
        for i in range(CONV_W):
            lo = ZPAD - CONV_HALO + i
            tap = zs if i == CONV_HALO else zbuf[s, lo:lo + seg_len, :]
            term = tap * cw_ref[i:i + 1, :]
            y = term if y is None else y + term
        out.append(cb[r0:r0 + seg_len] * (y + cb_ref[...]))
        tail = zbuf[s, ZPAD + seg_len - CONV_HALO:ZPAD + seg_len, :]
        tails.append(tail)
        zbuf[s, ZPAD - CONV_HALO:ZPAD, :] = tail
    return out, tails


def _qkv_kernel(*refs, tiles_per_seq, n_cast):
    (x_ref, g1_ref, wqkv_ref, wfl_ref, bfr_ref, qgc_ref, kgc_ref,
     ktm_ref, vtm_ref, lftm_ref) = refs[:10]
    cast_in = refs[10:10 + n_cast]
    (qt_ref, kt_ref, vt_ref, lftp_ref, lft_ref, lfp_ref, kb_ref,
     vtb_ref) = refs[10 + n_cast:18 + n_cast]
    cast_out = refs[18 + n_cast:18 + 2 * n_cast]
    kcar, vcar, lcar = refs[18 + 2 * n_cast:]
    rows = x_ref.shape[0]
    step = pl.program_id(0) % tiles_per_seq

    def tile_body(tile_idx):
        lane0 = tile_idx * rows

        def shifted_store(out_ref, car_ref, meta_ref, tile):
            left = meta_ref[:, 0:LANES] if tile_idx == 0 else car_ref[...]
            rolled = pltpu.roll(tile, N_META, axis=1)
            lane = lax.broadcasted_iota(jnp.int32, (tile.shape[0], LANES), 1)
            out_ref[:, lane0:lane0 + LANES] = jnp.where(lane < N_META, left, rolled[:, 0:LANES])
            out_ref[:, lane0 + LANES:lane0 + rows] = rolled[:, LANES:]
            if tile_idx < tiles_per_seq - 1:
                car_ref[...] = rolled[:, 0:LANES]
            else:
                out_ref[:, tiles_per_seq * rows:] = rolled[:, 0:N_META]

        xn = _rms_rows(x_ref[...], g1_ref[...])

        def feature_major(j):
            return _dot_nt(wqkv_ref[j * D_ATTN:(j + 1) * D_ATTN, :], xn)

        lfp = _log_sigmoid(_dot_nt(xn, wfl_ref[...]) + bfr_ref[...])
        lfp_ref[...] = lfp
        lft = lfp.T[0:N_HEADS, :]
        lft_ref[...] = lft
        qt = _head_norm_t(feature_major(0), qgc_ref[...])
        qt_ref[...] = (qt * (ATTN_SCALE * LOG2E)).astype(BF16)
        kt = _head_norm_t(feature_major(1), kgc_ref[...])
        vt = feature_major(2)
        kb_ref[...] = kt.T.astype(BF16)
        vtb_ref[...] = vt.astype(BF16)
        shifted_store(kt_ref, kcar, ktm_ref, kt)
        shifted_store(vt_ref, vcar, vtm_ref, vt)
        shifted_store(lftp_ref, lcar, lftm_ref, lft)
        for src, dst in zip(cast_in, cast_out):
            dst[...] = src[...].astype(BF16)

    for tile_idx in range(tiles_per_seq):
        pl.when(step == tile_idx)(functools.partial(tile_body, tile_idx))


def _project_qkv(x2d, wts, meta_cols, f32_wts, *, b, seq):
    rows = PROJ_ROWS
    tiles = seq // rows
    length = N_META + seq
    n_steps = b * tiles
    chunk_spec = lambda w: pl.BlockSpec((w.shape[0] // n_steps, w.shape[1]), lambda i: (i, 0))
    row_spec = lambda width: pl.BlockSpec((rows, width), lambda i: (i, 0))
    col_spec = lambda feat: pl.BlockSpec(
        (None, feat, rows), lambda i: (i // tiles, 0, i % tiles))
    seq_spec = lambda feat: pl.BlockSpec((None, feat, length), lambda i: (i // tiles, 0, 0))
    out_shape = (
        jax.ShapeDtypeStruct((b, D_ATTN, seq), BF16),
        jax.ShapeDtypeStruct((b, D_ATTN, length), F32),
        jax.ShapeDtypeStruct((b, D_ATTN, length), F32),
        jax.ShapeDtypeStruct((b, N_HEADS, length), F32),
        jax.ShapeDtypeStruct((b, N_HEADS, seq), F32),
        jax.ShapeDtypeStruct((b * seq, LANES), F32),
        jax.ShapeDtypeStruct((b * seq, D_ATTN), BF16),
        jax.ShapeDtypeStruct((b, D_ATTN, seq), BF16),
    )
    out_specs = (col_spec(D_ATTN), seq_spec(D_ATTN), seq_spec(D_ATTN), seq_spec(N_HEADS),
                 col_spec(N_HEADS), row_spec(LANES), row_spec(D_ATTN), col_spec(D_ATTN))
    out_shape = out_shape + tuple(jax.ShapeDtypeStruct(w.shape, BF16) for w in f32_wts)
    out_specs = out_specs + tuple(chunk_spec(w) for w in f32_wts)
    kern = functools.partial(_qkv_kernel, tiles_per_seq=tiles, n_cast=len(f32_wts))
    return pl.pallas_call(
        kern, grid=(n_steps,),
        in_specs=[row_spec(D_MODEL)] + [_const_spec(w.shape) for w in wts + meta_cols]
        + [chunk_spec(w) for w in f32_wts],
        out_specs=out_specs, out_shape=out_shape,
        scratch_shapes=[pltpu.VMEM((D_ATTN, LANES), F32),
                        pltpu.VMEM((D_ATTN, LANES), F32),
                        pltpu.VMEM((N_HEADS, LANES), F32)],
        compiler_params=_params(1), name="proj_qkv",
    )(x2d, *wts, *meta_cols, *f32_wts)


def _proj_small_kernel(meta_ref, xs_ref, state_ref, wt_ref, g1_ref, gains_ref, bfb_ref,
                       cwin_ref, cb_ref, bd_ref,
                       conv_ref, q_ref, k_ref, v_ref, km_ref, kt_ref, vt_ref,
                       lft_ref, lftm_ref, lftn_ref, lfpm_ref, lfpn_ref,
                       gate_ref, leftp_ref, zlast_ref,
                       wa_ref, wqkv_ref, wgl_ref, wfl_ref,
                       bfr_ref, qgc_ref, kgc_ref, cw_ref, zbuf):
    n_new, _, dec = lftn_ref.shape
    n_main = 3 * D_CONV + 3 * D_ATTN
    wa_ref[...] = wt_ref[0:3 * D_CONV, :].astype(BF16)
    wqkv_ref[...] = wt_ref[3 * D_CONV:n_main, :].astype(BF16)
    wgl_ref[...] = wt_ref[n_main + N_HEADS:, :].astype(BF16)
    w_fl = wt_ref[n_main:n_main + N_HEADS, :]
    wfl_ref[...] = jnp.concatenate([w_fl] * (LANES // N_HEADS), axis=0).astype(BF16)
    wflt_ref = wfl_ref.at[0:BF16_ROWS]

    qk = jnp.broadcast_to(gains_ref[...], (SUBLANES, LANES))
    kq = pltpu.roll(qk, HEAD_DIM, axis=1)
    first = lax.broadcasted_iota(jnp.int32, (SUBLANES, LANES), 1) < HEAD_DIM

    def per_feature(two_heads):
        rows = jnp.concatenate([two_heads] * (D_ATTN // LANES), axis=1)
        return rows[0:1, :], rows.T[:, 0:1]

    qgr, qgc = per_feature(jnp.where(first, qk, kq))
    kgr, kgc = per_feature(jnp.where(first, kq, qk))
    qgc_ref[...] = qgc
    kgc_ref[...] = kgc
    bfc = bfb_ref[:, 0:1]
    bfr_ref[...] = jnp.concatenate([bfb_ref[...]] * (LANES // N_HEADS), axis=0).T[0:1, :]
    cw_ref[...] = cwin_ref[0]

    xn = _rms_rows(jnp.concatenate([meta_ref[...], xs_ref[...]], axis=0), g1_ref[...])
    xn_new = xn[N_META:]
    zbuf[0, ZPAD - CONV_HALO:ZPAD, :] = jnp.zeros((CONV_HALO, D_CONV), F32)
    zbuf[1:, ZPAD - CONV_HALO:ZPAD, :] = state_ref[...]
    conv, tails = _short_conv(xn, wa_ref, cw_ref, cb_ref, zbuf, n_new + 1, dec)
    for s in range(n_new):
        conv_ref[s * dec:(s + 1) * dec, :] = conv[s + 1].astype(BF16)
        zlast_ref[s] = tails[s + 1]
    for i in range(leftp_ref.shape[0]):
        leftp_ref[i] = tails[0]

    def rows_major(j, rows):
        return _dot_nt(rows, wqkv_ref[j * D_ATTN:(j + 1) * D_ATTN, :])

    def feature_major(j):
        return _dot_nt(wqkv_ref[j * D_ATTN:(j + 1) * D_ATTN, :], xn)

    def head_norm(u, g_row):
        ssq = _dot((u * u).astype(BF16), bd_ref[...])
        return u * lax.rsqrt(ssq * (1.0 / HEAD_DIM) + EPS) * g_row

    q_ref[...] = (head_norm(rows_major(0, xn_new), qgr) * ATTN_SCALE).astype(BF16)
    k = head_norm(rows_major(1, xn), kgr)
    km_ref[...] = k[0:N_META].astype(BF16)
    k_ref[...] = k[N_META:]
    v_ref[...] = rows_major(2, xn_new)
    kt_ref[...] = _head_norm_t(feature_major(1), kgc_ref[...])
    vt_ref[...] = feature_major(2)
    lfp = _log_sigmoid(_dot_nt(xn, wfl_ref[...]) + bfr_ref[...])
    lfpm_ref[...] = lfp[0:N_META]
    lfpn_ref[...] = lfp[N_META:]
    lft = _log_sigmoid(_dot_nt(wflt_ref[...], xn)[0:N_HEADS] + bfc)
    lft_ref[...] = lft
    lftm_ref[...] = lft[:, 0:N_META]
    for s in range(n_new):
        lftn_ref[s] = lft[:, N_META + s * dec:N_META + (s + 1) * dec]
    gate_ref[...] = jax.nn.sigmoid(_dot_nt(xn_new, wgl_ref[...])).astype(BF16)


def _project_small(meta, x_new, state, wt, small_params, n_prompt):
    n_new = state.shape[0]
    n_rows = x_new.shape[0]
    dec = n_rows // n_new
    n_all = N_META + n_rows
    n_main = 3 * D_CONV + 3 * D_ATTN
    head_of = np.arange(D_ATTN) // HEAD_DIM
    same_head = jnp.asarray(head_of[:, None] == head_of[None, :], BF16)
    full = lambda *shape: pl.BlockSpec(shape, lambda i: (0,) * len(shape))
    out_shape = (
        jax.ShapeDtypeStruct((n_rows, D_CONV), BF16),
        jax.ShapeDtypeStruct((n_rows, D_ATTN), BF16),
        jax.ShapeDtypeStruct((n_rows, D_ATTN), F32),
        jax.ShapeDtypeStruct((n_rows, D_ATTN), F32),
        jax.ShapeDtypeStruct((N_META, D_ATTN), BF16),
        jax.ShapeDtypeStruct((D_ATTN, n_all), F32),
        jax.ShapeDtypeStruct((D_ATTN, n_all), F32),
        jax.ShapeDtypeStruct((N_HEADS, n_all), F32),
        jax.ShapeDtypeStruct((N_HEADS, N_META), F32),
        jax.ShapeDtypeStruct((n_new, N_HEADS, dec), F32),
        jax.ShapeDtypeStruct((N_META, LANES), F32),
        jax.ShapeDtypeStruct((n_rows, LANES), F32),
        jax.ShapeDtypeStruct((n_rows, 2 * D_MODEL), BF16),
        jax.ShapeDtypeStruct((n_prompt, CONV_HALO, D_CONV), F32),
        jax.ShapeDtypeStruct((n_new, CONV_HALO, D_CONV), F32),
        jax.ShapeDtypeStruct((3 * D_CONV, D_MODEL), BF16),
        jax.ShapeDtypeStruct((3 * D_ATTN, D_MODEL), BF16),
        jax.ShapeDtypeStruct((wt.shape[0] - n_main - N_HEADS, D_MODEL), BF16),
        jax.ShapeDtypeStruct((LANES, D_MODEL), BF16),
        jax.ShapeDtypeStruct((1, LANES), F32),
        jax.ShapeDtypeStruct((D_ATTN, 1), F32),
        jax.ShapeDtypeStruct((D_ATTN, 1), F32),
        jax.ShapeDtypeStruct((CONV_W, D_CONV), F32),
    )
    ins = (meta, x_new, state, wt) + tuple(small_params) + (same_head,)
    return pl.pallas_call(
        _proj_small_kernel, grid=(1,),
        in_specs=[full(*meta.shape), full(*x_new.shape), full(*state.shape)]
        + [_const_spec(w.shape) for w in ins[3:]],
        out_specs=tuple(full(*s.shape) for s in out_shape), out_shape=out_shape,
        scratch_shapes=[pltpu.VMEM((n_new + 1, dec + ZPAD, D_CONV), F32)],
        compiler_params=_params(1), name="proj_small",
    )(*ins)


def _prompt_attn_kernel(qt_ref, kb_ref, kbm_ref, vtb_ref, vtm_ref, lft_ref, lftm_ref, lfp_ref,
                        lfpm_ref, qg_ref, kg_ref, o_ref, kpos, vb, kbias, crow, cend, qcat,
                        m_s, acc_s, sbuf, *, seq):
    blk = ATTN_BLOCK
    n_blk = seq // blk
    n_bias = N_SPLIT * N_HEADS
    pad = blk - N_META

    kpos[0:pad, :] = jnp.zeros((pad, D_ATTN), BF16)
    kpos[pad:blk, :] = kbm_ref[...]
    kpos[blk:, :] = kb_ref[...]

    ones_row = (lax.broadcasted_iota(jnp.int32, (V_SLAB - HEAD_DIM, blk), 0) == 0).astype(BF16)
    lane_m = lax.broadcasted_iota(jnp.int32, (HEAD_DIM, LANES), 1)
    vb[0] = jnp.zeros((N_HEADS * V_SLAB, blk), BF16)
    for h in range(N_HEADS):
        rows = slice(h * HEAD_DIM, (h + 1) * HEAD_DIM)
        slab = slice(h * V_SLAB, h * V_SLAB + HEAD_DIM)
        meta = jnp.where(lane_m < N_META, vtm_ref[rows, 0:LANES], 0.0)
        vb[0, slab, blk - LANES:] = pltpu.roll(meta, LANES - N_META, axis=1).astype(BF16)
        for j in range(n_blk):
            vb[j + 1, slab, :] = vtb_ref[rows, j * blk:(j + 1) * blk]
        for j in range(n_blk + 1):
            vb[j, h * V_SLAB + HEAD_DIM:(h + 1) * V_SLAB, :] = ones_row

    qk_bound = (NORM_SLACK * HEAD_DIM * ATTN_SCALE * LOG2E
                * jnp.max(jnp.abs(qg_ref[...]), axis=1, keepdims=True)
                * jnp.max(jnp.abs(kg_ref[...]), axis=1, keepdims=True))

    upper, lower = _triangles(blk)

    def store_kbias(rows, c_col):
        hi, mid, lo = _split3(c_col * LOG2E)
        lane = lax.broadcasted_iota(jnp.int32, c_col.shape, 1)
        grp = lane // N_HEADS
        part = jnp.where(grp == 0, hi, jnp.where(grp == 1, mid, lo))
        kbias[rows, :] = jnp.where(lane < n_bias, -part,
                                   jnp.where(lane < 2 * n_bias, 1.0, 0.0)).astype(BF16)

    lane_p = lax.broadcasted_iota(jnp.int32, (pad, LANES), 1)
    kbias[0:pad, :] = jnp.where(lane_p < N_HEADS, -MASKED_BIAS,
                                jnp.where(jnp.logical_and(lane_p >= n_bias, lane_p < 2 * n_bias),
                                          1.0, 0.0)).astype(BF16)
    c_col = _cumsum_few(lfpm_ref[...], axis=0)
    store_kbias(slice(pad, blk), c_col)
    off_c = c_col[N_META - 1:N_META, :]
    for j in range(n_blk):
        c_col = _cumsum_rows(lower, lfp_ref[j * blk:(j + 1) * blk, :]) + off_c
        store_kbias(slice((j + 1) * blk, (j + 2) * blk), c_col)
        off_c = c_col[blk - 1:blk, :]

    meta_r = _cumsum_few(lftm_ref[...], axis=1)
    off_r = meta_r[:, N_META - 1:N_META]
    lane_h = lax.broadcasted_iota(jnp.int32, (N_HEADS, LANES), 1)
    c_end = jnp.where(lane_h == 0, off_r * LOG2E, 0.0)
    for j in range(n_blk):
        c_row = _cumsum_lanes(lft_ref[:, j * blk:(j + 1) * blk], upper) + off_r
        crow[j] = c_row * LOG2E
        off_r = c_row[:, blk - 1:blk]
        c_end = jnp.where(lane_h == j + 1, off_r * LOG2E, c_end)
    cend[...] = c_end

    row128 = lax.broadcasted_iota(jnp.int32, (LANES, blk), 0)
    krow = lax.broadcasted_iota(jnp.int32, (2 * blk, blk), 0)
    qcol = lax.broadcasted_iota(jnp.int32, (2 * blk, blk), 1)

    def q_block(t, _):
        tok0 = pl.multiple_of(t * blk, blk)
        c_q = crow[t]
        hi, mid, lo = _split3(c_q)
        bias_rows = jnp.concatenate(
            [jnp.ones((n_bias, blk), F32), hi, mid, lo,
             jnp.zeros((LANES - 2 * n_bias, blk), F32)], axis=0)
        for h in range(N_HEADS):
            pair = qt_ref[(h // 2) * LANES:(h // 2 + 1) * LANES, pl.ds(tok0, blk)]
            in_head = (row128 // HEAD_DIM) == (h % 2)
            qcat[h, 0:LANES, :] = jnp.where(in_head, pair, jnp.zeros_like(pair))
            qcat[h, LANES:, :] = jnp.where(row128 % N_HEADS == h, bias_rows, 0.0).astype(BF16)
        m_s[...] = jnp.full(m_s.shape, -jnp.inf, F32)
        acc_s[...] = jnp.zeros(acc_s.shape, F32)

        gap = 2.0 * qk_bound + c_q[:, 0:1] - cend[...]
        needed = jnp.logical_and(gap >= -SKIP_LOG2, lane_h < t)
        n_needed = jnp.max(
            jnp.sum(jnp.where(needed, 1.0, 0.0), axis=1, keepdims=True)).astype(jnp.int32)

        def key_pass(row0, n_rows, pv, visible):
            m_blk = []
            for h in range(N_HEADS):
                g = h // 2
                kc = jnp.concatenate([kpos[pl.ds(row0, n_rows), g * LANES:(g + 1) * LANES],
                                      kbias[pl.ds(row0, n_rows), :]], axis=1)
                s = _dot(kc, qcat[h])
                if visible is not None:
                    s = jnp.where(visible, s, -jnp.inf)
                sbuf[h, 0:n_rows, :] = s
                m_blk.append(jnp.max(s, axis=0, keepdims=True))
            for h in range(N_HEADS):
                slab = slice(h * V_SLAB, (h + 1) * V_SLAB)
                m_old = m_s[h:h + 1, :]
                m_new = jnp.maximum(m_old, m_blk[h])
                alpha = jnp.exp2(m_old - m_new)
                p = jnp.exp2(sbuf[h, 0:n_rows, :] - m_new)
                m_s[h:h + 1, :] = m_new
                acc_s[slab, :] = alpha * acc_s[slab, :] + pv(slab, p.astype(BF16))

        key_pass(tok0, 2 * blk,
                 lambda slab, p: (_dot(vb[t, slab, :], p[0:blk])
                                  + _dot(vb[t + 1, slab, :], p[blk:])),
                 krow <= qcol + blk)

        def older_block(j, _):
            key_pass(pl.multiple_of(j * blk, blk), blk,
                     lambda slab, p: _dot(vb[j, slab, :], p), None)
            return 0

        lax.fori_loop(t - n_needed, t, older_block, 0)

        o_t = []
        for h in range(N_HEADS):
            norm = acc_s[h * V_SLAB + HEAD_DIM:h * V_SLAB + HEAD_DIM + 1, :]
            o_t.append(acc_s[h * V_SLAB:h * V_SLAB + HEAD_DIM, :] * (1.0 / norm))
        o_ref[pl.ds(tok0, blk), :] = jnp.concatenate(o_t, axis=0).T.astype(BF16)
        return 0

    lax.fori_loop(0, n_blk, q_block, 0)


def _sample_attn_kernel(q_ref, kn_ref, vn_ref, lf_ref, lft_ref, ckt_ref, cvt_ref, clft_ref,
                        o_ref, crow, *, past, dec):
    blk = ATTN_BLOCK
    n_keys = past + LANES

    upper, _ = _triangles(blk)
    off = jnp.zeros((N_HEADS, 1), F32)
    for j in range(past // blk):
        loc = _cumsum_lanes(clft_ref[:, j * blk:(j + 1) * blk], upper) + off
        crow[:, j * blk:(j + 1) * blk] = loc
        off = loc[:, blk - 1:blk]
    crow[:, 0:past] = crow[:, 0:past] - off

    cq_c = _cumsum_few(lf_ref[:, 0:N_HEADS], axis=0)
    cq_r = _cumsum_few(lft_ref[...], axis=1)
    crow[:, past:] = jnp.full((N_HEADS, LANES), MASKED_BIAS, F32)
    crow[:, past:past + dec] = cq_r

    q = q_ref[...]
    lane_head = lax.broadcasted_iota(jnp.int32, (dec, D_ATTN), 1) // HEAD_DIM
    q_exp = jnp.concatenate(
        [jnp.where(lane_head == h, q, jnp.zeros_like(q)) for h in range(N_HEADS)], axis=0)
    pad_rows = jnp.zeros((LANES - dec, D_ATTN), BF16)
    k_new = jnp.concatenate([kn_ref[...].astype(BF16), pad_rows], axis=0)
    v_new = jnp.concatenate([vn_ref[...].astype(BF16), pad_rows], axis=0)
    s_all = jnp.concatenate(
        [_dot(q_exp, ckt_ref[...].astype(BF16)), _dot_nt(q_exp, k_new)], axis=1)

    kpos = lax.broadcasted_iota(jnp.int32, (dec, n_keys), 1)
    qpos = past + lax.broadcasted_iota(jnp.int32, (dec, n_keys), 0)
    visible = kpos <= qpos
    probs = []
    norms = []
    for h in range(N_HEADS):
        s = s_all[h * dec:(h + 1) * dec, :] + cq_c[:, h:h + 1] - crow[h:h + 1, :]
        s = jnp.where(visible, s, -jnp.inf)
        p = jnp.exp(s - jnp.max(s, axis=-1, keepdims=True))
        norms.append(jnp.sum(p, axis=-1, keepdims=True))
        probs.append(p.astype(BF16))
    p_all = jnp.concatenate(probs, axis=0)
    o_all = (_dot_nt(p_all[:, 0:past], cvt_ref[...].astype(BF16))
             + _dot(p_all[:, past:], v_new))
    out = jnp.zeros((dec, D_ATTN), F32)
    for h in range(N_HEADS):
        o = o_all[h * dec:(h + 1) * dec, :] / norms[h]
        out = out + jnp.where(lane_head == h, o, 0.0)
    o_ref[...] = out.astype(BF16)


def _attn_kernel(*refs, seq, past, dec):
    n_p, n_s = _N_PROMPT_IN, _N_SAMPLE_IN
    p_in, s_in = refs[0:n_p], refs[n_p:n_p + n_s]
    o_ref, os_ref = refs[n_p + n_s:n_p + n_s + 2]
    scratch = refs[n_p + n_s + 2:]
    _sample_attn_kernel(*s_in, os_ref, scratch[-1], past=past, dec=dec)
    _prompt_attn_kernel(*p_in, o_ref, *scratch[:-1], seq=seq)


_N_PROMPT_IN = 11
_N_SAMPLE_IN = 8


def _attention(prompt_in, sample_in):
    qt, kb, kb_meta, vtb, vt_meta, lft, lft_meta, lfp, lfp_meta, q_gain, k_gain = prompt_in
    q_s, _, _, _, _, cache_kt, _, _ = sample_in
    b, _, seq = qt.shape
    db, dec, _ = q_s.shape
    assert b == db, "one new stream and one running stream per grid step"
    past = cache_kt.shape[2]
    blk = ATTN_BLOCK
    n_blk = seq // blk
    n_pos = (n_blk + 1) * blk
    per_b = lambda *shape: pl.BlockSpec((None,) + shape, lambda i: (i,) + (0,) * len(shape))
    in_specs = [per_b(D_ATTN, seq), per_b(seq, D_ATTN), _const_spec(kb_meta.shape),
                per_b(D_ATTN, seq), _const_spec(vt_meta.shape),
                per_b(N_HEADS, seq), _const_spec(lft_meta.shape),
                per_b(seq, LANES), _const_spec(lfp_meta.shape),
                _const_spec(q_gain.shape), _const_spec(k_gain.shape),
                per_b(dec, D_ATTN), per_b(dec, D_ATTN), per_b(dec, D_ATTN),
                per_b(dec, LANES), per_b(N_HEADS, dec),
                per_b(D_ATTN, past), per_b(D_ATTN, past), per_b(N_HEADS, past)]
    kern = functools.partial(_attn_kernel, seq=seq, past=past, dec=dec)
    return pl.pallas_call(
        kern, grid=(b,), in_specs=in_specs,
        out_specs=(per_b(seq, D_ATTN), per_b(dec, D_ATTN)),
        out_shape=(jax.ShapeDtypeStruct((b, seq, D_ATTN), BF16),
                   jax.ShapeDtypeStruct((b, dec, D_ATTN), BF16)),
        scratch_shapes=[pltpu.VMEM((n_pos, D_ATTN), BF16),
                        pltpu.VMEM((n_blk + 1, N_HEADS * V_SLAB, blk), BF16),
                        pltpu.VMEM((n_pos, LANES), BF16),
                        pltpu.VMEM((n_blk, N_HEADS, blk), F32),
                        pltpu.VMEM((N_HEADS, LANES), F32),
                        pltpu.VMEM((N_HEADS, 2 * LANES, blk), BF16),
                        pltpu.VMEM((N_HEADS, blk), F32),
                        pltpu.VMEM((N_HEADS * V_SLAB, blk), F32),
                        pltpu.VMEM((N_HEADS, 2 * blk, blk), F32),
                        pltpu.VMEM((N_HEADS, past + LANES), F32)],
        compiler_params=_params(1), name="attention",
    )(*prompt_in, *sample_in)


def _merge_mlp_tail(x, conv_bf16, attn_ref, g_conv, g_attn, wbc_ref, wba_ref, wo_ref, g2_ref,
                    wup_ref, wdn_ref, y_ref):
    merged = g_conv * _dot(conv_bf16, wbc_ref[...]) + g_attn * _dot(attn_ref[...], wba_ref[...])
    h = x + _dot(merged.astype(BF16), wo_ref[...])
    hn = _rms_rows(h, g2_ref[...])
    acc = h
    for c in range(D_FF // D_MODEL):
        cols = slice(c * D_MODEL, (c + 1) * D_MODEL)
        a = jnp.maximum(_dot(hn, wup_ref[:, cols]), 0.0)
        acc = acc + _dot((a * a).astype(BF16), wdn_ref[cols, :])
    y_ref[...] = acc


def _branch_mlp_kernel(x_ref, left_ref, attn_ref, xs_ref, convs_ref, attns_ref, gates_ref,
                       g1_ref, wa_ref, wgl_ref, cw_ref, cb_ref,
                       wbc_ref, wba_ref, wo_ref, g2_ref, wup_ref, wdn_ref,
                       y_ref, zlast_ref, ys_ref, zbuf, *, n_tiles, tiles_per_seq):
    step = pl.program_id(0)

    @pl.when(jnp.logical_and(step % tiles_per_seq == 0, step < n_tiles))
    def _():
        zbuf[:, ZPAD - CONV_HALO:ZPAD, :] = left_ref[...]

    @pl.when(step < n_tiles)
    def _():
        x = x_ref[...]
        xn = _rms_rows(x, g1_ref[...])
        (conv,), (tail,) = _short_conv(xn, wa_ref, cw_ref, cb_ref, zbuf, 1, x.shape[0])
        zlast_ref[0] = tail
        g_conv = jax.nn.sigmoid(_dot_nt(xn, wgl_ref[0:D_MODEL, :]))
        g_attn = jax.nn.sigmoid(_dot_nt(xn, wgl_ref[D_MODEL:2 * D_MODEL, :]))
        _merge_mlp_tail(x, conv.astype(BF16), attn_ref, g_conv, g_attn, wbc_ref, wba_ref,
                        wo_ref, g2_ref, wup_ref, wdn_ref, y_ref)

    @pl.when(step == n_tiles)
    def _():
        _merge_mlp_tail(xs_ref[...], convs_ref[...], attns_ref,
                        gates_ref[:, 0:D_MODEL].astype(F32),
                        gates_ref[:, D_MODEL:2 * D_MODEL].astype(F32),
                        wbc_ref, wba_ref, wo_ref, g2_ref, wup_ref, wdn_ref, ys_ref)


def _branch_mlp(x2d, left, attn, small, wts, *, rows, seq):
    n_rows = x2d.shape[0]
    tiles = seq // rows
    n_tiles = n_rows // rows
    last = n_tiles - 1
    row_spec = lambda width: pl.BlockSpec((rows, width), lambda i: (jnp.minimum(i, last), 0))
    seq_spec = pl.BlockSpec((1, CONV_HALO, D_CONV),
                            lambda i: (jnp.minimum(i, last) // tiles, 0, 0))
    full = lambda a: pl.BlockSpec(a.shape, lambda i: (0,) * a.ndim)
    kern = functools.partial(_branch_mlp_kernel, n_tiles=n_tiles, tiles_per_seq=tiles)
    return pl.pallas_call(
        kern, grid=(n_tiles + 1,),
        in_specs=[row_spec(D_MODEL), seq_spec, row_spec(D_ATTN)] + [full(a) for a in small]
        + [_const_spec(w.shape) for w in wts],
        out_specs=(row_spec(D_MODEL), seq_spec, full(small[0])),
        out_shape=(jax.ShapeDtypeStruct((n_rows, D_MODEL), F32),
                   jax.ShapeDtypeStruct((n_rows // seq, CONV_HALO, D_CONV), F32),
                   jax.ShapeDtypeStruct(small[0].shape, F32)),
        scratch_shapes=[pltpu.VMEM((1, rows + ZPAD, D_CONV), F32)],
        compiler_params=_params(1), name="branch_mlp",
    )(x2d, left, attn, *small, *wts)


def kernel(x_prompt, x_sample, cache_k, cache_v, cache_logf, state_conv, meta,
           norm1_g, w_in, b_f, conv_w, conv_b, q_norm_g, k_norm_g,
           w_br_conv, w_br_attn, w_out, norm2_g, w_up, w_down):
    b, seq, _ = x_prompt.shape
    db, dec, _ = x_sample.shape
    past = cache_k.shape[2]
    length = N_META + seq
    assert dec == N_META, "the small tile treats the meta tokens as one more short sequence"

    wt = w_in[0].T
    assert 2 * HEAD_DIM == LANES, "the q and k gains share one vreg row"
    g1 = norm1_g[0][None, :]
    conv_bias = conv_b[0][None, :]
    gains = jnp.concatenate([q_norm_g[0], k_norm_g[0]])[None, :]
    bf_lanes = jnp.broadcast_to(b_f[0][:, None], (N_HEADS, LANES))

    x_new = x_sample.reshape(db * dec, D_MODEL)
    (conv_s, q_s, k_s, v_s, k_meta, kt_s, vt_s, lft_s, lft_meta, lft_new, lfp_meta, lfp_s,
     gate_s, left_p, zlast_s, wt_conv, wt_qkv, wt_gate, wt_fl,
     bf_row, qg_col, kg_col, conv_taps) = _project_small(
        meta, x_new, state_conv[0], wt, (g1, gains, bf_lanes, conv_w, conv_bias), b)
    qkv_wts = (g1, wt_qkv, wt_fl, bf_row, qg_col, kg_col)
    conv_wts = (conv_taps, conv_bias)

    x_rows = x_prompt.reshape(b * seq, D_MODEL)
    (qt_p, kt_p, vt_p, lftp_p, lft_p, lfp_p, kb_p, vtb_p,
     wbc_b, wba_b, wo_b, wup_b, wdn_b) = _project_qkv(
        x_rows, qkv_wts, (kt_s, vt_s, lft_s),
        (w_br_conv[0], w_br_attn[0], w_out[0], w_up[0], w_down[0]), b=b, seq=seq)
    mlp_wts = (wbc_b, wba_b, wo_b, norm2_g[0][None, :], wup_b, wdn_b)
    k_new = k_s.reshape(db, dec, D_ATTN)
    v_new = v_s.reshape(db, dec, D_ATTN)
    cache_kt = jnp.transpose(cache_k[0], (0, 2, 3, 1)).reshape(db, D_ATTN, past)
    cache_vt = jnp.transpose(cache_v[0], (0, 2, 3, 1)).reshape(db, D_ATTN, past)
    attn_p, attn_s = _attention(
        (qt_p, kb_p.reshape(b, seq, D_ATTN), k_meta, vtb_p, vt_s,
         lft_p, lft_meta, lfp_p.reshape(b, seq, LANES), lfp_meta,
         q_norm_g, k_norm_g),
        (q_s.reshape(db, dec, D_ATTN), k_new, v_new,
         lfp_s.reshape(db, dec, LANES), lft_new,
         cache_kt, cache_vt, jnp.swapaxes(cache_logf[0], 1, 2)))

    y_prompt, zlast_p, y_sample = _branch_mlp(
        x_rows, left_p, attn_p.reshape(b * seq, D_ATTN),
        (x_new, conv_s, attn_s.reshape(db * dec, D_ATTN), gate_s),
        (g1, wt_conv, wt_gate) + conv_wts + mlp_wts, rows=MLP_ROWS, seq=seq)

    def heads_last(t):
        return jnp.transpose(t.reshape(b, N_HEADS, HEAD_DIM, length), (0, 3, 1, 2))[None]

    return (y_prompt.reshape(b, seq, D_MODEL),
            y_sample.reshape(db, dec, D_MODEL),
            heads_last(kt_p),
            heads_last(vt_p),
            jnp.swapaxes(lftp_p, 1, 2)[None],
            zlast_p[None],
            k_new.reshape(1, db, dec, N_HEADS, HEAD_DIM),
            v_new.reshape(1, db, dec, N_HEADS, HEAD_DIM),
            jnp.swapaxes(lft_new, 1, 2)[None],
            zlast_s[None])
```

```python
import functools

import jax
import jax.numpy as jnp
import numpy as np
from jax import lax
from jax.experimental import pallas as pl
from jax.experimental.pallas import tpu as pltpu

D_MODEL = 1024
D_CONV = D_MODEL // 2
CONV_W = 3
N_HEADS = 8
HEAD_DIM = 64
D_ATTN = N_HEADS * HEAD_DIM
D_FF = 4 * D_MODEL
N_META = 16
EPS = 1e-6
ATTN_SCALE = HEAD_DIM ** -0.5

F32 = jnp.float32
BF16 = jnp.bfloat16

VMEM_LIMIT_BYTES = 56 * 1024 * 1024
LANES = 128
SUBLANES = 8
BF16_ROWS = 16
PROJ_ROWS = 1024
MLP_ROWS = 512
ATTN_BLOCK = 256
MASKED_BIAS = 1e30
CONV_HALO = CONV_W - 1
ZPAD = SUBLANES
N_SPLIT = 3
V_SLAB = HEAD_DIM + BF16_ROWS
LOG2E = 1.4426950408889634
SKIP_LOG2 = 40.0
NORM_SLACK = 1.02


def _dot(a, b):
    return jnp.dot(a, b, preferred_element_type=F32)


def _dot_nt(a, b):
    return lax.dot_general(a, b, (((1,), (1,)), ((), ())), preferred_element_type=F32)


def _log_sigmoid(x):
    return jnp.minimum(x, 0.0) - jnp.log1p(jnp.exp(-jnp.abs(x)))


def _cumsum_few(x, axis):
    n = x.shape[axis]
    idx = lax.broadcasted_iota(jnp.int32, x.shape, axis)
    out = jnp.zeros(x.shape, F32)
    for i in range(n):
        term = x[i:i + 1, :] if axis == 0 else x[:, i:i + 1]
        out = out + jnp.where(idx >= i, term, 0.0)
    return out


def _upper_triangle(n):
    r = lax.broadcasted_iota(jnp.int32, (n, n), 0)
    c = lax.broadcasted_iota(jnp.int32, (n, n), 1)
    return jnp.where(r <= c, 1.0, 0.0).astype(BF16)


def _split3(c):
    hi = c.astype(BF16).astype(F32)
    r1 = c - hi
    mid = r1.astype(BF16).astype(F32)
    return hi, mid, r1 - mid


def _cumsum_lanes(x, tri_upper):
    h = x.shape[0]
    pieces = jnp.concatenate(_split3(x), axis=0).astype(BF16)
    y = _dot(pieces, tri_upper)
    return y[0:h] + y[h:2 * h] + y[2 * h:3 * h]


def _const_spec(shape):
    nd = len(shape)
    return pl.BlockSpec(shape, lambda *_: (0,) * nd, pipeline_mode=pl.Buffered(1))


def _params(n_axes):
    return pltpu.CompilerParams(
        dimension_semantics=("arbitrary",) * n_axes,
        vmem_limit_bytes=VMEM_LIMIT_BYTES)


def _rms_rows(x, g_row):
    ms = jnp.mean(x * x, axis=-1, keepdims=True)
    return (x * lax.rsqrt(ms + EPS) * g_row).astype(BF16)


def _head_norm_t(ut, g_col):
    out = []
    for h in range(N_HEADS):
        blk = ut[h * HEAD_DIM:(h + 1) * HEAD_DIM, :]
        ms = jnp.mean(blk * blk, axis=0, keepdims=True)
        out.append(blk * lax.rsqrt(ms + EPS) * g_col[h * HEAD_DIM:(h + 1) * HEAD_DIM, :])
    return jnp.concatenate(out, axis=0)


def _short_conv(xn, wa_ref, cw_ref, cb_ref, zbuf, n_seg, seg_len):
    cb = _dot_nt(xn, wa_ref[0:D_CONV, :])
    z = (_dot_nt(xn, wa_ref[D_CONV:2 * D_CONV, :])
         * _dot_nt(xn, wa_ref[2 * D_CONV:3 * D_CONV, :]))
    out, tails = [], []
    for s in range(n_seg):
        r0 = s * seg_len
        zs = z[r0:r0 + seg_len]
        zbuf[s, ZPAD:ZPAD + seg_len, :] = zs
        y = None
        for i in range(CONV_W):
            lo = ZPAD - CONV_HALO + i
            tap = zs if i == CONV_HALO else zbuf[s, lo:lo + seg_len, :]
            term = tap * cw_ref[i:i + 1, :]
            y = term if y is None else y + term
        out.append(cb[r0:r0 + seg_len] * (y + cb_ref[...]))
        tail = zbuf[s, ZPAD + seg_len - CONV_HALO:ZPAD + seg_len, :]
        tails.append(tail)
        zbuf[s, ZPAD - CONV_HALO:ZPAD, :] = tail
    return out, tails


def _qkv_kernel(*refs, tiles_per_seq, n_cast):
    (x_ref, g1_ref, wqkv_ref, wfl_ref, bfr_ref, qgc_ref, kgc_ref,
     ktm_ref, vtm_ref, lftm_ref) = refs[:10]
    cast_in = refs[10:10 + n_cast]
    (qt_ref, kt_ref, vt_ref, lftp_ref, lft_ref, kb_ref,
     vtb_ref) = refs[10 + n_cast:17 + n_cast]
    cast_out = refs[17 + n_cast:17 + 2 * n_cast]
    kcar, vcar, lcar = refs[17 + 2 * n_cast:]
    rows = x_ref.shape[0]
    step = pl.program_id(0) % tiles_per_seq

    def tile_body(tile_idx):
        lane0 = tile_idx * rows

        def shifted_store(out_ref, car_ref, meta_ref, tile):
            left = meta_ref[:, 0:LANES] if tile_idx == 0 else car_ref[...]
            rolled = pltpu.roll(tile, N_META, axis=1)
            lane = lax.broadcasted_iota(jnp.int32, (tile.shape[0], LANES), 1)
            out_ref[:, lane0:lane0 + LANES] = jnp.where(lane < N_META, left, rolled[:, 0:LANES])
            out_ref[:, lane0 + LANES:lane0 + rows] = rolled[:, LANES:]
            if tile_idx < tiles_per_seq - 1:
                car_ref[...] = rolled[:, 0:LANES]
            else:
                out_ref[:, tiles_per_seq * rows:] = rolled[:, 0:N_META]

        xn = _rms_rows(x_ref[...], g1_ref[...])

        def feature_major(j):
            return _dot_nt(wqkv_ref[j * D_ATTN:(j + 1) * D_ATTN, :], xn)

        lfp = _log_sigmoid(_dot_nt(xn, wfl_ref[...]) + bfr_ref[...])
        lft = lfp.T[0:N_HEADS, :]
        lft_ref[...] = lft
        qt = _head_norm_t(feature_major(0), qgc_ref[...])
        qt_ref[...] = (qt * (ATTN_SCALE * LOG2E)).astype(BF16)
        kt = _head_norm_t(feature_major(1), kgc_ref[...])
        vt = feature_major(2)
        kb_ref[...] = kt.T.astype(BF16)
        vtb_ref[...] = vt.astype(BF16)
        shifted_store(kt_ref, kcar, ktm_ref, kt)
        shifted_store(vt_ref, vcar, vtm_ref, vt)
        shifted_store(lftp_ref, lcar, lftm_ref, lft)
        for src, dst in zip(cast_in, cast_out):
            dst[...] = src[...].astype(BF16)

    for tile_idx in range(tiles_per_seq):
        pl.when(step == tile_idx)(functools.partial(tile_body, tile_idx))


def _project_qkv(x2d, wts, meta_cols, f32_wts, *, b, seq):
    rows = PROJ_ROWS
    tiles = seq // rows
    length = N_META + seq
    n_steps = b * tiles
    chunk_spec = lambda w: pl.BlockSpec((w.shape[0] // n_steps, w.shape[1]), lambda i: (i, 0))
    row_spec = lambda width: pl.BlockSpec((rows, width), lambda i: (i, 0))
    col_spec = lambda feat: pl.BlockSpec(
        (None, feat, rows), lambda i: (i // tiles, 0, i % tiles))
    seq_spec = lambda feat: pl.BlockSpec((None, feat, length), lambda i: (i // tiles, 0, 0))
    out_shape = (
        jax.ShapeDtypeStruct((b, D_ATTN, seq), BF16),
        jax.ShapeDtypeStruct((b, D_ATTN, length), F32),
        jax.ShapeDtypeStruct((b, D_ATTN, length), F32),
        jax.ShapeDtypeStruct((b, N_HEADS, length), F32),
        jax.ShapeDtypeStruct((b, N_HEADS, seq), F32),
        jax.ShapeDtypeStruct((b * seq, D_ATTN), BF16),
        jax.ShapeDtypeStruct((b, D_ATTN, seq), BF16),
    )
    out_specs = (col_spec(D_ATTN), seq_spec(D_ATTN), seq_spec(D_ATTN), seq_spec(N_HEADS),
                 col_spec(N_HEADS), row_spec(D_ATTN), col_spec(D_ATTN))
    out_shape = out_shape + tuple(jax.ShapeDtypeStruct(w.shape, BF16) for w in f32_wts)
    out_specs = out_specs + tuple(chunk_spec(w) for w in f32_wts)
    kern = functools.partial(_qkv_kernel, tiles_per_seq=tiles, n_cast=len(f32_wts))
    return pl.pallas_call(
        kern, grid=(n_steps,),
        in_specs=[row_spec(D_MODEL)] + [_const_spec(w.shape) for w in wts + meta_cols]
        + [chunk_spec(w) for w in f32_wts],
        out_specs=out_specs, out_shape=out_shape,
        scratch_shapes=[pltpu.VMEM((D_ATTN, LANES), F32),
                        pltpu.VMEM((D_ATTN, LANES), F32),
                        pltpu.VMEM((N_HEADS, LANES), F32)],
        compiler_params=_params(1), name="proj_qkv",
    )(x2d, *wts, *meta_cols, *f32_wts)


def _proj_small_kernel(meta_ref, xs_ref, state_ref, wt_ref, g1_ref, gains_ref, bfb_ref,
                       cwin_ref, cb_ref, bd_ref,
                       conv_ref, q_ref, k_ref, v_ref, km_ref, kt_ref, vt_ref,
                       lft_ref, lftn_ref, lfpn_ref,
                       gate_ref, leftp_ref, zlast_ref,
                       wa_ref, wqkv_ref, wgl_ref, wfl_ref,
                       bfr_ref, qgc_ref, kgc_ref, cw_ref, zbuf):
    n_new, _, dec = lftn_ref.shape
    n_main = 3 * D_CONV + 3 * D_ATTN
    wa_ref[...] = wt_ref[0:3 * D_CONV, :].astype(BF16)
    wqkv_ref[...] = wt_ref[3 * D_CONV:n_main, :].astype(BF16)
    wgl_ref[...] = wt_ref[n_main + N_HEADS:, :].astype(BF16)
    w_fl = wt_ref[n_main:n_main + N_HEADS, :]
    wfl_ref[...] = jnp.concatenate([w_fl] * (LANES // N_HEADS), axis=0).astype(BF16)
    wflt_ref = wfl_ref.at[0:BF16_ROWS]

    qk = jnp.broadcast_to(gains_ref[...], (SUBLANES, LANES))
    kq = pltpu.roll(qk, HEAD_DIM, axis=1)
    first = lax.broadcasted_iota(jnp.int32, (SUBLANES, LANES), 1) < HEAD_DIM

    def per_feature(two_heads):
        rows = jnp.concatenate([two_heads] * (D_ATTN // LANES), axis=1)
        return rows[0:1, :], rows.T[:, 0:1]

    qgr, qgc = per_feature(jnp.where(first, qk, kq))
    kgr, kgc = per_feature(jnp.where(first, kq, qk))
    qgc_ref[...] = qgc
    kgc_ref[...] = kgc
    bfc = bfb_ref[:, 0:1]
    bfr_ref[...] = jnp.concatenate([bfb_ref[...]] * (LANES // N_HEADS), axis=0).T[0:1, :]
    cw_ref[...] = cwin_ref[0]

    xn = _rms_rows(jnp.concatenate([meta_ref[...], xs_ref[...]], axis=0), g1_ref[...])
    xn_new = xn[N_META:]
    zbuf[0, ZPAD - CONV_HALO:ZPAD, :] = jnp.zeros((CONV_HALO, D_CONV), F32)
    zbuf[1:, ZPAD - CONV_HALO:ZPAD, :] = state_ref[...]
    conv, tails = _short_conv(xn, wa_ref, cw_ref, cb_ref, zbuf, n_new + 1, dec)
    for s in range(n_new):
        conv_ref[s * dec:(s + 1) * dec, :] = conv[s + 1].astype(BF16)
        zlast_ref[s] = tails[s + 1]
    for i in range(leftp_ref.shape[0]):
        leftp_ref[i] = tails[0]

    def rows_major(j, rows):
        return _dot_nt(rows, wqkv_ref[j * D_ATTN:(j + 1) * D_ATTN, :])

    def feature_major(j):
        return _dot_nt(wqkv_ref[j * D_ATTN:(j + 1) * D_ATTN, :], xn)

    def head_norm(u, g_row):
        ssq = _dot((u * u).astype(BF16), bd_ref[...])
        return u * lax.rsqrt(ssq * (1.0 / HEAD_DIM) + EPS) * g_row

    q_ref[...] = (head_norm(rows_major(0, xn_new), qgr) * ATTN_SCALE).astype(BF16)
    k = head_norm(rows_major(1, xn), kgr)
    km_ref[...] = k[0:N_META].astype(BF16)
    k_ref[...] = k[N_META:]
    v_ref[...] = rows_major(2, xn_new)
    kt_ref[...] = _head_norm_t(feature_major(1), kgc_ref[...])
    vt_ref[...] = feature_major(2)
    lfpn_ref[...] = _log_sigmoid(_dot_nt(xn_new, wfl_ref[...]) + bfr_ref[...])
    lft = _log_sigmoid(_dot_nt(wflt_ref[...], xn)[0:N_HEADS] + bfc)
    lft_ref[...] = lft
    for s in range(n_new):
        lftn_ref[s] = lft[:, N_META + s * dec:N_META + (s + 1) * dec]
    gate_ref[...] = jax.nn.sigmoid(_dot_nt(xn_new, wgl_ref[...])).astype(BF16)


def _project_small(meta, x_new, state, wt, small_params, n_prompt):
    n_new = state.shape[0]
    n_rows = x_new.shape[0]
    dec = n_rows // n_new
    n_all = N_META + n_rows
    n_main = 3 * D_CONV + 3 * D_ATTN
    head_of = np.arange(D_ATTN) // HEAD_DIM
    same_head = jnp.asarray(head_of[:, None] == head_of[None, :], BF16)
    full = lambda *shape: pl.BlockSpec(shape, lambda i: (0,) * len(shape))
    out_shape = (
        jax.ShapeDtypeStruct((n_rows, D_CONV), BF16),
        jax.ShapeDtypeStruct((n_rows, D_ATTN), BF16),
        jax.ShapeDtypeStruct((n_rows, D_ATTN), F32),
        jax.ShapeDtypeStruct((n_rows, D_ATTN), F32),
        jax.ShapeDtypeStruct((N_META, D_ATTN), BF16),
        jax.ShapeDtypeStruct((D_ATTN, n_all), F32),
        jax.ShapeDtypeStruct((D_ATTN, n_all), F32),
        jax.ShapeDtypeStruct((N_HEADS, n_all), F32),
        jax.ShapeDtypeStruct((n_new, N_HEADS, dec), F32),
        jax.ShapeDtypeStruct((n_rows, LANES), F32),
        jax.ShapeDtypeStruct((n_rows, 2 * D_MODEL), BF16),
        jax.ShapeDtypeStruct((n_prompt, CONV_HALO, D_CONV), F32),
        jax.ShapeDtypeStruct((n_new, CONV_HALO, D_CONV), F32),
        jax.ShapeDtypeStruct((3 * D_CONV, D_MODEL), BF16),
        jax.ShapeDtypeStruct((3 * D_ATTN, D_MODEL), BF16),
        jax.ShapeDtypeStruct((wt.shape[0] - n_main - N_HEADS, D_MODEL), BF16),
        jax.ShapeDtypeStruct((LANES, D_MODEL), BF16),
        jax.ShapeDtypeStruct((1, LANES), F32),
        jax.ShapeDtypeStruct((D_ATTN, 1), F32),
        jax.ShapeDtypeStruct((D_ATTN, 1), F32),
        jax.ShapeDtypeStruct((CONV_W, D_CONV), F32),
    )
    ins = (meta, x_new, state, wt) + tuple(small_params) + (same_head,)
    return pl.pallas_call(
        _proj_small_kernel, grid=(1,),
        in_specs=[full(*meta.shape), full(*x_new.shape), full(*state.shape)]
        + [_const_spec(w.shape) for w in ins[3:]],
        out_specs=tuple(full(*s.shape) for s in out_shape), out_shape=out_shape,
        scratch_shapes=[pltpu.VMEM((n_new + 1, dec + ZPAD, D_CONV), F32)],
        compiler_params=_params(1), name="proj_small",
    )(*ins)


def _prompt_attn_kernel(qt_ref, kb_ref, kbm_ref, vtb_ref, vtm_ref, lft_ref, lftm_ref,
                        qg_ref, kg_ref, o_ref, kpos, vb, kbias, crow, cend, qcat,
                        m_s, acc_s, sbuf, *, seq):
    blk = ATTN_BLOCK
    n_blk = seq // blk
    n_bias = N_SPLIT * N_HEADS
    pad = blk - N_META

    kpos[0:pad, :] = jnp.zeros((pad, D_ATTN), BF16)
    kpos[pad:blk, :] = kbm_ref[...]
    kpos[blk:, :] = kb_ref[...]

    ones_row = (lax.broadcasted_iota(jnp.int32, (V_SLAB - HEAD_DIM, blk), 0) == 0).astype(BF16)
    lane_m = lax.broadcasted_iota(jnp.int32, (HEAD_DIM, LANES), 1)
    vb[0] = jnp.zeros((N_HEADS * V_SLAB, blk), BF16)
    for h in range(N_HEADS):
        rows = slice(h * HEAD_DIM, (h + 1) * HEAD_DIM)
        slab = slice(h * V_SLAB, h * V_SLAB + HEAD_DIM)
        meta = jnp.where(lane_m < N_META, vtm_ref[rows, 0:LANES], 0.0)
        vb[0, slab, blk - LANES:] = pltpu.roll(meta, LANES - N_META, axis=1).astype(BF16)
        for j in range(n_blk):
            vb[j + 1, slab, :] = vtb_ref[rows, j * blk:(j + 1) * blk]
        for j in range(n_blk + 1):
            vb[j, h * V_SLAB + HEAD_DIM:(h + 1) * V_SLAB, :] = ones_row

    qk_bound = (NORM_SLACK * HEAD_DIM * ATTN_SCALE * LOG2E
                * jnp.max(jnp.abs(qg_ref[...]), axis=1, keepdims=True)
                * jnp.max(jnp.abs(kg_ref[...]), axis=1, keepdims=True))

    upper = _upper_triangle(blk)
    ones_zeros = jnp.concatenate([jnp.ones((n_bias, blk), F32),
                                  jnp.zeros((LANES - 2 * n_bias, blk), F32)], axis=0)

    def store_kbias(j, c_log2, is_pad=None):
        neg = [-piece for piece in _split3(c_log2)]
        if is_pad is not None:
            neg = [jnp.where(is_pad, fill, piece)
                   for piece, fill in zip(neg, (-MASKED_BIAS, 0.0, 0.0))]
        kbias[j * blk:(j + 1) * blk, :] = jnp.concatenate(neg + [ones_zeros], axis=0).T.astype(BF16)

    lane_h = lax.broadcasted_iota(jnp.int32, (N_HEADS, LANES), 1)
    lf_meta = jnp.where(lane_h < N_META, lftm_ref[:, 0:LANES], 0.0)
    lf_blk0 = jnp.concatenate([jnp.zeros((N_HEADS, blk - LANES), F32),
                               pltpu.roll(lf_meta, LANES - N_META, axis=1)], axis=1)
    c_row = _cumsum_lanes(lf_blk0, upper)
    store_kbias(0, c_row * LOG2E, lax.broadcasted_iota(jnp.int32, (N_HEADS, blk), 1) < pad)
    off_r = c_row[:, blk - 1:blk]
    c_end = jnp.where(lane_h == 0, off_r * LOG2E, 0.0)
    for j in range(n_blk):
        c_row = _cumsum_lanes(lft_ref[:, j * blk:(j + 1) * blk], upper) + off_r
        crow[j] = c_row * LOG2E
        store_kbias(j + 1, c_row * LOG2E)
        off_r = c_row[:, blk - 1:blk]
        c_end = jnp.where(lane_h == j + 1, off_r * LOG2E, c_end)
    cend[...] = c_end

    row128 = lax.broadcasted_iota(jnp.int32, (LANES, blk), 0)
    krow = lax.broadcasted_iota(jnp.int32, (2 * blk, blk), 0)
    qcol = lax.broadcasted_iota(jnp.int32, (2 * blk, blk), 1)

    def q_block(t, _):
        tok0 = pl.multiple_of(t * blk, blk)
        c_q = crow[t]
        hi, mid, lo = _split3(c_q)
        bias_rows = jnp.concatenate(
            [jnp.ones((n_bias, blk), F32), hi, mid, lo,
             jnp.zeros((LANES - 2 * n_bias, blk), F32)], axis=0)
        for h in range(N_HEADS):
            pair = qt_ref[(h // 2) * LANES:(h // 2 + 1) * LANES, pl.ds(tok0, blk)]
            in_head = (row128 // HEAD_DIM) == (h % 2)
            qcat[h, 0:LANES, :] = jnp.where(in_head, pair, jnp.zeros_like(pair))
            qcat[h, LANES:, :] = jnp.where(row128 % N_HEADS == h, bias_rows, 0.0).astype(BF16)
        m_s[...] = jnp.full(m_s.shape, -jnp.inf, F32)
        acc_s[...] = jnp.zeros(acc_s.shape, F32)

        gap = 2.0 * qk_bound + c_q[:, 0:1] - cend[...]
        needed = jnp.logical_and(gap >= -SKIP_LOG2, lane_h < t)
        n_needed = jnp.max(
            jnp.sum(jnp.where(needed, 1.0, 0.0), axis=1, keepdims=True)).astype(jnp.int32)

        def key_pass(row0, n_rows, pv, visible):
            m_blk = []
            for h in range(N_HEADS):
                g = h // 2
                kc = jnp.concatenate([kpos[pl.ds(row0, n_rows), g * LANES:(g + 1) * LANES],
                                      kbias[pl.ds(row0, n_rows), :]], axis=1)
                s = _dot(kc, qcat[h])
                if visible is not None:
                    s = jnp.where(visible, s, -jnp.inf)
                sbuf[h, 0:n_rows, :] = s
                m_blk.append(jnp.max(s, axis=0, keepdims=True))
            for h in range(N_HEADS):
                slab = slice(h * V_SLAB, (h + 1) * V_SLAB)
                m_old = m_s[h:h + 1, :]
                m_new = jnp.maximum(m_old, m_blk[h])
                alpha = jnp.exp2(m_old - m_new)
                p = jnp.exp2(sbuf[h, 0:n_rows, :] - m_new)
                m_s[h:h + 1, :] = m_new
                acc_s[slab, :] = alpha * acc_s[slab, :] + pv(slab, p.astype(BF16))

        key_pass(tok0, 2 * blk,
                 lambda slab, p: (_dot(vb[t, slab, :], p[0:blk])
                                  + _dot(vb[t + 1, slab, :], p[blk:])),
                 krow <= qcol + blk)

        def older_block(j, _):
            key_pass(pl.multiple_of(j * blk, blk), blk,
                     lambda slab, p: _dot(vb[j, slab, :], p), None)
            return 0

        lax.fori_loop(t - n_needed, t, older_block, 0)

        o_t = []
        for h in range(N_HEADS):
            norm = acc_s[h * V_SLAB + HEAD_DIM:h * V_SLAB + HEAD_DIM + 1, :]
            o_t.append(acc_s[h * V_SLAB:h * V_SLAB + HEAD_DIM, :] * (1.0 / norm))
        o_ref[pl.ds(tok0, blk), :] = jnp.concatenate(o_t, axis=0).T.astype(BF16)
        return 0

    lax.fori_loop(0, n_blk, q_block, 0)


def _sample_attn_kernel(q_ref, kn_ref, vn_ref, lf_ref, lft_ref, ckt_ref, cvt_ref, clft_ref,
                        o_ref, crow, *, past, dec):
    blk = ATTN_BLOCK
    n_keys = past + LANES

    upper = _upper_triangle(blk)
    off = jnp.zeros((N_HEADS, 1), F32)
    for j in range(past // blk):
        loc = _cumsum_lanes(clft_ref[:, j * blk:(j + 1) * blk], upper) + off
        crow[:, j * blk:(j + 1) * blk] = loc
        off = loc[:, blk - 1:blk]
    crow[:, 0:past] = crow[:, 0:past] - off

    cq_c = _cumsum_few(lf_ref[:, 0:N_HEADS], axis=0)
    cq_r = _cumsum_few(lft_ref[...], axis=1)
    crow[:, past:] = jnp.full((N_HEADS, LANES), MASKED_BIAS, F32)
    crow[:, past:past + dec] = cq_r

    q = q_ref[...]
    lane_head = lax.broadcasted_iota(jnp.int32, (dec, D_ATTN), 1) // HEAD_DIM
    q_exp = jnp.concatenate(
        [jnp.where(lane_head == h, q, jnp.zeros_like(q)) for h in range(N_HEADS)], axis=0)
    pad_rows = jnp.zeros((LANES - dec, D_ATTN), BF16)
    k_new = jnp.concatenate([kn_ref[...].astype(BF16), pad_rows], axis=0)
    v_new = jnp.concatenate([vn_ref[...].astype(BF16), pad_rows], axis=0)
    s_all = jnp.concatenate(
        [_dot(q_exp, ckt_ref[...].astype(BF16)), _dot_nt(q_exp, k_new)], axis=1)

    kpos = lax.broadcasted_iota(jnp.int32, (dec, n_keys), 1)
    qpos = past + lax.broadcasted_iota(jnp.int32, (dec, n_keys), 0)
    visible = kpos <= qpos
    probs = []
    norms = []
    for h in range(N_HEADS):
        s = s_all[h * dec:(h + 1) * dec, :] + cq_c[:, h:h + 1] - crow[h:h + 1, :]
        s = jnp.where(visible, s, -jnp.inf)
        p = jnp.exp(s - jnp.max(s, axis=-1, keepdims=True))
        norms.append(jnp.sum(p, axis=-1, keepdims=True))
        probs.append(p.astype(BF16))
    p_all = jnp.concatenate(probs, axis=0)
    o_all = (_dot_nt(p_all[:, 0:past], cvt_ref[...].astype(BF16))
             + _dot(p_all[:, past:], v_new))
    out = jnp.zeros((dec, D_ATTN), F32)
    for h in range(N_HEADS):
        o = o_all[h * dec:(h + 1) * dec, :] / norms[h]
        out = out + jnp.where(lane_head == h, o, 0.0)
    o_ref[...] = out.astype(BF16)


def _attn_kernel(*refs, seq, past, dec):
    n_p, n_s = _N_PROMPT_IN, _N_SAMPLE_IN
    p_in, s_in = refs[0:n_p], refs[n_p:n_p + n_s]
    o_ref, os_ref = refs[n_p + n_s:n_p + n_s + 2]
    scratch = refs[n_p + n_s + 2:]
    _sample_attn_kernel(*s_in, os_ref, scratch[-1], past=past, dec=dec)
    _prompt_attn_kernel(*p_in, o_ref, *scratch[:-1], seq=seq)


_N_PROMPT_IN = 9
_N_SAMPLE_IN = 8


def _attention(prompt_in, sample_in):
    qt, kb, kb_meta, vtb, vt_meta, lft, lft_meta, q_gain, k_gain = prompt_in
    q_s, _, _, _, _, cache_kt, _, _ = sample_in
    b, _, seq = qt.shape
    db, dec, _ = q_s.shape
    assert b == db, "one new stream and one running stream per grid step"
    past = cache_kt.shape[2]
    blk = ATTN_BLOCK
    n_blk = seq // blk
    n_pos = (n_blk + 1) * blk
    per_b = lambda *shape: pl.BlockSpec((None,) + shape, lambda i: (i,) + (0,) * len(shape))
    in_specs = [per_b(D_ATTN, seq), per_b(seq, D_ATTN), _const_spec(kb_meta.shape),
                per_b(D_ATTN, seq), _const_spec(vt_meta.shape),
                per_b(N_HEADS, seq), _const_spec(lft_meta.shape),
                _const_spec(q_gain.shape), _const_spec(k_gain.shape),
                per_b(dec, D_ATTN), per_b(dec, D_ATTN), per_b(dec, D_ATTN),
                per_b(dec, LANES), per_b(N_HEADS, dec),
                per_b(D_ATTN, past), per_b(D_ATTN, past), per_b(N_HEADS, past)]
    kern = functools.partial(_attn_kernel, seq=seq, past=past, dec=dec)
    return pl.pallas_call(
        kern, grid=(b,), in_specs=in_specs,
        out_specs=(per_b(seq, D_ATTN), per_b(dec, D_ATTN)),
        out_shape=(jax.ShapeDtypeStruct((b, seq, D_ATTN), BF16),
                   jax.ShapeDtypeStruct((b, dec, D_ATTN), BF16)),
        scratch_shapes=[pltpu.VMEM((n_pos, D_ATTN), BF16),
                        pltpu.VMEM((n_blk + 1, N_HEADS * V_SLAB, blk), BF16),
                        pltpu.VMEM((n_pos, LANES), BF16),
                        pltpu.VMEM((n_blk, N_HEADS, blk), F32),
                        pltpu.VMEM((N_HEADS, LANES), F32),
                        pltpu.VMEM((N_HEADS, 2 * LANES, blk), BF16),
                        pltpu.VMEM((N_HEADS, blk), F32),
                        pltpu.VMEM((N_HEADS * V_SLAB, blk), F32),
                        pltpu.VMEM((N_HEADS, 2 * blk, blk), F32),
                        pltpu.VMEM((N_HEADS, past + LANES), F32)],
        compiler_params=_params(1), name="attention",
    )(*prompt_in, *sample_in)


def _merge_mlp_tail(x, conv_bf16, attn_ref, g_conv, g_attn, wbc_ref, wba_ref, wo_ref, g2_ref,
                    wup_ref, wdn_ref, y_ref):
    merged = g_conv * _dot(conv_bf16, wbc_ref[...]) + g_attn * _dot(attn_ref[...], wba_ref[...])
    h = x + _dot(merged.astype(BF16), wo_ref[...])
    hn = _rms_rows(h, g2_ref[...])
    acc = h
    for c in range(D_FF // D_MODEL):
        cols = slice(c * D_MODEL, (c + 1) * D_MODEL)
        a = jnp.maximum(_dot(hn, wup_ref[:, cols]), 0.0)
        acc = acc + _dot((a * a).astype(BF16), wdn_ref[cols, :])
    y_ref[...] = acc


def _branch_mlp_kernel(x_ref, left_ref, attn_ref, xs_ref, convs_ref, attns_ref, gates_ref,
                       g1_ref, wa_ref, wgl_ref, cw_ref, cb_ref,
                       wbc_ref, wba_ref, wo_ref, g2_ref, wup_ref, wdn_ref,
                       y_ref, zlast_ref, ys_ref, zbuf, *, n_tiles, tiles_per_seq):
    step = pl.program_id(0)

    @pl.when(jnp.logical_and(step % tiles_per_seq == 0, step < n_tiles))
    def _():
        zbuf[:, ZPAD - CONV_HALO:ZPAD, :] = left_ref[...]

    @pl.when(step < n_tiles)
    def _():
        x = x_ref[...]
        xn = _rms_rows(x, g1_ref[...])
        (conv,), (tail,) = _short_conv(xn, wa_ref, cw_ref, cb_ref, zbuf, 1, x.shape[0])
        zlast_ref[0] = tail
        g_conv = jax.nn.sigmoid(_dot_nt(xn, wgl_ref[0:D_MODEL, :]))
        g_attn = jax.nn.sigmoid(_dot_nt(xn, wgl_ref[D_MODEL:2 * D_MODEL, :]))
        _merge_mlp_tail(x, conv.astype(BF16), attn_ref, g_conv, g_attn, wbc_ref, wba_ref,
                        wo_ref, g2_ref, wup_ref, wdn_ref, y_ref)

    @pl.when(step == n_tiles)
    def _():
        _merge_mlp_tail(xs_ref[...], convs_ref[...], attns_ref,
                        gates_ref[:, 0:D_MODEL].astype(F32),
                        gates_ref[:, D_MODEL:2 * D_MODEL].astype(F32),
                        wbc_ref, wba_ref, wo_ref, g2_ref, wup_ref, wdn_ref, ys_ref)


def _branch_mlp(x2d, left, attn, small, wts, *, rows, seq):
    n_rows = x2d.shape[0]
    tiles = seq // rows
    n_tiles = n_rows // rows
    last = n_tiles - 1
    row_spec = lambda width: pl.BlockSpec((rows, width), lambda i: (jnp.minimum(i, last), 0))
    seq_spec = pl.BlockSpec((1, CONV_HALO, D_CONV),
                            lambda i: (jnp.minimum(i, last) // tiles, 0, 0))
    full = lambda a: pl.BlockSpec(a.shape, lambda i: (0,) * a.ndim)
    kern = functools.partial(_branch_mlp_kernel, n_tiles=n_tiles, tiles_per_seq=tiles)
    return pl.pallas_call(
        kern, grid=(n_tiles + 1,),
        in_specs=[row_spec(D_MODEL), seq_spec, row_spec(D_ATTN)] + [full(a) for a in small]
        + [_const_spec(w.shape) for w in wts],
        out_specs=(row_spec(D_MODEL), seq_spec, full(small[0])),
        out_shape=(jax.ShapeDtypeStruct((n_rows, D_MODEL), F32),
                   jax.ShapeDtypeStruct((n_rows // seq, CONV_HALO, D_CONV), F32),
                   jax.ShapeDtypeStruct(small[0].shape, F32)),
        scratch_shapes=[pltpu.VMEM((1, rows + ZPAD, D_CONV), F32)],
        compiler_params=_params(1), name="branch_mlp",
    )(x2d, left, attn, *small, *wts)


def kernel(x_prompt, x_sample, cache_k, cache_v, cache_logf, state_conv, meta,
           norm1_g, w_in, b_f, conv_w, conv_b, q_norm_g, k_norm_g,
           w_br_conv, w_br_attn, w_out, norm2_g, w_up, w_down):
    b, seq, _ = x_prompt.shape
    db, dec, _ = x_sample.shape
    past = cache_k.shape[2]
    length = N_META + seq
    assert dec == N_META, "the small tile treats the meta tokens as one more short sequence"

    wt = w_in[0].T
    assert 2 * HEAD_DIM == LANES, "the q and k gains share one vreg row"
    g1 = norm1_g[0][None, :]
    conv_bias = conv_b[0][None, :]
    gains = jnp.concatenate([q_norm_g[0], k_norm_g[0]])[None, :]
    bf_lanes = jnp.broadcast_to(b_f[0][:, None], (N_HEADS, LANES))

    x_new = x_sample.reshape(db * dec, D_MODEL)
    (conv_s, q_s, k_s, v_s, k_meta, kt_s, vt_s, lft_s, lft_new, lfp_s, gate_s, left_p, zlast_s, wt_conv, wt_qkv, wt_gate, wt_fl,
     bf_row, qg_col, kg_col, conv_taps) = _project_small(
        meta, x_new, state_conv[0], wt, (g1, gains, bf_lanes, conv_w, conv_bias), b)
    qkv_wts = (g1, wt_qkv, wt_fl, bf_row, qg_col, kg_col)
    conv_wts = (conv_taps, conv_bias)

    x_rows = x_prompt.reshape(b * seq, D_MODEL)
    (qt_p, kt_p, vt_p, lftp_p, lft_p, kb_p, vtb_p,
     wbc_b, wba_b, wo_b, wup_b, wdn_b) = _project_qkv(
        x_rows, qkv_wts, (kt_s, vt_s, lft_s),
        (w_br_conv[0], w_br_attn[0], w_out[0], w_up[0], w_down[0]), b=b, seq=seq)
    mlp_wts = (wbc_b, wba_b, wo_b, norm2_g[0][None, :], wup_b, wdn_b)
    k_new = k_s.reshape(db, dec, D_ATTN)
    v_new = v_s.reshape(db, dec, D_ATTN)
    cache_kt = jnp.transpose(cache_k[0], (0, 2, 3, 1)).reshape(db, D_ATTN, past)
    cache_vt = jnp.transpose(cache_v[0], (0, 2, 3, 1)).reshape(db, D_ATTN, past)
    attn_p, attn_s = _attention(
        (qt_p, kb_p.reshape(b, seq, D_ATTN), k_meta, vtb_p, vt_s,
         lft_p, lft_s, q_norm_g, k_norm_g),
        (q_s.reshape(db, dec, D_ATTN), k_new, v_new,
         lfp_s.reshape(db, dec, LANES), lft_new,
         cache_kt, cache_vt, jnp.swapaxes(cache_logf[0], 1, 2)))

    y_prompt, zlast_p, y_sample = _branch_mlp(
        x_rows, left_p, attn_p.reshape(b * seq, D_ATTN),
        (x_new, conv_s, attn_s.reshape(db * dec, D_ATTN), gate_s),
        (g1, wt_conv, wt_gate) + conv_wts + mlp_wts, rows=MLP_ROWS, seq=seq)

    def heads_last(t):
        return jnp.transpose(t.reshape(b, N_HEADS, HEAD_DIM, length), (0, 3, 1, 2))[None]

    return (y_prompt.reshape(b, seq, D_MODEL),
            y_sample.reshape(db, dec, D_MODEL),
            heads_last(kt_p),
            heads_last(vt_p),
            jnp.swapaxes(lftp_p, 1, 2)[None],
            zlast_p[None],
            k_new.reshape(1, db, dec, N_HEADS, HEAD_DIM),
            v_new.reshape(1, db, dec, N_HEADS, HEAD_DIM),
            jnp.swapaxes(lft_new, 1, 2)[None],
            zlast_s[None])
```

```python
import functools

import jax
import jax.numpy as jnp
import numpy as np
from jax import lax
from jax.experimental import pallas as pl
from jax.experimental.pallas import tpu as pltpu

D_MODEL = 1024
D_CONV = D_MODEL // 2
CONV_W = 3
N_HEADS = 8
HEAD_DIM = 64
D_ATTN = N_HEADS * HEAD_DIM
D_FF = 4 * D_MODEL
N_META = 16
EPS = 1e-6
ATTN_SCALE = HEAD_DIM ** -0.5

F32 = jnp.float32
BF16 = jnp.bfloat16

VMEM_LIMIT_BYTES = 56 * 1024 * 1024
LANES = 128
SUBLANES = 8
BF16_ROWS = 16
PROJ_ROWS = 1024
MLP_ROWS = 512
ATTN_BLOCK = 256
MASKED_BIAS = 1e30
CONV_HALO = CONV_W - 1
ZPAD = SUBLANES
N_SPLIT = 3
V_SLAB = HEAD_DIM + BF16_ROWS
LOG2E = 1.4426950408889634
SKIP_LOG2 = 40.0
NORM_SLACK = 1.02


def _dot(a, b):
    return jnp.dot(a, b, preferred_element_type=F32)


def _dot_nt(a, b):
    return lax.dot_general(a, b, (((1,), (1,)), ((), ())), preferred_element_type=F32)


def _log_sigmoid(x):
    return jnp.minimum(x, 0.0) - jnp.log1p(jnp.exp(-jnp.abs(x)))


def _cumsum_few(x, axis):
    n = x.shape[axis]
    idx = lax.broadcasted_iota(jnp.int32, x.shape, axis)
    out = jnp.zeros(x.shape, F32)
    for i in range(n):
        term = x[i:i + 1, :] if axis == 0 else x[:, i:i + 1]
        out = out + jnp.where(idx >= i, term, 0.0)
    return out


def _upper_triangle(n):
    r = lax.broadcasted_iota(jnp.int32, (n, n), 0)
    c = lax.broadcasted_iota(jnp.int32, (n, n), 1)
    return jnp.where(r <= c, 1.0, 0.0).astype(BF16)


def _split3(c):
    hi = c.astype(BF16).astype(F32)
    r1 = c - hi
    mid = r1.astype(BF16).astype(F32)
    return hi, mid, r1 - mid


def _cumsum_lanes(x, tri_upper):
    h = x.shape[0]
    pieces = jnp.concatenate(_split3(x), axis=0).astype(BF16)
    y = _dot(pieces, tri_upper)
    return y[0:h] + y[h:2 * h] + y[2 * h:3 * h]


def _const_spec(shape):
    nd = len(shape)
    return pl.BlockSpec(shape, lambda *_: (0,) * nd, pipeline_mode=pl.Buffered(1))


def _params(n_axes):
    return pltpu.CompilerParams(
        dimension_semantics=("arbitrary",) * n_axes,
        vmem_limit_bytes=VMEM_LIMIT_BYTES)


def _rms_rows(x, g_row):
    ms = jnp.mean(x * x, axis=-1, keepdims=True)
    return (x * lax.rsqrt(ms + EPS) * g_row).astype(BF16)


def _head_norm_t(ut, g_col):
    out = []
    for h in range(N_HEADS):
        blk = ut[h * HEAD_DIM:(h + 1) * HEAD_DIM, :]
        ms = jnp.mean(blk * blk, axis=0, keepdims=True)
        out.append(blk * lax.rsqrt(ms + EPS) * g_col[h * HEAD_DIM:(h + 1) * HEAD_DIM, :])
    return jnp.concatenate(out, axis=0)


def _short_conv(xn, wa_ref, cw_ref, cb_ref, zbuf, n_seg, seg_len):
    cb = _dot_nt(xn, wa_ref[0:D_CONV, :])
    z = (_dot_nt(xn, wa_ref[D_CONV:2 * D_CONV, :])
         * _dot_nt(xn, wa_ref[2 * D_CONV:3 * D_CONV, :]))
    out, tails = [], []
    for s in range(n_seg):
        r0 = s * seg_len
        zs = z[r0:r0 + seg_len]
        zbuf[s, ZPAD:ZPAD + seg_len, :] = zs
        y = None
        for i in range(CONV_W):
            lo = ZPAD - CONV_HALO + i
            tap = zs if i == CONV_HALO else zbuf[s, lo:lo + seg_len, :]
            term = tap * cw_ref[i:i + 1, :]
            y = term if y is None else y + term
        out.append(cb[r0:r0 + seg_len] * (y + cb_ref[...]))
        tail = zbuf[s, ZPAD + seg_len - CONV_HALO:ZPAD + seg_len, :]
        tails.append(tail)
        zbuf[s, ZPAD - CONV_HALO:ZPAD, :] = tail
    return out, tails


def _qkv_kernel(*refs, tiles_per_seq, n_cast):
    (x_ref, g1_ref, wqkv_ref, wfl_ref, bfr_ref, qgc_ref, kgc_ref,
     ktm_ref, vtm_ref, lftm_ref) = refs[:10]
    cast_in = refs[10:10 + n_cast]
    (qt_ref, kt_ref, vt_ref, lftp_ref, lft_ref, kb_ref,
     vtb_ref) = refs[10 + n_cast:17 + n_cast]
    cast_out = refs[17 + n_cast:17 + 2 * n_cast]
    kcar, vcar, lcar = refs[17 + 2 * n_cast:]
    rows = x_ref.shape[0]
    step = pl.program_id(0) % tiles_per_seq

    def tile_body(tile_idx):
        lane0 = tile_idx * rows

        def shifted_store(out_ref, car_ref, meta_ref, tile):
            left = meta_ref[:, 0:LANES] if tile_idx == 0 else car_ref[...]
            rolled = pltpu.roll(tile, N_META, axis=1)
            lane = lax.broadcasted_iota(jnp.int32, (tile.shape[0], LANES), 1)
            out_ref[:, lane0:lane0 + LANES] = jnp.where(lane < N_META, left, rolled[:, 0:LANES])
            out_ref[:, lane0 + LANES:lane0 + rows] = rolled[:, LANES:]
            if tile_idx < tiles_per_seq - 1:
                car_ref[...] = rolled[:, 0:LANES]
            else:
                out_ref[:, tiles_per_seq * rows:] = rolled[:, 0:N_META]

        xn = _rms_rows(x_ref[...], g1_ref[...])

        def feature_major(j):
            return _dot_nt(wqkv_ref[j * D_ATTN:(j + 1) * D_ATTN, :], xn)

        lfp = _log_sigmoid(_dot_nt(xn, wfl_ref[...]) + bfr_ref[...])
        lft = lfp.T[0:N_HEADS, :]
        lft_ref[...] = lft
        qt = _head_norm_t(feature_major(0), qgc_ref[...])
        qt_ref[...] = (qt * (ATTN_SCALE * LOG2E)).astype(BF16)
        kt = _head_norm_t(feature_major(1), kgc_ref[...])
        vt = feature_major(2)
        kb_ref[...] = kt.T.astype(BF16)
        vtb_ref[...] = vt.astype(BF16)
        shifted_store(kt_ref, kcar, ktm_ref, kt)
        shifted_store(vt_ref, vcar, vtm_ref, vt)
        shifted_store(lftp_ref, lcar, lftm_ref, lft)
        for src, dst in zip(cast_in, cast_out):
            dst[...] = src[...].astype(BF16)

    for tile_idx in range(tiles_per_seq):
        pl.when(step == tile_idx)(functools.partial(tile_body, tile_idx))


def _project_qkv(x2d, wts, meta_cols, f32_wts, *, b, seq):
    rows = PROJ_ROWS
    tiles = seq // rows
    length = N_META + seq
    n_steps = b * tiles
    chunk_spec = lambda w: pl.BlockSpec((w.shape[0] // n_steps, w.shape[1]), lambda i: (i, 0))
    row_spec = lambda width: pl.BlockSpec((rows, width), lambda i: (i, 0))
    col_spec = lambda feat: pl.BlockSpec(
        (None, feat, rows), lambda i: (i // tiles, 0, i % tiles))
    seq_spec = lambda feat: pl.BlockSpec((None, feat, length), lambda i: (i // tiles, 0, 0))
    out_shape = (
        jax.ShapeDtypeStruct((b, D_ATTN, seq), BF16),
        jax.ShapeDtypeStruct((b, D_ATTN, length), F32),
        jax.ShapeDtypeStruct((b, D_ATTN, length), F32),
        jax.ShapeDtypeStruct((b, N_HEADS, length), F32),
        jax.ShapeDtypeStruct((b, N_HEADS, seq), F32),
        jax.ShapeDtypeStruct((b * seq, D_ATTN), BF16),
        jax.ShapeDtypeStruct((b, D_ATTN, seq), BF16),
    )
    out_specs = (col_spec(D_ATTN), seq_spec(D_ATTN), seq_spec(D_ATTN), seq_spec(N_HEADS),
                 col_spec(N_HEADS), row_spec(D_ATTN), col_spec(D_ATTN))
    out_shape = out_shape + tuple(jax.ShapeDtypeStruct(w.shape, BF16) for w in f32_wts)
    out_specs = out_specs + tuple(chunk_spec(w) for w in f32_wts)
    kern = functools.partial(_qkv_kernel, tiles_per_seq=tiles, n_cast=len(f32_wts))
    return pl.pallas_call(
        kern, grid=(n_steps,),
        in_specs=[row_spec(D_MODEL)] + [_const_spec(w.shape) for w in wts + meta_cols]
        + [chunk_spec(w) for w in f32_wts],
        out_specs=out_specs, out_shape=out_shape,
        scratch_shapes=[pltpu.VMEM((D_ATTN, LANES), F32),
                        pltpu.VMEM((D_ATTN, LANES), F32),
                        pltpu.VMEM((N_HEADS, LANES), F32)],
        compiler_params=_params(1), name="proj_qkv",
    )(x2d, *wts, *meta_cols, *f32_wts)


def _proj_small_kernel(meta_ref, xs_ref, state_ref, wt_ref, g1_ref, gains_ref, bfb_ref,
                       cwin_ref, cb_ref, bd_ref,
                       conv_ref, q_ref, k_ref, v_ref, km_ref, kt_ref, vt_ref,
                       lft_ref, lftn_ref, lfpn_ref,
                       gate_ref, leftp_ref, zlast_ref,
                       wa_ref, wqkv_ref, wgl_ref, wfl_ref,
                       bfr_ref, qgc_ref, kgc_ref, cw_ref, zbuf):
    n_new, _, dec = lftn_ref.shape
    n_main = 3 * D_CONV + 3 * D_ATTN
    wa_ref[...] = wt_ref[0:3 * D_CONV, :].astype(BF16)
    wqkv_ref[...] = wt_ref[3 * D_CONV:n_main, :].astype(BF16)
    wgl_ref[...] = wt_ref[n_main + N_HEADS:, :].astype(BF16)
    w_fl = wt_ref[n_main:n_main + N_HEADS, :]
    wfl_ref[...] = jnp.concatenate([w_fl] * (LANES // N_HEADS), axis=0).astype(BF16)
    wflt_ref = wfl_ref.at[0:BF16_ROWS]

    qk = jnp.broadcast_to(gains_ref[...], (SUBLANES, LANES))
    kq = pltpu.roll(qk, HEAD_DIM, axis=1)
    first = lax.broadcasted_iota(jnp.int32, (SUBLANES, LANES), 1) < HEAD_DIM

    def per_feature(two_heads):
        rows = jnp.concatenate([two_heads] * (D_ATTN // LANES), axis=1)
        return rows[0:1, :], rows.T[:, 0:1]

    qgr, qgc = per_feature(jnp.where(first, qk, kq))
    kgr, kgc = per_feature(jnp.where(first, kq, qk))
    qgc_ref[...] = qgc
    kgc_ref[...] = kgc
    bfc = bfb_ref[:, 0:1]
    bfr_ref[...] = jnp.concatenate([bfb_ref[...]] * (LANES // N_HEADS), axis=0).T[0:1, :]
    cw_ref[...] = cwin_ref[0]

    xn = _rms_rows(jnp.concatenate([meta_ref[...], xs_ref[...]], axis=0), g1_ref[...])
    xn_new = xn[N_META:]
    zbuf[0, ZPAD - CONV_HALO:ZPAD, :] = jnp.zeros((CONV_HALO, D_CONV), F32)
    zbuf[1:, ZPAD - CONV_HALO:ZPAD, :] = state_ref[...]
    conv, tails = _short_conv(xn, wa_ref, cw_ref, cb_ref, zbuf, n_new + 1, dec)
    for s in range(n_new):
        conv_ref[s * dec:(s + 1) * dec, :] = conv[s + 1].astype(BF16)
        zlast_ref[s] = tails[s + 1]
    for i in range(leftp_ref.shape[0]):
        leftp_ref[i] = tails[0]

    def rows_major(j, rows):
        return _dot_nt(rows, wqkv_ref[j * D_ATTN:(j + 1) * D_ATTN, :])

    def feature_major(j):
        return _dot_nt(wqkv_ref[j * D_ATTN:(j + 1) * D_ATTN, :], xn)

    def head_norm(u, g_row):
        ssq = _dot((u * u).astype(BF16), bd_ref[...])
        return u * lax.rsqrt(ssq * (1.0 / HEAD_DIM) + EPS) * g_row

    q_ref[...] = (head_norm(rows_major(0, xn_new), qgr) * ATTN_SCALE).astype(BF16)
    k = head_norm(rows_major(1, xn), kgr)
    km_ref[...] = k[0:N_META].astype(BF16)
    k_ref[...] = k[N_META:]
    v_ref[...] = rows_major(2, xn_new)
    kt_ref[...] = _head_norm_t(feature_major(1), kgc_ref[...])
    vt_ref[...] = feature_major(2)
    lfpn_ref[...] = _log_sigmoid(_dot_nt(xn_new, wfl_ref[...]) + bfr_ref[...])
    lft = _log_sigmoid(_dot_nt(wflt_ref[...], xn)[0:N_HEADS] + bfc)
    lft_ref[...] = lft
    for s in range(n_new):
        lftn_ref[s] = lft[:, N_META + s * dec:N_META + (s + 1) * dec]
    gate_ref[...] = jax.nn.sigmoid(_dot_nt(xn_new, wgl_ref[...])).astype(BF16)


def _project_small(meta, x_new, state, wt, small_params, n_prompt):
    n_new = state.shape[0]
    n_rows = x_new.shape[0]
    dec = n_rows // n_new
    n_all = N_META + n_rows
    n_main = 3 * D_CONV + 3 * D_ATTN
    head_of = np.arange(D_ATTN) // HEAD_DIM
    same_head = jnp.asarray(head_of[:, None] == head_of[None, :], BF16)
    full = lambda *shape: pl.BlockSpec(shape, lambda i: (0,) * len(shape))
    out_shape = (
        jax.ShapeDtypeStruct((n_rows, D_CONV), BF16),
        jax.ShapeDtypeStruct((n_rows, D_ATTN), BF16),
        jax.ShapeDtypeStruct((n_rows, D_ATTN), F32),
        jax.ShapeDtypeStruct((n_rows, D_ATTN), F32),
        jax.ShapeDtypeStruct((N_META, D_ATTN), BF16),
        jax.ShapeDtypeStruct((D_ATTN, n_all), F32),
        jax.ShapeDtypeStruct((D_ATTN, n_all), F32),
        jax.ShapeDtypeStruct((N_HEADS, n_all), F32),
        jax.ShapeDtypeStruct((n_new, N_HEADS, dec), F32),
        jax.ShapeDtypeStruct((n_rows, LANES), F32),
        jax.ShapeDtypeStruct((n_rows, 2 * D_MODEL), BF16),
        jax.ShapeDtypeStruct((n_prompt, CONV_HALO, D_CONV), F32),
        jax.ShapeDtypeStruct((n_new, CONV_HALO, D_CONV), F32),
        jax.ShapeDtypeStruct((3 * D_CONV, D_MODEL), BF16),
        jax.ShapeDtypeStruct((3 * D_ATTN, D_MODEL), BF16),
        jax.ShapeDtypeStruct((wt.shape[0] - n_main - N_HEADS, D_MODEL), BF16),
        jax.ShapeDtypeStruct((LANES, D_MODEL), BF16),
        jax.ShapeDtypeStruct((1, LANES), F32),
        jax.ShapeDtypeStruct((D_ATTN, 1), F32),
        jax.ShapeDtypeStruct((D_ATTN, 1), F32),
        jax.ShapeDtypeStruct((CONV_W, D_CONV), F32),
    )
    ins = (meta, x_new, state, wt) + tuple(small_params) + (same_head,)
    return pl.pallas_call(
        _proj_small_kernel, grid=(1,),
        in_specs=[full(*meta.shape), full(*x_new.shape), full(*state.shape)]
        + [_const_spec(w.shape) for w in ins[3:]],
        out_specs=tuple(full(*s.shape) for s in out_shape), out_shape=out_shape,
        scratch_shapes=[pltpu.VMEM((n_new + 1, dec + ZPAD, D_CONV), F32)],
        compiler_params=_params(1), name="proj_small",
    )(*ins)


def _prompt_attn_kernel(qt_ref, kb_ref, kbm_ref, vtb_ref, vtm_ref, lft_ref, lftm_ref,
                        qg_ref, kg_ref, o_ref, kpos, vb, kbias, crow, cend, qcat,
                        m_s, acc_s, sbuf, *, seq):
    blk = ATTN_BLOCK
    n_blk = seq // blk
    n_bias = N_SPLIT * N_HEADS
    pad = blk - N_META

    kpos[0:pad, :] = jnp.zeros((pad, D_ATTN), BF16)
    kpos[pad:blk, :] = kbm_ref[...]
    kpos[blk:, :] = kb_ref[...]

    ones_row = (lax.broadcasted_iota(jnp.int32, (V_SLAB - HEAD_DIM, blk), 0) == 0).astype(BF16)
    lane_m = lax.broadcasted_iota(jnp.int32, (HEAD_DIM, LANES), 1)
    vb[0] = jnp.zeros((N_HEADS * V_SLAB, blk), BF16)
    for h in range(N_HEADS):
        rows = slice(h * HEAD_DIM, (h + 1) * HEAD_DIM)
        slab = slice(h * V_SLAB, h * V_SLAB + HEAD_DIM)
        meta = jnp.where(lane_m < N_META, vtm_ref[rows, 0:LANES], 0.0)
        vb[0, slab, blk - LANES:] = pltpu.roll(meta, LANES - N_META, axis=1).astype(BF16)
        for j in range(n_blk):
            vb[j + 1, slab, :] = vtb_ref[rows, j * blk:(j + 1) * blk]
        for j in range(n_blk + 1):
            vb[j, h * V_SLAB + HEAD_DIM:(h + 1) * V_SLAB, :] = ones_row

    qk_bound = (NORM_SLACK * HEAD_DIM * ATTN_SCALE * LOG2E
                * jnp.max(jnp.abs(qg_ref[...]), axis=1, keepdims=True)
                * jnp.max(jnp.abs(kg_ref[...]), axis=1, keepdims=True))

    upper = _upper_triangle(blk)
    ones_zeros = jnp.concatenate([jnp.ones((n_bias, blk), F32),
                                  jnp.zeros((LANES - 2 * n_bias, blk), F32)], axis=0)

    def store_kbias(j, c_log2, is_pad=None):
        neg = [-piece for piece in _split3(c_log2)]
        if is_pad is not None:
            neg = [jnp.where(is_pad, fill, piece)
                   for piece, fill in zip(neg, (-MASKED_BIAS, 0.0, 0.0))]
        kbias[j * blk:(j + 1) * blk, :] = jnp.concatenate(neg + [ones_zeros], axis=0).T.astype(BF16)

    lane_h = lax.broadcasted_iota(jnp.int32, (N_HEADS, LANES), 1)
    lf_meta = jnp.where(lane_h < N_META, lftm_ref[:, 0:LANES], 0.0)
    lf_blk0 = jnp.concatenate([jnp.zeros((N_HEADS, blk - LANES), F32),
                               pltpu.roll(lf_meta, LANES - N_META, axis=1)], axis=1)
    c_row = _cumsum_lanes(lf_blk0, upper)
    store_kbias(0, c_row * LOG2E, lax.broadcasted_iota(jnp.int32, (N_HEADS, blk), 1) < pad)
    off_r = c_row[:, blk - 1:blk]
    c_end = jnp.where(lane_h == 0, off_r * LOG2E, 0.0)
    for j in range(n_blk):
        c_row = _cumsum_lanes(lft_ref[:, j * blk:(j + 1) * blk], upper) + off_r
        crow[j] = c_row * LOG2E
        store_kbias(j + 1, c_row * LOG2E)
        off_r = c_row[:, blk - 1:blk]
        c_end = jnp.where(lane_h == j + 1, off_r * LOG2E, c_end)
    cend[...] = c_end

    row128 = lax.broadcasted_iota(jnp.int32, (LANES, blk), 0)
    krow = lax.broadcasted_iota(jnp.int32, (2 * blk, blk), 0)
    qcol = lax.broadcasted_iota(jnp.int32, (2 * blk, blk), 1)

    def q_block(t, _):
        tok0 = pl.multiple_of(t * blk, blk)
        c_q = crow[t]
        hi, mid, lo = _split3(c_q)
        bias_rows = jnp.concatenate(
            [jnp.ones((n_bias, blk), F32), hi, mid, lo,
             jnp.zeros((LANES - 2 * n_bias, blk), F32)], axis=0)
        for h in range(N_HEADS):
            pair = qt_ref[(h // 2) * LANES:(h // 2 + 1) * LANES, pl.ds(tok0, blk)]
            in_head = (row128 // HEAD_DIM) == (h % 2)
            qcat[h, 0:LANES, :] = jnp.where(in_head, pair, jnp.zeros_like(pair))
            qcat[h, LANES:, :] = jnp.where(row128 % N_HEADS == h, bias_rows, 0.0).astype(BF16)
        m_s[...] = jnp.full(m_s.shape, -jnp.inf, F32)
        acc_s[...] = jnp.zeros(acc_s.shape, F32)

        gap = 2.0 * qk_bound + c_q[:, 0:1] - cend[...]
        needed = jnp.logical_and(gap >= -SKIP_LOG2, lane_h < t)
        n_needed = jnp.max(
            jnp.sum(jnp.where(needed, 1.0, 0.0), axis=1, keepdims=True)).astype(jnp.int32)

        def key_pass(row0, n_rows, pv, visible):
            m_blk = []
            for h in range(N_HEADS):
                g = h // 2
                kc = jnp.concatenate([kpos[pl.ds(row0, n_rows), g * LANES:(g + 1) * LANES],
                                      kbias[pl.ds(row0, n_rows), :]], axis=1)
                s = _dot(kc, qcat[h])
                if visible is not None:
                    s = jnp.where(visible, s, -jnp.inf)
                sbuf[h, 0:n_rows, :] = s
                m_blk.append(jnp.max(s, axis=0, keepdims=True))
            for h in range(N_HEADS):
                slab = slice(h * V_SLAB, (h + 1) * V_SLAB)
                m_old = m_s[h:h + 1, :]
                m_new = jnp.maximum(m_old, m_blk[h])
                alpha = jnp.exp2(m_old - m_new)
                p = jnp.exp2(sbuf[h, 0:n_rows, :] - m_new)
                m_s[h:h + 1, :] = m_new
                acc_s[slab, :] = alpha * acc_s[slab, :] + pv(slab, p.astype(BF16))

        key_pass(tok0, 2 * blk,
                 lambda slab, p: (_dot(vb[t, slab, :], p[0:blk])
                                  + _dot(vb[t + 1, slab, :], p[blk:])),
                 krow <= qcol + blk)

        def older_block(j, _):
            key_pass(pl.multiple_of(j * blk, blk), blk,
                     lambda slab, p: _dot(vb[j, slab, :], p), None)
            return 0

        lax.fori_loop(t - n_needed, t, older_block, 0)

        o_t = []
        for h in range(N_HEADS):
            norm = acc_s[h * V_SLAB + HEAD_DIM:h * V_SLAB + HEAD_DIM + 1, :]
            o_t.append(acc_s[h * V_SLAB:h * V_SLAB + HEAD_DIM, :] * (1.0 / norm))
        o_ref[pl.ds(tok0, blk), :] = jnp.concatenate(o_t, axis=0).T.astype(BF16)
        return 0

    lax.fori_loop(0, n_blk, q_block, 0)


def _sample_attn_kernel(q_ref, kn_ref, vn_ref, lf_ref, lft_ref, ckt_ref, cvt_ref, clft_ref,
                        o_ref, crow, *, past, dec):
    blk = ATTN_BLOCK
    n_keys = past + LANES

    upper = _upper_triangle(blk)
    off = jnp.zeros((N_HEADS, 1), F32)
    for j in range(past // blk):
        loc = _cumsum_lanes(clft_ref[:, j * blk:(j + 1) * blk], upper) + off
        crow[:, j * blk:(j + 1) * blk] = loc
        off = loc[:, blk - 1:blk]
    crow[:, 0:past] = crow[:, 0:past] - off

    cq_c = _cumsum_few(lf_ref[:, 0:N_HEADS], axis=0)
    cq_r = _cumsum_few(lft_ref[...], axis=1)
    crow[:, past:] = jnp.full((N_HEADS, LANES), MASKED_BIAS, F32)
    crow[:, past:past + dec] = cq_r

    q = q_ref[...]
    lane_head = lax.broadcasted_iota(jnp.int32, (dec, D_ATTN), 1) // HEAD_DIM
    q_exp = jnp.concatenate(
        [jnp.where(lane_head == h, q, jnp.zeros_like(q)) for h in range(N_HEADS)], axis=0)
    pad_rows = jnp.zeros((LANES - dec, D_ATTN), BF16)
    k_new = jnp.concatenate([kn_ref[...].astype(BF16), pad_rows], axis=0)
    v_new = jnp.concatenate([vn_ref[...].astype(BF16), pad_rows], axis=0)
    s_all = jnp.concatenate(
        [_dot(q_exp, ckt_ref[...].astype(BF16)), _dot_nt(q_exp, k_new)], axis=1)

    kpos = lax.broadcasted_iota(jnp.int32, (dec, n_keys), 1)
    qpos = past + lax.broadcasted_iota(jnp.int32, (dec, n_keys), 0)
    visible = kpos <= qpos
    probs = []
    norms = []
    for h in range(N_HEADS):
        s = s_all[h * dec:(h + 1) * dec, :] + cq_c[:, h:h + 1] - crow[h:h + 1, :]
        s = jnp.where(visible, s, -jnp.inf)
        p = jnp.exp(s - jnp.max(s, axis=-1, keepdims=True))
        norms.append(jnp.sum(p, axis=-1, keepdims=True))
        probs.append(p.astype(BF16))
    p_all = jnp.concatenate(probs, axis=0)
    o_all = (_dot_nt(p_all[:, 0:past], cvt_ref[...].astype(BF16))
             + _dot(p_all[:, past:], v_new))
    out = jnp.zeros((dec, D_ATTN), F32)
    for h in range(N_HEADS):
        o = o_all[h * dec:(h + 1) * dec, :] / norms[h]
        out = out + jnp.where(lane_head == h, o, 0.0)
    o_ref[...] = out.astype(BF16)


def _attn_kernel(*refs, seq, past, dec, n_cast):
    n_p, n_s = _N_PROMPT_IN, _N_SAMPLE_IN
    n_in = n_p + n_s + n_cast
    p_in, s_in, cast_in = refs[0:n_p], refs[n_p:n_p + n_s], refs[n_p + n_s:n_in]
    o_ref, os_ref = refs[n_in:n_in + 2]
    cast_out = refs[n_in + 2:n_in + 2 + n_cast]
    scratch = refs[n_in + 2 + n_cast:]
    for src, dst in zip(cast_in, cast_out):
        dst[...] = src[...].astype(BF16)
    _sample_attn_kernel(*s_in, os_ref, scratch[-1], past=past, dec=dec)
    _prompt_attn_kernel(*p_in, o_ref, *scratch[:-1], seq=seq)


_N_PROMPT_IN = 9
_N_SAMPLE_IN = 8


def _attention(prompt_in, sample_in, f32_wts):
    qt, kb, kb_meta, vtb, vt_meta, lft, lft_meta, q_gain, k_gain = prompt_in
    q_s, _, _, _, _, cache_kt, _, _ = sample_in
    b, _, seq = qt.shape
    db, dec, _ = q_s.shape
    assert b == db, "one new stream and one running stream per grid step"
    past = cache_kt.shape[2]
    blk = ATTN_BLOCK
    n_blk = seq // blk
    n_pos = (n_blk + 1) * blk
    per_b = lambda *shape: pl.BlockSpec((None,) + shape, lambda i: (i,) + (0,) * len(shape))
    chunk_spec = lambda w: pl.BlockSpec((w.shape[0] // b, w.shape[1]), lambda i: (i, 0))
    in_specs = [per_b(D_ATTN, seq), per_b(seq, D_ATTN), _const_spec(kb_meta.shape),
                per_b(D_ATTN, seq), _const_spec(vt_meta.shape),
                per_b(N_HEADS, seq), _const_spec(lft_meta.shape),
                _const_spec(q_gain.shape), _const_spec(k_gain.shape),
                per_b(dec, D_ATTN), per_b(dec, D_ATTN), per_b(dec, D_ATTN),
                per_b(dec, LANES), per_b(N_HEADS, dec),
                per_b(D_ATTN, past), per_b(D_ATTN, past), per_b(N_HEADS, past)]
    in_specs += [chunk_spec(w) for w in f32_wts]
    kern = functools.partial(_attn_kernel, seq=seq, past=past, dec=dec, n_cast=len(f32_wts))
    return pl.pallas_call(
        kern, grid=(b,), in_specs=in_specs,
        out_specs=(per_b(seq, D_ATTN), per_b(dec, D_ATTN))
        + tuple(chunk_spec(w) for w in f32_wts),
        out_shape=(jax.ShapeDtypeStruct((b, seq, D_ATTN), BF16),
                   jax.ShapeDtypeStruct((b, dec, D_ATTN), BF16))
        + tuple(jax.ShapeDtypeStruct(w.shape, BF16) for w in f32_wts),
        scratch_shapes=[pltpu.VMEM((n_pos, D_ATTN), BF16),
                        pltpu.VMEM((n_blk + 1, N_HEADS * V_SLAB, blk), BF16),
                        pltpu.VMEM((n_pos, LANES), BF16),
                        pltpu.VMEM((n_blk, N_HEADS, blk), F32),
                        pltpu.VMEM((N_HEADS, LANES), F32),
                        pltpu.VMEM((N_HEADS, 2 * LANES, blk), BF16),
                        pltpu.VMEM((N_HEADS, blk), F32),
                        pltpu.VMEM((N_HEADS * V_SLAB, blk), F32),
                        pltpu.VMEM((N_HEADS, 2 * blk, blk), F32),
                        pltpu.VMEM((N_HEADS, past + LANES), F32)],
        compiler_params=_params(1), name="attention",
    )(*prompt_in, *sample_in, *f32_wts)


def _merge_mlp_tail(x, conv_bf16, attn_ref, g_conv, g_attn, wbc_ref, wba_ref, wo_ref, g2_ref,
                    wup_ref, wdn_ref, y_ref):
    merged = g_conv * _dot(conv_bf16, wbc_ref[...]) + g_attn * _dot(attn_ref[...], wba_ref[...])
    h = x + _dot(merged.astype(BF16), wo_ref[...])
    hn = _rms_rows(h, g2_ref[...])
    acc = h
    for c in range(D_FF // D_MODEL):
        cols = slice(c * D_MODEL, (c + 1) * D_MODEL)
        a = jnp.maximum(_dot(hn, wup_ref[:, cols]), 0.0)
        acc = acc + _dot((a * a).astype(BF16), wdn_ref[cols, :])
    y_ref[...] = acc


def _branch_mlp_kernel(x_ref, left_ref, attn_ref, xs_ref, convs_ref, attns_ref, gates_ref,
                       g1_ref, wa_ref, wgl_ref, cw_ref, cb_ref,
                       wbc_ref, wba_ref, wo_ref, g2_ref, wup_ref, wdn_ref,
                       y_ref, zlast_ref, ys_ref, zbuf, *, n_tiles, tiles_per_seq):
    step = pl.program_id(0)

    @pl.when(jnp.logical_and(step % tiles_per_seq == 0, step < n_tiles))
    def _():
        zbuf[:, ZPAD - CONV_HALO:ZPAD, :] = left_ref[...]

    @pl.when(step < n_tiles)
    def _():
        x = x_ref[...]
        xn = _rms_rows(x, g1_ref[...])
        (conv,), (tail,) = _short_conv(xn, wa_ref, cw_ref, cb_ref, zbuf, 1, x.shape[0])
        zlast_ref[0] = tail
        g_conv = jax.nn.sigmoid(_dot_nt(xn, wgl_ref[0:D_MODEL, :]))
        g_attn = jax.nn.sigmoid(_dot_nt(xn, wgl_ref[D_MODEL:2 * D_MODEL, :]))
        _merge_mlp_tail(x, conv.astype(BF16), attn_ref, g_conv, g_attn, wbc_ref, wba_ref,
                        wo_ref, g2_ref, wup_ref, wdn_ref, y_ref)

    @pl.when(step == n_tiles)
    def _():
        _merge_mlp_tail(xs_ref[...], convs_ref[...], attns_ref,
                        gates_ref[:, 0:D_MODEL].astype(F32),
                        gates_ref[:, D_MODEL:2 * D_MODEL].astype(F32),
                        wbc_ref, wba_ref, wo_ref, g2_ref, wup_ref, wdn_ref, ys_ref)


def _branch_mlp(x2d, left, attn, small, wts, *, rows, seq):
    n_rows = x2d.shape[0]
    tiles = seq // rows
    n_tiles = n_rows // rows
    last = n_tiles - 1
    row_spec = lambda width: pl.BlockSpec((rows, width), lambda i: (jnp.minimum(i, last), 0))
    seq_spec = pl.BlockSpec((1, CONV_HALO, D_CONV),
                            lambda i: (jnp.minimum(i, last) // tiles, 0, 0))
    full = lambda a: pl.BlockSpec(a.shape, lambda i: (0,) * a.ndim)
    kern = functools.partial(_branch_mlp_kernel, n_tiles=n_tiles, tiles_per_seq=tiles)
    return pl.pallas_call(
        kern, grid=(n_tiles + 1,),
        in_specs=[row_spec(D_MODEL), seq_spec, row_spec(D_ATTN)] + [full(a) for a in small]
        + [_const_spec(w.shape) for w in wts],
        out_specs=(row_spec(D_MODEL), seq_spec, full(small[0])),
        out_shape=(jax.ShapeDtypeStruct((n_rows, D_MODEL), F32),
                   jax.ShapeDtypeStruct((n_rows // seq, CONV_HALO, D_CONV), F32),
                   jax.ShapeDtypeStruct(small[0].shape, F32)),
        scratch_shapes=[pltpu.VMEM((1, rows + ZPAD, D_CONV), F32)],
        compiler_params=_params(1), name="branch_mlp",
    )(x2d, left, attn, *small, *wts)


def kernel(x_prompt, x_sample, cache_k, cache_v, cache_logf, state_conv, meta,
           norm1_g, w_in, b_f, conv_w, conv_b, q_norm_g, k_norm_g,
           w_br_conv, w_br_attn, w_out, norm2_g, w_up, w_down):
    b, seq, _ = x_prompt.shape
    db, dec, _ = x_sample.shape
    past = cache_k.shape[2]
    length = N_META + seq
    assert dec == N_META, "the small tile treats the meta tokens as one more short sequence"

    wt = w_in[0].T
    assert 2 * HEAD_DIM == LANES, "the q and k gains share one vreg row"
    g1 = norm1_g[0][None, :]
    conv_bias = conv_b[0][None, :]
    gains = jnp.concatenate([q_norm_g[0], k_norm_g[0]])[None, :]
    bf_lanes = jnp.broadcast_to(b_f[0][:, None], (N_HEADS, LANES))

    x_new = x_sample.reshape(db * dec, D_MODEL)
    (conv_s, q_s, k_s, v_s, k_meta, kt_s, vt_s, lft_s, lft_new, lfp_s, gate_s, left_p, zlast_s, wt_conv, wt_qkv, wt_gate, wt_fl,
     bf_row, qg_col, kg_col, conv_taps) = _project_small(
        meta, x_new, state_conv[0], wt, (g1, gains, bf_lanes, conv_w, conv_bias), b)
    qkv_wts = (g1, wt_qkv, wt_fl, bf_row, qg_col, kg_col)
    conv_wts = (conv_taps, conv_bias)

    x_rows = x_prompt.reshape(b * seq, D_MODEL)
    (qt_p, kt_p, vt_p, lftp_p, lft_p, kb_p, vtb_p,
     wbc_b, wba_b, wo_b, wdn_b) = _project_qkv(
        x_rows, qkv_wts, (kt_s, vt_s, lft_s),
        (w_br_conv[0], w_br_attn[0], w_out[0], w_down[0]), b=b, seq=seq)
    k_new = k_s.reshape(db, dec, D_ATTN)
    v_new = v_s.reshape(db, dec, D_ATTN)
    cache_kt = jnp.transpose(cache_k[0], (0, 2, 3, 1)).reshape(db, D_ATTN, past)
    cache_vt = jnp.transpose(cache_v[0], (0, 2, 3, 1)).reshape(db, D_ATTN, past)
    attn_p, attn_s, wup_b = _attention(
        (qt_p, kb_p.reshape(b, seq, D_ATTN), k_meta, vtb_p, vt_s,
         lft_p, lft_s, q_norm_g, k_norm_g),
        (q_s.reshape(db, dec, D_ATTN), k_new, v_new,
         lfp_s.reshape(db, dec, LANES), lft_new,
         cache_kt, cache_vt, jnp.swapaxes(cache_logf[0], 1, 2)),
        (w_up[0],))
    mlp_wts = (wbc_b, wba_b, wo_b, norm2_g[0][None, :], wup_b, wdn_b)

    y_prompt, zlast_p, y_sample = _branch_mlp(
        x_rows, left_p, attn_p.reshape(b * seq, D_ATTN),
        (x_new, conv_s, attn_s.reshape(db * dec, D_ATTN), gate_s),
        (g1, wt_conv, wt_gate) + conv_wts + mlp_wts, rows=MLP_ROWS, seq=seq)

    def heads_last(t):
        return jnp.transpose(t.reshape(b, N_HEADS, HEAD_DIM, length), (0, 3, 1, 2))[None]

    return (y_prompt.reshape(b, seq, D_MODEL),
            y_sample.reshape(db, dec, D_MODEL),
            heads_last(kt_p),
            heads_last(vt_p),
            jnp.swapaxes(lftp_p, 1, 2)[None],
            zlast_p[None],
            k_new.reshape(1, db, dec, N_HEADS, HEAD_DIM),
            v_new.reshape(1, db, dec, N_HEADS, HEAD_DIM),
            jnp.swapaxes(lft_new, 1, 2)[None],
            zlast_s[None])
```

```python
import functools

import jax
import jax.numpy as jnp
import numpy as np
from jax import lax
from jax.experimental import pallas as pl
from jax.experimental.pallas import tpu as pltpu

D_MODEL = 1024
D_CONV = D_MODEL // 2
CONV_W = 3
N_HEADS = 8
HEAD_DIM = 64
D_ATTN = N_HEADS * HEAD_DIM
D_FF = 4 * D_MODEL
N_META = 16
EPS = 1e-6
ATTN_SCALE = HEAD_DIM ** -0.5

F32 = jnp.float32
BF16 = jnp.bfloat16

VMEM_LIMIT_BYTES = 56 * 1024 * 1024
LANES = 128
SUBLANES = 8
BF16_ROWS = 16
PROJ_ROWS = 1024
MLP_ROWS = 512
ATTN_BLOCK = 256
MASKED_BIAS = 1e30
CONV_HALO = CONV_W - 1
ZPAD = SUBLANES
N_SPLIT = 3
V_SLAB = HEAD_DIM + BF16_ROWS
LOG2E = 1.4426950408889634
SKIP_LOG2 = 40.0
NORM_SLACK = 1.02


def _dot(a, b):
    return jnp.dot(a, b, preferred_element_type=F32)


def _dot_nt(a, b):
    return lax.dot_general(a, b, (((1,), (1,)), ((), ())), preferred_element_type=F32)


def _log_sigmoid(x):
    return jnp.minimum(x, 0.0) - jnp.log1p(jnp.exp(-jnp.abs(x)))


def _cumsum_few(x, axis):
    n = x.shape[axis]
    idx = lax.broadcasted_iota(jnp.int32, x.shape, axis)
    out = jnp.zeros(x.shape, F32)
    for i in range(n):
        term = x[i:i + 1, :] if axis == 0 else x[:, i:i + 1]
        out = out + jnp.where(idx >= i, term, 0.0)
    return out


def _upper_triangle(n):
    r = lax.broadcasted_iota(jnp.int32, (n, n), 0)
    c = lax.broadcasted_iota(jnp.int32, (n, n), 1)
    return jnp.where(r <= c, 1.0, 0.0).astype(BF16)


def _split3(c):
    hi = c.astype(BF16).astype(F32)
    r1 = c - hi
    mid = r1.astype(BF16).astype(F32)
    return hi, mid, r1 - mid


def _cumsum_lanes(x, tri_upper):
    h = x.shape[0]
    pieces = jnp.concatenate(_split3(x), axis=0).astype(BF16)
    y = _dot(pieces, tri_upper)
    return y[0:h] + y[h:2 * h] + y[2 * h:3 * h]


def _const_spec(shape):
    nd = len(shape)
    return pl.BlockSpec(shape, lambda *_: (0,) * nd, pipeline_mode=pl.Buffered(1))


def _params(n_axes):
    return pltpu.CompilerParams(
        dimension_semantics=("arbitrary",) * n_axes,
        vmem_limit_bytes=VMEM_LIMIT_BYTES)


def _rms_rows(x, g_row):
    ms = jnp.mean(x * x, axis=-1, keepdims=True)
    return (x * lax.rsqrt(ms + EPS) * g_row).astype(BF16)


def _head_norm_t(ut, g_col):
    out = []
    for h in range(N_HEADS):
        blk = ut[h * HEAD_DIM:(h + 1) * HEAD_DIM, :]
        ms = jnp.mean(blk * blk, axis=0, keepdims=True)
        out.append(blk * lax.rsqrt(ms + EPS) * g_col[h * HEAD_DIM:(h + 1) * HEAD_DIM, :])
    return jnp.concatenate(out, axis=0)


def _short_conv(xn, wa_ref, cw_ref, cb_ref, zbuf, n_seg, seg_len):
    cb = _dot_nt(xn, wa_ref[0:D_CONV, :])
    z = (_dot_nt(xn, wa_ref[D_CONV:2 * D_CONV, :])
         * _dot_nt(xn, wa_ref[2 * D_CONV:3 * D_CONV, :]))
    out, tails = [], []
    for s in range(n_seg):
        r0 = s * seg_len
        zs = z[r0:r0 + seg_len]
        zbuf[s, ZPAD:ZPAD + seg_len, :] = zs
        y = None
        for i in range(CONV_W):
            lo = ZPAD - CONV_HALO + i
            tap = zs if i == CONV_HALO else zbuf[s, lo:lo + seg_len, :]
            term = tap * cw_ref[i:i + 1, :]
            y = term if y is None else y + term
        out.append(cb[r0:r0 + seg_len] * (y + cb_ref[...]))
        tail = zbuf[s, ZPAD + seg_len - CONV_HALO:ZPAD + seg_len, :]
        tails.append(tail)
        zbuf[s, ZPAD - CONV_HALO:ZPAD, :] = tail
    return out, tails


def _qkv_kernel(*refs, tiles_per_seq, n_cast):
    (x_ref, g1_ref, wqkv_ref, wfl_ref, bfr_ref, qgc_ref, kgc_ref,
     ktm_ref, vtm_ref, lftm_ref, kbm_ref) = refs[:11]
    cast_in = refs[11:11 + n_cast]
    (qt_ref, kt_ref, vt_ref, lftp_ref, lft_ref, kb_ref,
     vtb_ref) = refs[11 + n_cast:18 + n_cast]
    cast_out = refs[18 + n_cast:18 + 2 * n_cast]
    kcar, vcar, lcar = refs[18 + 2 * n_cast:]
    rows = x_ref.shape[0]
    step = pl.program_id(0) % tiles_per_seq

    def tile_body(tile_idx):
        lane0 = tile_idx * rows

        def shifted_store(out_ref, car_ref, meta_ref, tile):
            left = meta_ref[:, 0:LANES] if tile_idx == 0 else car_ref[...]
            rolled = pltpu.roll(tile, N_META, axis=1)
            lane = lax.broadcasted_iota(jnp.int32, (tile.shape[0], LANES), 1)
            out_ref[:, lane0:lane0 + LANES] = jnp.where(lane < N_META, left, rolled[:, 0:LANES])
            out_ref[:, lane0 + LANES:lane0 + rows] = rolled[:, LANES:]
            if tile_idx < tiles_per_seq - 1:
                car_ref[...] = rolled[:, 0:LANES]
            else:
                out_ref[:, tiles_per_seq * rows:] = rolled[:, 0:N_META]

        xn = _rms_rows(x_ref[...], g1_ref[...])

        def feature_major(j):
            return _dot_nt(wqkv_ref[j * D_ATTN:(j + 1) * D_ATTN, :], xn)

        lfp = _log_sigmoid(_dot_nt(xn, wfl_ref[...]) + bfr_ref[...])
        lft = lfp.T[0:N_HEADS, :]
        lft_ref[...] = lft
        qt = _head_norm_t(feature_major(0), qgc_ref[...])
        qt_ref[...] = (qt * (ATTN_SCALE * LOG2E)).astype(BF16)
        kt = _head_norm_t(feature_major(1), kgc_ref[...])
        vt = feature_major(2)
        key0 = ATTN_BLOCK + lane0
        kb_ref[key0:key0 + rows, :] = kt.T.astype(BF16)
        if tile_idx == 0:
            kb_ref[0:ATTN_BLOCK - N_META, :] = jnp.zeros((ATTN_BLOCK - N_META, D_ATTN), BF16)
            kb_ref[ATTN_BLOCK - N_META:ATTN_BLOCK, :] = kbm_ref[...]
        vtb_ref[...] = vt.astype(BF16)
        shifted_store(kt_ref, kcar, ktm_ref, kt)
        shifted_store(vt_ref, vcar, vtm_ref, vt)
        shifted_store(lftp_ref, lcar, lftm_ref, lft)
        for src, dst in zip(cast_in, cast_out):
            dst[...] = src[...].astype(BF16)

    for tile_idx in range(tiles_per_seq):
        pl.when(step == tile_idx)(functools.partial(tile_body, tile_idx))


def _project_qkv(x2d, wts, meta_cols, f32_wts, *, b, seq):
    rows = PROJ_ROWS
    tiles = seq // rows
    length = N_META + seq
    n_steps = b * tiles
    chunk_spec = lambda w: pl.BlockSpec((w.shape[0] // n_steps, w.shape[1]), lambda i: (i, 0))
    row_spec = lambda width: pl.BlockSpec((rows, width), lambda i: (i, 0))
    col_spec = lambda feat: pl.BlockSpec(
        (None, feat, rows), lambda i: (i // tiles, 0, i % tiles))
    seq_spec = lambda feat: pl.BlockSpec((None, feat, length), lambda i: (i // tiles, 0, 0))
    out_shape = (
        jax.ShapeDtypeStruct((b, D_ATTN, seq), BF16),
        jax.ShapeDtypeStruct((b, D_ATTN, length), F32),
        jax.ShapeDtypeStruct((b, D_ATTN, length), F32),
        jax.ShapeDtypeStruct((b, N_HEADS, length), F32),
        jax.ShapeDtypeStruct((b, N_HEADS, seq), F32),
        jax.ShapeDtypeStruct((b, ATTN_BLOCK + seq, D_ATTN), BF16),
        jax.ShapeDtypeStruct((b, D_ATTN, seq), BF16),
    )
    key_spec = pl.BlockSpec((None, ATTN_BLOCK + seq, D_ATTN), lambda i: (i // tiles, 0, 0))
    out_specs = (col_spec(D_ATTN), seq_spec(D_ATTN), seq_spec(D_ATTN), seq_spec(N_HEADS),
                 col_spec(N_HEADS), key_spec, col_spec(D_ATTN))
    out_shape = out_shape + tuple(jax.ShapeDtypeStruct(w.shape, BF16) for w in f32_wts)
    out_specs = out_specs + tuple(chunk_spec(w) for w in f32_wts)
    kern = functools.partial(_qkv_kernel, tiles_per_seq=tiles, n_cast=len(f32_wts))
    return pl.pallas_call(
        kern, grid=(n_steps,),
        in_specs=[row_spec(D_MODEL)] + [_const_spec(w.shape) for w in wts + meta_cols]
        + [chunk_spec(w) for w in f32_wts],
        out_specs=out_specs, out_shape=out_shape,
        scratch_shapes=[pltpu.VMEM((D_ATTN, LANES), F32),
                        pltpu.VMEM((D_ATTN, LANES), F32),
                        pltpu.VMEM((N_HEADS, LANES), F32)],
        compiler_params=_params(1), name="proj_qkv",
    )(x2d, *wts, *meta_cols, *f32_wts)


def _proj_small_kernel(meta_ref, xs_ref, state_ref, wt_ref, g1_ref, gains_ref, bfb_ref,
                       cwin_ref, cb_ref, bd_ref,
                       conv_ref, q_ref, k_ref, v_ref, km_ref, kt_ref, vt_ref,
                       lft_ref, lftn_ref, lfpn_ref,
                       gate_ref, leftp_ref, zlast_ref,
                       wa_ref, wqkv_ref, wgl_ref, wfl_ref,
                       bfr_ref, qgc_ref, kgc_ref, cw_ref, zbuf):
    n_new, _, dec = lftn_ref.shape
    n_main = 3 * D_CONV + 3 * D_ATTN
    wa_ref[...] = wt_ref[0:3 * D_CONV, :].astype(BF16)
    wqkv_ref[...] = wt_ref[3 * D_CONV:n_main, :].astype(BF16)
    wgl_ref[...] = wt_ref[n_main + N_HEADS:, :].astype(BF16)
    w_fl = wt_ref[n_main:n_main + N_HEADS, :]
    wfl_ref[...] = jnp.concatenate([w_fl] * (LANES // N_HEADS), axis=0).astype(BF16)
    wflt_ref = wfl_ref.at[0:BF16_ROWS]

    qk = jnp.broadcast_to(gains_ref[...], (SUBLANES, LANES))
    kq = pltpu.roll(qk, HEAD_DIM, axis=1)
    first = lax.broadcasted_iota(jnp.int32, (SUBLANES, LANES), 1) < HEAD_DIM

    def per_feature(two_heads):
        rows = jnp.concatenate([two_heads] * (D_ATTN // LANES), axis=1)
        return rows[0:1, :], rows.T[:, 0:1]

    qgr, qgc = per_feature(jnp.where(first, qk, kq))
    kgr, kgc = per_feature(jnp.where(first, kq, qk))
    qgc_ref[...] = qgc
    kgc_ref[...] = kgc
    bfc = bfb_ref[:, 0:1]
    bfr_ref[...] = jnp.concatenate([bfb_ref[...]] * (LANES // N_HEADS), axis=0).T[0:1, :]
    cw_ref[...] = cwin_ref[0]

    xn = _rms_rows(jnp.concatenate([meta_ref[...], xs_ref[...]], axis=0), g1_ref[...])
    xn_new = xn[N_META:]
    zbuf[0, ZPAD - CONV_HALO:ZPAD, :] = jnp.zeros((CONV_HALO, D_CONV), F32)
    zbuf[1:, ZPAD - CONV_HALO:ZPAD, :] = state_ref[...]
    conv, tails = _short_conv(xn, wa_ref, cw_ref, cb_ref, zbuf, n_new + 1, dec)
    for s in range(n_new):
        conv_ref[s * dec:(s + 1) * dec, :] = conv[s + 1].astype(BF16)
        zlast_ref[s] = tails[s + 1]
    for i in range(leftp_ref.shape[0]):
        leftp_ref[i] = tails[0]

    def rows_major(j, rows):
        return _dot_nt(rows, wqkv_ref[j * D_ATTN:(j + 1) * D_ATTN, :])

    def feature_major(j):
        return _dot_nt(wqkv_ref[j * D_ATTN:(j + 1) * D_ATTN, :], xn)

    def head_norm(u, g_row):
        ssq = _dot((u * u).astype(BF16), bd_ref[...])
        return u * lax.rsqrt(ssq * (1.0 / HEAD_DIM) + EPS) * g_row

    q_ref[...] = (head_norm(rows_major(0, xn_new), qgr) * ATTN_SCALE).astype(BF16)
    k = head_norm(rows_major(1, xn), kgr)
    km_ref[...] = k[0:N_META].astype(BF16)
    k_ref[...] = k[N_META:]
    v_ref[...] = rows_major(2, xn_new)
    kt_ref[...] = _head_norm_t(feature_major(1), kgc_ref[...])
    vt_ref[...] = feature_major(2)
    lfpn_ref[...] = _log_sigmoid(_dot_nt(xn_new, wfl_ref[...]) + bfr_ref[...])
    lft = _log_sigmoid(_dot_nt(wflt_ref[...], xn)[0:N_HEADS] + bfc)
    lft_ref[...] = lft
    for s in range(n_new):
        lftn_ref[s] = lft[:, N_META + s * dec:N_META + (s + 1) * dec]
    gate_ref[...] = jax.nn.sigmoid(_dot_nt(xn_new, wgl_ref[...])).astype(BF16)


def _project_small(meta, x_new, state, wt, small_params, n_prompt):
    n_new = state.shape[0]
    n_rows = x_new.shape[0]
    dec = n_rows // n_new
    n_all = N_META + n_rows
    n_main = 3 * D_CONV + 3 * D_ATTN
    head_of = np.arange(D_ATTN) // HEAD_DIM
    same_head = jnp.asarray(head_of[:, None] == head_of[None, :], BF16)
    full = lambda *shape: pl.BlockSpec(shape, lambda i: (0,) * len(shape))
    out_shape = (
        jax.ShapeDtypeStruct((n_rows, D_CONV), BF16),
        jax.ShapeDtypeStruct((n_rows, D_ATTN), BF16),
        jax.ShapeDtypeStruct((n_rows, D_ATTN), F32),
        jax.ShapeDtypeStruct((n_rows, D_ATTN), F32),
        jax.ShapeDtypeStruct((N_META, D_ATTN), BF16),
        jax.ShapeDtypeStruct((D_ATTN, n_all), F32),
        jax.ShapeDtypeStruct((D_ATTN, n_all), F32),
        jax.ShapeDtypeStruct((N_HEADS, n_all), F32),
        jax.ShapeDtypeStruct((n_new, N_HEADS, dec), F32),
        jax.ShapeDtypeStruct((n_rows, LANES), F32),
        jax.ShapeDtypeStruct((n_rows, 2 * D_MODEL), BF16),
        jax.ShapeDtypeStruct((n_prompt, CONV_HALO, D_CONV), F32),
        jax.ShapeDtypeStruct((n_new, CONV_HALO, D_CONV), F32),
        jax.ShapeDtypeStruct((3 * D_CONV, D_MODEL), BF16),
        jax.ShapeDtypeStruct((3 * D_ATTN, D_MODEL), BF16),
        jax.ShapeDtypeStruct((wt.shape[0] - n_main - N_HEADS, D_MODEL), BF16),
        jax.ShapeDtypeStruct((LANES, D_MODEL), BF16),
        jax.ShapeDtypeStruct((1, LANES), F32),
        jax.ShapeDtypeStruct((D_ATTN, 1), F32),
        jax.ShapeDtypeStruct((D_ATTN, 1), F32),
        jax.ShapeDtypeStruct((CONV_W, D_CONV), F32),
    )
    ins = (meta, x_new, state, wt) + tuple(small_params) + (same_head,)
    return pl.pallas_call(
        _proj_small_kernel, grid=(1,),
        in_specs=[full(*meta.shape), full(*x_new.shape), full(*state.shape)]
        + [_const_spec(w.shape) for w in ins[3:]],
        out_specs=tuple(full(*s.shape) for s in out_shape), out_shape=out_shape,
        scratch_shapes=[pltpu.VMEM((n_new + 1, dec + ZPAD, D_CONV), F32)],
        compiler_params=_params(1), name="proj_small",
    )(*ins)


def _prompt_attn_kernel(qt_ref, kpos, vtb_ref, vtm_ref, lft_ref, lftm_ref,
                        qg_ref, kg_ref, o_ref, vb, kbias, crow, cend, qcat,
                        m_s, acc_s, sbuf, *, seq):
    blk = ATTN_BLOCK
    n_blk = seq // blk
    n_bias = N_SPLIT * N_HEADS
    pad = blk - N_META

    ones_row = (lax.broadcasted_iota(jnp.int32, (V_SLAB - HEAD_DIM, blk), 0) == 0).astype(BF16)
    lane_m = lax.broadcasted_iota(jnp.int32, (HEAD_DIM, LANES), 1)
    vb[0] = jnp.zeros((N_HEADS * V_SLAB, blk), BF16)
    for h in range(N_HEADS):
        rows = slice(h * HEAD_DIM, (h + 1) * HEAD_DIM)
        slab = slice(h * V_SLAB, h * V_SLAB + HEAD_DIM)
        meta = jnp.where(lane_m < N_META, vtm_ref[rows, 0:LANES], 0.0)
        vb[0, slab, blk - LANES:] = pltpu.roll(meta, LANES - N_META, axis=1).astype(BF16)
        for j in range(n_blk):
            vb[j + 1, slab, :] = vtb_ref[rows, j * blk:(j + 1) * blk]
        for j in range(n_blk + 1):
            vb[j, h * V_SLAB + HEAD_DIM:(h + 1) * V_SLAB, :] = ones_row

    qk_bound = (NORM_SLACK * HEAD_DIM * ATTN_SCALE * LOG2E
                * jnp.max(jnp.abs(qg_ref[...]), axis=1, keepdims=True)
                * jnp.max(jnp.abs(kg_ref[...]), axis=1, keepdims=True))

    upper = _upper_triangle(blk)
    ones_zeros = jnp.concatenate([jnp.ones((n_bias, blk), F32),
                                  jnp.zeros((LANES - 2 * n_bias, blk), F32)], axis=0)

    def store_kbias(j, c_log2, is_pad=None):
        neg = [-piece for piece in _split3(c_log2)]
        if is_pad is not None:
            neg = [jnp.where(is_pad, fill, piece)
                   for piece, fill in zip(neg, (-MASKED_BIAS, 0.0, 0.0))]
        kbias[j * blk:(j + 1) * blk, :] = jnp.concatenate(neg + [ones_zeros], axis=0).T.astype(BF16)

    lane_h = lax.broadcasted_iota(jnp.int32, (N_HEADS, LANES), 1)
    lf_meta = jnp.where(lane_h < N_META, lftm_ref[:, 0:LANES], 0.0)
    lf_blk0 = jnp.concatenate([jnp.zeros((N_HEADS, blk - LANES), F32),
                               pltpu.roll(lf_meta, LANES - N_META, axis=1)], axis=1)
    c_row = _cumsum_lanes(lf_blk0, upper)
    store_kbias(0, c_row * LOG2E, lax.broadcasted_iota(jnp.int32, (N_HEADS, blk), 1) < pad)
    off_r = c_row[:, blk - 1:blk]
    c_end = jnp.where(lane_h == 0, off_r * LOG2E, 0.0)
    for j in range(n_blk):
        c_row = _cumsum_lanes(lft_ref[:, j * blk:(j + 1) * blk], upper) + off_r
        crow[j] = c_row * LOG2E
        store_kbias(j + 1, c_row * LOG2E)
        off_r = c_row[:, blk - 1:blk]
        c_end = jnp.where(lane_h == j + 1, off_r * LOG2E, c_end)
    cend[...] = c_end

    row128 = lax.broadcasted_iota(jnp.int32, (LANES, blk), 0)
    krow = lax.broadcasted_iota(jnp.int32, (2 * blk, blk), 0)
    qcol = lax.broadcasted_iota(jnp.int32, (2 * blk, blk), 1)

    def q_block(t, _):
        tok0 = pl.multiple_of(t * blk, blk)
        c_q = crow[t]
        hi, mid, lo = _split3(c_q)
        bias_rows = jnp.concatenate(
            [jnp.ones((n_bias, blk), F32), hi, mid, lo,
             jnp.zeros((LANES - 2 * n_bias, blk), F32)], axis=0)
        for h in range(N_HEADS):
            pair = qt_ref[(h // 2) * LANES:(h // 2 + 1) * LANES, pl.ds(tok0, blk)]
            in_head = (row128 // HEAD_DIM) == (h % 2)
            qcat[h, 0:LANES, :] = jnp.where(in_head, pair, jnp.zeros_like(pair))
            qcat[h, LANES:, :] = jnp.where(row128 % N_HEADS == h, bias_rows, 0.0).astype(BF16)
        m_s[...] = jnp.full(m_s.shape, -jnp.inf, F32)
        acc_s[...] = jnp.zeros(acc_s.shape, F32)

        gap = 2.0 * qk_bound + c_q[:, 0:1] - cend[...]
        needed = jnp.logical_and(gap >= -SKIP_LOG2, lane_h < t)
        n_needed = jnp.max(
            jnp.sum(jnp.where(needed, 1.0, 0.0), axis=1, keepdims=True)).astype(jnp.int32)

        def key_pass(row0, n_rows, pv, visible):
            m_blk = []
            for h in range(N_HEADS):
                g = h // 2
                kc = jnp.concatenate([kpos[pl.ds(row0, n_rows), g * LANES:(g + 1) * LANES],
                                      kbias[pl.ds(row0, n_rows), :]], axis=1)
                s = _dot(kc, qcat[h])
                if visible is not None:
                    s = jnp.where(visible, s, -jnp.inf)
                sbuf[h, 0:n_rows, :] = s
                m_blk.append(jnp.max(s, axis=0, keepdims=True))
            for h in range(N_HEADS):
                slab = slice(h * V_SLAB, (h + 1) * V_SLAB)
                m_old = m_s[h:h + 1, :]
                m_new = jnp.maximum(m_old, m_blk[h])
                alpha = jnp.exp2(m_old - m_new)
                p = jnp.exp2(sbuf[h, 0:n_rows, :] - m_new)
                m_s[h:h + 1, :] = m_new
                acc_s[slab, :] = alpha * acc_s[slab, :] + pv(slab, p.astype(BF16))

        key_pass(tok0, 2 * blk,
                 lambda slab, p: (_dot(vb[t, slab, :], p[0:blk])
                                  + _dot(vb[t + 1, slab, :], p[blk:])),
                 krow <= qcol + blk)

        def older_block(j, _):
            key_pass(pl.multiple_of(j * blk, blk), blk,
                     lambda slab, p: _dot(vb[j, slab, :], p), None)
            return 0

        lax.fori_loop(t - n_needed, t, older_block, 0)

        o_t = []
        for h in range(N_HEADS):
            norm = acc_s[h * V_SLAB + HEAD_DIM:h * V_SLAB + HEAD_DIM + 1, :]
            o_t.append(acc_s[h * V_SLAB:h * V_SLAB + HEAD_DIM, :] * (1.0 / norm))
        o_ref[pl.ds(tok0, blk), :] = jnp.concatenate(o_t, axis=0).T.astype(BF16)
        return 0

    lax.fori_loop(0, n_blk, q_block, 0)


def _sample_attn_kernel(q_ref, kn_ref, vn_ref, lf_ref, lft_ref, ckt_ref, cvt_ref, clft_ref,
                        o_ref, crow, *, past, dec):
    blk = ATTN_BLOCK
    n_keys = past + LANES

    upper = _upper_triangle(blk)
    off = jnp.zeros((N_HEADS, 1), F32)
    for j in range(past // blk):
        loc = _cumsum_lanes(clft_ref[:, j * blk:(j + 1) * blk], upper) + off
        crow[:, j * blk:(j + 1) * blk] = loc
        off = loc[:, blk - 1:blk]
    crow[:, 0:past] = crow[:, 0:past] - off

    cq_c = _cumsum_few(lf_ref[:, 0:N_HEADS], axis=0)
    cq_r = _cumsum_few(lft_ref[...], axis=1)
    crow[:, past:] = jnp.full((N_HEADS, LANES), MASKED_BIAS, F32)
    crow[:, past:past + dec] = cq_r

    q = q_ref[...]
    lane_head = lax.broadcasted_iota(jnp.int32, (dec, D_ATTN), 1) // HEAD_DIM
    q_exp = jnp.concatenate(
        [jnp.where(lane_head == h, q, jnp.zeros_like(q)) for h in range(N_HEADS)], axis=0)
    pad_rows = jnp.zeros((LANES - dec, D_ATTN), BF16)
    k_new = jnp.concatenate([kn_ref[...].astype(BF16), pad_rows], axis=0)
    v_new = jnp.concatenate([vn_ref[...].astype(BF16), pad_rows], axis=0)
    s_all = jnp.concatenate(
        [_dot(q_exp, ckt_ref[...].astype(BF16)), _dot_nt(q_exp, k_new)], axis=1)

    kpos = lax.broadcasted_iota(jnp.int32, (dec, n_keys), 1)
    qpos = past + lax.broadcasted_iota(jnp.int32, (dec, n_keys), 0)
    visible = kpos <= qpos
    probs = []
    norms = []
    for h in range(N_HEADS):
        s = s_all[h * dec:(h + 1) * dec, :] + cq_c[:, h:h + 1] - crow[h:h + 1, :]
        s = jnp.where(visible, s, -jnp.inf)
        p = jnp.exp(s - jnp.max(s, axis=-1, keepdims=True))
        norms.append(jnp.sum(p, axis=-1, keepdims=True))
        probs.append(p.astype(BF16))
    p_all = jnp.concatenate(probs, axis=0)
    o_all = (_dot_nt(p_all[:, 0:past], cvt_ref[...].astype(BF16))
             + _dot(p_all[:, past:], v_new))
    out = jnp.zeros((dec, D_ATTN), F32)
    for h in range(N_HEADS):
        o = o_all[h * dec:(h + 1) * dec, :] / norms[h]
        out = out + jnp.where(lane_head == h, o, 0.0)
    o_ref[...] = out.astype(BF16)


def _attn_kernel(*refs, seq, past, dec, n_cast):
    n_p, n_s = _N_PROMPT_IN, _N_SAMPLE_IN
    n_in = n_p + n_s + n_cast
    p_in, s_in, cast_in = refs[0:n_p], refs[n_p:n_p + n_s], refs[n_p + n_s:n_in]
    o_ref, os_ref = refs[n_in:n_in + 2]
    cast_out = refs[n_in + 2:n_in + 2 + n_cast]
    scratch = refs[n_in + 2 + n_cast:]
    for src, dst in zip(cast_in, cast_out):
        dst[...] = src[...].astype(BF16)
    _sample_attn_kernel(*s_in, os_ref, scratch[-1], past=past, dec=dec)
    _prompt_attn_kernel(*p_in, o_ref, *scratch[:-1], seq=seq)


_N_PROMPT_IN = 8
_N_SAMPLE_IN = 8


def _attention(prompt_in, sample_in, f32_wts):
    qt, _, vtb, vt_meta, lft, lft_meta, q_gain, k_gain = prompt_in
    q_s, _, _, _, _, cache_kt, _, _ = sample_in
    b, _, seq = qt.shape
    db, dec, _ = q_s.shape
    assert b == db, "one new stream and one running stream per grid step"
    past = cache_kt.shape[2]
    blk = ATTN_BLOCK
    n_blk = seq // blk
    n_pos = (n_blk + 1) * blk
    per_b = lambda *shape: pl.BlockSpec((None,) + shape, lambda i: (i,) + (0,) * len(shape))
    chunk_spec = lambda w: pl.BlockSpec((w.shape[0] // b, w.shape[1]), lambda i: (i, 0))
    in_specs = [per_b(D_ATTN, seq), per_b(n_pos, D_ATTN),
                per_b(D_ATTN, seq), _const_spec(vt_meta.shape),
                per_b(N_HEADS, seq), _const_spec(lft_meta.shape),
                _const_spec(q_gain.shape), _const_spec(k_gain.shape),
                per_b(dec, D_ATTN), per_b(dec, D_ATTN), per_b(dec, D_ATTN),
                per_b(dec, LANES), per_b(N_HEADS, dec),
                per_b(D_ATTN, past), per_b(D_ATTN, past), per_b(N_HEADS, past)]
    in_specs += [chunk_spec(w) for w in f32_wts]
    kern = functools.partial(_attn_kernel, seq=seq, past=past, dec=dec, n_cast=len(f32_wts))
    return pl.pallas_call(
        kern, grid=(b,), in_specs=in_specs,
        out_specs=(per_b(seq, D_ATTN), per_b(dec, D_ATTN))
        + tuple(chunk_spec(w) for w in f32_wts),
        out_shape=(jax.ShapeDtypeStruct((b, seq, D_ATTN), BF16),
                   jax.ShapeDtypeStruct((b, dec, D_ATTN), BF16))
        + tuple(jax.ShapeDtypeStruct(w.shape, BF16) for w in f32_wts),
        scratch_shapes=[pltpu.VMEM((n_blk + 1, N_HEADS * V_SLAB, blk), BF16),
                        pltpu.VMEM((n_pos, LANES), BF16),
                        pltpu.VMEM((n_blk, N_HEADS, blk), F32),
                        pltpu.VMEM((N_HEADS, LANES), F32),
                        pltpu.VMEM((N_HEADS, 2 * LANES, blk), BF16),
                        pltpu.VMEM((N_HEADS, blk), F32),
                        pltpu.VMEM((N_HEADS * V_SLAB, blk), F32),
                        pltpu.VMEM((N_HEADS, 2 * blk, blk), F32),
                        pltpu.VMEM((N_HEADS, past + LANES), F32)],
        compiler_params=_params(1), name="attention",
    )(*prompt_in, *sample_in, *f32_wts)


def _merge_mlp_tail(x, conv_bf16, attn_ref, g_conv, g_attn, wbc_ref, wba_ref, wo_ref, g2_ref,
                    wup_ref, wdn_ref, y_ref):
    merged = g_conv * _dot(conv_bf16, wbc_ref[...]) + g_attn * _dot(attn_ref[...], wba_ref[...])
    h = x + _dot(merged.astype(BF16), wo_ref[...])
    hn = _rms_rows(h, g2_ref[...])
    acc = h
    for c in range(D_FF // D_MODEL):
        cols = slice(c * D_MODEL, (c + 1) * D_MODEL)
        a = jnp.maximum(_dot(hn, wup_ref[:, cols]), 0.0)
        acc = acc + _dot((a * a).astype(BF16), wdn_ref[cols, :])
    y_ref[...] = acc


def _branch_mlp_kernel(x_ref, left_ref, attn_ref, xs_ref, convs_ref, attns_ref, gates_ref,
                       g1_ref, wa_ref, wgl_ref, cw_ref, cb_ref,
                       wbc_ref, wba_ref, wo_ref, g2_ref, wup_ref, wdn_ref,
                       y_ref, zlast_ref, ys_ref, zbuf, *, n_tiles, tiles_per_seq):
    step = pl.program_id(0)

    @pl.when(jnp.logical_and(step % tiles_per_seq == 0, step < n_tiles))
    def _():
        zbuf[:, ZPAD - CONV_HALO:ZPAD, :] = left_ref[...]

    @pl.when(step < n_tiles)
    def _():
        x = x_ref[...]
        xn = _rms_rows(x, g1_ref[...])
        (conv,), (tail,) = _short_conv(xn, wa_ref, cw_ref, cb_ref, zbuf, 1, x.shape[0])
        zlast_ref[0] = tail
        g_conv = jax.nn.sigmoid(_dot_nt(xn, wgl_ref[0:D_MODEL, :]))
        g_attn = jax.nn.sigmoid(_dot_nt(xn, wgl_ref[D_MODEL:2 * D_MODEL, :]))
        _merge_mlp_tail(x, conv.astype(BF16), attn_ref, g_conv, g_attn, wbc_ref, wba_ref,
                        wo_ref, g2_ref, wup_ref, wdn_ref, y_ref)

    @pl.when(step == n_tiles)
    def _():
        _merge_mlp_tail(xs_ref[...], convs_ref[...], attns_ref,
                        gates_ref[:, 0:D_MODEL].astype(F32),
                        gates_ref[:, D_MODEL:2 * D_MODEL].astype(F32),
                        wbc_ref, wba_ref, wo_ref, g2_ref, wup_ref, wdn_ref, ys_ref)


def _branch_mlp(x2d, left, attn, small, wts, *, rows, seq):
    n_rows = x2d.shape[0]
    tiles = seq // rows
    n_tiles = n_rows // rows
    last = n_tiles - 1
    row_spec = lambda width: pl.BlockSpec((rows, width), lambda i: (jnp.minimum(i, last), 0))
    seq_spec = pl.BlockSpec((1, CONV_HALO, D_CONV),
                            lambda i: (jnp.minimum(i, last) // tiles, 0, 0))
    full = lambda a: pl.BlockSpec(a.shape, lambda i: (0,) * a.ndim)
    kern = functools.partial(_branch_mlp_kernel, n_tiles=n_tiles, tiles_per_seq=tiles)
    return pl.pallas_call(
        kern, grid=(n_tiles + 1,),
        in_specs=[row_spec(D_MODEL), seq_spec, row_spec(D_ATTN)] + [full(a) for a in small]
        + [_const_spec(w.shape) for w in wts],
        out_specs=(row_spec(D_MODEL), seq_spec, full(small[0])),
        out_shape=(jax.ShapeDtypeStruct((n_rows, D_MODEL), F32),
                   jax.ShapeDtypeStruct((n_rows // seq, CONV_HALO, D_CONV), F32),
                   jax.ShapeDtypeStruct(small[0].shape, F32)),
        scratch_shapes=[pltpu.VMEM((1, rows + ZPAD, D_CONV), F32)],
        compiler_params=_params(1), name="branch_mlp",
    )(x2d, left, attn, *small, *wts)


def kernel(x_prompt, x_sample, cache_k, cache_v, cache_logf, state_conv, meta,
           norm1_g, w_in, b_f, conv_w, conv_b, q_norm_g, k_norm_g,
           w_br_conv, w_br_attn, w_out, norm2_g, w_up, w_down):
    b, seq, _ = x_prompt.shape
    db, dec, _ = x_sample.shape
    past = cache_k.shape[2]
    length = N_META + seq
    assert dec == N_META, "the small tile treats the meta tokens as one more short sequence"

    wt = w_in[0].T
    assert 2 * HEAD_DIM == LANES, "the q and k gains share one vreg row"
    g1 = norm1_g[0][None, :]
    conv_bias = conv_b[0][None, :]
    gains = jnp.concatenate([q_norm_g[0], k_norm_g[0]])[None, :]
    bf_lanes = jnp.broadcast_to(b_f[0][:, None], (N_HEADS, LANES))

    x_new = x_sample.reshape(db * dec, D_MODEL)
    (conv_s, q_s, k_s, v_s, k_meta, kt_s, vt_s, lft_s, lft_new, lfp_s, gate_s, left_p, zlast_s, wt_conv, wt_qkv, wt_gate, wt_fl,
     bf_row, qg_col, kg_col, conv_taps) = _project_small(
        meta, x_new, state_conv[0], wt, (g1, gains, bf_lanes, conv_w, conv_bias), b)
    qkv_wts = (g1, wt_qkv, wt_fl, bf_row, qg_col, kg_col)
    conv_wts = (conv_taps, conv_bias)

    x_rows = x_prompt.reshape(b * seq, D_MODEL)
    (qt_p, kt_p, vt_p, lftp_p, lft_p, kb_p, vtb_p,
     wdn_b) = _project_qkv(
        x_rows, qkv_wts, (kt_s, vt_s, lft_s, k_meta), (w_down[0],), b=b, seq=seq)
    k_new = k_s.reshape(db, dec, D_ATTN)
    v_new = v_s.reshape(db, dec, D_ATTN)
    cache_kt = jnp.transpose(cache_k[0], (0, 2, 3, 1)).reshape(db, D_ATTN, past)
    cache_vt = jnp.transpose(cache_v[0], (0, 2, 3, 1)).reshape(db, D_ATTN, past)
    attn_p, attn_s, wup_b, wbc_b, wba_b, wo_b = _attention(
        (qt_p, kb_p, vtb_p, vt_s, lft_p, lft_s, q_norm_g, k_norm_g),
        (q_s.reshape(db, dec, D_ATTN), k_new, v_new,
         lfp_s.reshape(db, dec, LANES), lft_new,
         cache_kt, cache_vt, jnp.swapaxes(cache_logf[0], 1, 2)),
        (w_up[0], w_br_conv[0], w_br_attn[0], w_out[0]))
    mlp_wts = (wbc_b, wba_b, wo_b, norm2_g[0][None, :], wup_b, wdn_b)

    y_prompt, zlast_p, y_sample = _branch_mlp(
        x_rows, left_p, attn_p.reshape(b * seq, D_ATTN),
        (x_new, conv_s, attn_s.reshape(db * dec, D_ATTN), gate_s),
        (g1, wt_conv, wt_gate) + conv_wts + mlp_wts, rows=MLP_ROWS, seq=seq)

    def heads_last(t):
        return jnp.transpose(t.reshape(b, N_HEADS, HEAD_DIM, length), (0, 3, 1, 2))[None]

    return (y_prompt.reshape(b, seq, D_MODEL),
            y_sample.reshape(db, dec, D_MODEL),
            heads_last(kt_p),
            heads_last(vt_p),
            jnp.swapaxes(lftp_p, 1, 2)[None],
            zlast_p[None],
            k_new.reshape(1, db, dec, N_HEADS, HEAD_DIM),
            v_new.reshape(1, db, dec, N_HEADS, HEAD_DIM),
            jnp.swapaxes(lft_new, 1, 2)[None],
            zlast_s[None])
```

```python
import functools

import jax
import jax.numpy as jnp
import numpy as np
from jax import lax
from jax.experimental import pallas as pl
from jax.experimental.pallas import tpu as pltpu

D_MODEL = 1024
D_CONV = D_MODEL // 2
CONV_W = 3
N_HEADS = 8
HEAD_DIM = 64
D_ATTN = N_HEADS * HEAD_DIM
D_FF = 4 * D_MODEL
N_META = 16
EPS = 1e-6
ATTN_SCALE = HEAD_DIM ** -0.5

F32 = jnp.float32
BF16 = jnp.bfloat16

VMEM_LIMIT_BYTES = 56 * 1024 * 1024
LANES = 128
SUBLANES = 8
BF16_ROWS = 16
PROJ_ROWS = 1024
MLP_ROWS = 512
ATTN_BLOCK = 256
MASKED_BIAS = 1e30
CONV_HALO = CONV_W - 1
ZPAD = SUBLANES
N_SPLIT = 3
V_SLAB = HEAD_DIM + BF16_ROWS
LOG2E = 1.4426950408889634
SKIP_LOG2 = 40.0
NORM_SLACK = 1.02


def _dot(a, b):
    return jnp.dot(a, b, preferred_element_type=F32)


def _dot_nt(a, b):
    return lax.dot_general(a, b, (((1,), (1,)), ((), ())), preferred_element_type=F32)


def _log_sigmoid(x):
    return jnp.minimum(x, 0.0) - jnp.log1p(jnp.exp(-jnp.abs(x)))


def _cumsum_few(x, axis):
    n = x.shape[axis]
    idx = lax.broadcasted_iota(jnp.int32, x.shape, axis)
    out = jnp.zeros(x.shape, F32)
    for i in range(n):
        term = x[i:i + 1, :] if axis == 0 else x[:, i:i + 1]
        out = out + jnp.where(idx >= i, term, 0.0)
    return out


def _upper_triangle(n):
    r = lax.broadcasted_iota(jnp.int32, (n, n), 0)
    c = lax.broadcasted_iota(jnp.int32, (n, n), 1)
    return jnp.where(r <= c, 1.0, 0.0).astype(BF16)


def _split3(c):
    hi = c.astype(BF16).astype(F32)
    r1 = c - hi
    mid = r1.astype(BF16).astype(F32)
    return hi, mid, r1 - mid


def _cumsum_lanes(x, tri_upper):
    h = x.shape[0]
    pieces = jnp.concatenate(_split3(x), axis=0).astype(BF16)
    y = _dot(pieces, tri_upper)
    return y[0:h] + y[h:2 * h] + y[2 * h:3 * h]


def _const_spec(shape):
    nd = len(shape)
    return pl.BlockSpec(shape, lambda *_: (0,) * nd, pipeline_mode=pl.Buffered(1))


def _params(n_axes):
    return pltpu.CompilerParams(
        dimension_semantics=("arbitrary",) * n_axes,
        vmem_limit_bytes=VMEM_LIMIT_BYTES)


def _rms_rows(x, g_row):
    ms = jnp.mean(x * x, axis=-1, keepdims=True)
    return (x * lax.rsqrt(ms + EPS) * g_row).astype(BF16)


def _head_norm_t(ut, g_col):
    out = []
    for h in range(N_HEADS):
        blk = ut[h * HEAD_DIM:(h + 1) * HEAD_DIM, :]
        ms = jnp.mean(blk * blk, axis=0, keepdims=True)
        out.append(blk * lax.rsqrt(ms + EPS) * g_col[h * HEAD_DIM:(h + 1) * HEAD_DIM, :])
    return jnp.concatenate(out, axis=0)


def _short_conv(xn, wa_ref, cw_ref, cb_ref, zbuf, n_seg, seg_len):
    cb = _dot_nt(xn, wa_ref[0:D_CONV, :])
    z = (_dot_nt(xn, wa_ref[D_CONV:2 * D_CONV, :])
         * _dot_nt(xn, wa_ref[2 * D_CONV:3 * D_CONV, :]))
    out, tails = [], []
    for s in range(n_seg):
        r0 = s * seg_len
        zs = z[r0:r0 + seg_len]
        zbuf[s, ZPAD:ZPAD + seg_len, :] = zs
        y = None
        for i in range(CONV_W):
            lo = ZPAD - CONV_HALO + i
            tap = zs if i == CONV_HALO else zbuf[s, lo:lo + seg_len, :]
            term = tap * cw_ref[i:i + 1, :]
            y = term if y is None else y + term
        out.append(cb[r0:r0 + seg_len] * (y + cb_ref[...]))
        tail = zbuf[s, ZPAD + seg_len - CONV_HALO:ZPAD + seg_len, :]
        tails.append(tail)
        zbuf[s, ZPAD - CONV_HALO:ZPAD, :] = tail
    return out, tails


def _qkv_kernel(*refs, tiles_per_seq, n_cast):
    (x_ref, g1_ref, wqkv_ref, bfb_ref, qgc_ref, kgc_ref,
     ktm_ref, vtm_ref, lftm_ref, kbm_ref) = refs[:10]
    cast_in = refs[10:10 + n_cast]
    (qt_ref, kt_ref, vt_ref, lftp_ref, lft_ref, kb_ref,
     vtb_ref) = refs[10 + n_cast:17 + n_cast]
    cast_out = refs[17 + n_cast:17 + 2 * n_cast]
    kcar, vcar, lcar = refs[17 + 2 * n_cast:]
    rows = x_ref.shape[0]
    step = pl.program_id(0) % tiles_per_seq

    def tile_body(tile_idx):
        lane0 = tile_idx * rows

        def shifted_store(out_ref, car_ref, meta_ref, tile):
            left = meta_ref[:, 0:LANES] if tile_idx == 0 else car_ref[...]
            rolled = pltpu.roll(tile, N_META, axis=1)
            lane = lax.broadcasted_iota(jnp.int32, (tile.shape[0], LANES), 1)
            out_ref[:, lane0:lane0 + LANES] = jnp.where(lane < N_META, left, rolled[:, 0:LANES])
            out_ref[:, lane0 + LANES:lane0 + rows] = rolled[:, LANES:]
            if tile_idx < tiles_per_seq - 1:
                car_ref[...] = rolled[:, 0:LANES]
            else:
                out_ref[:, tiles_per_seq * rows:] = rolled[:, 0:N_META]

        xn = _rms_rows(x_ref[...], g1_ref[...])

        def feature_major(j):
            return _dot_nt(wqkv_ref[j * D_ATTN:(j + 1) * D_ATTN, :], xn)

        qt = _head_norm_t(feature_major(0), qgc_ref[...])
        qt_ref[...] = (qt * (ATTN_SCALE * LOG2E)).astype(BF16)
        kt = _head_norm_t(feature_major(1), kgc_ref[...])
        v_f = _dot_nt(wqkv_ref[2 * D_ATTN:3 * D_ATTN + BF16_ROWS, :], xn)
        vt = v_f[0:D_ATTN]
        lft = _log_sigmoid(v_f[D_ATTN:D_ATTN + N_HEADS] + bfb_ref[:, 0:1])
        lft_ref[...] = lft
        key0 = ATTN_BLOCK + lane0
        kb_ref[key0:key0 + rows, :] = kt.T.astype(BF16)
        if tile_idx == 0:
            kb_ref[0:ATTN_BLOCK - N_META, :] = jnp.zeros((ATTN_BLOCK - N_META, D_ATTN), BF16)
            kb_ref[ATTN_BLOCK - N_META:ATTN_BLOCK, :] = kbm_ref[...]
        vtb_ref[...] = vt.astype(BF16)
        shifted_store(kt_ref, kcar, ktm_ref, kt)
        shifted_store(vt_ref, vcar, vtm_ref, vt)
        shifted_store(lftp_ref, lcar, lftm_ref, lft)
        for src, dst in zip(cast_in, cast_out):
            dst[...] = src[...].astype(BF16)

    for tile_idx in range(tiles_per_seq):
        pl.when(step == tile_idx)(functools.partial(tile_body, tile_idx))


def _project_qkv(x2d, wts, meta_cols, f32_wts, *, b, seq):
    rows = PROJ_ROWS
    tiles = seq // rows
    length = N_META + seq
    n_steps = b * tiles
    chunk_spec = lambda w: pl.BlockSpec((w.shape[0] // n_steps, w.shape[1]), lambda i: (i, 0))
    row_spec = lambda width: pl.BlockSpec((rows, width), lambda i: (i, 0))
    col_spec = lambda feat: pl.BlockSpec(
        (None, feat, rows), lambda i: (i // tiles, 0, i % tiles))
    seq_spec = lambda feat: pl.BlockSpec((None, feat, length), lambda i: (i // tiles, 0, 0))
    out_shape = (
        jax.ShapeDtypeStruct((b, D_ATTN, seq), BF16),
        jax.ShapeDtypeStruct((b, D_ATTN, length), F32),
        jax.ShapeDtypeStruct((b, D_ATTN, length), F32),
        jax.ShapeDtypeStruct((b, N_HEADS, length), F32),
        jax.ShapeDtypeStruct((b, N_HEADS, seq), F32),
        jax.ShapeDtypeStruct((b, ATTN_BLOCK + seq, D_ATTN), BF16),
        jax.ShapeDtypeStruct((b, D_ATTN, seq), BF16),
    )
    key_spec = pl.BlockSpec((None, ATTN_BLOCK + seq, D_ATTN), lambda i: (i // tiles, 0, 0))
    out_specs = (col_spec(D_ATTN), seq_spec(D_ATTN), seq_spec(D_ATTN), seq_spec(N_HEADS),
                 col_spec(N_HEADS), key_spec, col_spec(D_ATTN))
    out_shape = out_shape + tuple(jax.ShapeDtypeStruct(w.shape, BF16) for w in f32_wts)
    out_specs = out_specs + tuple(chunk_spec(w) for w in f32_wts)
    kern = functools.partial(_qkv_kernel, tiles_per_seq=tiles, n_cast=len(f32_wts))
    return pl.pallas_call(
        kern, grid=(n_steps,),
        in_specs=[row_spec(D_MODEL)] + [_const_spec(w.shape) for w in wts + meta_cols]
        + [chunk_spec(w) for w in f32_wts],
        out_specs=out_specs, out_shape=out_shape,
        scratch_shapes=[pltpu.VMEM((D_ATTN, LANES), F32),
                        pltpu.VMEM((D_ATTN, LANES), F32),
                        pltpu.VMEM((N_HEADS, LANES), F32)],
        compiler_params=_params(1), name="proj_qkv",
    )(x2d, *wts, *meta_cols, *f32_wts)


def _proj_small_kernel(meta_ref, xs_ref, state_ref, wt_ref, g1_ref, gains_ref, bfb_ref,
                       cwin_ref, cb_ref, bd_ref,
                       conv_ref, q_ref, k_ref, v_ref, km_ref, kt_ref, vt_ref,
                       lft_ref, lftn_ref, lfpn_ref,
                       gate_ref, leftp_ref, zlast_ref,
                       wa_ref, wqkv_ref, wgl_ref, wfl_ref,
                       bfr_ref, qgc_ref, kgc_ref, cw_ref, zbuf):
    n_new, _, dec = lftn_ref.shape
    n_main = 3 * D_CONV + 3 * D_ATTN
    wa_ref[...] = wt_ref[0:3 * D_CONV, :].astype(BF16)
    wqkv_ref[0:3 * D_ATTN, :] = wt_ref[3 * D_CONV:n_main, :].astype(BF16)
    wgl_ref[...] = wt_ref[n_main + N_HEADS:, :].astype(BF16)
    w_fl = wt_ref[n_main:n_main + N_HEADS, :]
    wfl_ref[...] = jnp.concatenate([w_fl] * (LANES // N_HEADS), axis=0).astype(BF16)
    wqkv_ref[3 * D_ATTN:, :] = wfl_ref[0:BF16_ROWS, :]
    wflt_ref = wfl_ref.at[0:BF16_ROWS]

    qk = jnp.broadcast_to(gains_ref[...], (SUBLANES, LANES))
    kq = pltpu.roll(qk, HEAD_DIM, axis=1)
    first = lax.broadcasted_iota(jnp.int32, (SUBLANES, LANES), 1) < HEAD_DIM

    def per_feature(two_heads):
        rows = jnp.concatenate([two_heads] * (D_ATTN // LANES), axis=1)
        return rows[0:1, :], rows.T[:, 0:1]

    qgr, qgc = per_feature(jnp.where(first, qk, kq))
    kgr, kgc = per_feature(jnp.where(first, kq, qk))
    qgc_ref[...] = qgc
    kgc_ref[...] = kgc
    bfc = bfb_ref[:, 0:1]
    bfr_ref[...] = jnp.concatenate([bfb_ref[...]] * (LANES // N_HEADS), axis=0).T[0:1, :]
    cw_ref[...] = cwin_ref[0]

    xn = _rms_rows(jnp.concatenate([meta_ref[...], xs_ref[...]], axis=0), g1_ref[...])
    xn_new = xn[N_META:]
    zbuf[0, ZPAD - CONV_HALO:ZPAD, :] = jnp.zeros((CONV_HALO, D_CONV), F32)
    zbuf[1:, ZPAD - CONV_HALO:ZPAD, :] = state_ref[...]
    conv, tails = _short_conv(xn, wa_ref, cw_ref, cb_ref, zbuf, n_new + 1, dec)
    for s in range(n_new):
        conv_ref[s * dec:(s + 1) * dec, :] = conv[s + 1].astype(BF16)
        zlast_ref[s] = tails[s + 1]
    for i in range(leftp_ref.shape[0]):
        leftp_ref[i] = tails[0]

    def rows_major(j, rows):
        return _dot_nt(rows, wqkv_ref[j * D_ATTN:(j + 1) * D_ATTN, :])

    def feature_major(j):
        return _dot_nt(wqkv_ref[j * D_ATTN:(j + 1) * D_ATTN, :], xn)

    def head_norm(u, g_row):
        ssq = _dot((u * u).astype(BF16), bd_ref[...])
        return u * lax.rsqrt(ssq * (1.0 / HEAD_DIM) + EPS) * g_row

    q_ref[...] = (head_norm(rows_major(0, xn_new), qgr) * ATTN_SCALE).astype(BF16)
    k = head_norm(rows_major(1, xn), kgr)
    km_ref[...] = k[0:N_META].astype(BF16)
    k_ref[...] = k[N_META:]
    v_ref[...] = rows_major(2, xn_new)
    kt_ref[...] = _head_norm_t(feature_major(1), kgc_ref[...])
    vt_ref[...] = feature_major(2)
    lfpn_ref[...] = _log_sigmoid(_dot_nt(xn_new, wfl_ref[...]) + bfr_ref[...])
    lft = _log_sigmoid(_dot_nt(wflt_ref[...], xn)[0:N_HEADS] + bfc)
    lft_ref[...] = lft
    for s in range(n_new):
        lftn_ref[s] = lft[:, N_META + s * dec:N_META + (s + 1) * dec]
    gate_ref[...] = jax.nn.sigmoid(_dot_nt(xn_new, wgl_ref[...])).astype(BF16)


def _project_small(meta, x_new, state, wt, small_params, n_prompt):
    n_new = state.shape[0]
    n_rows = x_new.shape[0]
    dec = n_rows // n_new
    n_all = N_META + n_rows
    n_main = 3 * D_CONV + 3 * D_ATTN
    head_of = np.arange(D_ATTN) // HEAD_DIM
    same_head = jnp.asarray(head_of[:, None] == head_of[None, :], BF16)
    full = lambda *shape: pl.BlockSpec(shape, lambda i: (0,) * len(shape))
    out_shape = (
        jax.ShapeDtypeStruct((n_rows, D_CONV), BF16),
        jax.ShapeDtypeStruct((n_rows, D_ATTN), BF16),
        jax.ShapeDtypeStruct((n_rows, D_ATTN), F32),
        jax.ShapeDtypeStruct((n_rows, D_ATTN), F32),
        jax.ShapeDtypeStruct((N_META, D_ATTN), BF16),
        jax.ShapeDtypeStruct((D_ATTN, n_all), F32),
        jax.ShapeDtypeStruct((D_ATTN, n_all), F32),
        jax.ShapeDtypeStruct((N_HEADS, n_all), F32),
        jax.ShapeDtypeStruct((n_new, N_HEADS, dec), F32),
        jax.ShapeDtypeStruct((n_rows, LANES), F32),
        jax.ShapeDtypeStruct((n_rows, 2 * D_MODEL), BF16),
        jax.ShapeDtypeStruct((n_prompt, CONV_HALO, D_CONV), F32),
        jax.ShapeDtypeStruct((n_new, CONV_HALO, D_CONV), F32),
        jax.ShapeDtypeStruct((3 * D_CONV, D_MODEL), BF16),
        jax.ShapeDtypeStruct((3 * D_ATTN + BF16_ROWS, D_MODEL), BF16),
        jax.ShapeDtypeStruct((wt.shape[0] - n_main - N_HEADS, D_MODEL), BF16),
        jax.ShapeDtypeStruct((LANES, D_MODEL), BF16),
        jax.ShapeDtypeStruct((1, LANES), F32),
        jax.ShapeDtypeStruct((D_ATTN, 1), F32),
        jax.ShapeDtypeStruct((D_ATTN, 1), F32),
        jax.ShapeDtypeStruct((CONV_W, D_CONV), F32),
    )
    ins = (meta, x_new, state, wt) + tuple(small_params) + (same_head,)
    return pl.pallas_call(
        _proj_small_kernel, grid=(1,),
        in_specs=[full(*meta.shape), full(*x_new.shape), full(*state.shape)]
        + [_const_spec(w.shape) for w in ins[3:]],
        out_specs=tuple(full(*s.shape) for s in out_shape), out_shape=out_shape,
        scratch_shapes=[pltpu.VMEM((n_new + 1, dec + ZPAD, D_CONV), F32)],
        compiler_params=_params(1), name="proj_small",
    )(*ins)


def _prompt_attn_kernel(qt_ref, kpos, vtb_ref, vtm_ref, lft_ref, lftm_ref,
                        qg_ref, kg_ref, o_ref, vb, kbias, crow, cend, qcat,
                        m_s, acc_s, sbuf, *, seq):
    blk = ATTN_BLOCK
    n_blk = seq // blk
    n_bias = N_SPLIT * N_HEADS
    pad = blk - N_META

    ones_row = (lax.broadcasted_iota(jnp.int32, (V_SLAB - HEAD_DIM, blk), 0) == 0).astype(BF16)
    lane_m = lax.broadcasted_iota(jnp.int32, (HEAD_DIM, LANES), 1)
    vb[0] = jnp.zeros((N_HEADS * V_SLAB, blk), BF16)
    for h in range(N_HEADS):
        rows = slice(h * HEAD_DIM, (h + 1) * HEAD_DIM)
        slab = slice(h * V_SLAB, h * V_SLAB + HEAD_DIM)
        meta = jnp.where(lane_m < N_META, vtm_ref[rows, 0:LANES], 0.0)
        vb[0, slab, blk - LANES:] = pltpu.roll(meta, LANES - N_META, axis=1).astype(BF16)
        for j in range(n_blk):
            vb[j + 1, slab, :] = vtb_ref[rows, j * blk:(j + 1) * blk]
        for j in range(n_blk + 1):
            vb[j, h * V_SLAB + HEAD_DIM:(h + 1) * V_SLAB, :] = ones_row

    qk_bound = (NORM_SLACK * HEAD_DIM * ATTN_SCALE * LOG2E
                * jnp.max(jnp.abs(qg_ref[...]), axis=1, keepdims=True)
                * jnp.max(jnp.abs(kg_ref[...]), axis=1, keepdims=True))

    upper = _upper_triangle(blk)
    ones_zeros = jnp.concatenate([jnp.ones((n_bias, blk), F32),
                                  jnp.zeros((LANES - 2 * n_bias, blk), F32)], axis=0)

    def store_kbias(j, c_log2, is_pad=None):
        neg = [-piece for piece in _split3(c_log2)]
        if is_pad is not None:
            neg = [jnp.where(is_pad, fill, piece)
                   for piece, fill in zip(neg, (-MASKED_BIAS, 0.0, 0.0))]
        kbias[j * blk:(j + 1) * blk, :] = jnp.concatenate(neg + [ones_zeros], axis=0).T.astype(BF16)

    lane_h = lax.broadcasted_iota(jnp.int32, (N_HEADS, LANES), 1)
    lf_meta = jnp.where(lane_h < N_META, lftm_ref[:, 0:LANES], 0.0)
    lf_blk0 = jnp.concatenate([jnp.zeros((N_HEADS, blk - LANES), F32),
                               pltpu.roll(lf_meta, LANES - N_META, axis=1)], axis=1)
    c_row = _cumsum_lanes(lf_blk0, upper)
    store_kbias(0, c_row * LOG2E, lax.broadcasted_iota(jnp.int32, (N_HEADS, blk), 1) < pad)
    off_r = c_row[:, blk - 1:blk]
    c_end = jnp.where(lane_h == 0, off_r * LOG2E, 0.0)
    for j in range(n_blk):
        c_row = _cumsum_lanes(lft_ref[:, j * blk:(j + 1) * blk], upper) + off_r
        crow[j] = c_row * LOG2E
        store_kbias(j + 1, c_row * LOG2E)
        off_r = c_row[:, blk - 1:blk]
        c_end = jnp.where(lane_h == j + 1, off_r * LOG2E, c_end)
    cend[...] = c_end

    row128 = lax.broadcasted_iota(jnp.int32, (LANES, blk), 0)
    krow = lax.broadcasted_iota(jnp.int32, (2 * blk, blk), 0)
    qcol = lax.broadcasted_iota(jnp.int32, (2 * blk, blk), 1)

    def q_block(t, _):
        tok0 = pl.multiple_of(t * blk, blk)
        c_q = crow[t]
        hi, mid, lo = _split3(c_q)
        bias_rows = jnp.concatenate(
            [jnp.ones((n_bias, blk), F32), hi, mid, lo,
             jnp.zeros((LANES - 2 * n_bias, blk), F32)], axis=0)
        for h in range(N_HEADS):
            pair = qt_ref[(h // 2) * LANES:(h // 2 + 1) * LANES, pl.ds(tok0, blk)]
            in_head = (row128 // HEAD_DIM) == (h % 2)
            qcat[h, 0:LANES, :] = jnp.where(in_head, pair, jnp.zeros_like(pair))
            qcat[h, LANES:, :] = jnp.where(row128 % N_HEADS == h, bias_rows, 0.0).astype(BF16)
        m_s[...] = jnp.full(m_s.shape, -jnp.inf, F32)
        acc_s[...] = jnp.zeros(acc_s.shape, F32)

        gap = 2.0 * qk_bound + c_q[:, 0:1] - cend[...]
        needed = jnp.logical_and(gap >= -SKIP_LOG2, lane_h < t)
        n_needed = jnp.max(
            jnp.sum(jnp.where(needed, 1.0, 0.0), axis=1, keepdims=True)).astype(jnp.int32)

        def key_pass(row0, n_rows, pv, visible):
            m_blk = []
            for h in range(N_HEADS):
                g = h // 2
                kc = jnp.concatenate([kpos[pl.ds(row0, n_rows), g * LANES:(g + 1) * LANES],
                                      kbias[pl.ds(row0, n_rows), :]], axis=1)
                s = _dot(kc, qcat[h])
                if visible is not None:
                    s = jnp.where(visible, s, -jnp.inf)
                sbuf[h, 0:n_rows, :] = s
                m_blk.append(jnp.max(s, axis=0, keepdims=True))
            for h in range(N_HEADS):
                slab = slice(h * V_SLAB, (h + 1) * V_SLAB)
                m_old = m_s[h:h + 1, :]
                m_new = jnp.maximum(m_old, m_blk[h])
                alpha = jnp.exp2(m_old - m_new)
                p = jnp.exp2(sbuf[h, 0:n_rows, :] - m_new)
                m_s[h:h + 1, :] = m_new
                acc_s[slab, :] = alpha * acc_s[slab, :] + pv(slab, p.astype(BF16))

        key_pass(tok0, 2 * blk,
                 lambda slab, p: (_dot(vb[t, slab, :], p[0:blk])
                                  + _dot(vb[t + 1, slab, :], p[blk:])),
                 krow <= qcol + blk)

        def older_block(j, _):
            key_pass(pl.multiple_of(j * blk, blk), blk,
                     lambda slab, p: _dot(vb[j, slab, :], p), None)
            return 0

        lax.fori_loop(t - n_needed, t, older_block, 0)

        o_t = []
        for h in range(N_HEADS):
            norm = acc_s[h * V_SLAB + HEAD_DIM:h * V_SLAB + HEAD_DIM + 1, :]
            o_t.append(acc_s[h * V_SLAB:h * V_SLAB + HEAD_DIM, :] * (1.0 / norm))
        o_ref[pl.ds(tok0, blk), :] = jnp.concatenate(o_t, axis=0).T.astype(BF16)
        return 0

    lax.fori_loop(0, n_blk, q_block, 0)


def _sample_attn_kernel(q_ref, kn_ref, vn_ref, lf_ref, lft_ref, ckt_ref, cvt_ref, clft_ref,
                        o_ref, crow, *, past, dec):
    blk = ATTN_BLOCK
    n_keys = past + LANES

    upper = _upper_triangle(blk)
    off = jnp.zeros((N_HEADS, 1), F32)
    for j in range(past // blk):
        loc = _cumsum_lanes(clft_ref[:, j * blk:(j + 1) * blk], upper) + off
        crow[:, j * blk:(j + 1) * blk] = loc
        off = loc[:, blk - 1:blk]
    crow[:, 0:past] = crow[:, 0:past] - off

    cq_c = _cumsum_few(lf_ref[:, 0:N_HEADS], axis=0)
    cq_r = _cumsum_few(lft_ref[...], axis=1)
    crow[:, past:] = jnp.full((N_HEADS, LANES), MASKED_BIAS, F32)
    crow[:, past:past + dec] = cq_r

    q = q_ref[...]
    lane_head = lax.broadcasted_iota(jnp.int32, (dec, D_ATTN), 1) // HEAD_DIM
    q_exp = jnp.concatenate(
        [jnp.where(lane_head == h, q, jnp.zeros_like(q)) for h in range(N_HEADS)], axis=0)
    pad_rows = jnp.zeros((LANES - dec, D_ATTN), BF16)
    k_new = jnp.concatenate([kn_ref[...].astype(BF16), pad_rows], axis=0)
    v_new = jnp.concatenate([vn_ref[...].astype(BF16), pad_rows], axis=0)
    s_all = jnp.concatenate(
        [_dot(q_exp, ckt_ref[...].astype(BF16)), _dot_nt(q_exp, k_new)], axis=1)

    kpos = lax.broadcasted_iota(jnp.int32, (dec, n_keys), 1)
    qpos = past + lax.broadcasted_iota(jnp.int32, (dec, n_keys), 0)
    visible = kpos <= qpos
    probs = []
    norms = []
    for h in range(N_HEADS):
        s = s_all[h * dec:(h + 1) * dec, :] + cq_c[:, h:h + 1] - crow[h:h + 1, :]
        s = jnp.where(visible, s, -jnp.inf)
        p = jnp.exp(s - jnp.max(s, axis=-1, keepdims=True))
        norms.append(jnp.sum(p, axis=-1, keepdims=True))
        probs.append(p.astype(BF16))
    p_all = jnp.concatenate(probs, axis=0)
    o_all = (_dot_nt(p_all[:, 0:past], cvt_ref[...].astype(BF16))
             + _dot(p_all[:, past:], v_new))
    out = jnp.zeros((dec, D_ATTN), F32)
    for h in range(N_HEADS):
        o = o_all[h * dec:(h + 1) * dec, :] / norms[h]
        out = out + jnp.where(lane_head == h, o, 0.0)
    o_ref[...] = out.astype(BF16)


def _attn_kernel(*refs, seq, past, dec, n_cast):
    n_p, n_s = _N_PROMPT_IN, _N_SAMPLE_IN
    n_in = n_p + n_s + n_cast
    p_in, s_in, cast_in = refs[0:n_p], refs[n_p:n_p + n_s], refs[n_p + n_s:n_in]
    o_ref, os_ref = refs[n_in:n_in + 2]
    cast_out = refs[n_in + 2:n_in + 2 + n_cast]
    scratch = refs[n_in + 2 + n_cast:]
    for src, dst in zip(cast_in, cast_out):
        dst[...] = src[...].astype(BF16)
    _sample_attn_kernel(*s_in, os_ref, scratch[-1], past=past, dec=dec)
    _prompt_attn_kernel(*p_in, o_ref, *scratch[:-1], seq=seq)


_N_PROMPT_IN = 8
_N_SAMPLE_IN = 8


def _attention(prompt_in, sample_in, f32_wts):
    qt, _, vtb, vt_meta, lft, lft_meta, q_gain, k_gain = prompt_in
    q_s, _, _, _, _, cache_kt, _, _ = sample_in
    b, _, seq = qt.shape
    db, dec, _ = q_s.shape
    assert b == db, "one new stream and one running stream per grid step"
    past = cache_kt.shape[2]
    blk = ATTN_BLOCK
    n_blk = seq // blk
    n_pos = (n_blk + 1) * blk
    per_b = lambda *shape: pl.BlockSpec((None,) + shape, lambda i: (i,) + (0,) * len(shape))
    chunk_spec = lambda w: pl.BlockSpec((w.shape[0] // b, w.shape[1]), lambda i: (i, 0))
    in_specs = [per_b(D_ATTN, seq), per_b(n_pos, D_ATTN),
                per_b(D_ATTN, seq), _const_spec(vt_meta.shape),
                per_b(N_HEADS, seq), _const_spec(lft_meta.shape),
                _const_spec(q_gain.shape), _const_spec(k_gain.shape),
                per_b(dec, D_ATTN), per_b(dec, D_ATTN), per_b(dec, D_ATTN),
                per_b(dec, LANES), per_b(N_HEADS, dec),
                per_b(D_ATTN, past), per_b(D_ATTN, past), per_b(N_HEADS, past)]
    in_specs += [chunk_spec(w) for w in f32_wts]
    kern = functools.partial(_attn_kernel, seq=seq, past=past, dec=dec, n_cast=len(f32_wts))
    return pl.pallas_call(
        kern, grid=(b,), in_specs=in_specs,
        out_specs=(per_b(seq, D_ATTN), per_b(dec, D_ATTN))
        + tuple(chunk_spec(w) for w in f32_wts),
        out_shape=(jax.ShapeDtypeStruct((b, seq, D_ATTN), BF16),
                   jax.ShapeDtypeStruct((b, dec, D_ATTN), BF16))
        + tuple(jax.ShapeDtypeStruct(w.shape, BF16) for w in f32_wts),
        scratch_shapes=[pltpu.VMEM((n_blk + 1, N_HEADS * V_SLAB, blk), BF16),
                        pltpu.VMEM((n_pos, LANES), BF16),
                        pltpu.VMEM((n_blk, N_HEADS, blk), F32),
                        pltpu.VMEM((N_HEADS, LANES), F32),
                        pltpu.VMEM((N_HEADS, 2 * LANES, blk), BF16),
                        pltpu.VMEM((N_HEADS, blk), F32),
                        pltpu.VMEM((N_HEADS * V_SLAB, blk), F32),
                        pltpu.VMEM((N_HEADS, 2 * blk, blk), F32),
                        pltpu.VMEM((N_HEADS, past + LANES), F32)],
        compiler_params=_params(1), name="attention",
    )(*prompt_in, *sample_in, *f32_wts)


def _merge_mlp_tail(x, conv_bf16, attn_ref, g_conv, g_attn, wbc_ref, wba_ref, wo_ref, g2_ref,
                    wup_ref, wdn_ref, y_ref):
    merged = g_conv * _dot(conv_bf16, wbc_ref[...]) + g_attn * _dot(attn_ref[...], wba_ref[...])
    h = x + _dot(merged.astype(BF16), wo_ref[...])
    hn = _rms_rows(h, g2_ref[...])
    acc = h
    for c in range(D_FF // D_MODEL):
        cols = slice(c * D_MODEL, (c + 1) * D_MODEL)
        a = jnp.maximum(_dot(hn, wup_ref[:, cols]), 0.0)
        acc = acc + _dot((a * a).astype(BF16), wdn_ref[cols, :])
    y_ref[...] = acc


def _branch_mlp_kernel(x_ref, left_ref, attn_ref, xs_ref, convs_ref, attns_ref, gates_ref,
                       g1_ref, wa_ref, wgl_ref, cw_ref, cb_ref,
                       wbc_ref, wba_ref, wo_ref, g2_ref, wup_ref, wdn_ref,
                       y_ref, zlast_ref, ys_ref, zbuf, *, n_tiles, tiles_per_seq):
    step = pl.program_id(0)

    @pl.when(jnp.logical_and(step % tiles_per_seq == 0, step < n_tiles))
    def _():
        zbuf[:, ZPAD - CONV_HALO:ZPAD, :] = left_ref[...]

    @pl.when(step < n_tiles)
    def _():
        x = x_ref[...]
        xn = _rms_rows(x, g1_ref[...])
        (conv,), (tail,) = _short_conv(xn, wa_ref, cw_ref, cb_ref, zbuf, 1, x.shape[0])
        zlast_ref[0] = tail
        g_conv = jax.nn.sigmoid(_dot_nt(xn, wgl_ref[0:D_MODEL, :]))
        g_attn = jax.nn.sigmoid(_dot_nt(xn, wgl_ref[D_MODEL:2 * D_MODEL, :]))
        _merge_mlp_tail(x, conv.astype(BF16), attn_ref, g_conv, g_attn, wbc_ref, wba_ref,
                        wo_ref, g2_ref, wup_ref, wdn_ref, y_ref)

    @pl.when(step == n_tiles)
    def _():
        _merge_mlp_tail(xs_ref[...], convs_ref[...], attns_ref,
                        gates_ref[:, 0:D_MODEL].astype(F32),
                        gates_ref[:, D_MODEL:2 * D_MODEL].astype(F32),
                        wbc_ref, wba_ref, wo_ref, g2_ref, wup_ref, wdn_ref, ys_ref)


def _branch_mlp(x2d, left, attn, small, wts, *, rows, seq):
    n_rows = x2d.shape[0]
    tiles = seq // rows
    n_tiles = n_rows // rows
    last = n_tiles - 1
    row_spec = lambda width: pl.BlockSpec((rows, width), lambda i: (jnp.minimum(i, last), 0))
    seq_spec = pl.BlockSpec((1, CONV_HALO, D_CONV),
                            lambda i: (jnp.minimum(i, last) // tiles, 0, 0))
    full = lambda a: pl.BlockSpec(a.shape, lambda i: (0,) * a.ndim)
    kern = functools.partial(_branch_mlp_kernel, n_tiles=n_tiles, tiles_per_seq=tiles)
    return pl.pallas_call(
        kern, grid=(n_tiles + 1,),
        in_specs=[row_spec(D_MODEL), seq_spec, row_spec(D_ATTN)] + [full(a) for a in small]
        + [_const_spec(w.shape) for w in wts],
        out_specs=(row_spec(D_MODEL), seq_spec, full(small[0])),
        out_shape=(jax.ShapeDtypeStruct((n_rows, D_MODEL), F32),
                   jax.ShapeDtypeStruct((n_rows // seq, CONV_HALO, D_CONV), F32),
                   jax.ShapeDtypeStruct(small[0].shape, F32)),
        scratch_shapes=[pltpu.VMEM((1, rows + ZPAD, D_CONV), F32)],
        compiler_params=_params(1), name="branch_mlp",
    )(x2d, left, attn, *small, *wts)


def kernel(x_prompt, x_sample, cache_k, cache_v, cache_logf, state_conv, meta,
           norm1_g, w_in, b_f, conv_w, conv_b, q_norm_g, k_norm_g,
           w_br_conv, w_br_attn, w_out, norm2_g, w_up, w_down):
    b, seq, _ = x_prompt.shape
    db, dec, _ = x_sample.shape
    past = cache_k.shape[2]
    length = N_META + seq
    assert dec == N_META, "the small tile treats the meta tokens as one more short sequence"

    wt = w_in[0].T
    assert 2 * HEAD_DIM == LANES, "the q and k gains share one vreg row"
    g1 = norm1_g[0][None, :]
    conv_bias = conv_b[0][None, :]
    gains = jnp.concatenate([q_norm_g[0], k_norm_g[0]])[None, :]
    bf_lanes = jnp.broadcast_to(b_f[0][:, None], (N_HEADS, LANES))

    x_new = x_sample.reshape(db * dec, D_MODEL)
    (conv_s, q_s, k_s, v_s, k_meta, kt_s, vt_s, lft_s, lft_new, lfp_s, gate_s, left_p, zlast_s, wt_conv, wt_qkv, wt_gate, wt_fl,
     bf_row, qg_col, kg_col, conv_taps) = _project_small(
        meta, x_new, state_conv[0], wt, (g1, gains, bf_lanes, conv_w, conv_bias), b)
    qkv_wts = (g1, wt_qkv, bf_lanes, qg_col, kg_col)
    conv_wts = (conv_taps, conv_bias)

    x_rows = x_prompt.reshape(b * seq, D_MODEL)
    (qt_p, kt_p, vt_p, lftp_p, lft_p, kb_p, vtb_p,
     wdn_b) = _project_qkv(
        x_rows, qkv_wts, (kt_s, vt_s, lft_s, k_meta), (w_down[0],), b=b, seq=seq)
    k_new = k_s.reshape(db, dec, D_ATTN)
    v_new = v_s.reshape(db, dec, D_ATTN)
    cache_kt = jnp.transpose(cache_k[0], (0, 2, 3, 1)).reshape(db, D_ATTN, past)
    cache_vt = jnp.transpose(cache_v[0], (0, 2, 3, 1)).reshape(db, D_ATTN, past)
    attn_p, attn_s, wup_b, wbc_b, wba_b, wo_b = _attention(
        (qt_p, kb_p, vtb_p, vt_s, lft_p, lft_s, q_norm_g, k_norm_g),
        (q_s.reshape(db, dec, D_ATTN), k_new, v_new,
         lfp_s.reshape(db, dec, LANES), lft_new,
         cache_kt, cache_vt, jnp.swapaxes(cache_logf[0], 1, 2)),
        (w_up[0], w_br_conv[0], w_br_attn[0], w_out[0]))
    mlp_wts = (wbc_b, wba_b, wo_b, norm2_g[0][None, :], wup_b, wdn_b)

    y_prompt, zlast_p, y_sample = _branch_mlp(
        x_rows, left_p, attn_p.reshape(b * seq, D_ATTN),
        (x_new, conv_s, attn_s.reshape(db * dec, D_ATTN), gate_s),
        (g1, wt_conv, wt_gate) + conv_wts + mlp_wts, rows=MLP_ROWS, seq=seq)

    def heads_last(t):
        return jnp.transpose(t.reshape(b, N_HEADS, HEAD_DIM, length), (0, 3, 1, 2))[None]

    return (y_prompt.reshape(b, seq, D_MODEL),
            y_sample.reshape(db, dec, D_MODEL),
            heads_last(kt_p),
            heads_last(vt_p),
            jnp.swapaxes(lftp_p, 1, 2)[None],
            zlast_p[None],
            k_new.reshape(1, db, dec, N_HEADS, HEAD_DIM),
            v_new.reshape(1, db, dec, N_HEADS, HEAD_DIM),
            jnp.swapaxes(lft_new, 1, 2)[None],
            zlast_s[None])
```

```python
import functools

import jax
import jax.numpy as jnp
import numpy as np
from jax import lax
from jax.experimental import pallas as pl
from jax.experimental.pallas import tpu as pltpu

D_MODEL = 1024
D_CONV = D_MODEL // 2
CONV_W = 3
N_HEADS = 8
HEAD_DIM = 64
D_ATTN = N_HEADS * HEAD_DIM
D_FF = 4 * D_MODEL
N_META = 16
EPS = 1e-6
ATTN_SCALE = HEAD_DIM ** -0.5

F32 = jnp.float32
BF16 = jnp.bfloat16

VMEM_LIMIT_BYTES = 56 * 1024 * 1024
LANES = 128
SUBLANES = 8
BF16_ROWS = 16
PROJ_ROWS = 512
MLP_ROWS = 512
ATTN_BLOCK = 256
MASKED_BIAS = 1e30
CONV_HALO = CONV_W - 1
ZPAD = SUBLANES
N_SPLIT = 3
V_SLAB = HEAD_DIM + BF16_ROWS
LOG2E = 1.4426950408889634
SKIP_LOG2 = 40.0
NORM_SLACK = 1.02


def _dot(a, b):
    return jnp.dot(a, b, preferred_element_type=F32)


def _dot_nt(a, b):
    return lax.dot_general(a, b, (((1,), (1,)), ((), ())), preferred_element_type=F32)


def _log_sigmoid(x):
    return jnp.minimum(x, 0.0) - jnp.log1p(jnp.exp(-jnp.abs(x)))


def _cumsum_few(x, axis):
    n = x.shape[axis]
    idx = lax.broadcasted_iota(jnp.int32, x.shape, axis)
    out = jnp.zeros(x.shape, F32)
    for i in range(n):
        term = x[i:i + 1, :] if axis == 0 else x[:, i:i + 1]
        out = out + jnp.where(idx >= i, term, 0.0)
    return out


def _upper_triangle(n):
    r = lax.broadcasted_iota(jnp.int32, (n, n), 0)
    c = lax.broadcasted_iota(jnp.int32, (n, n), 1)
    return jnp.where(r <= c, 1.0, 0.0).astype(BF16)


def _split3(c):
    hi = c.astype(BF16).astype(F32)
    r1 = c - hi
    mid = r1.astype(BF16).astype(F32)
    return hi, mid, r1 - mid


def _cumsum_lanes(x, tri_upper):
    h = x.shape[0]
    pieces = jnp.concatenate(_split3(x), axis=0).astype(BF16)
    y = _dot(pieces, tri_upper)
    return y[0:h] + y[h:2 * h] + y[2 * h:3 * h]


def _const_spec(shape):
    nd = len(shape)
    return pl.BlockSpec(shape, lambda *_: (0,) * nd, pipeline_mode=pl.Buffered(1))


def _params(n_axes):
    return pltpu.CompilerParams(
        dimension_semantics=("arbitrary",) * n_axes,
        vmem_limit_bytes=VMEM_LIMIT_BYTES)


def _rms_rows(x, g_row):
    ms = jnp.mean(x * x, axis=-1, keepdims=True)
    return (x * lax.rsqrt(ms + EPS) * g_row).astype(BF16)


def _head_norm_t(ut, g_col):
    out = []
    for h in range(N_HEADS):
        blk = ut[h * HEAD_DIM:(h + 1) * HEAD_DIM, :]
        ms = jnp.mean(blk * blk, axis=0, keepdims=True)
        out.append(blk * lax.rsqrt(ms + EPS) * g_col[h * HEAD_DIM:(h + 1) * HEAD_DIM, :])
    return jnp.concatenate(out, axis=0)


def _short_conv(xn, wa_ref, cw_ref, cb_ref, zbuf, n_seg, seg_len):
    cb = _dot_nt(xn, wa_ref[0:D_CONV, :])
    z = (_dot_nt(xn, wa_ref[D_CONV:2 * D_CONV, :])
         * _dot_nt(xn, wa_ref[2 * D_CONV:3 * D_CONV, :]))
    out, tails = [], []
    for s in range(n_seg):
        r0 = s * seg_len
        zs = z[r0:r0 + seg_len]
        zbuf[s, ZPAD:ZPAD + seg_len, :] = zs
        y = None
        for i in range(CONV_W):
            lo = ZPAD - CONV_HALO + i
            tap = zs if i == CONV_HALO else zbuf[s, lo:lo + seg_len, :]
            term = tap * cw_ref[i:i + 1, :]
            y = term if y is None else y + term
        out.append(cb[r0:r0 + seg_len] * (y + cb_ref[...]))
        tail = zbuf[s, ZPAD + seg_len - CONV_HALO:ZPAD + seg_len, :]
        tails.append(tail)
        zbuf[s, ZPAD - CONV_HALO:ZPAD, :] = tail
    return out, tails


def _qkv_kernel(*refs, tiles_per_seq, n_cast):
    (x_ref, g1_ref, wqkv_ref, bfb_ref, qgc_ref, kgc_ref,
     ktm_ref, vtm_ref, lftm_ref, kbm_ref) = refs[:10]
    cast_in = refs[10:10 + n_cast]
    (qt_ref, kt_ref, vt_ref, lftp_ref, lft_ref, kb_ref,
     vtb_ref) = refs[10 + n_cast:17 + n_cast]
    cast_out = refs[17 + n_cast:17 + 2 * n_cast]
    kcar, vcar, lcar = refs[17 + 2 * n_cast:]
    rows = x_ref.shape[0]
    step = pl.program_id(0) % tiles_per_seq

    def tile_body(tile_idx):
        lane0 = tile_idx * rows

        def shifted_store(out_ref, car_ref, meta_ref, tile):
            left = meta_ref[:, 0:LANES] if tile_idx == 0 else car_ref[...]
            rolled = pltpu.roll(tile, N_META, axis=1)
            lane = lax.broadcasted_iota(jnp.int32, (tile.shape[0], LANES), 1)
            out_ref[:, lane0:lane0 + LANES] = jnp.where(lane < N_META, left, rolled[:, 0:LANES])
            out_ref[:, lane0 + LANES:lane0 + rows] = rolled[:, LANES:]
            if tile_idx < tiles_per_seq - 1:
                car_ref[...] = rolled[:, 0:LANES]
            else:
                out_ref[:, tiles_per_seq * rows:] = rolled[:, 0:N_META]

        xn = _rms_rows(x_ref[...], g1_ref[...])

        def feature_major(j):
            return _dot_nt(wqkv_ref[j * D_ATTN:(j + 1) * D_ATTN, :], xn)

        qt = _head_norm_t(feature_major(0), qgc_ref[...])
        qt_ref[...] = (qt * (ATTN_SCALE * LOG2E)).astype(BF16)
        kt = _head_norm_t(feature_major(1), kgc_ref[...])
        v_f = _dot_nt(wqkv_ref[2 * D_ATTN:3 * D_ATTN + BF16_ROWS, :], xn)
        vt = v_f[0:D_ATTN]
        lft = _log_sigmoid(v_f[D_ATTN:D_ATTN + N_HEADS] + bfb_ref[:, 0:1])
        lft_ref[...] = lft
        key0 = ATTN_BLOCK + lane0
        kb_ref[key0:key0 + rows, :] = kt.T.astype(BF16)
        if tile_idx == 0:
            kb_ref[0:ATTN_BLOCK - N_META, :] = jnp.zeros((ATTN_BLOCK - N_META, D_ATTN), BF16)
            kb_ref[ATTN_BLOCK - N_META:ATTN_BLOCK, :] = kbm_ref[...]
        vtb_ref[...] = vt.astype(BF16)
        shifted_store(kt_ref, kcar, ktm_ref, kt)
        shifted_store(vt_ref, vcar, vtm_ref, vt)
        shifted_store(lftp_ref, lcar, lftm_ref, lft)
        for src, dst in zip(cast_in, cast_out):
            dst[...] = src[...].astype(BF16)

    for tile_idx in range(tiles_per_seq):
        pl.when(step == tile_idx)(functools.partial(tile_body, tile_idx))


def _project_qkv(x2d, wts, meta_cols, f32_wts, *, b, seq):
    rows = PROJ_ROWS
    tiles = seq // rows
    length = N_META + seq
    n_steps = b * tiles
    chunk_spec = lambda w: pl.BlockSpec((w.shape[0] // n_steps, w.shape[1]), lambda i: (i, 0))
    row_spec = lambda width: pl.BlockSpec((rows, width), lambda i: (i, 0))
    col_spec = lambda feat: pl.BlockSpec(
        (None, feat, rows), lambda i: (i // tiles, 0, i % tiles))
    seq_spec = lambda feat: pl.BlockSpec((None, feat, length), lambda i: (i // tiles, 0, 0))
    out_shape = (
        jax.ShapeDtypeStruct((b, D_ATTN, seq), BF16),
        jax.ShapeDtypeStruct((b, D_ATTN, length), F32),
        jax.ShapeDtypeStruct((b, D_ATTN, length), F32),
        jax.ShapeDtypeStruct((b, N_HEADS, length), F32),
        jax.ShapeDtypeStruct((b, N_HEADS, seq), F32),
        jax.ShapeDtypeStruct((b, ATTN_BLOCK + seq, D_ATTN), BF16),
        jax.ShapeDtypeStruct((b, D_ATTN, seq), BF16),
    )
    key_spec = pl.BlockSpec((None, ATTN_BLOCK + seq, D_ATTN), lambda i: (i // tiles, 0, 0))
    out_specs = (col_spec(D_ATTN), seq_spec(D_ATTN), seq_spec(D_ATTN), seq_spec(N_HEADS),
                 col_spec(N_HEADS), key_spec, col_spec(D_ATTN))
    out_shape = out_shape + tuple(jax.ShapeDtypeStruct(w.shape, BF16) for w in f32_wts)
    out_specs = out_specs + tuple(chunk_spec(w) for w in f32_wts)
    kern = functools.partial(_qkv_kernel, tiles_per_seq=tiles, n_cast=len(f32_wts))
    return pl.pallas_call(
        kern, grid=(n_steps,),
        in_specs=[row_spec(D_MODEL)] + [_const_spec(w.shape) for w in wts + meta_cols]
        + [chunk_spec(w) for w in f32_wts],
        out_specs=out_specs, out_shape=out_shape,
        scratch_shapes=[pltpu.VMEM((D_ATTN, LANES), F32),
                        pltpu.VMEM((D_ATTN, LANES), F32),
                        pltpu.VMEM((N_HEADS, LANES), F32)],
        compiler_params=_params(1), name="proj_qkv",
    )(x2d, *wts, *meta_cols, *f32_wts)


def _proj_small_kernel(meta_ref, xs_ref, state_ref, wt_ref, g1_ref, gains_ref, bfb_ref,
                       cwin_ref, cb_ref, bd_ref,
                       conv_ref, q_ref, k_ref, v_ref, km_ref, kt_ref, vt_ref,
                       lft_ref, lftn_ref, lfpn_ref,
                       gate_ref, leftp_ref, zlast_ref,
                       wa_ref, wqkv_ref, wgl_ref, wfl_ref,
                       bfr_ref, qgc_ref, kgc_ref, cw_ref, zbuf):
    n_new, _, dec = lftn_ref.shape
    n_main = 3 * D_CONV + 3 * D_ATTN
    wa_ref[...] = wt_ref[0:3 * D_CONV, :].astype(BF16)
    wqkv_ref[0:3 * D_ATTN, :] = wt_ref[3 * D_CONV:n_main, :].astype(BF16)
    wgl_ref[...] = wt_ref[n_main + N_HEADS:, :].astype(BF16)
    w_fl = wt_ref[n_main:n_main + N_HEADS, :]
    wfl_ref[...] = jnp.concatenate([w_fl] * (LANES // N_HEADS), axis=0).astype(BF16)
    wqkv_ref[3 * D_ATTN:, :] = wfl_ref[0:BF16_ROWS, :]
    wflt_ref = wfl_ref.at[0:BF16_ROWS]

    qk = jnp.broadcast_to(gains_ref[...], (SUBLANES, LANES))
    kq = pltpu.roll(qk, HEAD_DIM, axis=1)
    first = lax.broadcasted_iota(jnp.int32, (SUBLANES, LANES), 1) < HEAD_DIM

    def per_feature(two_heads):
        rows = jnp.concatenate([two_heads] * (D_ATTN // LANES), axis=1)
        return rows[0:1, :], rows.T[:, 0:1]

    qgr, qgc = per_feature(jnp.where(first, qk, kq))
    kgr, kgc = per_feature(jnp.where(first, kq, qk))
    qgc_ref[...] = qgc
    kgc_ref[...] = kgc
    bfc = bfb_ref[:, 0:1]
    bfr_ref[...] = jnp.concatenate([bfb_ref[...]] * (LANES // N_HEADS), axis=0).T[0:1, :]
    cw_ref[...] = cwin_ref[0]

    xn = _rms_rows(jnp.concatenate([meta_ref[...], xs_ref[...]], axis=0), g1_ref[...])
    xn_new = xn[N_META:]
    zbuf[0, ZPAD - CONV_HALO:ZPAD, :] = jnp.zeros((CONV_HALO, D_CONV), F32)
    zbuf[1:, ZPAD - CONV_HALO:ZPAD, :] = state_ref[...]
    conv, tails = _short_conv(xn, wa_ref, cw_ref, cb_ref, zbuf, n_new + 1, dec)
    for s in range(n_new):
        conv_ref[s * dec:(s + 1) * dec, :] = conv[s + 1].astype(BF16)
        zlast_ref[s] = tails[s + 1]
    for i in range(leftp_ref.shape[0]):
        leftp_ref[i] = tails[0]

    def rows_major(j, rows):
        return _dot_nt(rows, wqkv_ref[j * D_ATTN:(j + 1) * D_ATTN, :])

    def feature_major(j):
        return _dot_nt(wqkv_ref[j * D_ATTN:(j + 1) * D_ATTN, :], xn)

    def head_norm(u, g_row):
        ssq = _dot((u * u).astype(BF16), bd_ref[...])
        return u * lax.rsqrt(ssq * (1.0 / HEAD_DIM) + EPS) * g_row

    q_ref[...] = (head_norm(rows_major(0, xn_new), qgr) * ATTN_SCALE).astype(BF16)
    k = head_norm(rows_major(1, xn), kgr)
    km_ref[...] = k[0:N_META].astype(BF16)
    k_ref[...] = k[N_META:]
    v_ref[...] = rows_major(2, xn_new)
    kt_ref[...] = _head_norm_t(feature_major(1), kgc_ref[...])
    vt_ref[...] = feature_major(2)
    lfpn_ref[...] = _log_sigmoid(_dot_nt(xn_new, wfl_ref[...]) + bfr_ref[...])
    lft = _log_sigmoid(_dot_nt(wflt_ref[...], xn)[0:N_HEADS] + bfc)
    lft_ref[...] = lft
    for s in range(n_new):
        lftn_ref[s] = lft[:, N_META + s * dec:N_META + (s + 1) * dec]
    gate_ref[...] = jax.nn.sigmoid(_dot_nt(xn_new, wgl_ref[...])).astype(BF16)


def _project_small(meta, x_new, state, wt, small_params, n_prompt):
    n_new = state.shape[0]
    n_rows = x_new.shape[0]
    dec = n_rows // n_new
    n_all = N_META + n_rows
    n_main = 3 * D_CONV + 3 * D_ATTN
    head_of = np.arange(D_ATTN) // HEAD_DIM
    same_head = jnp.asarray(head_of[:, None] == head_of[None, :], BF16)
    full = lambda *shape: pl.BlockSpec(shape, lambda i: (0,) * len(shape))
    out_shape = (
        jax.ShapeDtypeStruct((n_rows, D_CONV), BF16),
        jax.ShapeDtypeStruct((n_rows, D_ATTN), BF16),
        jax.ShapeDtypeStruct((n_rows, D_ATTN), F32),
        jax.ShapeDtypeStruct((n_rows, D_ATTN), F32),
        jax.ShapeDtypeStruct((N_META, D_ATTN), BF16),
        jax.ShapeDtypeStruct((D_ATTN, n_all), F32),
        jax.ShapeDtypeStruct((D_ATTN, n_all), F32),
        jax.ShapeDtypeStruct((N_HEADS, n_all), F32),
        jax.ShapeDtypeStruct((n_new, N_HEADS, dec), F32),
        jax.ShapeDtypeStruct((n_rows, LANES), F32),
        jax.ShapeDtypeStruct((n_rows, 2 * D_MODEL), BF16),
        jax.ShapeDtypeStruct((n_prompt, CONV_HALO, D_CONV), F32),
        jax.ShapeDtypeStruct((n_new, CONV_HALO, D_CONV), F32),
        jax.ShapeDtypeStruct((3 * D_CONV, D_MODEL), BF16),
        jax.ShapeDtypeStruct((3 * D_ATTN + BF16_ROWS, D_MODEL), BF16),
        jax.ShapeDtypeStruct((wt.shape[0] - n_main - N_HEADS, D_MODEL), BF16),
        jax.ShapeDtypeStruct((LANES, D_MODEL), BF16),
        jax.ShapeDtypeStruct((1, LANES), F32),
        jax.ShapeDtypeStruct((D_ATTN, 1), F32),
        jax.ShapeDtypeStruct((D_ATTN, 1), F32),
        jax.ShapeDtypeStruct((CONV_W, D_CONV), F32),
    )
    ins = (meta, x_new, state, wt) + tuple(small_params) + (same_head,)
    return pl.pallas_call(
        _proj_small_kernel, grid=(1,),
        in_specs=[full(*meta.shape), full(*x_new.shape), full(*state.shape)]
        + [_const_spec(w.shape) for w in ins[3:]],
        out_specs=tuple(full(*s.shape) for s in out_shape), out_shape=out_shape,
        scratch_shapes=[pltpu.VMEM((n_new + 1, dec + ZPAD, D_CONV), F32)],
        compiler_params=_params(1), name="proj_small",
    )(*ins)


def _prompt_attn_kernel(qt_ref, kpos, vtb_ref, vtm_ref, lft_ref, lftm_ref,
                        qg_ref, kg_ref, o_ref, vb, kbias, crow, cend, qcat,
                        m_s, acc_s, sbuf, *, seq):
    blk = ATTN_BLOCK
    n_blk = seq // blk
    n_bias = N_SPLIT * N_HEADS
    pad = blk - N_META

    ones_row = (lax.broadcasted_iota(jnp.int32, (V_SLAB - HEAD_DIM, blk), 0) == 0).astype(BF16)
    lane_m = lax.broadcasted_iota(jnp.int32, (HEAD_DIM, LANES), 1)
    vb[0] = jnp.zeros((N_HEADS * V_SLAB, blk), BF16)
    for h in range(N_HEADS):
        rows = slice(h * HEAD_DIM, (h + 1) * HEAD_DIM)
        slab = slice(h * V_SLAB, h * V_SLAB + HEAD_DIM)
        meta = jnp.where(lane_m < N_META, vtm_ref[rows, 0:LANES], 0.0)
        vb[0, slab, blk - LANES:] = pltpu.roll(meta, LANES - N_META, axis=1).astype(BF16)
        for j in range(n_blk):
            vb[j + 1, slab, :] = vtb_ref[rows, j * blk:(j + 1) * blk]
        for j in range(n_blk + 1):
            vb[j, h * V_SLAB + HEAD_DIM:(h + 1) * V_SLAB, :] = ones_row

    qk_bound = (NORM_SLACK * HEAD_DIM * ATTN_SCALE * LOG2E
                * jnp.max(jnp.abs(qg_ref[...]), axis=1, keepdims=True)
                * jnp.max(jnp.abs(kg_ref[...]), axis=1, keepdims=True))

    upper = _upper_triangle(blk)
    ones_zeros = jnp.concatenate([jnp.ones((n_bias, blk), F32),
                                  jnp.zeros((LANES - 2 * n_bias, blk), F32)], axis=0)

    def store_kbias(j, c_log2, is_pad=None):
        neg = [-piece for piece in _split3(c_log2)]
        if is_pad is not None:
            neg = [jnp.where(is_pad, fill, piece)
                   for piece, fill in zip(neg, (-MASKED_BIAS, 0.0, 0.0))]
        kbias[j * blk:(j + 1) * blk, :] = jnp.concatenate(neg + [ones_zeros], axis=0).T.astype(BF16)

    lane_h = lax.broadcasted_iota(jnp.int32, (N_HEADS, LANES), 1)
    lf_meta = jnp.where(lane_h < N_META, lftm_ref[:, 0:LANES], 0.0)
    lf_blk0 = jnp.concatenate([jnp.zeros((N_HEADS, blk - LANES), F32),
                               pltpu.roll(lf_meta, LANES - N_META, axis=1)], axis=1)
    c_row = _cumsum_lanes(lf_blk0, upper)
    store_kbias(0, c_row * LOG2E, lax.broadcasted_iota(jnp.int32, (N_HEADS, blk), 1) < pad)
    off_r = c_row[:, blk - 1:blk]
    c_end = jnp.where(lane_h == 0, off_r * LOG2E, 0.0)
    for j in range(n_blk):
        c_row = _cumsum_lanes(lft_ref[:, j * blk:(j + 1) * blk], upper) + off_r
        crow[j] = c_row * LOG2E
        store_kbias(j + 1, c_row * LOG2E)
        off_r = c_row[:, blk - 1:blk]
        c_end = jnp.where(lane_h == j + 1, off_r * LOG2E, c_end)
    cend[...] = c_end

    row128 = lax.broadcasted_iota(jnp.int32, (LANES, blk), 0)
    krow = lax.broadcasted_iota(jnp.int32, (2 * blk, blk), 0)
    qcol = lax.broadcasted_iota(jnp.int32, (2 * blk, blk), 1)

    def q_block(t, _):
        tok0 = pl.multiple_of(t * blk, blk)
        c_q = crow[t]
        hi, mid, lo = _split3(c_q)
        bias_rows = jnp.concatenate(
            [jnp.ones((n_bias, blk), F32), hi, mid, lo,
             jnp.zeros((LANES - 2 * n_bias, blk), F32)], axis=0)
        for h in range(N_HEADS):
            pair = qt_ref[(h // 2) * LANES:(h // 2 + 1) * LANES, pl.ds(tok0, blk)]
            in_head = (row128 // HEAD_DIM) == (h % 2)
            qcat[h, 0:LANES, :] = jnp.where(in_head, pair, jnp.zeros_like(pair))
            qcat[h, LANES:, :] = jnp.where(row128 % N_HEADS == h, bias_rows, 0.0).astype(BF16)
        m_s[...] = jnp.full(m_s.shape, -jnp.inf, F32)
        acc_s[...] = jnp.zeros(acc_s.shape, F32)

        gap = 2.0 * qk_bound + c_q[:, 0:1] - cend[...]
        needed = jnp.logical_and(gap >= -SKIP_LOG2, lane_h < t)
        n_needed = jnp.max(
            jnp.sum(jnp.where(needed, 1.0, 0.0), axis=1, keepdims=True)).astype(jnp.int32)

        def key_pass(row0, n_rows, pv, visible):
            m_blk = []
            for h in range(N_HEADS):
                g = h // 2
                kc = jnp.concatenate([kpos[pl.ds(row0, n_rows), g * LANES:(g + 1) * LANES],
                                      kbias[pl.ds(row0, n_rows), :]], axis=1)
                s = _dot(kc, qcat[h])
                if visible is not None:
                    s = jnp.where(visible, s, -jnp.inf)
                sbuf[h, 0:n_rows, :] = s
                m_blk.append(jnp.max(s, axis=0, keepdims=True))
            for h in range(N_HEADS):
                slab = slice(h * V_SLAB, (h + 1) * V_SLAB)
                m_old = m_s[h:h + 1, :]
                m_new = jnp.maximum(m_old, m_blk[h])
                alpha = jnp.exp2(m_old - m_new)
                p = jnp.exp2(sbuf[h, 0:n_rows, :] - m_new)
                m_s[h:h + 1, :] = m_new
                acc_s[slab, :] = alpha * acc_s[slab, :] + pv(slab, p.astype(BF16))

        key_pass(tok0, 2 * blk,
                 lambda slab, p: (_dot(vb[t, slab, :], p[0:blk])
                                  + _dot(vb[t + 1, slab, :], p[blk:])),
                 krow <= qcol + blk)

        def older_block(j, _):
            key_pass(pl.multiple_of(j * blk, blk), blk,
                     lambda slab, p: _dot(vb[j, slab, :], p), None)
            return 0

        lax.fori_loop(t - n_needed, t, older_block, 0)

        o_t = []
        for h in range(N_HEADS):
            norm = acc_s[h * V_SLAB + HEAD_DIM:h * V_SLAB + HEAD_DIM + 1, :]
            o_t.append(acc_s[h * V_SLAB:h * V_SLAB + HEAD_DIM, :] * (1.0 / norm))
        o_ref[pl.ds(tok0, blk), :] = jnp.concatenate(o_t, axis=0).T.astype(BF16)
        return 0

    lax.fori_loop(0, n_blk, q_block, 0)


def _sample_attn_kernel(q_ref, kn_ref, vn_ref, lf_ref, lft_ref, ckt_ref, cvt_ref, clft_ref,
                        o_ref, crow, *, past, dec):
    blk = ATTN_BLOCK
    n_keys = past + LANES

    upper = _upper_triangle(blk)
    off = jnp.zeros((N_HEADS, 1), F32)
    for j in range(past // blk):
        loc = _cumsum_lanes(clft_ref[:, j * blk:(j + 1) * blk], upper) + off
        crow[:, j * blk:(j + 1) * blk] = loc
        off = loc[:, blk - 1:blk]
    crow[:, 0:past] = crow[:, 0:past] - off

    cq_c = _cumsum_few(lf_ref[:, 0:N_HEADS], axis=0)
    cq_r = _cumsum_few(lft_ref[...], axis=1)
    crow[:, past:] = jnp.full((N_HEADS, LANES), MASKED_BIAS, F32)
    crow[:, past:past + dec] = cq_r

    q = q_ref[...]
    lane_head = lax.broadcasted_iota(jnp.int32, (dec, D_ATTN), 1) // HEAD_DIM
    q_exp = jnp.concatenate(
        [jnp.where(lane_head == h, q, jnp.zeros_like(q)) for h in range(N_HEADS)], axis=0)
    pad_rows = jnp.zeros((LANES - dec, D_ATTN), BF16)
    k_new = jnp.concatenate([kn_ref[...].astype(BF16), pad_rows], axis=0)
    v_new = jnp.concatenate([vn_ref[...].astype(BF16), pad_rows], axis=0)
    s_all = jnp.concatenate(
        [_dot(q_exp, ckt_ref[...].astype(BF16)), _dot_nt(q_exp, k_new)], axis=1)

    kpos = lax.broadcasted_iota(jnp.int32, (dec, n_keys), 1)
    qpos = past + lax.broadcasted_iota(jnp.int32, (dec, n_keys), 0)
    visible = kpos <= qpos
    probs = []
    norms = []
    for h in range(N_HEADS):
        s = s_all[h * dec:(h + 1) * dec, :] + cq_c[:, h:h + 1] - crow[h:h + 1, :]
        s = jnp.where(visible, s, -jnp.inf)
        p = jnp.exp(s - jnp.max(s, axis=-1, keepdims=True))
        norms.append(jnp.sum(p, axis=-1, keepdims=True))
        probs.append(p.astype(BF16))
    p_all = jnp.concatenate(probs, axis=0)
    o_all = (_dot_nt(p_all[:, 0:past], cvt_ref[...].astype(BF16))
             + _dot(p_all[:, past:], v_new))
    out = jnp.zeros((dec, D_ATTN), F32)
    for h in range(N_HEADS):
        o = o_all[h * dec:(h + 1) * dec, :] / norms[h]
        out = out + jnp.where(lane_head == h, o, 0.0)
    o_ref[...] = out.astype(BF16)


def _attn_kernel(*refs, seq, past, dec, n_cast):
    n_p, n_s = _N_PROMPT_IN, _N_SAMPLE_IN
    n_in = n_p + n_s + n_cast
    p_in, s_in, cast_in = refs[0:n_p], refs[n_p:n_p + n_s], refs[n_p + n_s:n_in]
    o_ref, os_ref = refs[n_in:n_in + 2]
    cast_out = refs[n_in + 2:n_in + 2 + n_cast]
    scratch = refs[n_in + 2 + n_cast:]
    for src, dst in zip(cast_in, cast_out):
        dst[...] = src[...].astype(BF16)
    _sample_attn_kernel(*s_in, os_ref, scratch[-1], past=past, dec=dec)
    _prompt_attn_kernel(*p_in, o_ref, *scratch[:-1], seq=seq)


_N_PROMPT_IN = 8
_N_SAMPLE_IN = 8


def _attention(prompt_in, sample_in, f32_wts):
    qt, _, vtb, vt_meta, lft, lft_meta, q_gain, k_gain = prompt_in
    q_s, _, _, _, _, cache_kt, _, _ = sample_in
    b, _, seq = qt.shape
    db, dec, _ = q_s.shape
    assert b == db, "one new stream and one running stream per grid step"
    past = cache_kt.shape[2]
    blk = ATTN_BLOCK
    n_blk = seq // blk
    n_pos = (n_blk + 1) * blk
    per_b = lambda *shape: pl.BlockSpec((None,) + shape, lambda i: (i,) + (0,) * len(shape))
    chunk_spec = lambda w: pl.BlockSpec((w.shape[0] // b, w.shape[1]), lambda i: (i, 0))
    in_specs = [per_b(D_ATTN, seq), per_b(n_pos, D_ATTN),
                per_b(D_ATTN, seq), _const_spec(vt_meta.shape),
                per_b(N_HEADS, seq), _const_spec(lft_meta.shape),
                _const_spec(q_gain.shape), _const_spec(k_gain.shape),
                per_b(dec, D_ATTN), per_b(dec, D_ATTN), per_b(dec, D_ATTN),
                per_b(dec, LANES), per_b(N_HEADS, dec),
                per_b(D_ATTN, past), per_b(D_ATTN, past), per_b(N_HEADS, past)]
    in_specs += [chunk_spec(w) for w in f32_wts]
    kern = functools.partial(_attn_kernel, seq=seq, past=past, dec=dec, n_cast=len(f32_wts))
    return pl.pallas_call(
        kern, grid=(b,), in_specs=in_specs,
        out_specs=(per_b(seq, D_ATTN), per_b(dec, D_ATTN))
        + tuple(chunk_spec(w) for w in f32_wts),
        out_shape=(jax.ShapeDtypeStruct((b, seq, D_ATTN), BF16),
                   jax.ShapeDtypeStruct((b, dec, D_ATTN), BF16))
        + tuple(jax.ShapeDtypeStruct(w.shape, BF16) for w in f32_wts),
        scratch_shapes=[pltpu.VMEM((n_blk + 1, N_HEADS * V_SLAB, blk), BF16),
                        pltpu.VMEM((n_pos, LANES), BF16),
                        pltpu.VMEM((n_blk, N_HEADS, blk), F32),
                        pltpu.VMEM((N_HEADS, LANES), F32),
                        pltpu.VMEM((N_HEADS, 2 * LANES, blk), BF16),
                        pltpu.VMEM((N_HEADS, blk), F32),
                        pltpu.VMEM((N_HEADS * V_SLAB, blk), F32),
                        pltpu.VMEM((N_HEADS, 2 * blk, blk), F32),
                        pltpu.VMEM((N_HEADS, past + LANES), F32)],
        compiler_params=_params(1), name="attention",
    )(*prompt_in, *sample_in, *f32_wts)


def _merge_mlp_tail(x, conv_bf16, attn_ref, g_conv, g_attn, wbc_ref, wba_ref, wo_ref, g2_ref,
                    wup_ref, wdn_ref, y_ref):
    merged = g_conv * _dot(conv_bf16, wbc_ref[...]) + g_attn * _dot(attn_ref[...], wba_ref[...])
    h = x + _dot(merged.astype(BF16), wo_ref[...])
    hn = _rms_rows(h, g2_ref[...])
    acc = h
    for c in range(D_FF // D_MODEL):
        cols = slice(c * D_MODEL, (c + 1) * D_MODEL)
        a = jnp.maximum(_dot(hn, wup_ref[:, cols]), 0.0)
        acc = acc + _dot((a * a).astype(BF16), wdn_ref[cols, :])
    y_ref[...] = acc


def _branch_mlp_kernel(x_ref, left_ref, attn_ref, xs_ref, convs_ref, attns_ref, gates_ref,
                       g1_ref, wa_ref, wgl_ref, cw_ref, cb_ref,
                       wbc_ref, wba_ref, wo_ref, g2_ref, wup_ref, wdn_ref,
                       y_ref, zlast_ref, ys_ref, zbuf, *, n_tiles, tiles_per_seq):
    step = pl.program_id(0)

    @pl.when(jnp.logical_and(step % tiles_per_seq == 0, step < n_tiles))
    def _():
        zbuf[:, ZPAD - CONV_HALO:ZPAD, :] = left_ref[...]

    @pl.when(step < n_tiles)
    def _():
        x = x_ref[...]
        xn = _rms_rows(x, g1_ref[...])
        (conv,), (tail,) = _short_conv(xn, wa_ref, cw_ref, cb_ref, zbuf, 1, x.shape[0])
        zlast_ref[0] = tail
        g_conv = jax.nn.sigmoid(_dot_nt(xn, wgl_ref[0:D_MODEL, :]))
        g_attn = jax.nn.sigmoid(_dot_nt(xn, wgl_ref[D_MODEL:2 * D_MODEL, :]))
        _merge_mlp_tail(x, conv.astype(BF16), attn_ref, g_conv, g_attn, wbc_ref, wba_ref,
                        wo_ref, g2_ref, wup_ref, wdn_ref, y_ref)

    @pl.when(step == n_tiles)
    def _():
        _merge_mlp_tail(xs_ref[...], convs_ref[...], attns_ref,
                        gates_ref[:, 0:D_MODEL].astype(F32),
                        gates_ref[:, D_MODEL:2 * D_MODEL].astype(F32),
                        wbc_ref, wba_ref, wo_ref, g2_ref, wup_ref, wdn_ref, ys_ref)


def _branch_mlp(x2d, left, attn, small, wts, *, rows, seq):
    n_rows = x2d.shape[0]
    tiles = seq // rows
    n_tiles = n_rows // rows
    last = n_tiles - 1
    row_spec = lambda width: pl.BlockSpec((rows, width), lambda i: (jnp.minimum(i, last), 0))
    seq_spec = pl.BlockSpec((1, CONV_HALO, D_CONV),
                            lambda i: (jnp.minimum(i, last) // tiles, 0, 0))
    full = lambda a: pl.BlockSpec(a.shape, lambda i: (0,) * a.ndim)
    kern = functools.partial(_branch_mlp_kernel, n_tiles=n_tiles, tiles_per_seq=tiles)
    return pl.pallas_call(
        kern, grid=(n_tiles + 1,),
        in_specs=[row_spec(D_MODEL), seq_spec, row_spec(D_ATTN)] + [full(a) for a in small]
        + [_const_spec(w.shape) for w in wts],
        out_specs=(row_spec(D_MODEL), seq_spec, full(small[0])),
        out_shape=(jax.ShapeDtypeStruct((n_rows, D_MODEL), F32),
                   jax.ShapeDtypeStruct((n_rows // seq, CONV_HALO, D_CONV), F32),
                   jax.ShapeDtypeStruct(small[0].shape, F32)),
        scratch_shapes=[pltpu.VMEM((1, rows + ZPAD, D_CONV), F32)],
        compiler_params=_params(1), name="branch_mlp",
    )(x2d, left, attn, *small, *wts)


def kernel(x_prompt, x_sample, cache_k, cache_v, cache_logf, state_conv, meta,
           norm1_g, w_in, b_f, conv_w, conv_b, q_norm_g, k_norm_g,
           w_br_conv, w_br_attn, w_out, norm2_g, w_up, w_down):
    b, seq, _ = x_prompt.shape
    db, dec, _ = x_sample.shape
    past = cache_k.shape[2]
    length = N_META + seq
    assert dec == N_META, "the small tile treats the meta tokens as one more short sequence"

    wt = w_in[0].T
    assert 2 * HEAD_DIM == LANES, "the q and k gains share one vreg row"
    g1 = norm1_g[0][None, :]
    conv_bias = conv_b[0][None, :]
    gains = jnp.concatenate([q_norm_g[0], k_norm_g[0]])[None, :]
    bf_lanes = jnp.broadcast_to(b_f[0][:, None], (N_HEADS, LANES))

    x_new = x_sample.reshape(db * dec, D_MODEL)
    (conv_s, q_s, k_s, v_s, k_meta, kt_s, vt_s, lft_s, lft_new, lfp_s, gate_s, left_p, zlast_s, wt_conv, wt_qkv, wt_gate, wt_fl,
     bf_row, qg_col, kg_col, conv_taps) = _project_small(
        meta, x_new, state_conv[0], wt, (g1, gains, bf_lanes, conv_w, conv_bias), b)
    qkv_wts = (g1, wt_qkv, bf_lanes, qg_col, kg_col)
    conv_wts = (conv_taps, conv_bias)

    x_rows = x_prompt.reshape(b * seq, D_MODEL)
    (qt_p, kt_p, vt_p, lftp_p, lft_p, kb_p, vtb_p,
     wdn_b) = _project_qkv(
        x_rows, qkv_wts, (kt_s, vt_s, lft_s, k_meta), (w_down[0],), b=b, seq=seq)
    k_new = k_s.reshape(db, dec, D_ATTN)
    v_new = v_s.reshape(db, dec, D_ATTN)
    cache_kt = jnp.transpose(cache_k[0], (0, 2, 3, 1)).reshape(db, D_ATTN, past)
    cache_vt = jnp.transpose(cache_v[0], (0, 2, 3, 1)).reshape(db, D_ATTN, past)
    attn_p, attn_s, wup_b, wbc_b, wba_b, wo_b = _attention(
        (qt_p, kb_p, vtb_p, vt_s, lft_p, lft_s, q_norm_g, k_norm_g),
        (q_s.reshape(db, dec, D_ATTN), k_new, v_new,
         lfp_s.reshape(db, dec, LANES), lft_new,
         cache_kt, cache_vt, jnp.swapaxes(cache_logf[0], 1, 2)),
        (w_up[0], w_br_conv[0], w_br_attn[0], w_out[0]))
    mlp_wts = (wbc_b, wba_b, wo_b, norm2_g[0][None, :], wup_b, wdn_b)

    y_prompt, zlast_p, y_sample = _branch_mlp(
        x_rows, left_p, attn_p.reshape(b * seq, D_ATTN),
        (x_new, conv_s, attn_s.reshape(db * dec, D_ATTN), gate_s),
        (g1, wt_conv, wt_gate) + conv_wts + mlp_wts, rows=MLP_ROWS, seq=seq)

    def heads_last(t):
        return jnp.transpose(t.reshape(b, N_HEADS, HEAD_DIM, length), (0, 3, 1, 2))[None]

    return (y_prompt.reshape(b, seq, D_MODEL),
            y_sample.reshape(db, dec, D_MODEL),
            heads_last(kt_p),
            heads_last(vt_p),
            jnp.swapaxes(lftp_p, 1, 2)[None],
            zlast_p[None],
            k_new.reshape(1, db, dec, N_HEADS, HEAD_DIM),
            v_new.reshape(1, db, dec, N_HEADS, HEAD_DIM),
            jnp.swapaxes(lft_new, 1, 2)[None],
            zlast_s[None])
```

```python
import functools

import jax
import jax.numpy as jnp
import numpy as np
from jax import lax
from jax.experimental import pallas as pl
from jax.experimental.pallas import tpu as pltpu

D_MODEL = 1024
D_CONV = D_MODEL // 2
CONV_W = 3
N_HEADS = 8
HEAD_DIM = 64
D_ATTN = N_HEADS * HEAD_DIM
D_FF = 4 * D_MODEL
N_META = 16
EPS = 1e-6
ATTN_SCALE = HEAD_DIM ** -0.5

F32 = jnp.float32
BF16 = jnp.bfloat16

VMEM_LIMIT_BYTES = 56 * 1024 * 1024
LANES = 128
SUBLANES = 8
BF16_ROWS = 16
PROJ_ROWS = 1024
X_SLOTS = 3
MLP_ROWS = 512
ATTN_BLOCK = 256
MASKED_BIAS = 1e30
CONV_HALO = CONV_W - 1
ZPAD = SUBLANES
N_SPLIT = 3
V_SLAB = HEAD_DIM + BF16_ROWS
LOG2E = 1.4426950408889634
SKIP_LOG2 = 40.0
NORM_SLACK = 1.02


def _dot(a, b):
    return jnp.dot(a, b, preferred_element_type=F32)


def _dot_nt(a, b):
    return lax.dot_general(a, b, (((1,), (1,)), ((), ())), preferred_element_type=F32)


def _log_sigmoid(x):
    return jnp.minimum(x, 0.0) - jnp.log1p(jnp.exp(-jnp.abs(x)))


def _cumsum_few(x, axis):
    n = x.shape[axis]
    idx = lax.broadcasted_iota(jnp.int32, x.shape, axis)
    out = jnp.zeros(x.shape, F32)
    for i in range(n):
        term = x[i:i + 1, :] if axis == 0 else x[:, i:i + 1]
        out = out + jnp.where(idx >= i, term, 0.0)
    return out


def _upper_triangle(n):
    r = lax.broadcasted_iota(jnp.int32, (n, n), 0)
    c = lax.broadcasted_iota(jnp.int32, (n, n), 1)
    return jnp.where(r <= c, 1.0, 0.0).astype(BF16)


def _split3(c):
    hi = c.astype(BF16).astype(F32)
    r1 = c - hi
    mid = r1.astype(BF16).astype(F32)
    return hi, mid, r1 - mid


def _cumsum_lanes(x, tri_upper):
    h = x.shape[0]
    pieces = jnp.concatenate(_split3(x), axis=0).astype(BF16)
    y = _dot(pieces, tri_upper)
    return y[0:h] + y[h:2 * h] + y[2 * h:3 * h]


def _const_spec(shape):
    nd = len(shape)
    return pl.BlockSpec(shape, lambda *_: (0,) * nd, pipeline_mode=pl.Buffered(1))


def _params(n_axes):
    return pltpu.CompilerParams(
        dimension_semantics=("arbitrary",) * n_axes,
        vmem_limit_bytes=VMEM_LIMIT_BYTES)


def _rms_rows(x, g_row):
    ms = jnp.mean(x * x, axis=-1, keepdims=True)
    return (x * lax.rsqrt(ms + EPS) * g_row).astype(BF16)


def _head_norm_t(ut, g_col):
    out = []
    for h in range(N_HEADS):
        blk = ut[h * HEAD_DIM:(h + 1) * HEAD_DIM, :]
        ms = jnp.mean(blk * blk, axis=0, keepdims=True)
        out.append(blk * lax.rsqrt(ms + EPS) * g_col[h * HEAD_DIM:(h + 1) * HEAD_DIM, :])
    return jnp.concatenate(out, axis=0)


def _short_conv(xn, wa_ref, cw_ref, cb_ref, zbuf, n_seg, seg_len):
    cb = _dot_nt(xn, wa_ref[0:D_CONV, :])
    z = (_dot_nt(xn, wa_ref[D_CONV:2 * D_CONV, :])
         * _dot_nt(xn, wa_ref[2 * D_CONV:3 * D_CONV, :]))
    out, tails = [], []
    for s in range(n_seg):
        r0 = s * seg_len
        zs = z[r0:r0 + seg_len]
        zbuf[s, ZPAD:ZPAD + seg_len, :] = zs
        y = None
        for i in range(CONV_W):
            lo = ZPAD - CONV_HALO + i
            tap = zs if i == CONV_HALO else zbuf[s, lo:lo + seg_len, :]
            term = tap * cw_ref[i:i + 1, :]
            y = term if y is None else y + term
        out.append(cb[r0:r0 + seg_len] * (y + cb_ref[...]))
        tail = zbuf[s, ZPAD + seg_len - CONV_HALO:ZPAD + seg_len, :]
        tails.append(tail)
        zbuf[s, ZPAD - CONV_HALO:ZPAD, :] = tail
    return out, tails


def _qkv_kernel(*refs, tiles_per_seq, n_cast, rows):
    (x_hbm, g1_ref, wqkv_ref, bfb_ref, qgc_ref, kgc_ref,
     ktm_ref, vtm_ref, lftm_ref, kbm_ref) = refs[:10]
    cast_in = refs[10:10 + n_cast]
    (qt_ref, kt_ref, vt_ref, lftp_ref, lft_ref, kb_ref,
     vtb_ref) = refs[10 + n_cast:17 + n_cast]
    cast_out = refs[17 + n_cast:17 + 2 * n_cast]
    kcar, vcar, lcar, xbuf, xsem = refs[17 + 2 * n_cast:]
    grid_step = pl.program_id(0)
    n_steps = pl.num_programs(0)
    step = grid_step % tiles_per_seq

    def x_copy(s):
        slot = s % X_SLOTS
        return pltpu.make_async_copy(x_hbm.at[pl.ds(pl.multiple_of(s * rows, rows), rows), :],
                                     xbuf.at[slot], xsem.at[slot])

    @pl.when(grid_step == 0)
    def _():
        for s in range(X_SLOTS - 1):
            x_copy(s).start()

    @pl.when(grid_step + (X_SLOTS - 1) < n_steps)
    def _():
        x_copy(grid_step + (X_SLOTS - 1)).start()

    x_copy(grid_step).wait()
    x_ref = xbuf.at[grid_step % X_SLOTS]

    def tile_body(tile_idx):
        lane0 = tile_idx * rows

        def shifted_store(out_ref, car_ref, meta_ref, tile):
            left = meta_ref[:, 0:LANES] if tile_idx == 0 else car_ref[...]
            rolled = pltpu.roll(tile, N_META, axis=1)
            lane = lax.broadcasted_iota(jnp.int32, (tile.shape[0], LANES), 1)
            out_ref[:, lane0:lane0 + LANES] = jnp.where(lane < N_META, left, rolled[:, 0:LANES])
            out_ref[:, lane0 + LANES:lane0 + rows] = rolled[:, LANES:]
            if tile_idx < tiles_per_seq - 1:
                car_ref[...] = rolled[:, 0:LANES]
            else:
                out_ref[:, tiles_per_seq * rows:] = rolled[:, 0:N_META]

        xn = _rms_rows(x_ref[...], g1_ref[...])

        def feature_major(j):
            return _dot_nt(wqkv_ref[j * D_ATTN:(j + 1) * D_ATTN, :], xn)

        qt = _head_norm_t(feature_major(0), qgc_ref[...])
        qt_ref[...] = (qt * (ATTN_SCALE * LOG2E)).astype(BF16)
        kt = _head_norm_t(feature_major(1), kgc_ref[...])
        v_f = _dot_nt(wqkv_ref[2 * D_ATTN:3 * D_ATTN + BF16_ROWS, :], xn)
        vt = v_f[0:D_ATTN]
        lft = _log_sigmoid(v_f[D_ATTN:D_ATTN + N_HEADS] + bfb_ref[:, 0:1])
        lft_ref[...] = lft
        key0 = ATTN_BLOCK + lane0
        kb_ref[key0:key0 + rows, :] = kt.T.astype(BF16)
        if tile_idx == 0:
            kb_ref[0:ATTN_BLOCK - N_META, :] = jnp.zeros((ATTN_BLOCK - N_META, D_ATTN), BF16)
            kb_ref[ATTN_BLOCK - N_META:ATTN_BLOCK, :] = kbm_ref[...]
        vtb_ref[...] = vt.astype(BF16)
        shifted_store(kt_ref, kcar, ktm_ref, kt)
        shifted_store(vt_ref, vcar, vtm_ref, vt)
        shifted_store(lftp_ref, lcar, lftm_ref, lft)
        for src, dst in zip(cast_in, cast_out):
            dst[...] = src[...].astype(BF16)

    for tile_idx in range(tiles_per_seq):
        pl.when(step == tile_idx)(functools.partial(tile_body, tile_idx))


def _project_qkv(x2d, wts, meta_cols, f32_wts, *, b, seq):
    rows = PROJ_ROWS
    tiles = seq // rows
    length = N_META + seq
    n_steps = b * tiles
    chunk_spec = lambda w: pl.BlockSpec((w.shape[0] // n_steps, w.shape[1]), lambda i: (i, 0))
    row_spec = lambda width: pl.BlockSpec((rows, width), lambda i: (i, 0))
    col_spec = lambda feat: pl.BlockSpec(
        (None, feat, rows), lambda i: (i // tiles, 0, i % tiles))
    seq_spec = lambda feat: pl.BlockSpec((None, feat, length), lambda i: (i // tiles, 0, 0))
    out_shape = (
        jax.ShapeDtypeStruct((b, D_ATTN, seq), BF16),
        jax.ShapeDtypeStruct((b, D_ATTN, length), F32),
        jax.ShapeDtypeStruct((b, D_ATTN, length), F32),
        jax.ShapeDtypeStruct((b, N_HEADS, length), F32),
        jax.ShapeDtypeStruct((b, N_HEADS, seq), F32),
        jax.ShapeDtypeStruct((b, ATTN_BLOCK + seq, D_ATTN), BF16),
        jax.ShapeDtypeStruct((b, D_ATTN, seq), BF16),
    )
    key_spec = pl.BlockSpec((None, ATTN_BLOCK + seq, D_ATTN), lambda i: (i // tiles, 0, 0))
    out_specs = (col_spec(D_ATTN), seq_spec(D_ATTN), seq_spec(D_ATTN), seq_spec(N_HEADS),
                 col_spec(N_HEADS), key_spec, col_spec(D_ATTN))
    out_shape = out_shape + tuple(jax.ShapeDtypeStruct(w.shape, BF16) for w in f32_wts)
    out_specs = out_specs + tuple(chunk_spec(w) for w in f32_wts)
    assert n_steps >= X_SLOTS - 1, "the input ring is primed with X_SLOTS - 1 tiles"
    kern = functools.partial(_qkv_kernel, tiles_per_seq=tiles, n_cast=len(f32_wts), rows=rows)
    return pl.pallas_call(
        kern, grid=(n_steps,),
        in_specs=[pl.BlockSpec(memory_space=pl.ANY)]
        + [_const_spec(w.shape) for w in wts + meta_cols]
        + [chunk_spec(w) for w in f32_wts],
        out_specs=out_specs, out_shape=out_shape,
        scratch_shapes=[pltpu.VMEM((D_ATTN, LANES), F32),
                        pltpu.VMEM((D_ATTN, LANES), F32),
                        pltpu.VMEM((N_HEADS, LANES), F32),
                        pltpu.VMEM((X_SLOTS, rows, D_MODEL), F32),
                        pltpu.SemaphoreType.DMA((X_SLOTS,))],
        compiler_params=_params(1), name="proj_qkv",
    )(x2d, *wts, *meta_cols, *f32_wts)


def _proj_small_kernel(meta_ref, xs_ref, state_ref, wt_ref, g1_ref, gains_ref, bfb_ref,
                       cwin_ref, cb_ref, bd_ref,
                       conv_ref, q_ref, k_ref, v_ref, km_ref, kt_ref, vt_ref,
                       lft_ref, lftn_ref, lfpn_ref,
                       gate_ref, leftp_ref, zlast_ref,
                       wa_ref, wqkv_ref, wgl_ref, wfl_ref,
                       bfr_ref, qgc_ref, kgc_ref, cw_ref, zbuf):
    n_new, _, dec = lftn_ref.shape
    n_main = 3 * D_CONV + 3 * D_ATTN
    wa_ref[...] = wt_ref[0:3 * D_CONV, :].astype(BF16)
    wqkv_ref[0:3 * D_ATTN, :] = wt_ref[3 * D_CONV:n_main, :].astype(BF16)
    wgl_ref[...] = wt_ref[n_main + N_HEADS:, :].astype(BF16)
    w_fl = wt_ref[n_main:n_main + N_HEADS, :]
    wfl_ref[...] = jnp.concatenate([w_fl] * (LANES // N_HEADS), axis=0).astype(BF16)
    wqkv_ref[3 * D_ATTN:, :] = wfl_ref[0:BF16_ROWS, :]
    wflt_ref = wfl_ref.at[0:BF16_ROWS]

    qk = jnp.broadcast_to(gains_ref[...], (SUBLANES, LANES))
    kq = pltpu.roll(qk, HEAD_DIM, axis=1)
    first = lax.broadcasted_iota(jnp.int32, (SUBLANES, LANES), 1) < HEAD_DIM

    def per_feature(two_heads):
        rows = jnp.concatenate([two_heads] * (D_ATTN // LANES), axis=1)
        return rows[0:1, :], rows.T[:, 0:1]

    qgr, qgc = per_feature(jnp.where(first, qk, kq))
    kgr, kgc = per_feature(jnp.where(first, kq, qk))
    qgc_ref[...] = qgc
    kgc_ref[...] = kgc
    bfc = bfb_ref[:, 0:1]
    bfr_ref[...] = jnp.concatenate([bfb_ref[...]] * (LANES // N_HEADS), axis=0).T[0:1, :]
    cw_ref[...] = cwin_ref[0]

    xn = _rms_rows(jnp.concatenate([meta_ref[...], xs_ref[...]], axis=0), g1_ref[...])
    xn_new = xn[N_META:]
    zbuf[0, ZPAD - CONV_HALO:ZPAD, :] = jnp.zeros((CONV_HALO, D_CONV), F32)
    zbuf[1:, ZPAD - CONV_HALO:ZPAD, :] = state_ref[...]
    conv, tails = _short_conv(xn, wa_ref, cw_ref, cb_ref, zbuf, n_new + 1, dec)
    for s in range(n_new):
        conv_ref[s * dec:(s + 1) * dec, :] = conv[s + 1].astype(BF16)
        zlast_ref[s] = tails[s + 1]
    for i in range(leftp_ref.shape[0]):
        leftp_ref[i] = tails[0]

    def rows_major(j, rows):
        return _dot_nt(rows, wqkv_ref[j * D_ATTN:(j + 1) * D_ATTN, :])

    def feature_major(j):
        return _dot_nt(wqkv_ref[j * D_ATTN:(j + 1) * D_ATTN, :], xn)

    def head_norm(u, g_row):
        ssq = _dot((u * u).astype(BF16), bd_ref[...])
        return u * lax.rsqrt(ssq * (1.0 / HEAD_DIM) + EPS) * g_row

    q_ref[...] = (head_norm(rows_major(0, xn_new), qgr) * ATTN_SCALE).astype(BF16)
    k = head_norm(rows_major(1, xn), kgr)
    km_ref[...] = k[0:N_META].astype(BF16)
    k_ref[...] = k[N_META:]
    v_ref[...] = rows_major(2, xn_new)
    kt_ref[...] = _head_norm_t(feature_major(1), kgc_ref[...])
    vt_ref[...] = feature_major(2)
    lfpn_ref[...] = _log_sigmoid(_dot_nt(xn_new, wfl_ref[...]) + bfr_ref[...])
    lft = _log_sigmoid(_dot_nt(wflt_ref[...], xn)[0:N_HEADS] + bfc)
    lft_ref[...] = lft
    for s in range(n_new):
        lftn_ref[s] = lft[:, N_META + s * dec:N_META + (s + 1) * dec]
    gate_ref[...] = jax.nn.sigmoid(_dot_nt(xn_new, wgl_ref[...])).astype(BF16)


def _project_small(meta, x_new, state, wt, small_params, n_prompt):
    n_new = state.shape[0]
    n_rows = x_new.shape[0]
    dec = n_rows // n_new
    n_all = N_META + n_rows
    n_main = 3 * D_CONV + 3 * D_ATTN
    head_of = np.arange(D_ATTN) // HEAD_DIM
    same_head = jnp.asarray(head_of[:, None] == head_of[None, :], BF16)
    full = lambda *shape: pl.BlockSpec(shape, lambda i: (0,) * len(shape))
    out_shape = (
        jax.ShapeDtypeStruct((n_rows, D_CONV), BF16),
        jax.ShapeDtypeStruct((n_rows, D_ATTN), BF16),
        jax.ShapeDtypeStruct((n_rows, D_ATTN), F32),
        jax.ShapeDtypeStruct((n_rows, D_ATTN), F32),
        jax.ShapeDtypeStruct((N_META, D_ATTN), BF16),
        jax.ShapeDtypeStruct((D_ATTN, n_all), F32),
        jax.ShapeDtypeStruct((D_ATTN, n_all), F32),
        jax.ShapeDtypeStruct((N_HEADS, n_all), F32),
        jax.ShapeDtypeStruct((n_new, N_HEADS, dec), F32),
        jax.ShapeDtypeStruct((n_rows, LANES), F32),
        jax.ShapeDtypeStruct((n_rows, 2 * D_MODEL), BF16),
        jax.ShapeDtypeStruct((n_prompt, CONV_HALO, D_CONV), F32),
        jax.ShapeDtypeStruct((n_new, CONV_HALO, D_CONV), F32),
        jax.ShapeDtypeStruct((3 * D_CONV, D_MODEL), BF16),
        jax.ShapeDtypeStruct((3 * D_ATTN + BF16_ROWS, D_MODEL), BF16),
        jax.ShapeDtypeStruct((wt.shape[0] - n_main - N_HEADS, D_MODEL), BF16),
        jax.ShapeDtypeStruct((LANES, D_MODEL), BF16),
        jax.ShapeDtypeStruct((1, LANES), F32),
        jax.ShapeDtypeStruct((D_ATTN, 1), F32),
        jax.ShapeDtypeStruct((D_ATTN, 1), F32),
        jax.ShapeDtypeStruct((CONV_W, D_CONV), F32),
    )
    ins = (meta, x_new, state, wt) + tuple(small_params) + (same_head,)
    return pl.pallas_call(
        _proj_small_kernel, grid=(1,),
        in_specs=[full(*meta.shape), full(*x_new.shape), full(*state.shape)]
        + [_const_spec(w.shape) for w in ins[3:]],
        out_specs=tuple(full(*s.shape) for s in out_shape), out_shape=out_shape,
        scratch_shapes=[pltpu.VMEM((n_new + 1, dec + ZPAD, D_CONV), F32)],
        compiler_params=_params(1), name="proj_small",
    )(*ins)


def _prompt_attn_kernel(qt_ref, kpos, vtb_ref, vtm_ref, lft_ref, lftm_ref,
                        qg_ref, kg_ref, o_ref, vb, kbias, crow, cend, qcat,
                        m_s, acc_s, sbuf, *, seq):
    blk = ATTN_BLOCK
    n_blk = seq // blk
    n_bias = N_SPLIT * N_HEADS
    pad = blk - N_META

    ones_row = (lax.broadcasted_iota(jnp.int32, (V_SLAB - HEAD_DIM, blk), 0) == 0).astype(BF16)
    lane_m = lax.broadcasted_iota(jnp.int32, (HEAD_DIM, LANES), 1)
    vb[0] = jnp.zeros((N_HEADS * V_SLAB, blk), BF16)
    for h in range(N_HEADS):
        rows = slice(h * HEAD_DIM, (h + 1) * HEAD_DIM)
        slab = slice(h * V_SLAB, h * V_SLAB + HEAD_DIM)
        meta = jnp.where(lane_m < N_META, vtm_ref[rows, 0:LANES], 0.0)
        vb[0, slab, blk - LANES:] = pltpu.roll(meta, LANES - N_META, axis=1).astype(BF16)
        for j in range(n_blk):
            vb[j + 1, slab, :] = vtb_ref[rows, j * blk:(j + 1) * blk]
        for j in range(n_blk + 1):
            vb[j, h * V_SLAB + HEAD_DIM:(h + 1) * V_SLAB, :] = ones_row

    qk_bound = (NORM_SLACK * HEAD_DIM * ATTN_SCALE * LOG2E
                * jnp.max(jnp.abs(qg_ref[...]), axis=1, keepdims=True)
                * jnp.max(jnp.abs(kg_ref[...]), axis=1, keepdims=True))

    upper = _upper_triangle(blk)
    ones_zeros = jnp.concatenate([jnp.ones((n_bias, blk), F32),
                                  jnp.zeros((LANES - 2 * n_bias, blk), F32)], axis=0)

    def store_kbias(j, c_log2, is_pad=None):
        neg = [-piece for piece in _split3(c_log2)]
        if is_pad is not None:
            neg = [jnp.where(is_pad, fill, piece)
                   for piece, fill in zip(neg, (-MASKED_BIAS, 0.0, 0.0))]
        kbias[j * blk:(j + 1) * blk, :] = jnp.concatenate(neg + [ones_zeros], axis=0).T.astype(BF16)

    lane_h = lax.broadcasted_iota(jnp.int32, (N_HEADS, LANES), 1)
    lf_meta = jnp.where(lane_h < N_META, lftm_ref[:, 0:LANES], 0.0)
    lf_blk0 = jnp.concatenate([jnp.zeros((N_HEADS, blk - LANES), F32),
                               pltpu.roll(lf_meta, LANES - N_META, axis=1)], axis=1)
    c_row = _cumsum_lanes(lf_blk0, upper)
    store_kbias(0, c_row * LOG2E, lax.broadcasted_iota(jnp.int32, (N_HEADS, blk), 1) < pad)
    off_r = c_row[:, blk - 1:blk]
    c_end = jnp.where(lane_h == 0, off_r * LOG2E, 0.0)
    for j in range(n_blk):
        c_row = _cumsum_lanes(lft_ref[:, j * blk:(j + 1) * blk], upper) + off_r
        crow[j] = c_row * LOG2E
        store_kbias(j + 1, c_row * LOG2E)
        off_r = c_row[:, blk - 1:blk]
        c_end = jnp.where(lane_h == j + 1, off_r * LOG2E, c_end)
    cend[...] = c_end

    row128 = lax.broadcasted_iota(jnp.int32, (LANES, blk), 0)
    krow = lax.broadcasted_iota(jnp.int32, (2 * blk, blk), 0)
    qcol = lax.broadcasted_iota(jnp.int32, (2 * blk, blk), 1)

    def q_block(t, _):
        tok0 = pl.multiple_of(t * blk, blk)
        c_q = crow[t]
        hi, mid, lo = _split3(c_q)
        bias_rows = jnp.concatenate(
            [jnp.ones((n_bias, blk), F32), hi, mid, lo,
             jnp.zeros((LANES - 2 * n_bias, blk), F32)], axis=0)
        for h in range(N_HEADS):
            pair = qt_ref[(h // 2) * LANES:(h // 2 + 1) * LANES, pl.ds(tok0, blk)]
            in_head = (row128 // HEAD_DIM) == (h % 2)
            qcat[h, 0:LANES, :] = jnp.where(in_head, pair, jnp.zeros_like(pair))
            qcat[h, LANES:, :] = jnp.where(row128 % N_HEADS == h, bias_rows, 0.0).astype(BF16)
        m_s[...] = jnp.full(m_s.shape, -jnp.inf, F32)
        acc_s[...] = jnp.zeros(acc_s.shape, F32)

        gap = 2.0 * qk_bound + c_q[:, 0:1] - cend[...]
        needed = jnp.logical_and(gap >= -SKIP_LOG2, lane_h < t)
        n_needed = jnp.max(
            jnp.sum(jnp.where(needed, 1.0, 0.0), axis=1, keepdims=True)).astype(jnp.int32)

        def key_pass(row0, n_rows, pv, visible):
            m_blk = []
            for h in range(N_HEADS):
                g = h // 2
                kc = jnp.concatenate([kpos[pl.ds(row0, n_rows), g * LANES:(g + 1) * LANES],
                                      kbias[pl.ds(row0, n_rows), :]], axis=1)
                s = _dot(kc, qcat[h])
                if visible is not None:
                    s = jnp.where(visible, s, -jnp.inf)
                sbuf[h, 0:n_rows, :] = s
                m_blk.append(jnp.max(s, axis=0, keepdims=True))
            for h in range(N_HEADS):
                slab = slice(h * V_SLAB, (h + 1) * V_SLAB)
                m_old = m_s[h:h + 1, :]
                m_new = jnp.maximum(m_old, m_blk[h])
                alpha = jnp.exp2(m_old - m_new)
                p = jnp.exp2(sbuf[h, 0:n_rows, :] - m_new)
                m_s[h:h + 1, :] = m_new
                acc_s[slab, :] = alpha * acc_s[slab, :] + pv(slab, p.astype(BF16))

        key_pass(tok0, 2 * blk,
                 lambda slab, p: (_dot(vb[t, slab, :], p[0:blk])
                                  + _dot(vb[t + 1, slab, :], p[blk:])),
                 krow <= qcol + blk)

        def older_block(j, _):
            key_pass(pl.multiple_of(j * blk, blk), blk,
                     lambda slab, p: _dot(vb[j, slab, :], p), None)
            return 0

        lax.fori_loop(t - n_needed, t, older_block, 0)

        o_t = []
        for h in range(N_HEADS):
            norm = acc_s[h * V_SLAB + HEAD_DIM:h * V_SLAB + HEAD_DIM + 1, :]
            o_t.append(acc_s[h * V_SLAB:h * V_SLAB + HEAD_DIM, :] * (1.0 / norm))
        o_ref[pl.ds(tok0, blk), :] = jnp.concatenate(o_t, axis=0).T.astype(BF16)
        return 0

    lax.fori_loop(0, n_blk, q_block, 0)


def _sample_attn_kernel(q_ref, kn_ref, vn_ref, lf_ref, lft_ref, ckt_ref, cvt_ref, clft_ref,
                        o_ref, crow, *, past, dec):
    blk = ATTN_BLOCK
    n_keys = past + LANES

    upper = _upper_triangle(blk)
    off = jnp.zeros((N_HEADS, 1), F32)
    for j in range(past // blk):
        loc = _cumsum_lanes(clft_ref[:, j * blk:(j + 1) * blk], upper) + off
        crow[:, j * blk:(j + 1) * blk] = loc
        off = loc[:, blk - 1:blk]
    crow[:, 0:past] = crow[:, 0:past] - off

    cq_c = _cumsum_few(lf_ref[:, 0:N_HEADS], axis=0)
    cq_r = _cumsum_few(lft_ref[...], axis=1)
    crow[:, past:] = jnp.full((N_HEADS, LANES), MASKED_BIAS, F32)
    crow[:, past:past + dec] = cq_r

    q = q_ref[...]
    lane_head = lax.broadcasted_iota(jnp.int32, (dec, D_ATTN), 1) // HEAD_DIM
    q_exp = jnp.concatenate(
        [jnp.where(lane_head == h, q, jnp.zeros_like(q)) for h in range(N_HEADS)], axis=0)
    pad_rows = jnp.zeros((LANES - dec, D_ATTN), BF16)
    k_new = jnp.concatenate([kn_ref[...].astype(BF16), pad_rows], axis=0)
    v_new = jnp.concatenate([vn_ref[...].astype(BF16), pad_rows], axis=0)
    s_all = jnp.concatenate(
        [_dot(q_exp, ckt_ref[...].astype(BF16)), _dot_nt(q_exp, k_new)], axis=1)

    kpos = lax.broadcasted_iota(jnp.int32, (dec, n_keys), 1)
    qpos = past + lax.broadcasted_iota(jnp.int32, (dec, n_keys), 0)
    visible = kpos <= qpos
    probs = []
    norms = []
    for h in range(N_HEADS):
        s = s_all[h * dec:(h + 1) * dec, :] + cq_c[:, h:h + 1] - crow[h:h + 1, :]
        s = jnp.where(visible, s, -jnp.inf)
        p = jnp.exp(s - jnp.max(s, axis=-1, keepdims=True))
        norms.append(jnp.sum(p, axis=-1, keepdims=True))
        probs.append(p.astype(BF16))
    p_all = jnp.concatenate(probs, axis=0)
    o_all = (_dot_nt(p_all[:, 0:past], cvt_ref[...].astype(BF16))
             + _dot(p_all[:, past:], v_new))
    out = jnp.zeros((dec, D_ATTN), F32)
    for h in range(N_HEADS):
        o = o_all[h * dec:(h + 1) * dec, :] / norms[h]
        out = out + jnp.where(lane_head == h, o, 0.0)
    o_ref[...] = out.astype(BF16)


def _attn_kernel(*refs, seq, past, dec, n_cast):
    n_p, n_s = _N_PROMPT_IN, _N_SAMPLE_IN
    n_in = n_p + n_s + n_cast
    p_in, s_in, cast_in = refs[0:n_p], refs[n_p:n_p + n_s], refs[n_p + n_s:n_in]
    o_ref, os_ref = refs[n_in:n_in + 2]
    cast_out = refs[n_in + 2:n_in + 2 + n_cast]
    scratch = refs[n_in + 2 + n_cast:]
    for src, dst in zip(cast_in, cast_out):
        dst[...] = src[...].astype(BF16)
    _sample_attn_kernel(*s_in, os_ref, scratch[-1], past=past, dec=dec)
    _prompt_attn_kernel(*p_in, o_ref, *scratch[:-1], seq=seq)


_N_PROMPT_IN = 8
_N_SAMPLE_IN = 8


def _attention(prompt_in, sample_in, f32_wts):
    qt, _, vtb, vt_meta, lft, lft_meta, q_gain, k_gain = prompt_in
    q_s, _, _, _, _, cache_kt, _, _ = sample_in
    b, _, seq = qt.shape
    db, dec, _ = q_s.shape
    assert b == db, "one new stream and one running stream per grid step"
    past = cache_kt.shape[2]
    blk = ATTN_BLOCK
    n_blk = seq // blk
    n_pos = (n_blk + 1) * blk
    per_b = lambda *shape: pl.BlockSpec((None,) + shape, lambda i: (i,) + (0,) * len(shape))
    chunk_spec = lambda w: pl.BlockSpec((w.shape[0] // b, w.shape[1]), lambda i: (i, 0))
    in_specs = [per_b(D_ATTN, seq), per_b(n_pos, D_ATTN),
                per_b(D_ATTN, seq), _const_spec(vt_meta.shape),
                per_b(N_HEADS, seq), _const_spec(lft_meta.shape),
                _const_spec(q_gain.shape), _const_spec(k_gain.shape),
                per_b(dec, D_ATTN), per_b(dec, D_ATTN), per_b(dec, D_ATTN),
                per_b(dec, LANES), per_b(N_HEADS, dec),
                per_b(D_ATTN, past), per_b(D_ATTN, past), per_b(N_HEADS, past)]
    in_specs += [chunk_spec(w) for w in f32_wts]
    kern = functools.partial(_attn_kernel, seq=seq, past=past, dec=dec, n_cast=len(f32_wts))
    return pl.pallas_call(
        kern, grid=(b,), in_specs=in_specs,
        out_specs=(per_b(seq, D_ATTN), per_b(dec, D_ATTN))
        + tuple(chunk_spec(w) for w in f32_wts),
        out_shape=(jax.ShapeDtypeStruct((b, seq, D_ATTN), BF16),
                   jax.ShapeDtypeStruct((b, dec, D_ATTN), BF16))
        + tuple(jax.ShapeDtypeStruct(w.shape, BF16) for w in f32_wts),
        scratch_shapes=[pltpu.VMEM((n_blk + 1, N_HEADS * V_SLAB, blk), BF16),
                        pltpu.VMEM((n_pos, LANES), BF16),
                        pltpu.VMEM((n_blk, N_HEADS, blk), F32),
                        pltpu.VMEM((N_HEADS, LANES), F32),
                        pltpu.VMEM((N_HEADS, 2 * LANES, blk), BF16),
                        pltpu.VMEM((N_HEADS, blk), F32),
                        pltpu.VMEM((N_HEADS * V_SLAB, blk), F32),
                        pltpu.VMEM((N_HEADS, 2 * blk, blk), F32),
                        pltpu.VMEM((N_HEADS, past + LANES), F32)],
        compiler_params=_params(1), name="attention",
    )(*prompt_in, *sample_in, *f32_wts)


def _merge_mlp_tail(x, conv_bf16, attn_ref, g_conv, g_attn, wbc_ref, wba_ref, wo_ref, g2_ref,
                    wup_ref, wdn_ref, y_ref):
    merged = g_conv * _dot(conv_bf16, wbc_ref[...]) + g_attn * _dot(attn_ref[...], wba_ref[...])
    h = x + _dot(merged.astype(BF16), wo_ref[...])
    hn = _rms_rows(h, g2_ref[...])
    acc = h
    for c in range(D_FF // D_MODEL):
        cols = slice(c * D_MODEL, (c + 1) * D_MODEL)
        a = jnp.maximum(_dot(hn, wup_ref[:, cols]), 0.0)
        acc = acc + _dot((a * a).astype(BF16), wdn_ref[cols, :])
    y_ref[...] = acc


def _branch_mlp_kernel(x_ref, left_ref, attn_ref, xs_ref, convs_ref, attns_ref, gates_ref,
                       g1_ref, wa_ref, wgl_ref, cw_ref, cb_ref,
                       wbc_ref, wba_ref, wo_ref, g2_ref, wup_ref, wdn_ref,
                       y_ref, zlast_ref, ys_ref, zbuf, *, n_tiles, tiles_per_seq):
    step = pl.program_id(0)

    @pl.when(jnp.logical_and(step % tiles_per_seq == 0, step < n_tiles))
    def _():
        zbuf[:, ZPAD - CONV_HALO:ZPAD, :] = left_ref[...]

    @pl.when(step < n_tiles)
    def _():
        x = x_ref[...]
        xn = _rms_rows(x, g1_ref[...])
        (conv,), (tail,) = _short_conv(xn, wa_ref, cw_ref, cb_ref, zbuf, 1, x.shape[0])
        zlast_ref[0] = tail
        g_conv = jax.nn.sigmoid(_dot_nt(xn, wgl_ref[0:D_MODEL, :]))
        g_attn = jax.nn.sigmoid(_dot_nt(xn, wgl_ref[D_MODEL:2 * D_MODEL, :]))
        _merge_mlp_tail(x, conv.astype(BF16), attn_ref, g_conv, g_attn, wbc_ref, wba_ref,
                        wo_ref, g2_ref, wup_ref, wdn_ref, y_ref)

    @pl.when(step == n_tiles)
    def _():
        _merge_mlp_tail(xs_ref[...], convs_ref[...], attns_ref,
                        gates_ref[:, 0:D_MODEL].astype(F32),
                        gates_ref[:, D_MODEL:2 * D_MODEL].astype(F32),
                        wbc_ref, wba_ref, wo_ref, g2_ref, wup_ref, wdn_ref, ys_ref)


def _branch_mlp(x2d, left, attn, small, wts, *, rows, seq):
    n_rows = x2d.shape[0]
    tiles = seq // rows
    n_tiles = n_rows // rows
    last = n_tiles - 1
    row_spec = lambda width: pl.BlockSpec((rows, width), lambda i: (jnp.minimum(i, last), 0))
    seq_spec = pl.BlockSpec((1, CONV_HALO, D_CONV),
                            lambda i: (jnp.minimum(i, last) // tiles, 0, 0))
    full = lambda a: pl.BlockSpec(a.shape, lambda i: (0,) * a.ndim)
    kern = functools.partial(_branch_mlp_kernel, n_tiles=n_tiles, tiles_per_seq=tiles)
    return pl.pallas_call(
        kern, grid=(n_tiles + 1,),
        in_specs=[row_spec(D_MODEL), seq_spec, row_spec(D_ATTN)] + [full(a) for a in small]
        + [_const_spec(w.shape) for w in wts],
        out_specs=(row_spec(D_MODEL), seq_spec, full(small[0])),
        out_shape=(jax.ShapeDtypeStruct((n_rows, D_MODEL), F32),
                   jax.ShapeDtypeStruct((n_rows // seq, CONV_HALO, D_CONV), F32),
                   jax.ShapeDtypeStruct(small[0].shape, F32)),
        scratch_shapes=[pltpu.VMEM((1, rows + ZPAD, D_CONV), F32)],
        compiler_params=_params(1), name="branch_mlp",
    )(x2d, left, attn, *small, *wts)


def kernel(x_prompt, x_sample, cache_k, cache_v, cache_logf, state_conv, meta,
           norm1_g, w_in, b_f, conv_w, conv_b, q_norm_g, k_norm_g,
           w_br_conv, w_br_attn, w_out, norm2_g, w_up, w_down):
    b, seq, _ = x_prompt.shape
    db, dec, _ = x_sample.shape
    past = cache_k.shape[2]
    length = N_META + seq
    assert dec == N_META, "the small tile treats the meta tokens as one more short sequence"

    wt = w_in[0].T
    assert 2 * HEAD_DIM == LANES, "the q and k gains share one vreg row"
    g1 = norm1_g[0][None, :]
    conv_bias = conv_b[0][None, :]
    gains = jnp.concatenate([q_norm_g[0], k_norm_g[0]])[None, :]
    bf_lanes = jnp.broadcast_to(b_f[0][:, None], (N_HEADS, LANES))

    x_new = x_sample.reshape(db * dec, D_MODEL)
    (conv_s, q_s, k_s, v_s, k_meta, kt_s, vt_s, lft_s, lft_new, lfp_s, gate_s, left_p, zlast_s, wt_conv, wt_qkv, wt_gate, wt_fl,
     bf_row, qg_col, kg_col, conv_taps) = _project_small(
        meta, x_new, state_conv[0], wt, (g1, gains, bf_lanes, conv_w, conv_bias), b)
    qkv_wts = (g1, wt_qkv, bf_lanes, qg_col, kg_col)
    conv_wts = (conv_taps, conv_bias)

    x_rows = x_prompt.reshape(b * seq, D_MODEL)
    (qt_p, kt_p, vt_p, lftp_p, lft_p, kb_p, vtb_p,
     wdn_b) = _project_qkv(
        x_rows, qkv_wts, (kt_s, vt_s, lft_s, k_meta), (w_down[0],), b=b, seq=seq)
    k_new = k_s.reshape(db, dec, D_ATTN)
    v_new = v_s.reshape(db, dec, D_ATTN)
    cache_kt = jnp.transpose(cache_k[0], (0, 2, 3, 1)).reshape(db, D_ATTN, past)
    cache_vt = jnp.transpose(cache_v[0], (0, 2, 3, 1)).reshape(db, D_ATTN, past)
    attn_p, attn_s, wup_b, wbc_b, wba_b, wo_b = _attention(
        (qt_p, kb_p, vtb_p, vt_s, lft_p, lft_s, q_norm_g, k_norm_g),
        (q_s.reshape(db, dec, D_ATTN), k_new, v_new,
         lfp_s.reshape(db, dec, LANES), lft_new,
         cache_kt, cache_vt, jnp.swapaxes(cache_logf[0], 1, 2)),
        (w_up[0], w_br_conv[0], w_br_attn[0], w_out[0]))
    mlp_wts = (wbc_b, wba_b, wo_b, norm2_g[0][None, :], wup_b, wdn_b)

    y_prompt, zlast_p, y_sample = _branch_mlp(
        x_rows, left_p, attn_p.reshape(b * seq, D_ATTN),
        (x_new, conv_s, attn_s.reshape(db * dec, D_ATTN), gate_s),
        (g1, wt_conv, wt_gate) + conv_wts + mlp_wts, rows=MLP_ROWS, seq=seq)

    def heads_last(t):
        return jnp.transpose(t.reshape(b, N_HEADS, HEAD_DIM, length), (0, 3, 1, 2))[None]

    return (y_prompt.reshape(b, seq, D_MODEL),
            y_sample.reshape(db, dec, D_MODEL),
            heads_last(kt_p),
            heads_last(vt_p),
            jnp.swapaxes(lftp_p, 1, 2)[None],
            zlast_p[None],
            k_new.reshape(1, db, dec, N_HEADS, HEAD_DIM),
            v_new.reshape(1, db, dec, N_HEADS, HEAD_DIM),
            jnp.swapaxes(lft_new, 1, 2)[None],
            zlast_s[None])
```

```python
import functools

import jax
import jax.numpy as jnp
import numpy as np
from jax import lax
from jax.experimental import pallas as pl
from jax.experimental.pallas import tpu as pltpu

D_MODEL = 1024
D_CONV = D_MODEL // 2
CONV_W = 3
N_HEADS = 8
HEAD_DIM = 64
D_ATTN = N_HEADS * HEAD_DIM
D_FF = 4 * D_MODEL
N_META = 16
EPS = 1e-6
ATTN_SCALE = HEAD_DIM ** -0.5

F32 = jnp.float32
BF16 = jnp.bfloat16

VMEM_LIMIT_BYTES = 56 * 1024 * 1024
LANES = 128
SUBLANES = 8
BF16_ROWS = 16
PROJ_ROWS = 1024
MLP_ROWS = 512
ATTN_BLOCK = 256
MASKED_BIAS = 1e30
CONV_HALO = CONV_W - 1
ZPAD = SUBLANES
N_SPLIT = 3
V_SLAB = HEAD_DIM + BF16_ROWS
LOG2E = 1.4426950408889634
SKIP_LOG2 = 40.0
NORM_SLACK = 1.02


def _dot(a, b):
    return jnp.dot(a, b, preferred_element_type=F32)


def _dot_nt(a, b):
    return lax.dot_general(a, b, (((1,), (1,)), ((), ())), preferred_element_type=F32)


def _log_sigmoid(x):
    return jnp.minimum(x, 0.0) - jnp.log1p(jnp.exp(-jnp.abs(x)))


def _cumsum_few(x, axis):
    n = x.shape[axis]
    idx = lax.broadcasted_iota(jnp.int32, x.shape, axis)
    out = jnp.zeros(x.shape, F32)
    for i in range(n):
        term = x[i:i + 1, :] if axis == 0 else x[:, i:i + 1]
        out = out + jnp.where(idx >= i, term, 0.0)
    return out


def _upper_triangle(n):
    r = lax.broadcasted_iota(jnp.int32, (n, n), 0)
    c = lax.broadcasted_iota(jnp.int32, (n, n), 1)
    return jnp.where(r <= c, 1.0, 0.0).astype(BF16)


def _split3(c):
    hi = c.astype(BF16).astype(F32)
    r1 = c - hi
    mid = r1.astype(BF16).astype(F32)
    return hi, mid, r1 - mid


def _cumsum_lanes(x, tri_upper):
    h = x.shape[0]
    pieces = jnp.concatenate(_split3(x), axis=0).astype(BF16)
    y = _dot(pieces, tri_upper)
    return y[0:h] + y[h:2 * h] + y[2 * h:3 * h]


def _const_spec(shape):
    nd = len(shape)
    return pl.BlockSpec(shape, lambda *_: (0,) * nd, pipeline_mode=pl.Buffered(1))


def _params(n_axes):
    return pltpu.CompilerParams(
        dimension_semantics=("arbitrary",) * n_axes,
        vmem_limit_bytes=VMEM_LIMIT_BYTES)


def _rms_rows(x, g_row):
    ms = jnp.mean(x * x, axis=-1, keepdims=True)
    return (x * lax.rsqrt(ms + EPS) * g_row).astype(BF16)


def _head_norm_t(ut, g_col):
    out = []
    for h in range(N_HEADS):
        blk = ut[h * HEAD_DIM:(h + 1) * HEAD_DIM, :]
        ms = jnp.mean(blk * blk, axis=0, keepdims=True)
        out.append(blk * lax.rsqrt(ms + EPS) * g_col[h * HEAD_DIM:(h + 1) * HEAD_DIM, :])
    return jnp.concatenate(out, axis=0)


def _short_conv(xn, wa_ref, cw_ref, cb_ref, zbuf, n_seg, seg_len):
    cb = _dot_nt(xn, wa_ref[0:D_CONV, :])
    z = (_dot_nt(xn, wa_ref[D_CONV:2 * D_CONV, :])
         * _dot_nt(xn, wa_ref[2 * D_CONV:3 * D_CONV, :]))
    out, tails = [], []
    for s in range(n_seg):
        r0 = s * seg_len
        zs = z[r0:r0 + seg_len]
        zbuf[s, ZPAD:ZPAD + seg_len, :] = zs
        y = None
        for i in range(CONV_W):
            lo = ZPAD - CONV_HALO + i
            tap = zs if i == CONV_HALO else zbuf[s, lo:lo + seg_len, :]
            term = tap * cw_ref[i:i + 1, :]
            y = term if y is None else y + term
        out.append(cb[r0:r0 + seg_len] * (y + cb_ref[...]))
        tail = zbuf[s, ZPAD + seg_len - CONV_HALO:ZPAD + seg_len, :]
        tails.append(tail)
        zbuf[s, ZPAD - CONV_HALO:ZPAD, :] = tail
    return out, tails


def _qkv_kernel(*refs, tiles_per_seq, n_cast):
    (x_ref, g1_ref, wqkv_ref, bfb_ref, qgc_ref, kgc_ref,
     ktm_ref, vtm_ref, lftm_ref, kbm_ref) = refs[:10]
    cast_in = refs[10:10 + n_cast]
    (qt_ref, kt_ref, vt_ref, lftp_ref, lft_ref, kb_ref,
     vtb_ref) = refs[10 + n_cast:17 + n_cast]
    cast_out = refs[17 + n_cast:17 + 2 * n_cast]
    kcar, vcar, lcar = refs[17 + 2 * n_cast:]
    rows = x_ref.shape[0]
    step = pl.program_id(0) % tiles_per_seq

    def tile_body(tile_idx):
        lane0 = tile_idx * rows

        def shifted_store(out_ref, car_ref, meta_ref, tile):
            left = meta_ref[:, 0:LANES] if tile_idx == 0 else car_ref[...]
            rolled = pltpu.roll(tile, N_META, axis=1)
            lane = lax.broadcasted_iota(jnp.int32, (tile.shape[0], LANES), 1)
            out_ref[:, lane0:lane0 + LANES] = jnp.where(lane < N_META, left, rolled[:, 0:LANES])
            out_ref[:, lane0 + LANES:lane0 + rows] = rolled[:, LANES:]
            if tile_idx < tiles_per_seq - 1:
                car_ref[...] = rolled[:, 0:LANES]
            else:
                out_ref[:, tiles_per_seq * rows:] = rolled[:, 0:N_META]

        xn = _rms_rows(x_ref[...], g1_ref[...])

        def feature_major(j):
            return _dot_nt(wqkv_ref[j * D_ATTN:(j + 1) * D_ATTN, :], xn)

        qt = _head_norm_t(feature_major(0), qgc_ref[...])
        qt_ref[...] = (qt * (ATTN_SCALE * LOG2E)).astype(BF16)
        kt = _head_norm_t(feature_major(1), kgc_ref[...])
        v_f = _dot_nt(wqkv_ref[2 * D_ATTN:3 * D_ATTN + BF16_ROWS, :], xn)
        vt = v_f[0:D_ATTN]
        lft = _log_sigmoid(v_f[D_ATTN:D_ATTN + N_HEADS] + bfb_ref[:, 0:1])
        lft_ref[...] = lft
        key0 = ATTN_BLOCK + lane0
        kb_ref[key0:key0 + rows, :] = kt.T.astype(BF16)
        if tile_idx == 0:
            kb_ref[0:ATTN_BLOCK - N_META, :] = jnp.zeros((ATTN_BLOCK - N_META, D_ATTN), BF16)
            kb_ref[ATTN_BLOCK - N_META:ATTN_BLOCK, :] = kbm_ref[...]
        vtb_ref[...] = vt.astype(BF16)
        shifted_store(kt_ref, kcar, ktm_ref, kt)
        shifted_store(vt_ref, vcar, vtm_ref, vt)
        shifted_store(lftp_ref, lcar, lftm_ref, lft)
        for src, dst in zip(cast_in, cast_out):
            dst[...] = src[...].astype(BF16)

    for tile_idx in range(tiles_per_seq):
        pl.when(step == tile_idx)(functools.partial(tile_body, tile_idx))


def _project_qkv(x2d, wts, meta_cols, f32_wts, *, b, seq):
    rows = PROJ_ROWS
    tiles = seq // rows
    length = N_META + seq
    n_steps = b * tiles
    chunk_spec = lambda w: pl.BlockSpec((w.shape[0] // n_steps, w.shape[1]), lambda i: (i, 0))
    row_spec = lambda width: pl.BlockSpec((rows, width), lambda i: (i, 0))
    col_spec = lambda feat: pl.BlockSpec(
        (None, feat, rows), lambda i: (i // tiles, 0, i % tiles))
    seq_spec = lambda feat: pl.BlockSpec((None, feat, length), lambda i: (i // tiles, 0, 0))
    out_shape = (
        jax.ShapeDtypeStruct((b, D_ATTN, seq), BF16),
        jax.ShapeDtypeStruct((b, D_ATTN, length), F32),
        jax.ShapeDtypeStruct((b, D_ATTN, length), F32),
        jax.ShapeDtypeStruct((b, N_HEADS, length), F32),
        jax.ShapeDtypeStruct((b, N_HEADS, seq), F32),
        jax.ShapeDtypeStruct((b, ATTN_BLOCK + seq, D_ATTN), BF16),
        jax.ShapeDtypeStruct((b, D_ATTN, seq), BF16),
    )
    key_spec = pl.BlockSpec((None, ATTN_BLOCK + seq, D_ATTN), lambda i: (i // tiles, 0, 0))
    out_specs = (col_spec(D_ATTN), seq_spec(D_ATTN), seq_spec(D_ATTN), seq_spec(N_HEADS),
                 col_spec(N_HEADS), key_spec, col_spec(D_ATTN))
    out_shape = out_shape + tuple(jax.ShapeDtypeStruct(w.shape, BF16) for w in f32_wts)
    out_specs = out_specs + tuple(chunk_spec(w) for w in f32_wts)
    kern = functools.partial(_qkv_kernel, tiles_per_seq=tiles, n_cast=len(f32_wts))
    return pl.pallas_call(
        kern, grid=(n_steps,),
        in_specs=[row_spec(D_MODEL)] + [_const_spec(w.shape) for w in wts + meta_cols]
        + [chunk_spec(w) for w in f32_wts],
        out_specs=out_specs, out_shape=out_shape,
        scratch_shapes=[pltpu.VMEM((D_ATTN, LANES), F32),
                        pltpu.VMEM((D_ATTN, LANES), F32),
                        pltpu.VMEM((N_HEADS, LANES), F32)],
        compiler_params=_params(1), name="proj_qkv",
    )(x2d, *wts, *meta_cols, *f32_wts)


def _proj_small_kernel(meta_ref, xs_ref, state_ref, wt_hbm, g1_ref, gains_ref, bfb_ref,
                       cwin_ref, cb_ref, bd_ref,
                       conv_ref, q_ref, k_ref, v_ref, km_ref, kt_ref, vt_ref,
                       lft_ref, lftn_ref, lfpn_ref,
                       gate_ref, leftp_ref, zlast_ref,
                       wa_ref, wqkv_ref, wgl_ref, wfl_ref,
                       bfr_ref, qgc_ref, kgc_ref, cw_ref, zbuf, wt_ref, wsem):
    n_new, _, dec = lftn_ref.shape
    n_main = 3 * D_CONV + 3 * D_ATTN
    bounds = (0, 3 * D_CONV, n_main + N_HEADS, wt_hbm.shape[0])
    loads = [pltpu.make_async_copy(wt_hbm.at[lo:hi, :], wt_ref.at[lo:hi, :], wsem.at[i])
             for i, (lo, hi) in enumerate(zip(bounds[:-1], bounds[1:]))]
    for load in loads:
        load.start()
    wflt_ref = wfl_ref.at[0:BF16_ROWS]

    qk = jnp.broadcast_to(gains_ref[...], (SUBLANES, LANES))
    kq = pltpu.roll(qk, HEAD_DIM, axis=1)
    first = lax.broadcasted_iota(jnp.int32, (SUBLANES, LANES), 1) < HEAD_DIM

    def per_feature(two_heads):
        rows = jnp.concatenate([two_heads] * (D_ATTN // LANES), axis=1)
        return rows[0:1, :], rows.T[:, 0:1]

    qgr, qgc = per_feature(jnp.where(first, qk, kq))
    kgr, kgc = per_feature(jnp.where(first, kq, qk))
    qgc_ref[...] = qgc
    kgc_ref[...] = kgc
    bfc = bfb_ref[:, 0:1]
    bfr_ref[...] = jnp.concatenate([bfb_ref[...]] * (LANES // N_HEADS), axis=0).T[0:1, :]
    cw_ref[...] = cwin_ref[0]

    xn = _rms_rows(jnp.concatenate([meta_ref[...], xs_ref[...]], axis=0), g1_ref[...])
    xn_new = xn[N_META:]
    zbuf[0, ZPAD - CONV_HALO:ZPAD, :] = jnp.zeros((CONV_HALO, D_CONV), F32)
    zbuf[1:, ZPAD - CONV_HALO:ZPAD, :] = state_ref[...]
    loads[0].wait()
    wa_ref[...] = wt_ref[0:3 * D_CONV, :].astype(BF16)
    conv, tails = _short_conv(xn, wa_ref, cw_ref, cb_ref, zbuf, n_new + 1, dec)
    for s in range(n_new):
        conv_ref[s * dec:(s + 1) * dec, :] = conv[s + 1].astype(BF16)
        zlast_ref[s] = tails[s + 1]
    for i in range(leftp_ref.shape[0]):
        leftp_ref[i] = tails[0]

    def rows_major(j, rows):
        return _dot_nt(rows, wqkv_ref[j * D_ATTN:(j + 1) * D_ATTN, :])

    def feature_major(j):
        return _dot_nt(wqkv_ref[j * D_ATTN:(j + 1) * D_ATTN, :], xn)

    def head_norm(u, g_row):
        ssq = _dot((u * u).astype(BF16), bd_ref[...])
        return u * lax.rsqrt(ssq * (1.0 / HEAD_DIM) + EPS) * g_row

    loads[1].wait()
    wqkv_ref[0:3 * D_ATTN, :] = wt_ref[3 * D_CONV:n_main, :].astype(BF16)
    w_fl = wt_ref[n_main:n_main + N_HEADS, :]
    wfl_ref[...] = jnp.concatenate([w_fl] * (LANES // N_HEADS), axis=0).astype(BF16)
    wqkv_ref[3 * D_ATTN:, :] = wfl_ref[0:BF16_ROWS, :]
    q_ref[...] = (head_norm(rows_major(0, xn_new), qgr) * ATTN_SCALE).astype(BF16)
    k = head_norm(rows_major(1, xn), kgr)
    km_ref[...] = k[0:N_META].astype(BF16)
    k_ref[...] = k[N_META:]
    v_ref[...] = rows_major(2, xn_new)
    kt_ref[...] = _head_norm_t(feature_major(1), kgc_ref[...])
    vt_ref[...] = feature_major(2)
    lfpn_ref[...] = _log_sigmoid(_dot_nt(xn_new, wfl_ref[...]) + bfr_ref[...])
    lft = _log_sigmoid(_dot_nt(wflt_ref[...], xn)[0:N_HEADS] + bfc)
    lft_ref[...] = lft
    for s in range(n_new):
        lftn_ref[s] = lft[:, N_META + s * dec:N_META + (s + 1) * dec]
    loads[2].wait()
    wgl_ref[...] = wt_ref[n_main + N_HEADS:, :].astype(BF16)
    gate_ref[...] = jax.nn.sigmoid(_dot_nt(xn_new, wgl_ref[...])).astype(BF16)


def _project_small(meta, x_new, state, wt, small_params, n_prompt):
    n_new = state.shape[0]
    n_rows = x_new.shape[0]
    dec = n_rows // n_new
    n_all = N_META + n_rows
    n_main = 3 * D_CONV + 3 * D_ATTN
    head_of = np.arange(D_ATTN) // HEAD_DIM
    same_head = jnp.asarray(head_of[:, None] == head_of[None, :], BF16)
    full = lambda *shape: pl.BlockSpec(shape, lambda i: (0,) * len(shape))
    out_shape = (
        jax.ShapeDtypeStruct((n_rows, D_CONV), BF16),
        jax.ShapeDtypeStruct((n_rows, D_ATTN), BF16),
        jax.ShapeDtypeStruct((n_rows, D_ATTN), F32),
        jax.ShapeDtypeStruct((n_rows, D_ATTN), F32),
        jax.ShapeDtypeStruct((N_META, D_ATTN), BF16),
        jax.ShapeDtypeStruct((D_ATTN, n_all), F32),
        jax.ShapeDtypeStruct((D_ATTN, n_all), F32),
        jax.ShapeDtypeStruct((N_HEADS, n_all), F32),
        jax.ShapeDtypeStruct((n_new, N_HEADS, dec), F32),
        jax.ShapeDtypeStruct((n_rows, LANES), F32),
        jax.ShapeDtypeStruct((n_rows, 2 * D_MODEL), BF16),
        jax.ShapeDtypeStruct((n_prompt, CONV_HALO, D_CONV), F32),
        jax.ShapeDtypeStruct((n_new, CONV_HALO, D_CONV), F32),
        jax.ShapeDtypeStruct((3 * D_CONV, D_MODEL), BF16),
        jax.ShapeDtypeStruct((3 * D_ATTN + BF16_ROWS, D_MODEL), BF16),
        jax.ShapeDtypeStruct((wt.shape[0] - n_main - N_HEADS, D_MODEL), BF16),
        jax.ShapeDtypeStruct((LANES, D_MODEL), BF16),
        jax.ShapeDtypeStruct((1, LANES), F32),
        jax.ShapeDtypeStruct((D_ATTN, 1), F32),
        jax.ShapeDtypeStruct((D_ATTN, 1), F32),
        jax.ShapeDtypeStruct((CONV_W, D_CONV), F32),
    )
    ins = (meta, x_new, state, wt) + tuple(small_params) + (same_head,)
    return pl.pallas_call(
        _proj_small_kernel, grid=(1,),
        in_specs=[full(*meta.shape), full(*x_new.shape), full(*state.shape)]
        + [pl.BlockSpec(memory_space=pl.ANY)] + [_const_spec(w.shape) for w in ins[4:]],
        out_specs=tuple(full(*s.shape) for s in out_shape), out_shape=out_shape,
        scratch_shapes=[pltpu.VMEM((n_new + 1, dec + ZPAD, D_CONV), F32),
                        pltpu.VMEM(wt.shape, F32),
                        pltpu.SemaphoreType.DMA((3,))],
        compiler_params=_params(1), name="proj_small",
    )(*ins)


def _prompt_attn_kernel(qt_ref, kpos, vtb_ref, vtm_ref, lft_ref, lftm_ref,
                        qg_ref, kg_ref, o_ref, vb, kbias, crow, cend, qcat,
                        m_s, acc_s, sbuf, *, seq):
    blk = ATTN_BLOCK
    n_blk = seq // blk
    n_bias = N_SPLIT * N_HEADS
    pad = blk - N_META

    ones_row = (lax.broadcasted_iota(jnp.int32, (V_SLAB - HEAD_DIM, blk), 0) == 0).astype(BF16)
    lane_m = lax.broadcasted_iota(jnp.int32, (HEAD_DIM, LANES), 1)
    vb[0] = jnp.zeros((N_HEADS * V_SLAB, blk), BF16)
    for h in range(N_HEADS):
        rows = slice(h * HEAD_DIM, (h + 1) * HEAD_DIM)
        slab = slice(h * V_SLAB, h * V_SLAB + HEAD_DIM)
        meta = jnp.where(lane_m < N_META, vtm_ref[rows, 0:LANES], 0.0)
        vb[0, slab, blk - LANES:] = pltpu.roll(meta, LANES - N_META, axis=1).astype(BF16)
        for j in range(n_blk):
            vb[j + 1, slab, :] = vtb_ref[rows, j * blk:(j + 1) * blk]
        for j in range(n_blk + 1):
            vb[j, h * V_SLAB + HEAD_DIM:(h + 1) * V_SLAB, :] = ones_row

    qk_bound = (NORM_SLACK * HEAD_DIM * ATTN_SCALE * LOG2E
                * jnp.max(jnp.abs(qg_ref[...]), axis=1, keepdims=True)
                * jnp.max(jnp.abs(kg_ref[...]), axis=1, keepdims=True))

    upper = _upper_triangle(blk)
    ones_zeros = jnp.concatenate([jnp.ones((n_bias, blk), F32),
                                  jnp.zeros((LANES - 2 * n_bias, blk), F32)], axis=0)

    def store_kbias(j, c_log2, is_pad=None):
        neg = [-piece for piece in _split3(c_log2)]
        if is_pad is not None:
            neg = [jnp.where(is_pad, fill, piece)
                   for piece, fill in zip(neg, (-MASKED_BIAS, 0.0, 0.0))]
        kbias[j * blk:(j + 1) * blk, :] = jnp.concatenate(neg + [ones_zeros], axis=0).T.astype(BF16)

    lane_h = lax.broadcasted_iota(jnp.int32, (N_HEADS, LANES), 1)
    lf_meta = jnp.where(lane_h < N_META, lftm_ref[:, 0:LANES], 0.0)
    lf_blk0 = jnp.concatenate([jnp.zeros((N_HEADS, blk - LANES), F32),
                               pltpu.roll(lf_meta, LANES - N_META, axis=1)], axis=1)
    c_row = _cumsum_lanes(lf_blk0, upper)
    store_kbias(0, c_row * LOG2E, lax.broadcasted_iota(jnp.int32, (N_HEADS, blk), 1) < pad)
    off_r = c_row[:, blk - 1:blk]
    c_end = jnp.where(lane_h == 0, off_r * LOG2E, 0.0)
    for j in range(n_blk):
        c_row = _cumsum_lanes(lft_ref[:, j * blk:(j + 1) * blk], upper) + off_r
        crow[j] = c_row * LOG2E
        store_kbias(j + 1, c_row * LOG2E)
        off_r = c_row[:, blk - 1:blk]
        c_end = jnp.where(lane_h == j + 1, off_r * LOG2E, c_end)
    cend[...] = c_end

    row128 = lax.broadcasted_iota(jnp.int32, (LANES, blk), 0)
    krow = lax.broadcasted_iota(jnp.int32, (2 * blk, blk), 0)
    qcol = lax.broadcasted_iota(jnp.int32, (2 * blk, blk), 1)

    def q_block(t, _):
        tok0 = pl.multiple_of(t * blk, blk)
        c_q = crow[t]
        hi, mid, lo = _split3(c_q)
        bias_rows = jnp.concatenate(
            [jnp.ones((n_bias, blk), F32), hi, mid, lo,
             jnp.zeros((LANES - 2 * n_bias, blk), F32)], axis=0)
        for h in range(N_HEADS):
            pair = qt_ref[(h // 2) * LANES:(h // 2 + 1) * LANES, pl.ds(tok0, blk)]
            in_head = (row128 // HEAD_DIM) == (h % 2)
            qcat[h, 0:LANES, :] = jnp.where(in_head, pair, jnp.zeros_like(pair))
            qcat[h, LANES:, :] = jnp.where(row128 % N_HEADS == h, bias_rows, 0.0).astype(BF16)
        m_s[...] = jnp.full(m_s.shape, -jnp.inf, F32)
        acc_s[...] = jnp.zeros(acc_s.shape, F32)

        gap = 2.0 * qk_bound + c_q[:, 0:1] - cend[...]
        needed = jnp.logical_and(gap >= -SKIP_LOG2, lane_h < t)
        n_needed = jnp.max(
            jnp.sum(jnp.where(needed, 1.0, 0.0), axis=1, keepdims=True)).astype(jnp.int32)

        def key_pass(row0, n_rows, pv, visible):
            m_blk = []
            for h in range(N_HEADS):
                g = h // 2
                kc = jnp.concatenate([kpos[pl.ds(row0, n_rows), g * LANES:(g + 1) * LANES],
                                      kbias[pl.ds(row0, n_rows), :]], axis=1)
                s = _dot(kc, qcat[h])
                if visible is not None:
                    s = jnp.where(visible, s, -jnp.inf)
                sbuf[h, 0:n_rows, :] = s
                m_blk.append(jnp.max(s, axis=0, keepdims=True))
            for h in range(N_HEADS):
                slab = slice(h * V_SLAB, (h + 1) * V_SLAB)
                m_old = m_s[h:h + 1, :]
                m_new = jnp.maximum(m_old, m_blk[h])
                alpha = jnp.exp2(m_old - m_new)
                p = jnp.exp2(sbuf[h, 0:n_rows, :] - m_new)
                m_s[h:h + 1, :] = m_new
                acc_s[slab, :] = alpha * acc_s[slab, :] + pv(slab, p.astype(BF16))

        key_pass(tok0, 2 * blk,
                 lambda slab, p: (_dot(vb[t, slab, :], p[0:blk])
                                  + _dot(vb[t + 1, slab, :], p[blk:])),
                 krow <= qcol + blk)

        def older_block(j, _):
            key_pass(pl.multiple_of(j * blk, blk), blk,
                     lambda slab, p: _dot(vb[j, slab, :], p), None)
            return 0

        lax.fori_loop(t - n_needed, t, older_block, 0)

        o_t = []
        for h in range(N_HEADS):
            norm = acc_s[h * V_SLAB + HEAD_DIM:h * V_SLAB + HEAD_DIM + 1, :]
            o_t.append(acc_s[h * V_SLAB:h * V_SLAB + HEAD_DIM, :] * (1.0 / norm))
        o_ref[pl.ds(tok0, blk), :] = jnp.concatenate(o_t, axis=0).T.astype(BF16)
        return 0

    lax.fori_loop(0, n_blk, q_block, 0)


def _sample_attn_kernel(q_ref, kn_ref, vn_ref, lf_ref, lft_ref, ckt_ref, cvt_ref, clft_ref,
                        o_ref, crow, *, past, dec):
    blk = ATTN_BLOCK
    n_keys = past + LANES

    upper = _upper_triangle(blk)
    off = jnp.zeros((N_HEADS, 1), F32)
    for j in range(past // blk):
        loc = _cumsum_lanes(clft_ref[:, j * blk:(j + 1) * blk], upper) + off
        crow[:, j * blk:(j + 1) * blk] = loc
        off = loc[:, blk - 1:blk]
    crow[:, 0:past] = crow[:, 0:past] - off

    cq_c = _cumsum_few(lf_ref[:, 0:N_HEADS], axis=0)
    cq_r = _cumsum_few(lft_ref[...], axis=1)
    crow[:, past:] = jnp.full((N_HEADS, LANES), MASKED_BIAS, F32)
    crow[:, past:past + dec] = cq_r

    q = q_ref[...]
    lane_head = lax.broadcasted_iota(jnp.int32, (dec, D_ATTN), 1) // HEAD_DIM
    q_exp = jnp.concatenate(
        [jnp.where(lane_head == h, q, jnp.zeros_like(q)) for h in range(N_HEADS)], axis=0)
    pad_rows = jnp.zeros((LANES - dec, D_ATTN), BF16)
    k_new = jnp.concatenate([kn_ref[...].astype(BF16), pad_rows], axis=0)
    v_new = jnp.concatenate([vn_ref[...].astype(BF16), pad_rows], axis=0)
    s_all = jnp.concatenate(
        [_dot(q_exp, ckt_ref[...].astype(BF16)), _dot_nt(q_exp, k_new)], axis=1)

    kpos = lax.broadcasted_iota(jnp.int32, (dec, n_keys), 1)
    qpos = past + lax.broadcasted_iota(jnp.int32, (dec, n_keys), 0)
    visible = kpos <= qpos
    probs = []
    norms = []
    for h in range(N_HEADS):
        s = s_all[h * dec:(h + 1) * dec, :] + cq_c[:, h:h + 1] - crow[h:h + 1, :]
        s = jnp.where(visible, s, -jnp.inf)
        p = jnp.exp(s - jnp.max(s, axis=-1, keepdims=True))
        norms.append(jnp.sum(p, axis=-1, keepdims=True))
        probs.append(p.astype(BF16))
    p_all = jnp.concatenate(probs, axis=0)
    o_all = (_dot_nt(p_all[:, 0:past], cvt_ref[...].astype(BF16))
             + _dot(p_all[:, past:], v_new))
    out = jnp.zeros((dec, D_ATTN), F32)
    for h in range(N_HEADS):
        o = o_all[h * dec:(h + 1) * dec, :] / norms[h]
        out = out + jnp.where(lane_head == h, o, 0.0)
    o_ref[...] = out.astype(BF16)


def _attn_kernel(*refs, seq, past, dec, n_cast):
    n_p, n_s = _N_PROMPT_IN, _N_SAMPLE_IN
    n_in = n_p + n_s + n_cast
    p_in, s_in, cast_in = refs[0:n_p], refs[n_p:n_p + n_s], refs[n_p + n_s:n_in]
    o_ref, os_ref = refs[n_in:n_in + 2]
    cast_out = refs[n_in + 2:n_in + 2 + n_cast]
    scratch = refs[n_in + 2 + n_cast:]
    for src, dst in zip(cast_in, cast_out):
        dst[...] = src[...].astype(BF16)
    _sample_attn_kernel(*s_in, os_ref, scratch[-1], past=past, dec=dec)
    _prompt_attn_kernel(*p_in, o_ref, *scratch[:-1], seq=seq)


_N_PROMPT_IN = 8
_N_SAMPLE_IN = 8


def _attention(prompt_in, sample_in, f32_wts):
    qt, _, vtb, vt_meta, lft, lft_meta, q_gain, k_gain = prompt_in
    q_s, _, _, _, _, cache_kt, _, _ = sample_in
    b, _, seq = qt.shape
    db, dec, _ = q_s.shape
    assert b == db, "one new stream and one running stream per grid step"
    past = cache_kt.shape[2]
    blk = ATTN_BLOCK
    n_blk = seq // blk
    n_pos = (n_blk + 1) * blk
    per_b = lambda *shape: pl.BlockSpec((None,) + shape, lambda i: (i,) + (0,) * len(shape))
    chunk_spec = lambda w: pl.BlockSpec((w.shape[0] // b, w.shape[1]), lambda i: (i, 0))
    in_specs = [per_b(D_ATTN, seq), per_b(n_pos, D_ATTN),
                per_b(D_ATTN, seq), _const_spec(vt_meta.shape),
                per_b(N_HEADS, seq), _const_spec(lft_meta.shape),
                _const_spec(q_gain.shape), _const_spec(k_gain.shape),
                per_b(dec, D_ATTN), per_b(dec, D_ATTN), per_b(dec, D_ATTN),
                per_b(dec, LANES), per_b(N_HEADS, dec),
                per_b(D_ATTN, past), per_b(D_ATTN, past), per_b(N_HEADS, past)]
    in_specs += [chunk_spec(w) for w in f32_wts]
    kern = functools.partial(_attn_kernel, seq=seq, past=past, dec=dec, n_cast=len(f32_wts))
    return pl.pallas_call(
        kern, grid=(b,), in_specs=in_specs,
        out_specs=(per_b(seq, D_ATTN), per_b(dec, D_ATTN))
        + tuple(chunk_spec(w) for w in f32_wts),
        out_shape=(jax.ShapeDtypeStruct((b, seq, D_ATTN), BF16),
                   jax.ShapeDtypeStruct((b, dec, D_ATTN), BF16))
        + tuple(jax.ShapeDtypeStruct(w.shape, BF16) for w in f32_wts),
        scratch_shapes=[pltpu.VMEM((n_blk + 1, N_HEADS * V_SLAB, blk), BF16),
                        pltpu.VMEM((n_pos, LANES), BF16),
                        pltpu.VMEM((n_blk, N_HEADS, blk), F32),
                        pltpu.VMEM((N_HEADS, LANES), F32),
                        pltpu.VMEM((N_HEADS, 2 * LANES, blk), BF16),
                        pltpu.VMEM((N_HEADS, blk), F32),
                        pltpu.VMEM((N_HEADS * V_SLAB, blk), F32),
                        pltpu.VMEM((N_HEADS, 2 * blk, blk), F32),
                        pltpu.VMEM((N_HEADS, past + LANES), F32)],
        compiler_params=_params(1), name="attention",
    )(*prompt_in, *sample_in, *f32_wts)


def _merge_mlp_tail(x, conv_bf16, attn_ref, g_conv, g_attn, wbc_ref, wba_ref, wo_ref, g2_ref,
                    wup_ref, wdn_ref, y_ref):
    merged = g_conv * _dot(conv_bf16, wbc_ref[...]) + g_attn * _dot(attn_ref[...], wba_ref[...])
    h = x + _dot(merged.astype(BF16), wo_ref[...])
    hn = _rms_rows(h, g2_ref[...])
    acc = h
    for c in range(D_FF // D_MODEL):
        cols = slice(c * D_MODEL, (c + 1) * D_MODEL)
        a = jnp.maximum(_dot(hn, wup_ref[:, cols]), 0.0)
        acc = acc + _dot((a * a).astype(BF16), wdn_ref[cols, :])
    y_ref[...] = acc


def _branch_mlp_kernel(x_ref, left_ref, attn_ref, xs_ref, convs_ref, attns_ref, gates_ref,
                       g1_ref, wa_ref, wgl_ref, cw_ref, cb_ref,
                       wbc_ref, wba_ref, wo_ref, g2_ref, wup_ref, wdn_ref,
                       y_ref, zlast_ref, ys_ref, zbuf, *, n_tiles, tiles_per_seq):
    step = pl.program_id(0)

    @pl.when(jnp.logical_and(step % tiles_per_seq == 0, step < n_tiles))
    def _():
        zbuf[:, ZPAD - CONV_HALO:ZPAD, :] = left_ref[...]

    @pl.when(step < n_tiles)
    def _():
        x = x_ref[...]
        xn = _rms_rows(x, g1_ref[...])
        (conv,), (tail,) = _short_conv(xn, wa_ref, cw_ref, cb_ref, zbuf, 1, x.shape[0])
        zlast_ref[0] = tail
        g_conv = jax.nn.sigmoid(_dot_nt(xn, wgl_ref[0:D_MODEL, :]))
        g_attn = jax.nn.sigmoid(_dot_nt(xn, wgl_ref[D_MODEL:2 * D_MODEL, :]))
        _merge_mlp_tail(x, conv.astype(BF16), attn_ref, g_conv, g_attn, wbc_ref, wba_ref,
                        wo_ref, g2_ref, wup_ref, wdn_ref, y_ref)

    @pl.when(step == n_tiles)
    def _():
        _merge_mlp_tail(xs_ref[...], convs_ref[...], attns_ref,
                        gates_ref[:, 0:D_MODEL].astype(F32),
                        gates_ref[:, D_MODEL:2 * D_MODEL].astype(F32),
                        wbc_ref, wba_ref, wo_ref, g2_ref, wup_ref, wdn_ref, ys_ref)


def _branch_mlp(x2d, left, attn, small, wts, *, rows, seq):
    n_rows = x2d.shape[0]
    tiles = seq // rows
    n_tiles = n_rows // rows
    last = n_tiles - 1
    row_spec = lambda width: pl.BlockSpec((rows, width), lambda i: (jnp.minimum(i, last), 0))
    seq_spec = pl.BlockSpec((1, CONV_HALO, D_CONV),
                            lambda i: (jnp.minimum(i, last) // tiles, 0, 0))
    full = lambda a: pl.BlockSpec(a.shape, lambda i: (0,) * a.ndim)
    kern = functools.partial(_branch_mlp_kernel, n_tiles=n_tiles, tiles_per_seq=tiles)
    return pl.pallas_call(
        kern, grid=(n_tiles + 1,),
        in_specs=[row_spec(D_MODEL), seq_spec, row_spec(D_ATTN)] + [full(a) for a in small]
        + [_const_spec(w.shape) for w in wts],
        out_specs=(row_spec(D_MODEL), seq_spec, full(small[0])),
        out_shape=(jax.ShapeDtypeStruct((n_rows, D_MODEL), F32),
                   jax.ShapeDtypeStruct((n_rows // seq, CONV_HALO, D_CONV), F32),
                   jax.ShapeDtypeStruct(small[0].shape, F32)),
        scratch_shapes=[pltpu.VMEM((1, rows + ZPAD, D_CONV), F32)],
        compiler_params=_params(1), name="branch_mlp",
    )(x2d, left, attn, *small, *wts)


def kernel(x_prompt, x_sample, cache_k, cache_v, cache_logf, state_conv, meta,
           norm1_g, w_in, b_f, conv_w, conv_b, q_norm_g, k_norm_g,
           w_br_conv, w_br_attn, w_out, norm2_g, w_up, w_down):
    b, seq, _ = x_prompt.shape
    db, dec, _ = x_sample.shape
    past = cache_k.shape[2]
    length = N_META + seq
    assert dec == N_META, "the small tile treats the meta tokens as one more short sequence"

    wt = w_in[0].T
    assert 2 * HEAD_DIM == LANES, "the q and k gains share one vreg row"
    g1 = norm1_g[0][None, :]
    conv_bias = conv_b[0][None, :]
    gains = jnp.concatenate([q_norm_g[0], k_norm_g[0]])[None, :]
    bf_lanes = jnp.broadcast_to(b_f[0][:, None], (N_HEADS, LANES))

    x_new = x_sample.reshape(db * dec, D_MODEL)
    (conv_s, q_s, k_s, v_s, k_meta, kt_s, vt_s, lft_s, lft_new, lfp_s, gate_s, left_p, zlast_s, wt_conv, wt_qkv, wt_gate, wt_fl,
     bf_row, qg_col, kg_col, conv_taps) = _project_small(
        meta, x_new, state_conv[0], wt, (g1, gains, bf_lanes, conv_w, conv_bias), b)
    qkv_wts = (g1, wt_qkv, bf_lanes, qg_col, kg_col)
    conv_wts = (conv_taps, conv_bias)

    x_rows = x_prompt.reshape(b * seq, D_MODEL)
    (qt_p, kt_p, vt_p, lftp_p, lft_p, kb_p, vtb_p,
     wdn_b) = _project_qkv(
        x_rows, qkv_wts, (kt_s, vt_s, lft_s, k_meta), (w_down[0],), b=b, seq=seq)
    k_new = k_s.reshape(db, dec, D_ATTN)
    v_new = v_s.reshape(db, dec, D_ATTN)
    cache_kt = jnp.transpose(cache_k[0], (0, 2, 3, 1)).reshape(db, D_ATTN, past)
    cache_vt = jnp.transpose(cache_v[0], (0, 2, 3, 1)).reshape(db, D_ATTN, past)
    attn_p, attn_s, wup_b, wbc_b, wba_b, wo_b = _attention(
        (qt_p, kb_p, vtb_p, vt_s, lft_p, lft_s, q_norm_g, k_norm_g),
        (q_s.reshape(db, dec, D_ATTN), k_new, v_new,
         lfp_s.reshape(db, dec, LANES), lft_new,
         cache_kt, cache_vt, jnp.swapaxes(cache_logf[0], 1, 2)),
        (w_up[0], w_br_conv[0], w_br_attn[0], w_out[0]))
    mlp_wts = (wbc_b, wba_b, wo_b, norm2_g[0][None, :], wup_b, wdn_b)

    y_prompt, zlast_p, y_sample = _branch_mlp(
        x_rows, left_p, attn_p.reshape(b * seq, D_ATTN),
        (x_new, conv_s, attn_s.reshape(db * dec, D_ATTN), gate_s),
        (g1, wt_conv, wt_gate) + conv_wts + mlp_wts, rows=MLP_ROWS, seq=seq)

    def heads_last(t):
        return jnp.transpose(t.reshape(b, N_HEADS, HEAD_DIM, length), (0, 3, 1, 2))[None]

    return (y_prompt.reshape(b, seq, D_MODEL),
            y_sample.reshape(db, dec, D_MODEL),
            heads_last(kt_p),
            heads_last(vt_p),
            jnp.swapaxes(lftp_p, 1, 2)[None],
            zlast_p[None],
            k_new.reshape(1, db, dec, N_HEADS, HEAD_DIM),
            v_new.reshape(1, db, dec, N_HEADS, HEAD_DIM),
            jnp.swapaxes(lft_new, 1, 2)[None],
            zlast_s[None])
```
